```python
import math
import jax
import jax.numpy as jnp
from jax import lax
import numpy as np

D_MODEL = 2048
BATCH = 4
SEQ = 2048
DEPTH = 2
DEC_BATCH = 32
DEC_SEQ = 8
PAST_LEN = 8192
PAGE_SIZE = 128

MIX_WIDTH = D_MODEL
A_HEADS = 8
A_DH = 64
A_VD = 2 * A_DH
A_Q = A_HEADS * 2 * A_DH
A_V = A_HEADS * A_VD
A_COLS = 2 * A_Q + A_V
B_HEADS = 16
B_DH = 64
B_W = B_HEADS * B_DH
B_DECAY_LORA = 96
B_AAA_LORA = 96
B_GATE_LORA = 256
B_COLS = 3 * B_W + B_DECAY_LORA + B_AAA_LORA + B_GATE_LORA
B_GN_EPS = 64e-5
EVEN_COLS = A_COLS + B_COLS
C_QK_HEADS = 4
C_V_HEADS = 8
C_DH = 128
C_CONV = 4
C_QK = C_QK_HEADS * C_DH
C_V = C_V_HEADS * C_DH
C_CONV_CH = 2 * C_QK + C_V
C_COLS = C_CONV_CH + C_V + 2 * C_V_HEADS
C_CHUNK = 64
D_HEADS = 4
D_DK = 128
D_DV = 256
D_K = D_HEADS * D_DK
D_V = D_HEADS * D_DV
D_GATE_LORA = 16
D_GATE_NORM = 16.0
D_COLS = 2 * D_K + 2 * D_V + D_GATE_LORA
D_CHUNK = 16
ODD_COLS = C_COLS + D_COLS
N_MEM = 256
X_HEADS = 4
X_DH = 128
X_W = X_HEADS * X_DH
FF_DENSE = 5632
N_EXPERTS = 8
TOP_K = 2
FF_EXPERT = 2816
Q_BLOCK = 128
EPS = 1e-6

kernel_name = 'hybrid_diffattn_rwkv7_gdn_gla_decode_step'


def _rms_norm(x, w, eps=EPS):
    xf = x.astype(jnp.float32)
    y = xf * lax.rsqrt(jnp.mean(xf * xf, axis=-1, keepdims=True) + eps)
    return (y * w.astype(jnp.float32)).astype(x.dtype)


def _l2_norm(x, eps=EPS):
    xf = x.astype(jnp.float32)
    return xf * lax.rsqrt(jnp.sum(xf * xf, axis=-1, keepdims=True) + eps)


def _alibi_slopes():
    return jnp.exp2(-(8.0 / A_HEADS) * jnp.arange(1, A_HEADS + 1, dtype=jnp.float32))


def _diff_lambda(lq1, lk1, lq2, lk2, lam_init):
    f = lambda a: a.astype(jnp.float32)
    return jnp.exp(jnp.sum(f(lq1) * f(lk1))) - jnp.exp(jnp.sum(f(lq2) * f(lk2))) + lam_init


def _diff_qkv(pa, q_norm, k_norm):
    b, t, _ = pa.shape
    q = _rms_norm(pa[..., :A_Q].reshape(b, t, A_HEADS, 2, A_DH), q_norm)
    k = _rms_norm(pa[..., A_Q:2 * A_Q].reshape(b, t, A_HEADS, 2, A_DH), k_norm)
    v = pa[..., 2 * A_Q:].reshape(b, t, A_HEADS, A_VD)
    return q, k, v


def _diff_attend(q, segments, q_pos, lam):
    slopes = _alibi_slopes()
    scale = A_DH ** -0.5
    scores = []
    for k, _, k_pos in segments:
        dist = (q_pos[:, None] - k_pos[None, :]).astype(jnp.float32)
        bias = jnp.where(dist >= 0, -slopes[:, None, None] * dist, -jnp.inf)
        s = jnp.einsum('bqhmd,bkhmd->mbhqk', q, k).astype(jnp.float32) * scale
        scores.append(s + bias)
    p = jax.nn.softmax(jnp.concatenate(scores, axis=-1), axis=-1)
    p = p[0] - lam * p[1]
    out = 0
    off = 0
    for _, v, _ in segments:
        tk = v.shape[1]
        out = out + jnp.einsum('bhqk,bkhd->bqhd', p[..., off:off + tk].astype(v.dtype), v)
        off += tk
    return out


def _diff_attention_prompt(q, k, v, lam):
    b, t = q.shape[:2]
    blk = Q_BLOCK if t % Q_BLOCK == 0 else t
    nb = t // blk
    pos = jnp.arange(t, dtype=jnp.int32)
    qb = jnp.moveaxis(q.reshape(b, nb, blk, A_HEADS, 2, A_DH), 1, 0)
    out = lax.map(lambda a: _diff_attend(a[0], ((k, v, pos),), a[1], lam), (qb, pos.reshape(nb, blk)))
    return jnp.moveaxis(out, 0, 1).reshape(b, t, A_HEADS, A_VD)


def _diff_heads_out(o, subln, lam_init):
    b, t = o.shape[:2]
    return (_rms_norm(o, subln) * (1.0 - lam_init)).reshape(b, t, A_V)


def _rwkv7(pb, prev, s0, mu, w0, w2, a0, a2, g2, k_k, k_a, r_k, lnx_w, lnx_b):
    f32 = jnp.float32
    b, t, _ = pb.shape
    shifted = jnp.concatenate([prev.astype(pb.dtype), pb[:, :-1]], axis=1)
    xs = pb + (shifted - pb) * mu
    r = xs[..., :B_W]
    k = xs[..., B_W:2 * B_W]
    v = xs[..., 2 * B_W:3 * B_W]
    lw = xs[..., 3 * B_W:3 * B_W + B_DECAY_LORA]
    la = xs[..., 3 * B_W + B_DECAY_LORA:3 * B_W + B_DECAY_LORA + B_AAA_LORA]
    lg = xs[..., 3 * B_W + B_DECAY_LORA + B_AAA_LORA:]
    logw = -jax.nn.softplus(-(w0 + jnp.tanh(lw) @ w2).astype(f32)) - 0.5
    decay = jnp.exp(-jnp.exp(logw))
    a = jax.nn.sigmoid((a0 + la @ a2).astype(f32))
    g = jax.nn.sigmoid(lg) @ g2
    hs = lambda z: z.reshape(b, t, B_HEADS, B_DH)
    kk = _l2_norm(hs(k * k_k))
    kf = k.astype(f32) * (1.0 + (a - 1.0) * k_a)
    r_, k_, v_, w_, a_ = hs(r.astype(f32)), hs(kf), hs(v.astype(f32)), hs(decay), hs(a)

    def step(s, inp):
        rt, wt, kt, vt, kkt, at = inp
        sa = jnp.einsum('bhvk,bhk->bhv', s, -kkt)
        s = s * wt[:, :, None, :] + sa[..., None] * (kkt * at)[:, :, None, :] + vt[..., None] * kt[:, :, None, :]
        return s, jnp.einsum('bhvk,bhk->bhv', s, rt)

    tm = lambda z: jnp.moveaxis(z, 1, 0)
    s, y = lax.scan(step, s0.astype(f32), (tm(r_), tm(w_), tm(k_), tm(v_), tm(kk), tm(a_)))
    y = jnp.moveaxis(y, 0, 1)
    mean = jnp.mean(y, axis=-1, keepdims=True)
    var = jnp.mean(jnp.square(y - mean), axis=-1, keepdims=True)
    y = (y - mean) * lax.rsqrt(var + B_GN_EPS) * lnx_w.reshape(B_HEADS, B_DH) + lnx_b.reshape(B_HEADS, B_DH)
    y = y + jnp.sum(r_ * k_ * r_k, axis=-1, keepdims=True) * v_
    out = y.reshape(b, t, B_W) * g
    return out.astype(pb.dtype), s, pb[:, -1:]


def _short_conv(x, prev, w):
    t = x.shape[1]
    xc = jnp.concatenate([prev.astype(x.dtype), x], axis=1)
    y = sum(xc[:, j:j + t] * w[j] for j in range(C_CONV))
    return jax.nn.silu(y), xc[:, -(C_CONV - 1):]


def _chunks(z, b, n, c, h):
    z = z.reshape(b, n, c, h, *z.shape[3:])
    return jnp.moveaxis(z, (1, 3), (0, 2))


def _chunk_gated_delta(q, k, v, g, beta, s0):
    b, t, h, _ = k.shape
    c = C_CHUNK if t % C_CHUNK == 0 else t
    n = t // c
    qc, kc, vc, gc, bc = (_chunks(z, b, n, c, h) for z in (q, k, v, g, beta))
    gc = jnp.cumsum(gc, axis=-1)
    incl = jnp.tril(jnp.ones((c, c), bool))
    strict = jnp.tril(jnp.ones((c, c), bool), -1)
    decay = jnp.exp(jnp.where(incl, gc[..., :, None] - gc[..., None, :], -jnp.inf))
    kb = kc * bc[..., None]
    l_mat = jnp.where(strict, jnp.einsum('...id,...jd->...ij', kb, kc) * decay, 0.0)
    ut = jnp.eye(c, dtype=jnp.float32) + l_mat
    u = lax.linalg.triangular_solve(ut, vc * bc[..., None], left_side=True, lower=True, unit_diagonal=True)
    w = lax.linalg.triangular_solve(ut, kb * jnp.exp(gc)[..., None], left_side=True, lower=True, unit_diagonal=True)

    def step(s, inp):
        q_, k_, u_, w_, g_, d_ = inp
        v_new = u_ - jnp.einsum('bhcd,bhdv->bhcv', w_, s)
        attn = jnp.einsum('bhid,bhjd->bhij', q_, k_) * d_
        o = jnp.einsum('bhcd,bhdv->bhcv', q_ * jnp.exp(g_)[..., None], s) + jnp.einsum('bhij,bhjv->bhiv', attn, v_new)
        g_last = g_[..., -1]
        s = s * jnp.exp(g_last)[..., None, None] + jnp.einsum(
            'bhcd,bhcv->bhdv', k_ * jnp.exp(g_last[..., None] - g_)[..., None], v_new)
        return s, o

    s, o = lax.scan(step, s0, (qc, kc, u, w, gc, decay))
    return jnp.transpose(o, (1, 0, 3, 2, 4)).reshape(b, t, h, -1), s


def _gated_deltanet(pc, conv_prev, s0, conv_w, a_log, dt_bias, onorm):
    f32 = jnp.float32
    b, t, _ = pc.shape
    qkv, conv_new = _short_conv(pc[..., :C_CONV_CH], conv_prev, conv_w)
    z = pc[..., C_CONV_CH:C_CONV_CH + C_V]
    bet = pc[..., C_CONV_CH + C_V:C_CONV_CH + C_V + C_V_HEADS]
    a = pc[..., C_CONV_CH + C_V + C_V_HEADS:]
    rep = C_V_HEADS // C_QK_HEADS
    q = jnp.repeat(_l2_norm(qkv[..., :C_QK].reshape(b, t, C_QK_HEADS, C_DH)), rep, axis=2) * (C_DH ** -0.5)
    k = jnp.repeat(_l2_norm(qkv[..., C_QK:2 * C_QK].reshape(b, t, C_QK_HEADS, C_DH)), rep, axis=2)
    v = qkv[..., 2 * C_QK:].reshape(b, t, C_V_HEADS, C_DH).astype(f32)
    beta = jax.nn.sigmoid(bet.astype(f32))
    g = -jnp.exp(a_log.astype(f32)) * jax.nn.softplus(a.astype(f32) + dt_bias.astype(f32))
    o, s = _chunk_gated_delta(q, k, v, g, beta, s0.astype(f32))
    o = _rms_norm(o, onorm) * jax.nn.silu(z.reshape(b, t, C_V_HEADS, C_DH).astype(f32))
    return o.reshape(b, t, C_V).astype(pc.dtype), s, conv_new


def _chunk_gla(q, k, v, gk, s0):
    b, t, h, _ = k.shape
    c = D_CHUNK if t % D_CHUNK == 0 else t
    n = t // c
    qc, kc, vc, gc = (_chunks(z, b, n, c, h) for z in (q, k, v, gk))
    bc = jnp.cumsum(gc, axis=-2)
    incl = jnp.tril(jnp.ones((c, c), bool))[:, :, None]

    def step(s, inp):
        q_, k_, v_, b_ = inp
        dec = jnp.exp(jnp.where(incl, b_[..., :, None, :] - b_[..., None, :, :], -jnp.inf))
        attn = jnp.einsum('bhid,bhjd,bhijd->bhij', q_, k_, dec)
        o = jnp.einsum('bhid,bhdv->bhiv', q_ * jnp.exp(b_), s) + jnp.einsum('bhij,bhjv->bhiv', attn, v_)
        b_last = b_[..., -1:, :]
        s = s * jnp.exp(b_last)[..., 0, :, None] + jnp.einsum('bhcd,bhcv->bhdv', k_ * jnp.exp(b_last - b_), v_)
        return s, o

    s, o = lax.scan(step, s0, (qc, kc, vc, bc))
    return jnp.transpose(o, (1, 0, 3, 2, 4)).reshape(b, t, h, -1), s


def _gla(pd, s0, w_gk2, b_gk, onorm):
    f32 = jnp.float32
    b, t, _ = pd.shape
    q = pd[..., :D_K].reshape(b, t, D_HEADS, D_DK).astype(f32) * (D_DK ** -0.5)
    k = pd[..., D_K:2 * D_K].reshape(b, t, D_HEADS, D_DK).astype(f32)
    v = pd[..., 2 * D_K:2 * D_K + D_V].reshape(b, t, D_HEADS, D_DV).astype(f32)
    gate = pd[..., 2 * D_K + D_V:2 * D_K + 2 * D_V].reshape(b, t, D_HEADS, D_DV).astype(f32)
    glr = pd[..., 2 * D_K + 2 * D_V:]
    gk = jax.nn.log_sigmoid((glr @ w_gk2 + b_gk).astype(f32)) / D_GATE_NORM
    o, s = _chunk_gla(q, k, v, gk.reshape(b, t, D_HEADS, D_DK), s0.astype(f32))
    o = _rms_norm(o, onorm) * jax.nn.silu(gate)
    return o.reshape(b, t, D_V).astype(pd.dtype), s


def _mem_kv(mem, norm_w, w_k, w_v, k_norm):
    b, m, _ = mem.shape
    h = _rms_norm(mem, norm_w)
    k = _rms_norm((h @ w_k).reshape(b, m, X_HEADS, X_DH), k_norm)
    v = (h @ w_v).reshape(b, m, X_HEADS, X_DH)
    return k, v


def _cross_attn(h, mk, mv, w_q, q_norm, w_o):
    b, t, _ = h.shape
    q = _rms_norm((h @ w_q).reshape(b, t, X_HEADS, X_DH), q_norm)
    s = jnp.einsum('bthd,bmhd->bhtm', q, mk).astype(jnp.float32) * (X_DH ** -0.5)
    p = jax.nn.softmax(s, axis=-1).astype(mv.dtype)
    return jnp.einsum('bhtm,bmhd->bthd', p, mv).reshape(b, t, X_W) @ w_o


def _swiglu(h, wg, wu, wd):
    return (jax.nn.silu(h @ wg) * (h @ wu)) @ wd


def _moe(h, w_router, wg, wu, wd):
    b, t, d = h.shape
    x = h.reshape(b * t, d)
    logits = (x @ w_router).astype(jnp.float32)
    top_v, top_i = lax.top_k(logits, TOP_K)
    gates = jax.nn.softmax(top_v, axis=-1)
    combine = jnp.einsum('nk,nke->ne', gates, jax.nn.one_hot(top_i, N_EXPERTS, dtype=jnp.float32))
    out = jnp.zeros((b * t, d), jnp.float32)
    for e in range(N_EXPERTS):
        out = out + combine[:, e:e + 1] * _swiglu(x, wg[e], wu[e], wd[e]).astype(jnp.float32)
    return out.reshape(b, t, d).astype(h.dtype)


def setup_inputs(seed: int = 0) -> dict:
    key = jax.random.key(seed)
    keys = jax.random.split(key, 128)
    counter = [0]

    def nk():
        counter[0] += 1
        return keys[counter[0] - 1]

    f32 = jnp.float32

    def nrm(shape, scale=1.0):
        return scale * jax.random.normal(nk(), shape, f32)

    def gain(shape):
        return 1.0 + 0.02 * jax.random.normal(nk(), shape, f32)

    ne = (DEPTH + 1) // 2
    no = DEPTH // 2
    n_pages = PAST_LEN // PAGE_SIZE
    n_used = DEC_BATCH * n_pages
    n_pool = n_used + max(1, n_used // 4)
    page_table = jax.random.permutation(nk(), n_pool)[:n_used].reshape(DEC_BATCH, n_pages).astype(jnp.int32)
    dt = jnp.exp(jax.random.uniform(nk(), (no, C_V_HEADS), f32, math.log(1e-3), math.log(1e-1)))
    c_dt_bias = dt + jnp.log(-jnp.expm1(-dt))
    c_a_log = jnp.log(jax.random.uniform(nk(), (no, C_V_HEADS), f32, 1.0, 16.0))
    b_w0 = jax.random.uniform(nk(), (ne, B_W), f32, -6.0, -1.0)
    b_mu = jax.random.uniform(nk(), (ne, B_COLS), f32)
    sd = D_MODEL ** -0.5
    return {
        'x_prompt': nrm((BATCH, SEQ, D_MODEL)),
        'x_sample': nrm((DEC_BATCH, DEC_SEQ, D_MODEL)),
        'cache_diff_k': nrm((ne, n_pool, PAGE_SIZE, A_HEADS, 2 * A_DH)),
        'cache_diff_v': nrm((ne, n_pool, PAGE_SIZE, A_HEADS, A_VD)),
        'state_rwkv': nrm((ne, DEC_BATCH, B_HEADS, B_DH, B_DH), 0.1),
        'state_rwkv_shift': nrm((ne, DEC_BATCH, 1, B_COLS)),
        'cache_mem_k': nrm((DEPTH, DEC_BATCH, N_MEM, X_HEADS, X_DH)),
        'cache_mem_v': nrm((DEPTH, DEC_BATCH, N_MEM, X_HEADS, X_DH)),
        'state_gdn': nrm((no, DEC_BATCH, C_V_HEADS, C_DH, C_DH), 0.1),
        'state_gdn_conv': nrm((no, DEC_BATCH, C_CONV - 1, C_CONV_CH)),
        'state_gla': nrm((no, DEC_BATCH, D_HEADS, D_DK, D_DV), 0.1),
        'page_table': page_table,
        'mem_prompt': nrm((BATCH, N_MEM, D_MODEL)),
        'norm_mix_w': gain((DEPTH, D_MODEL)),
        'norm_cross_w': gain((DEPTH, D_MODEL)),
        'norm_mem_w': gain((DEPTH, D_MODEL)),
        'norm_ffn_w': gain((DEPTH, D_MODEL)),
        'w_in_even': nrm((ne, D_MODEL, EVEN_COLS), sd),
        'w_out_even': nrm((ne, MIX_WIDTH, D_MODEL), MIX_WIDTH ** -0.5),
        'a_q_norm': gain((ne, A_DH)),
        'a_k_norm': gain((ne, A_DH)),
        'a_lam_q1': nrm((ne, A_DH), 0.1),
        'a_lam_k1': nrm((ne, A_DH), 0.1),
        'a_lam_q2': nrm((ne, A_DH), 0.1),
        'a_lam_k2': nrm((ne, A_DH), 0.1),
        'a_subln': gain((ne, A_VD)),
        'b_mu': b_mu,
        'b_w0': b_w0,
        'b_w2': nrm((ne, B_DECAY_LORA, B_W), 0.1),
        'b_a0': nrm((ne, B_W), 0.1),
        'b_a2': nrm((ne, B_AAA_LORA, B_W), 0.1),
        'b_g2': nrm((ne, B_GATE_LORA, B_W), B_GATE_LORA ** -0.5),
        'b_k_k': 0.85 + 0.02 * jax.random.normal(nk(), (ne, B_W), f32),
        'b_k_a': gain((ne, B_W)),
        'b_r_k': nrm((ne, B_HEADS, B_DH), 0.1),
        'b_lnx_w': gain((ne, B_W)),
        'b_lnx_b': nrm((ne, B_W), 0.02),
        'ffd_w_gate': nrm((ne, D_MODEL, FF_DENSE), sd),
        'ffd_w_up': nrm((ne, D_MODEL, FF_DENSE), sd),
        'ffd_w_down': nrm((ne, FF_DENSE, D_MODEL), FF_DENSE ** -0.5),
        'w_in_odd': nrm((no, D_MODEL, ODD_COLS), sd),
        'w_out_odd': nrm((no, MIX_WIDTH, D_MODEL), MIX_WIDTH ** -0.5),
        'c_conv_w': nrm((no, C_CONV, C_CONV_CH), C_CONV ** -0.5),
        'c_a_log': c_a_log,
        'c_dt_bias': c_dt_bias,
        'c_onorm': gain((no, C_DH)),
        'd_w_gk2': nrm((no, D_GATE_LORA, D_K), D_GATE_LORA ** -0.5),
        'd_b_gk': nrm((no, D_K), 0.1),
        'd_onorm': gain((no, D_DV)),
        'moe_w_router': nrm((no, D_MODEL, N_EXPERTS), sd),
        'moe_w_gate': nrm((no, N_EXPERTS, D_MODEL, FF_EXPERT), sd),
        'moe_w_up': nrm((no, N_EXPERTS, D_MODEL, FF_EXPERT), sd),
        'moe_w_down': nrm((no, N_EXPERTS, FF_EXPERT, D_MODEL), FF_EXPERT ** -0.5),
        'x_w_q': nrm((DEPTH, D_MODEL, X_W), sd),
        'x_w_k': nrm((DEPTH, D_MODEL, X_W), sd),
        'x_w_v': nrm((DEPTH, D_MODEL, X_W), sd),
        'x_w_o': nrm((DEPTH, X_W, D_MODEL), X_W ** -0.5),
        'x_q_norm': gain((DEPTH, X_DH)),
        'x_k_norm': gain((DEPTH, X_DH)),
    }


def reference(x_prompt, x_sample, cache_diff_k, cache_diff_v, state_rwkv, state_rwkv_shift, cache_mem_k,
              cache_mem_v, state_gdn, state_gdn_conv, state_gla, page_table, mem_prompt, norm_mix_w,
              norm_cross_w, norm_mem_w, norm_ffn_w, w_in_even, w_out_even, a_q_norm, a_k_norm, a_lam_q1,
              a_lam_k1, a_lam_q2, a_lam_k2, a_subln, b_mu, b_w0, b_w2, b_a0, b_a2, b_g2, b_k_k, b_k_a, b_r_k,
              b_lnx_w, b_lnx_b, ffd_w_gate, ffd_w_up, ffd_w_down, w_in_odd, w_out_odd, c_conv_w, c_a_log,
              c_dt_bias, c_onorm, d_w_gk2, d_b_gk, d_onorm, moe_w_router, moe_w_gate, moe_w_up, moe_w_down,
              x_w_q, x_w_k, x_w_v, x_w_o, x_q_norm, x_k_norm):
    f32 = jnp.float32
    xp, xs = x_prompt, x_sample
    bp = xp.shape[0]
    bs, ts, _ = xs.shape
    past = page_table.shape[1] * cache_diff_k.shape[2]
    pos_past = jnp.arange(past, dtype=jnp.int32)
    pos_new = past + jnp.arange(ts, dtype=jnp.int32)
    dk_p, dv_p, dk_s, dv_s = [], [], [], []
    rw_p, rw_s, sh_p, sh_s = [], [], [], []
    mk_l, mv_l = [], []
    gd_p, gd_s, cv_p, cv_s, gl_p, gl_s = [], [], [], [], [], []
    for layer in range(DEPTH):
        hp = _rms_norm(xp, norm_mix_w[layer])
        hs = _rms_norm(xs, norm_mix_w[layer])
        i = layer // 2
        if layer % 2 == 0:
            lam_init = 0.8 - 0.6 * math.exp(-0.3 * layer)
            lam = _diff_lambda(a_lam_q1[i], a_lam_k1[i], a_lam_q2[i], a_lam_k2[i], lam_init)
            pp = hp @ w_in_even[i]
            ps = hs @ w_in_even[i]
            qa, ka, va = _diff_qkv(pp[..., :A_COLS], a_q_norm[i], a_k_norm[i])
            oa_p = _diff_heads_out(_diff_attention_prompt(qa, ka, va, lam), a_subln[i], lam_init)
            qn, kn, vn = _diff_qkv(ps[..., :A_COLS], a_q_norm[i], a_k_norm[i])
            pk = cache_diff_k[i, page_table].reshape(bs, past, A_HEADS, 2, A_DH)
            pv = cache_diff_v[i, page_table].reshape(bs, past, A_HEADS, A_VD)
            oa_s = _diff_attend(qn, ((pk, pv, pos_past), (kn, vn, pos_new)), pos_new, lam)
            oa_s = _diff_heads_out(oa_s, a_subln[i], lam_init)
            dk_p.append(ka.reshape(bp, -1, A_HEADS, 2 * A_DH))
            dv_p.append(va)
            dk_s.append(kn.reshape(bs, ts, A_HEADS, 2 * A_DH))
            dv_s.append(vn)
            ob_p, s_p, last_p = _rwkv7(pp[..., A_COLS:], jnp.zeros((bp, 1, B_COLS), pp.dtype),
                                       jnp.zeros((bp, B_HEADS, B_DH, B_DH), f32), b_mu[i], b_w0[i], b_w2[i],
                                       b_a0[i], b_a2[i], b_g2[i], b_k_k[i], b_k_a[i], b_r_k[i], b_lnx_w[i],
                                       b_lnx_b[i])
            ob_s, s_s, last_s = _rwkv7(ps[..., A_COLS:], state_rwkv_shift[i], state_rwkv[i], b_mu[i], b_w0[i],
                                       b_w2[i], b_a0[i], b_a2[i], b_g2[i], b_k_k[i], b_k_a[i], b_r_k[i],
                                       b_lnx_w[i], b_lnx_b[i])
            rw_p.append(s_p)
            rw_s.append(s_s)
            sh_p.append(last_p)
            sh_s.append(last_s)
            xp = xp + jnp.concatenate([oa_p, ob_p], axis=-1) @ w_out_even[i]
            xs = xs + jnp.concatenate([oa_s, ob_s], axis=-1) @ w_out_even[i]
        else:
            pp = hp @ w_in_odd[i]
            ps = hs @ w_in_odd[i]
            oc_p, sc_p, cb_p = _gated_deltanet(pp[..., :C_COLS], jnp.zeros((bp, C_CONV - 1, C_CONV_CH), pp.dtype),
                                               jnp.zeros((bp, C_V_HEADS, C_DH, C_DH), f32), c_conv_w[i],
                                               c_a_log[i], c_dt_bias[i], c_onorm[i])
            oc_s, sc_s, cb_s = _gated_deltanet(ps[..., :C_COLS], state_gdn_conv[i], state_gdn[i], c_conv_w[i],
                                               c_a_log[i], c_dt_bias[i], c_onorm[i])
            od_p, sd_p = _gla(pp[..., C_COLS:], jnp.zeros((bp, D_HEADS, D_DK, D_DV), f32), d_w_gk2[i], d_b_gk[i],
                              d_onorm[i])
            od_s, sd_s = _gla(ps[..., C_COLS:], state_gla[i], d_w_gk2[i], d_b_gk[i], d_onorm[i])
            gd_p.append(sc_p)
            gd_s.append(sc_s)
            cv_p.append(cb_p)
            cv_s.append(cb_s)
            gl_p.append(sd_p)
            gl_s.append(sd_s)
            xp = xp + jnp.concatenate([oc_p, od_p], axis=-1) @ w_out_odd[i]
            xs = xs + jnp.concatenate([oc_s, od_s], axis=-1) @ w_out_odd[i]
        mk_p, mv_p = _mem_kv(mem_prompt, norm_mem_w[layer], x_w_k[layer], x_w_v[layer], x_k_norm[layer])
        mk_l.append(mk_p)
        mv_l.append(mv_p)
        xp = xp + _cross_attn(_rms_norm(xp, norm_cross_w[layer]), mk_p, mv_p, x_w_q[layer], x_q_norm[layer],
                              x_w_o[layer])
        xs = xs + _cross_attn(_rms_norm(xs, norm_cross_w[layer]), cache_mem_k[layer], cache_mem_v[layer],
                              x_w_q[layer], x_q_norm[layer], x_w_o[layer])
        hp = _rms_norm(xp, norm_ffn_w[layer])
        hs = _rms_norm(xs, norm_ffn_w[layer])
        if layer % 2 == 0:
            xp = xp + _swiglu(hp, ffd_w_gate[i], ffd_w_up[i], ffd_w_down[i])
            xs = xs + _swiglu(hs, ffd_w_gate[i], ffd_w_up[i], ffd_w_down[i])
        else:
            xp = xp + _moe(hp, moe_w_router[i], moe_w_gate[i], moe_w_up[i], moe_w_down[i])
            xs = xs + _moe(hs, moe_w_router[i], moe_w_gate[i], moe_w_up[i], moe_w_down[i])
    y_prompt = xp
    y_sample = xs
    new_diff_k_prompt = jnp.stack(dk_p)
    new_diff_v_prompt = jnp.stack(dv_p)
    new_diff_k_sample = jnp.stack(dk_s)
    new_diff_v_sample = jnp.stack(dv_s)
    new_rwkv_prompt = jnp.stack(rw_p)
    new_rwkv_sample = jnp.stack(rw_s)
    new_shift_prompt = jnp.stack(sh_p)
    new_shift_sample = jnp.stack(sh_s)
    new_mem_k_prompt = jnp.stack(mk_l)
    new_mem_v_prompt = jnp.stack(mv_l)
    new_gdn_prompt = jnp.stack(gd_p)
    new_gdn_sample = jnp.stack(gd_s)
    new_gdn_conv_prompt = jnp.stack(cv_p)
    new_gdn_conv_sample = jnp.stack(cv_s)
    new_gla_prompt = jnp.stack(gl_p)
    new_gla_sample = jnp.stack(gl_s)
    return (y_prompt, y_sample, new_diff_k_prompt, new_diff_v_prompt, new_diff_k_sample, new_diff_v_sample,
            new_rwkv_prompt, new_rwkv_sample, new_shift_prompt, new_shift_sample, new_mem_k_prompt,
            new_mem_v_prompt, new_gdn_prompt, new_gdn_sample, new_gdn_conv_prompt, new_gdn_conv_sample,
            new_gla_prompt, new_gla_sample)
```

```python
import functools
import math

import jax
import jax.numpy as jnp
from jax import lax
from jax.experimental import pallas as pl
from jax.experimental.pallas import tpu as pltpu

F32 = jnp.float32
BF16 = jnp.bfloat16
NEG_INF = float("-inf")

D_MODEL = 2048
A_HEADS = 8
A_DH = 64
A_Q = 1024
A_COLS = 3072
B_HEADS = 16
B_DH = 64
B_W = 1024
B_DECAY_LORA = 96
B_AAA_LORA = 96
B_GATE_LORA = 256
B_COLS = 3520
B_GN_EPS = 64e-5
C_QK_HEADS = 4
C_V_HEADS = 8
C_DH = 128
C_CONV = 4
C_QK = 512
C_V = 1024
C_CONV_CH = 2048
C_CHUNK = 64
D_HEADS = 4
D_DK = 128
D_DV = 256
D_K = 512
D_V = 1024
D_GATE_LORA = 16
D_GATE_NORM = 16.0
D_CHUNK = 16
N_MEM = 256
X_HEADS = 4
X_DH = 128
X_W = 512
FF_DENSE = 5632
N_EXPERTS = 8
FF_EXPERT = 2816
EPS = 1e-6

LANES = 128
VMEM_LIMIT = 56 * 1024 * 1024


def _cparams(sem):
    return pltpu.CompilerParams(dimension_semantics=sem, vmem_limit_bytes=VMEM_LIMIT)


def _sigmoid(x):
    return 1.0 / (1.0 + jnp.exp(-x))


def _silu(x):
    return x * _sigmoid(x)


def _softplus(x):
    return jnp.maximum(x, 0.0) + jnp.log(1.0 + jnp.exp(-jnp.abs(x)))


def _rms_rows(x, w):
    ms = jnp.mean(x * x, axis=-1, keepdims=True)
    return x * lax.rsqrt(ms + EPS) * w


def _group_sum(x, gsize):
    lane = lax.broadcasted_iota(jnp.int32, (1, LANES), 1)
    outs = []
    for c in range(x.shape[1] // LANES):
        xc = x[:, c * LANES:(c + 1) * LANES]
        if gsize == LANES:
            ss = jnp.broadcast_to(jnp.sum(xc, axis=-1, keepdims=True), xc.shape)
        else:
            lo = jnp.sum(jnp.where(lane < 64, xc, 0.0), axis=-1, keepdims=True)
            hi = jnp.sum(jnp.where(lane >= 64, xc, 0.0), axis=-1, keepdims=True)
            ss = jnp.where(lane < 64, lo, hi)
        outs.append(ss)
    return outs[0] if len(outs) == 1 else jnp.concatenate(outs, axis=1)


def _group_norm_scale(x, gsize, inv_n, eps):
    return x * lax.rsqrt(_group_sum(x * x, gsize) * inv_n + eps)


def _split3(a):
    a1 = a.astype(BF16)
    r1 = a - a1.astype(F32)
    a2 = r1.astype(BF16)
    a3 = (r1 - a2.astype(F32)).astype(BF16)
    return a1, a2, a3


def _dot_f32(a, b):
    a1, a2, a3 = _split3(a)
    b1, b2, b3 = _split3(b)
    d = lambda x, y: jnp.dot(x, y, preferred_element_type=F32)
    return d(a1, b3) + d(a3, b1) + d(a2, b2) + d(a1, b2) + d(a2, b1) + d(a1, b1)


def _norm_mm_kernel(x_ref, nw_ref, w_ref, hw_ref, o_ref, h_scr, *, n_norm_tiles, gsize):
    j = pl.program_id(1)

    @pl.when(j == 0)
    def _():
        h_scr[...] = _rms_rows(x_ref[...], nw_ref[...]).astype(BF16)

    acc = jnp.dot(h_scr[...], w_ref[...], preferred_element_type=F32)
    if n_norm_tiles == 0:
        o_ref[...] = acc
    else:
        @pl.when(j < n_norm_tiles)
        def _():
            o_ref[...] = _group_norm_scale(acc, gsize, 1.0 / gsize, EPS) * hw_ref[...]

        @pl.when(j >= n_norm_tiles)
        def _():
            o_ref[...] = acc


def _norm_matmul(x, nw, w, *, tm, tn, head_w=None, n_norm_cols=0, gsize=LANES):
    m, k = x.shape
    n = w.shape[1]
    assert m % tm == 0 and n % tn == 0 and n_norm_cols % tn == 0
    if head_w is None:
        head_w = jnp.ones((1, n), F32)
    kern = functools.partial(_norm_mm_kernel, n_norm_tiles=n_norm_cols // tn, gsize=gsize)
    return pl.pallas_call(
        kern,
        grid=(m // tm, n // tn),
        in_specs=[
            pl.BlockSpec((tm, k), lambda i, j: (i, 0)),
            pl.BlockSpec((1, k), lambda i, j: (0, 0)),
            pl.BlockSpec((k, tn), lambda i, j: (0, j)),
            pl.BlockSpec((1, tn), lambda i, j: (0, j)),
        ],
        out_specs=pl.BlockSpec((tm, tn), lambda i, j: (i, j)),
        out_shape=jax.ShapeDtypeStruct((m, n), F32),
        scratch_shapes=[pltpu.VMEM((tm, k), BF16)],
        compiler_params=_cparams(("parallel", "arbitrary")),
        name="norm_matmul",
    )(x, nw.reshape(1, k), w, head_w)


def _mm_res_kernel(*refs, n_in):
    x_ref = refs[2 * n_in]
    o_ref = refs[2 * n_in + 1]
    acc = x_ref[...]
    for t in range(n_in):
        acc = acc + jnp.dot(refs[t][...], refs[n_in + t][...], preferred_element_type=F32)
    o_ref[...] = acc


def _matmul_residual(acts, weights, x, *, tm, tn):
    m, n = x.shape
    n_in = len(acts)
    in_specs = [pl.BlockSpec((tm, a.shape[1]), lambda i, j: (i, 0)) for a in acts]
    in_specs += [pl.BlockSpec((w.shape[0], tn), lambda i, j: (0, j)) for w in weights]
    in_specs += [pl.BlockSpec((tm, tn), lambda i, j: (i, j))]
    return pl.pallas_call(
        functools.partial(_mm_res_kernel, n_in=n_in),
        grid=(m // tm, n // tn),
        in_specs=in_specs,
        out_specs=pl.BlockSpec((tm, tn), lambda i, j: (i, j)),
        out_shape=jax.ShapeDtypeStruct((m, n), F32),
        compiler_params=_cparams(("parallel", "arbitrary")),
        name="matmul_residual",
    )(*acts, *weights, x)


def _swiglu_kernel(x_ref, nw_ref, wg_ref, wu_ref, wd_ref, o_ref, h_scr):
    f = pl.program_id(1)

    @pl.when(f == 0)
    def _():
        x = x_ref[...]
        h_scr[...] = _rms_rows(x, nw_ref[...]).astype(BF16)
        o_ref[...] = x

    h = h_scr[...]
    g = jnp.dot(h, wg_ref[...], preferred_element_type=F32)
    u = jnp.dot(h, wu_ref[...], preferred_element_type=F32)
    a = (_silu(g) * u).astype(BF16)
    o_ref[...] += jnp.dot(a, wd_ref[...], preferred_element_type=F32)


def _swiglu_ffn(x, nw, wg, wu, wd, *, tm, tf):
    m, d = x.shape
    ff = wg.shape[1]
    return pl.pallas_call(
        _swiglu_kernel,
        grid=(m // tm, ff // tf),
        in_specs=[
            pl.BlockSpec((tm, d), lambda i, f: (i, 0)),
            pl.BlockSpec((1, d), lambda i, f: (0, 0)),
            pl.BlockSpec((d, tf), lambda i, f: (0, f)),
            pl.BlockSpec((d, tf), lambda i, f: (0, f)),
            pl.BlockSpec((tf, d), lambda i, f: (f, 0)),
        ],
        out_specs=pl.BlockSpec((tm, d), lambda i, f: (i, 0)),
        out_shape=jax.ShapeDtypeStruct((m, d), F32),
        scratch_shapes=[pltpu.VMEM((tm, d), BF16)],
        compiler_params=_cparams(("parallel", "arbitrary")),
        name="swiglu_ffn",
    )(x, nw.reshape(1, d), wg, wu, wd)


def _moe_kernel(x_ref, nw_ref, wr_ref, wg_ref, wu_ref, wd_ref, o_ref, h_scr, comb_scr):
    e = pl.program_id(1)
    f = pl.program_id(2)
    tm = x_ref.shape[0]
    lane = lax.broadcasted_iota(jnp.int32, (tm, LANES), 1)

    @pl.when((e == 0) & (f == 0))
    def _():
        x = x_ref[...]
        hf = _rms_rows(x, nw_ref[...])
        h_scr[...] = hf.astype(BF16)
        o_ref[...] = x
        logits = jnp.where(lane < N_EXPERTS, _dot_f32(hf, wr_ref[...]), NEG_INF)
        m1 = jnp.max(logits, axis=-1, keepdims=True)
        i1 = jnp.min(jnp.where(logits == m1, lane, LANES), axis=-1, keepdims=True)
        rest = jnp.where(lane == i1, NEG_INF, logits)
        m2 = jnp.max(rest, axis=-1, keepdims=True)
        i2 = jnp.min(jnp.where(rest == m2, lane, LANES), axis=-1, keepdims=True)
        e2 = jnp.exp(m2 - m1)
        g1 = 1.0 / (1.0 + e2)
        g2 = e2 / (1.0 + e2)
        comb_scr[...] = jnp.where(lane == i1, g1, 0.0) + jnp.where(lane == i2, g2, 0.0)

    c = jnp.sum(jnp.where(lane == e, comb_scr[...], 0.0), axis=-1, keepdims=True)
    h = h_scr[...]
    g = jnp.dot(h, wg_ref[...], preferred_element_type=F32)
    u = jnp.dot(h, wu_ref[...], preferred_element_type=F32)
    a = (_silu(g) * u * c).astype(BF16)
    o_ref[...] += jnp.dot(a, wd_ref[...], preferred_element_type=F32)


def _moe_ffn(x, nw, w_router, wg, wu, wd, *, tm, tf):
    m, d = x.shape
    ne, _, ff = wg.shape
    wr = jnp.zeros((d, LANES), F32).at[:, :ne].set(w_router)
    return pl.pallas_call(
        _moe_kernel,
        grid=(m // tm, ne, ff // tf),
        in_specs=[
            pl.BlockSpec((tm, d), lambda i, e, f: (i, 0)),
            pl.BlockSpec((1, d), lambda i, e, f: (0, 0)),
            pl.BlockSpec((d, LANES), lambda i, e, f: (0, 0)),
            pl.BlockSpec((None, d, tf), lambda i, e, f: (e, 0, f)),
            pl.BlockSpec((None, d, tf), lambda i, e, f: (e, 0, f)),
            pl.BlockSpec((None, tf, d), lambda i, e, f: (e, f, 0)),
        ],
        out_specs=pl.BlockSpec((tm, d), lambda i, e, f: (i, 0)),
        out_shape=jax.ShapeDtypeStruct((m, d), F32),
        scratch_shapes=[pltpu.VMEM((tm, d), BF16), pltpu.VMEM((tm, LANES), F32)],
        compiler_params=_cparams(("parallel", "arbitrary", "arbitrary")),
        name="moe_ffn",
    )(x, nw.reshape(1, d), wr, wg, wu, wd)


def _diff_lambda_vec(lam_ref, lam_init):
    lv = lam_ref[...]
    d1 = jnp.sum(lv[0:1, :] * lv[1:2, :], axis=-1, keepdims=True)
    d2 = jnp.sum(lv[2:3, :] * lv[3:4, :], axis=-1, keepdims=True)
    return jnp.exp(d1) - jnp.exp(d2) + lam_init


def _softmax_rows(s):
    m = jnp.max(s, axis=-1, keepdims=True)
    e = jnp.exp(s - m)
    return e, jnp.sum(e, axis=-1, keepdims=True)


def _diffattn_prompt_kernel(slope_ref, q_ref, k_ref, v_ref, lam_ref, subln_ref, o_ref, *, tq, lam_init):
    t = q_ref.shape[0]
    h = pl.program_id(1)
    slope = slope_ref[h]
    lane = lax.broadcasted_iota(jnp.int32, (1, LANES), 1)
    lam = _diff_lambda_vec(lam_ref, lam_init)
    kb = k_ref[...].astype(BF16)
    vb = v_ref[...].astype(BF16)
    nt = (((1,), (1,)), ((), ()))
    for qi in range(t // tq):
        n_keys = (qi + 1) * tq
        q = q_ref[qi * tq:(qi + 1) * tq, :] * (A_DH ** -0.5)
        q1 = jnp.where(lane < A_DH, q, 0.0).astype(BF16)
        q2 = jnp.where(lane >= A_DH, q, 0.0).astype(BF16)
        kk = kb[:n_keys]
        row = lax.broadcasted_iota(jnp.int32, (tq, n_keys), 0) + qi * tq
        col = lax.broadcasted_iota(jnp.int32, (tq, n_keys), 1)
        dist = (row - col).astype(F32)
        bias = jnp.where(dist >= 0.0, -slope * dist, NEG_INF)
        e1, l1 = _softmax_rows(lax.dot_general(q1, kk, nt, preferred_element_type=F32) + bias)
        e2, l2 = _softmax_rows(lax.dot_general(q2, kk, nt, preferred_element_type=F32) + bias)
        p = e1 * (1.0 / l1) - e2 * (lam / l2)
        o = jnp.dot(p.astype(BF16), vb[:n_keys], preferred_element_type=F32)
        o = _rms_rows(o, subln_ref[...]) * (1.0 - lam_init)
        o_ref[qi * tq:(qi + 1) * tq, :] = o.astype(BF16)


def _diff_attention_prompt(pa, n_batch, t, slopes, lam_stack, subln, lam_init, *, tq):
    kern = functools.partial(_diffattn_prompt_kernel, tq=tq, lam_init=lam_init)
    return pl.pallas_call(
        kern,
        grid=(n_batch, A_HEADS),
        in_specs=[
            pl.BlockSpec(memory_space=pltpu.SMEM),
            pl.BlockSpec((t, LANES), lambda b, h: (b, h)),
            pl.BlockSpec((t, LANES), lambda b, h: (b, A_HEADS + h)),
            pl.BlockSpec((t, LANES), lambda b, h: (b, 2 * A_HEADS + h)),
            pl.BlockSpec((4, A_DH), lambda b, h: (0, 0)),
            pl.BlockSpec((1, LANES), lambda b, h: (0, 0)),
        ],
        out_specs=pl.BlockSpec((t, LANES), lambda b, h: (b, h)),
        out_shape=jax.ShapeDtypeStruct((n_batch * t, A_HEADS * LANES), BF16),
        compiler_params=_cparams(("parallel", "parallel")),
        name="diff_attention_prompt",
    )(slopes, pa, pa, pa, lam_stack, subln.reshape(1, LANES))


def _diffattn_sample_kernel(pt_ref, q_ref, kn_ref, vn_ref, kc_ref, vc_ref, lam_ref, subln_ref, o_ref,
                            qq_scr, m_scr, l_scr, acc_scr, *, past, lam_init):
    del pt_ref
    p = pl.program_id(1)
    n_pages = pl.num_programs(1)
    ts = q_ref.shape[0]
    page = kc_ref.shape[0]
    lane = lax.broadcasted_iota(jnp.int32, (1, LANES), 1)
    nt = (((1,), (1,)), ((), ()))

    @pl.when(p == 0)
    def _():
        for h in range(A_HEADS):
            q = q_ref[:, h * LANES:(h + 1) * LANES] * (A_DH ** -0.5)
            q1 = jnp.where(lane < A_DH, q, 0.0)
            q2 = jnp.where(lane >= A_DH, q, 0.0)
            qq_scr[h] = jnp.concatenate([q1, q2], axis=0).astype(BF16)
        m_scr[...] = jnp.full(m_scr.shape, NEG_INF, F32)
        l_scr[...] = jnp.zeros(l_scr.shape, F32)
        acc_scr[...] = jnp.zeros(acc_scr.shape, F32)

    row = lax.broadcasted_iota(jnp.int32, (2 * ts, page), 0)
    col = lax.broadcasted_iota(jnp.int32, (2 * ts, page), 1)
    tq_idx = jnp.where(row >= ts, row - ts, row)

    def update(h, s, vh):
        m_old = m_scr[h]
        m_new = jnp.maximum(m_old, jnp.max(s, axis=-1, keepdims=True))
        alpha = jnp.exp(m_old - m_new)
        pe = jnp.exp(s - m_new)
        l_scr[h] = alpha * l_scr[h] + jnp.sum(pe, axis=-1, keepdims=True)
        acc_scr[h] = alpha * acc_scr[h] + jnp.dot(pe.astype(BF16), vh, preferred_element_type=F32)
        m_scr[h] = m_new

    dist_past = (past + tq_idx - (p * page + col)).astype(F32)
    for h in range(A_HEADS):
        slope = 2.0 ** (-(8.0 / A_HEADS) * (h + 1))
        kh = kc_ref[:, h * LANES:(h + 1) * LANES].astype(BF16)
        vh = vc_ref[:, h * LANES:(h + 1) * LANES].astype(BF16)
        s = lax.dot_general(qq_scr[h], kh, nt, preferred_element_type=F32) - slope * dist_past
        update(h, s, vh)

    @pl.when(p == n_pages - 1)
    def _():
        lam = _diff_lambda_vec(lam_ref, lam_init)
        d_new = (tq_idx - col).astype(F32)
        pad = jnp.zeros((page - ts, LANES), F32)
        for h in range(A_HEADS):
            slope = 2.0 ** (-(8.0 / A_HEADS) * (h + 1))
            kh = jnp.concatenate([kn_ref[:, h * LANES:(h + 1) * LANES], pad], axis=0).astype(BF16)
            vh = jnp.concatenate([vn_ref[:, h * LANES:(h + 1) * LANES], pad], axis=0).astype(BF16)
            s = lax.dot_general(qq_scr[h], kh, nt, preferred_element_type=F32)
            s = jnp.where((d_new >= 0.0) & (col < ts), s - slope * d_new, NEG_INF)
            update(h, s, vh)
            on = acc_scr[h] / l_scr[h]
            o = on[:ts] - lam * on[ts:]
            o = _rms_rows(o, subln_ref[...]) * (1.0 - lam_init)
            o_ref[:, h * LANES:(h + 1) * LANES] = o.astype(BF16)


def _diff_attention_sample(pa, row0, n_batch, ts, cache_k, cache_v, page_table, lam_stack, subln, lam_init):
    n_pages = page_table.shape[1]
    page = cache_k.shape[1]
    assert row0 % ts == 0 and page >= ts
    rb = row0 // ts
    w = A_HEADS * LANES
    kern = functools.partial(_diffattn_sample_kernel, past=n_pages * page, lam_init=lam_init)
    grid_spec = pltpu.PrefetchScalarGridSpec(
        num_scalar_prefetch=1,
        grid=(n_batch, n_pages),
        in_specs=[
            pl.BlockSpec((ts, w), lambda b, p, pt: (rb + b, 0)),
            pl.BlockSpec((ts, w), lambda b, p, pt: (rb + b, 1)),
            pl.BlockSpec((ts, w), lambda b, p, pt: (rb + b, 2)),
            pl.BlockSpec((None, page, w), lambda b, p, pt: (pt[b, p], 0, 0)),
            pl.BlockSpec((None, page, w), lambda b, p, pt: (pt[b, p], 0, 0)),
            pl.BlockSpec((4, A_DH), lambda b, p, pt: (0, 0)),
            pl.BlockSpec((1, LANES), lambda b, p, pt: (0, 0)),
        ],
        out_specs=pl.BlockSpec((ts, w), lambda b, p, pt: (b, 0)),
        scratch_shapes=[
            pltpu.VMEM((A_HEADS, 2 * ts, LANES), BF16),
            pltpu.VMEM((A_HEADS, 2 * ts, LANES), F32),
            pltpu.VMEM((A_HEADS, 2 * ts, LANES), F32),
            pltpu.VMEM((A_HEADS, 2 * ts, LANES), F32),
        ],
    )
    return pl.pallas_call(
        kern,
        grid_spec=grid_spec,
        out_shape=jax.ShapeDtypeStruct((n_batch * ts, w), BF16),
        compiler_params=_cparams(("parallel", "arbitrary")),
        name="diff_attention_sample",
    )(page_table, pa, pa, pa, cache_k, cache_v, lam_stack, subln.reshape(1, LANES))


def _cross_attn_kernel(q_ref, k_ref, v_ref, qn_ref, o_ref):
    nt = (((1,), (1,)), ((), ()))
    for h in range(X_HEADS):
        sl = slice(h * LANES, (h + 1) * LANES)
        q = _rms_rows(q_ref[:, sl], qn_ref[...]) * (X_DH ** -0.5)
        s = lax.dot_general(q.astype(BF16), k_ref[:, sl].astype(BF16), nt, preferred_element_type=F32)
        e, l = _softmax_rows(s)
        p = (e * (1.0 / l)).astype(BF16)
        o_ref[:, sl] = jnp.dot(p, v_ref[:, sl].astype(BF16), preferred_element_type=F32).astype(BF16)


def _cross_attention(q, row0, n_batch, t, mk, mv, q_norm, *, tq):
    assert t % tq == 0 and row0 % tq == 0
    nq = t // tq
    rb = row0 // tq
    return pl.pallas_call(
        _cross_attn_kernel,
        grid=(n_batch, nq),
        in_specs=[
            pl.BlockSpec((tq, X_W), lambda b, i: (rb + b * nq + i, 0)),
            pl.BlockSpec((None, N_MEM, X_W), lambda b, i: (b, 0, 0)),
            pl.BlockSpec((None, N_MEM, X_W), lambda b, i: (b, 0, 0)),
            pl.BlockSpec((1, LANES), lambda b, i: (0, 0)),
        ],
        out_specs=pl.BlockSpec((tq, X_W), lambda b, i: (b * nq + i, 0)),
        out_shape=jax.ShapeDtypeStruct((n_batch * t, X_W), BF16),
        compiler_params=_cparams(("parallel", "parallel")),
        name="cross_attention",
    )(q, mk, mv, q_norm.reshape(1, LANES))


PB_LG = 3 * B_W
PB_LW = PB_LG + B_GATE_LORA
PB_LA = PB_LW + LANES
PB_COLS = PB_LA + LANES


def _rwkv_prep_kernel(pb_ref, ps_ref, mu_ref, w0_ref, w2_ref, a0_ref, a2_ref, g2_ref, kk_ref, ka_ref,
                      r_ref, w_ref, kf_ref, v_ref, nkk_ref, b_ref, g_ref):
    pb = pb_ref[...]
    xs = pb + (ps_ref[...] - pb) * mu_ref[...]
    k = xs[:, B_W:2 * B_W]
    lg = xs[:, PB_LG:PB_LW]
    lw = xs[:, PB_LW:PB_LA]
    la = xs[:, PB_LA:PB_COLS]
    wl = w0_ref[...] + jnp.dot(jnp.tanh(lw).astype(BF16), w2_ref[...], preferred_element_type=F32)
    logw = -_softplus(-wl) - 0.5
    a = _sigmoid(a0_ref[...] + jnp.dot(la.astype(BF16), a2_ref[...], preferred_element_type=F32))
    kk = _group_norm_scale(k * kk_ref[...], B_DH, 1.0, EPS)
    r_ref[...] = xs[:, :B_W]
    w_ref[...] = jnp.exp(-jnp.exp(logw))
    kf_ref[...] = k * (1.0 + (a - 1.0) * ka_ref[...])
    v_ref[...] = xs[:, 2 * B_W:3 * B_W]
    nkk_ref[...] = -kk
    b_ref[...] = kk * a
    g_ref[...] = jnp.dot(_sigmoid(lg).astype(BF16), g2_ref[...], preferred_element_type=F32)


def _rwkv_prep(pb, ps, mu, w0, w2, a0, a2, g2, k_k, k_a, *, tm):
    m = pb.shape[0]
    row = lambda n: pl.BlockSpec((1, n), lambda i: (0, 0))
    full = lambda a: pl.BlockSpec(a.shape, lambda i: (0, 0))
    tok = lambda n: pl.BlockSpec((tm, n), lambda i: (i, 0))
    out = jax.ShapeDtypeStruct((m, B_W), F32)
    return pl.pallas_call(
        _rwkv_prep_kernel,
        grid=(m // tm,),
        in_specs=[tok(PB_COLS), tok(PB_COLS), row(PB_COLS), row(B_W), full(w2), row(B_W), full(a2), full(g2),
                  row(B_W), row(B_W)],
        out_specs=[tok(B_W)] * 7,
        out_shape=[out] * 7,
        compiler_params=_cparams(("parallel",)),
        name="rwkv_prep",
    )(pb, ps, mu, w0.reshape(1, B_W), w2, a0.reshape(1, B_W), a2, g2, k_k.reshape(1, B_W), k_a.reshape(1, B_W))


RWKV_GROUP = 8


def _rwkv_scan_kernel(r_ref, w_ref, kf_ref, v_ref, nkk_ref, b_ref, s0_ref, y_ref, so_ref, s_scr):
    tblk = pl.program_id(1)
    tb = r_ref.shape[0]
    n_pairs = B_HEADS // 2

    @pl.when(tblk == 0)
    def _():
        s_scr[...] = s0_ref[...]

    lane = lax.broadcasted_iota(jnp.int32, (B_DH, LANES), 1)
    sub = lax.broadcasted_iota(jnp.int32, (B_DH, LANES), 0)
    diag = jnp.bitwise_and(lane, B_DH - 1) == sub
    lo_half = lane < B_DH
    gi = lax.broadcasted_iota(jnp.int32, (LANES, LANES), 0) // B_DH
    gj = lax.broadcasted_iota(jnp.int32, (LANES, LANES), 1) // B_DH
    gmat = jnp.where(gi == gj, 1.0, 0.0).astype(BF16)
    zpad = jnp.zeros((LANES - RWKV_GROUP, LANES), F32)

    def group(g, carry):
        t0 = pl.multiple_of(g * RWKV_GROUP, RWKV_GROUP)
        rows = pl.ds(t0, RWKV_GROUP)
        for p in range(n_pairs):
            sl = slice(p * LANES, (p + 1) * LANES)
            r8, w8, kf8, nkk8, b8 = (ref[rows, sl] for ref in (r_ref, w_ref, kf_ref, nkk_ref, b_ref))
            vt = jnp.concatenate([v_ref[rows, sl], zpad], axis=0).T
            s = s_scr[p]
            ys = []
            for u in range(RWKV_GROUP):
                sa = jnp.dot((s * nkk8[u:u + 1]).astype(BF16), gmat, preferred_element_type=F32)
                vfull = jnp.broadcast_to(vt[:, u:u + 1], (LANES, LANES))
                vcol = jnp.where(lo_half, vfull[:B_DH], vfull[B_DH:])
                s = s * w8[u:u + 1] + sa * b8[u:u + 1] + vcol * kf8[u:u + 1]
                yt = jnp.dot((s * r8[u:u + 1]).astype(BF16), gmat, preferred_element_type=F32)
                ys.append(jnp.sum(jnp.where(diag, yt, 0.0), axis=0, keepdims=True))
            s_scr[p] = s
            y_ref[rows, sl] = jnp.concatenate(ys, axis=0)
        return carry

    lax.fori_loop(0, tb // RWKV_GROUP, group, 0)

    @pl.when(tblk == pl.num_programs(1) - 1)
    def _():
        so_ref[...] = s_scr[...]


def _rwkv_scan(r, w, kf, v, nkk, b, s0, row0, n_seq, t, *, tb):
    assert t % tb == 0 and row0 % tb == 0 and tb % RWKV_GROUP == 0
    nb = t // tb
    rb = row0 // tb
    tok = pl.BlockSpec((tb, B_W), lambda s, i: (rb + s * nb + i, 0))
    st = pl.BlockSpec((None, B_HEADS // 2, B_DH, LANES), lambda s, i: (s, 0, 0, 0))
    return pl.pallas_call(
        _rwkv_scan_kernel,
        grid=(n_seq, nb),
        in_specs=[tok] * 6 + [st],
        out_specs=[pl.BlockSpec((tb, B_W), lambda s, i: (s * nb + i, 0)), st],
        out_shape=[jax.ShapeDtypeStruct((n_seq * t, B_W), F32),
                   jax.ShapeDtypeStruct((n_seq, B_HEADS // 2, B_DH, LANES), F32)],
        scratch_shapes=[pltpu.VMEM((B_HEADS // 2, B_DH, LANES), F32)],
        compiler_params=_cparams(("parallel", "arbitrary")),
        name="rwkv_scan",
    )(r, w, kf, v, nkk, b, s0)


def _rwkv_post_kernel(y_ref, r_ref, kf_ref, v_ref, g_ref, lw_ref, lb_ref, rk_ref, o_ref):
    y = y_ref[...]
    mean = _group_sum(y, B_DH) * (1.0 / B_DH)
    yc = y - mean
    var = _group_sum(yc * yc, B_DH) * (1.0 / B_DH)
    yn = yc * lax.rsqrt(var + B_GN_EPS) * lw_ref[...] + lb_ref[...]
    bonus = _group_sum(r_ref[...] * kf_ref[...] * rk_ref[...], B_DH)
    o_ref[...] = ((yn + bonus * v_ref[...]) * g_ref[...]).astype(BF16)


def _rwkv_post(y, r, kf, v, g, lnx_w, lnx_b, r_k, *, tm):
    m = y.shape[0]
    tok = pl.BlockSpec((tm, B_W), lambda i: (i, 0))
    row = pl.BlockSpec((1, B_W), lambda i: (0, 0))
    return pl.pallas_call(
        _rwkv_post_kernel,
        grid=(m // tm,),
        in_specs=[tok] * 5 + [row] * 3,
        out_specs=tok,
        out_shape=jax.ShapeDtypeStruct((m, B_W), BF16),
        compiler_params=_cparams(("parallel",)),
        name="rwkv_post",
    )(y, r, kf, v, g, lnx_w.reshape(1, B_W), lnx_b.reshape(1, B_W), r_k.reshape(1, B_W))


def _rwkv_cols(a):
    lw0 = 3 * B_W
    la0 = lw0 + B_DECAY_LORA
    lg0 = la0 + B_AAA_LORA
    z = jnp.zeros(a.shape[:-1] + (LANES - B_DECAY_LORA,), a.dtype)
    return jnp.concatenate([a[..., :lw0], a[..., lg0:], a[..., lw0:la0], z, a[..., la0:lg0], z], axis=-1)


def _rwkv_cols_inv(a):
    return jnp.concatenate([a[..., :PB_LG], a[..., PB_LW:PB_LW + B_DECAY_LORA],
                            a[..., PB_LA:PB_LA + B_AAA_LORA], a[..., PB_LG:PB_LW]], axis=-1)


def _pad_rows(a, n):
    return jnp.pad(a, ((0, n - a.shape[0]), (0, 0)))


def _rwkv_state_to_pairs(s):
    n = s.shape[0]
    s = s.reshape(n, B_HEADS // 2, 2, B_DH, B_DH)
    return jnp.transpose(s, (0, 1, 3, 2, 4)).reshape(n, B_HEADS // 2, B_DH, LANES)


def _rwkv_state_from_pairs(s):
    n = s.shape[0]
    s = s.reshape(n, B_HEADS // 2, B_DH, 2, B_DH)
    return jnp.transpose(s, (0, 1, 3, 2, 4)).reshape(n, B_HEADS, B_DH, B_DH)


def _rwkv_mixer(pb, n_p, t_p, n_s, t_s, prev_s, state_s, mu, w0, w2, a0, a2, g2, k_k, k_a, r_k, lnx_w, lnx_b,
                *, tm, tb):
    mp = n_p * t_p
    pbp = pb[:mp].reshape(n_p, t_p, PB_COLS)
    pbs = pb[mp:].reshape(n_s, t_s, PB_COLS)
    sh_p = jnp.concatenate([jnp.zeros((n_p, 1, PB_COLS), F32), pbp[:, :-1]], axis=1)
    sh_s = jnp.concatenate([_rwkv_cols(prev_s), pbs[:, :-1]], axis=1)
    ps = jnp.concatenate([sh_p.reshape(mp, PB_COLS), sh_s.reshape(n_s * t_s, PB_COLS)], axis=0)
    r, w, kf, v, nkk, b, g = _rwkv_prep(
        pb, ps, _rwkv_cols(mu.reshape(1, B_COLS)), w0, _pad_rows(w2, LANES).astype(BF16), a0,
        _pad_rows(a2, LANES).astype(BF16), g2.astype(BF16), k_k, k_a, tm=tm)
    s0_p = jnp.zeros((n_p, B_HEADS // 2, B_DH, LANES), F32)
    y_p, st_p = _rwkv_scan(r, w, kf, v, nkk, b, s0_p, 0, n_p, t_p, tb=tb)
    y_s, st_s = _rwkv_scan(r, w, kf, v, nkk, b, _rwkv_state_to_pairs(state_s), mp, n_s, t_s, tb=t_s)
    y = jnp.concatenate([y_p, y_s], axis=0)
    ob = _rwkv_post(y, r, kf, v, g, lnx_w, lnx_b, r_k.reshape(B_W), tm=tm)
    return ob, _rwkv_state_from_pairs(st_p), _rwkv_state_from_pairs(st_s)


PC_Z = C_CONV_CH
PC_BETA = PC_Z + C_V
PC_COLS = PC_BETA + 2 * LANES
HALO = 8


def _chunk_cumsum(g, chunk):
    rowc = jnp.bitwise_and(lax.broadcasted_iota(jnp.int32, g.shape, 0), chunk - 1)
    k = 1
    while k < chunk:
        g = g + jnp.where(rowc >= k, pltpu.roll(g, k, 0), 0.0)
        k *= 2
    return g


def _transpose_rows(a):
    c = a.shape[0]
    if c < LANES:
        a = jnp.concatenate([a, jnp.zeros((LANES - c, LANES), F32)], axis=0)
    return a.T


def _pad_chunk_rows(a):
    c = a.shape[0]
    return a if c == LANES else jnp.concatenate([a, jnp.zeros((LANES - c, a.shape[1]), a.dtype)], axis=0)


def _gdn_prep_kernel(x_ref, halo_ref, st_ref, cw_ref, sm_ref, alog_ref, dtb_ref,
                     q_ref, k_ref, v_ref, beta_ref, gc_ref, *, chunk):
    i = pl.program_id(1)
    x = x_ref[...]
    tm = x.shape[0]
    halo = jnp.where(i == 0, st_ref[...], halo_ref[...])
    row8 = lax.broadcasted_iota(jnp.int32, (HALO, x.shape[1]), 0)
    acc = x * cw_ref[C_CONV - 1:C_CONV, :]
    for s in range(1, C_CONV):
        xs = pltpu.roll(x, s, 0)
        first = jnp.where(row8 < s, pltpu.roll(halo, s, 0), xs[:HALO])
        xs = first if tm == HALO else jnp.concatenate([first, xs[HALO:]], axis=0)
        acc = acc + xs * cw_ref[C_CONV - 1 - s:C_CONV - s, :]
    y = _silu(acc)
    q_ref[...] = _group_norm_scale(y[:, :C_QK], C_DH, 1.0, EPS) * (C_DH ** -0.5)
    k_ref[...] = _group_norm_scale(y[:, C_QK:2 * C_QK], C_DH, 1.0, EPS)
    v_ref[...] = y[:, 2 * C_QK:]
    sm = sm_ref[...]
    beta_ref[...] = _sigmoid(sm[:, :LANES])
    g = -jnp.exp(alog_ref[...]) * _softplus(sm[:, LANES:] + dtb_ref[...])
    gc_ref[...] = _chunk_cumsum(g, chunk)


def _gdn_prep(pc, conv_state, row0, n_seq, t, conv_w, a_log, dt_bias, *, tm, chunk):
    assert t % tm == 0 and row0 % tm == 0 and tm % HALO == 0 and tm % chunk == 0 and chunk & (chunk - 1) == 0
    nb = t // tm
    rb = row0 // tm
    hb = tm // HALO
    tok_in = lambda n, cb: pl.BlockSpec((tm, n), lambda s, i: (rb + s * nb + i, cb))
    tok_out = lambda n: pl.BlockSpec((tm, n), lambda s, i: (s * nb + i, 0))
    row = lambda n: pl.BlockSpec((1, n), lambda s, i: (0, 0))
    sds = lambda n: jax.ShapeDtypeStruct((n_seq * t, n), F32)
    lanes8 = lambda a: jnp.zeros((1, LANES), F32).at[0, :C_V_HEADS].set(a)
    return pl.pallas_call(
        functools.partial(_gdn_prep_kernel, chunk=chunk),
        grid=(n_seq, nb),
        in_specs=[
            tok_in(C_CONV_CH, 0),
            pl.BlockSpec((HALO, C_CONV_CH), lambda s, i: (jnp.maximum((rb + s * nb + i) * hb - 1, 0), 0)),
            pl.BlockSpec((None, HALO, C_CONV_CH), lambda s, i: (s, 0, 0)),
            pl.BlockSpec((C_CONV, C_CONV_CH), lambda s, i: (0, 0)),
            tok_in(2 * LANES, PC_BETA // (2 * LANES)),
            row(LANES), row(LANES),
        ],
        out_specs=[tok_out(C_QK), tok_out(C_QK), tok_out(C_V), tok_out(LANES), tok_out(LANES)],
        out_shape=[sds(C_QK), sds(C_QK), sds(C_V), sds(LANES), sds(LANES)],
        compiler_params=_cparams(("parallel", "arbitrary")),
        name="gdn_prep",
    )(pc, pc, conv_state, conv_w, pc, lanes8(a_log), lanes8(dt_bias))


def _gdn_chunk_kernel(q_ref, k_ref, v_ref, beta_ref, gc_ref, z_ref, on_ref, s0_ref, o_ref, so_ref, s_scr, *, chunk):
    h = pl.program_id(1)
    i = pl.program_id(2)
    tb = q_ref.shape[0]

    @pl.when(i == 0)
    def _():
        s_scr[...] = s0_ref[...]

    lane = lax.broadcasted_iota(jnp.int32, (chunk, LANES), 1)
    ri = lax.broadcasted_iota(jnp.int32, (chunk, chunk), 0)
    ci = lax.broadcasted_iota(jnp.int32, (chunk, chunk), 1)
    nt = (((1,), (1,)), ((), ()))
    ones = jnp.ones((chunk, LANES), BF16)
    for c in range(tb // chunk):
        rs = slice(c * chunk, (c + 1) * chunk)
        q, k, v = q_ref[rs, :], k_ref[rs, :], v_ref[rs, :]
        beta = jnp.sum(jnp.where(lane == h, beta_ref[rs, :], 0.0), axis=-1, keepdims=True)
        gcol = jnp.sum(jnp.where(lane == h, gc_ref[rs, :], 0.0), axis=-1, keepdims=True)
        g0 = jnp.where(lane == 0, gcol, 0.0)
        grow = sum(lax.dot_general(ones, part, nt, preferred_element_type=F32) for part in _split3(g0))
        dec = jnp.exp(jnp.where(ri >= ci, gcol - grow, NEG_INF))
        kb = k * beta
        kbf = k.astype(BF16)
        lmat = jnp.where(ri > ci, lax.dot_general(kb.astype(BF16), kbf, nt, preferred_element_type=F32) * dec, 0.0)
        x = jnp.concatenate([v * beta, kb * jnp.exp(gcol)], axis=1)
        for j in range(chunk - 1):
            x = x - lmat[:, j:j + 1] * x[j:j + 1, :]
        s = s_scr[...]
        sb = s.astype(BF16)
        v_new = x[:, :C_DH] - jnp.dot(x[:, C_DH:].astype(BF16), sb, preferred_element_type=F32)
        attn = lax.dot_general(q.astype(BF16), kbf, nt, preferred_element_type=F32) * dec
        o = jnp.dot((q * jnp.exp(gcol)).astype(BF16), sb, preferred_element_type=F32)
        o = o + jnp.dot(attn.astype(BF16), v_new.astype(BF16), preferred_element_type=F32)
        glast = gcol[chunk - 1:chunk, :]
        kdt = _transpose_rows(k * jnp.exp(glast - gcol)).astype(BF16)
        s_scr[...] = s * jnp.exp(glast) + jnp.dot(kdt, _pad_chunk_rows(v_new).astype(BF16),
                                                  preferred_element_type=F32)
        o = _rms_rows(o, on_ref[...]) * _silu(z_ref[rs, :])
        o_ref[rs, :] = o.astype(BF16)

    @pl.when(i == pl.num_programs(2) - 1)
    def _():
        so_ref[...] = s_scr[...]


def _gdn_chunks(q, k, v, beta, gc, pc, row0, n_seq, t, onorm, s0, *, tb, chunk):
    assert t % tb == 0 and row0 % tb == 0 and tb % chunk == 0
    nb = t // tb
    rb = row0 // tb
    rep = C_V_HEADS // C_QK_HEADS
    tok = lambda cb: pl.BlockSpec((tb, LANES), cb)
    st = pl.BlockSpec((None, None, C_DH, C_DH), lambda s, h, i: (s, h, 0, 0))
    return pl.pallas_call(
        functools.partial(_gdn_chunk_kernel, chunk=chunk),
        grid=(n_seq, C_V_HEADS, nb),
        in_specs=[
            tok(lambda s, h, i: (s * nb + i, h // rep)),
            tok(lambda s, h, i: (s * nb + i, h // rep)),
            tok(lambda s, h, i: (s * nb + i, h)),
            tok(lambda s, h, i: (s * nb + i, 0)),
            tok(lambda s, h, i: (s * nb + i, 0)),
            tok(lambda s, h, i: (rb + s * nb + i, PC_Z // LANES + h)),
            pl.BlockSpec((1, LANES), lambda s, h, i: (0, 0)),
            st,
        ],
        out_specs=[tok(lambda s, h, i: (s * nb + i, h)), st],
        out_shape=[jax.ShapeDtypeStruct((n_seq * t, C_V), BF16),
                   jax.ShapeDtypeStruct((n_seq, C_V_HEADS, C_DH, C_DH), F32)],
        scratch_shapes=[pltpu.VMEM((C_DH, C_DH), F32)],
        compiler_params=_cparams(("parallel", "parallel", "arbitrary")),
        name="gdn_chunks",
    )(q, k, v, beta, gc, pc, onorm.reshape(1, LANES), s0)


def _gdn_group(pc, conv_state3, s0, row0, n_seq, t, conv_w, a_log, dt_bias, onorm, *, tm, tb):
    chunk = C_CHUNK if t % C_CHUNK == 0 else t
    conv_state = jnp.pad(conv_state3, ((0, 0), (HALO - (C_CONV - 1), 0), (0, 0)))
    q, k, v, beta, gc = _gdn_prep(pc, conv_state, row0, n_seq, t, conv_w, a_log, dt_bias, tm=tm, chunk=chunk)
    return _gdn_chunks(q, k, v, beta, gc, pc, row0, n_seq, t, onorm, s0, tb=tb, chunk=chunk)


PD_V = 2 * D_K
PD_GATE = PD_V + D_V
PD_GLR = PD_GATE + D_V
PD_COLS = PD_GLR + LANES


def _gla_kernel(q_ref, k_ref, v_ref, gate_ref, glr_ref, w2_ref, bgk_ref, on_ref, s0_ref, o_ref, so_ref, s_scr,
                *, chunk):
    i = pl.program_id(2)
    tb = q_ref.shape[0]

    @pl.when(i == 0)
    def _():
        s_scr[...] = s0_ref[...]

    pre = jnp.dot(glr_ref[...].astype(BF16), w2_ref[...], preferred_element_type=F32) + bgk_ref[...]
    bcum = _chunk_cumsum(-_softplus(-pre) * (1.0 / D_GATE_NORM), chunk)
    ri = lax.broadcasted_iota(jnp.int32, (chunk, 1), 0)
    ci = lax.broadcasted_iota(jnp.int32, (chunk, chunk), 1)
    for c in range(tb // chunk):
        rs = slice(c * chunk, (c + 1) * chunk)
        q = q_ref[rs, :] * (D_DK ** -0.5)
        k, v, bc = k_ref[rs, :], v_ref[rs, :], bcum[rs, :]
        attn = jnp.zeros((chunk, chunk), F32)
        for j in range(chunk):
            e = jnp.exp(jnp.where(ri >= j, bc - bc[j:j + 1, :], NEG_INF))
            col = jnp.sum(q * k[j:j + 1, :] * e, axis=-1, keepdims=True)
            attn = jnp.where(ci == j, col, attn)
        s = s_scr[...]
        o = jnp.dot((q * jnp.exp(bc)).astype(BF16), s.astype(BF16), preferred_element_type=F32)
        o = o + jnp.dot(attn.astype(BF16), v.astype(BF16), preferred_element_type=F32)
        blast = bc[chunk - 1:chunk, :]
        ebt = jnp.broadcast_to(jnp.exp(blast), (LANES, LANES)).T
        kdt = _transpose_rows(k * jnp.exp(blast - bc)).astype(BF16)
        s_scr[...] = s * jnp.concatenate([ebt, ebt], axis=1) + jnp.dot(
            kdt, _pad_chunk_rows(v).astype(BF16), preferred_element_type=F32)
        o = _rms_rows(o, on_ref[...]) * _silu(gate_ref[rs, :])
        o_ref[rs, :] = o.astype(BF16)

    @pl.when(i == pl.num_programs(2) - 1)
    def _():
        so_ref[...] = s_scr[...]


def _gla_group(pd, s0, row0, n_seq, t, w_gk2, b_gk, onorm, *, tb):
    chunk = D_CHUNK if t % D_CHUNK == 0 else t
    assert t % tb == 0 and row0 % tb == 0 and tb % chunk == 0 and chunk & (chunk - 1) == 0
    nb = t // tb
    rb = row0 // tb
    w2 = _pad_rows(w_gk2, LANES).astype(BF16)
    tok = lambda n, cb: pl.BlockSpec((tb, n), cb)
    st = pl.BlockSpec((None, None, D_DK, D_DV), lambda s, h, i: (s, h, 0, 0))
    return pl.pallas_call(
        functools.partial(_gla_kernel, chunk=chunk),
        grid=(n_seq, D_HEADS, nb),
        in_specs=[
            tok(D_DK, lambda s, h, i: (rb + s * nb + i, h)),
            tok(D_DK, lambda s, h, i: (rb + s * nb + i, D_K // D_DK + h)),
            tok(D_DV, lambda s, h, i: (rb + s * nb + i, PD_V // D_DV + h)),
            tok(D_DV, lambda s, h, i: (rb + s * nb + i, PD_GATE // D_DV + h)),
            tok(LANES, lambda s, h, i: (rb + s * nb + i, PD_GLR // LANES)),
            pl.BlockSpec((LANES, D_DK), lambda s, h, i: (0, h)),
            pl.BlockSpec((1, D_DK), lambda s, h, i: (0, h)),
            pl.BlockSpec((1, D_DV), lambda s, h, i: (0, 0)),
            st,
        ],
        out_specs=[tok(D_DV, lambda s, h, i: (s * nb + i, h)), st],
        out_shape=[jax.ShapeDtypeStruct((n_seq * t, D_V), BF16),
                   jax.ShapeDtypeStruct((n_seq, D_HEADS, D_DK, D_DV), F32)],
        scratch_shapes=[pltpu.VMEM((D_DK, D_DV), F32)],
        compiler_params=_cparams(("parallel", "parallel", "arbitrary")),
        name="gla_chunks",
    )(pd, pd, pd, pd, pd, w2, b_gk.reshape(1, D_K), onorm.reshape(1, D_DV), s0)


TM = 768
TM_SMALL = 256


def _cross_block(x, layer, n_p, t_p, n_s, t_s, mem_prompt, cache_mem_k, cache_mem_v, norm_mem_w, norm_cross_w,
                 x_w_q, x_w_k, x_w_v, x_w_o, x_q_norm, x_k_norm):
    d = x.shape[1]
    mp = n_p * t_p
    wkv = jnp.concatenate([x_w_k[layer], x_w_v[layer]], axis=1).astype(BF16)
    hw = jnp.concatenate([jnp.tile(x_k_norm[layer], X_HEADS), jnp.ones((X_W,), F32)]).reshape(1, 2 * X_W)
    mem = mem_prompt.reshape(n_p * N_MEM, d)
    kv = _norm_matmul(mem, norm_mem_w[layer], wkv, tm=min(512, n_p * N_MEM), tn=X_W, head_w=hw,
                      n_norm_cols=X_W, gsize=X_DH)
    mk_p = kv[:, :X_W].reshape(n_p, N_MEM, X_W)
    mv_p = kv[:, X_W:].reshape(n_p, N_MEM, X_W)
    qx = _norm_matmul(x, norm_cross_w[layer], x_w_q[layer].astype(BF16), tm=TM, tn=X_W)
    ca_p = _cross_attention(qx, 0, n_p, t_p, mk_p, mv_p, x_q_norm[layer], tq=min(512, t_p))
    ca_s = _cross_attention(qx, mp, n_s, t_s, cache_mem_k[layer].reshape(n_s, N_MEM, X_W),
                            cache_mem_v[layer].reshape(n_s, N_MEM, X_W), x_q_norm[layer], tq=t_s)
    ca = jnp.concatenate([ca_p, ca_s], axis=0)
    x = _matmul_residual([ca], [x_w_o[layer].astype(BF16)], x, tm=TM, tn=1024)
    return x, mk_p.reshape(n_p, N_MEM, X_HEADS, X_DH), mv_p.reshape(n_p, N_MEM, X_HEADS, X_DH)


def kernel(x_prompt, x_sample, cache_diff_k, cache_diff_v, state_rwkv, state_rwkv_shift, cache_mem_k, cache_mem_v, state_gdn, state_gdn_conv, state_gla, page_table, mem_prompt, norm_mix_w, norm_cross_w, norm_mem_w, norm_ffn_w, w_in_even, w_out_even, a_q_norm, a_k_norm, a_lam_q1, a_lam_k1, a_lam_q2, a_lam_k2, a_subln, b_mu, b_w0, b_w2, b_a0, b_a2, b_g2, b_k_k, b_k_a, b_r_k, b_lnx_w, b_lnx_b, ffd_w_gate, ffd_w_up, ffd_w_down, w_in_odd, w_out_odd, c_conv_w, c_a_log, c_dt_bias, c_onorm, d_w_gk2, d_b_gk, d_onorm, moe_w_router, moe_w_gate, moe_w_up, moe_w_down, x_w_q, x_w_k, x_w_v, x_w_o, x_q_norm, x_k_norm):
    n_p, t_p, d = x_prompt.shape
    n_s, t_s, _ = x_sample.shape
    mp, ms = n_p * t_p, n_s * t_s
    depth = norm_mix_w.shape[0]
    x = jnp.concatenate([x_prompt.reshape(mp, d), x_sample.reshape(ms, d)], axis=0)
    assert (mp + ms) % TM == 0 and (mp + ms) % TM_SMALL == 0
    page = cache_diff_k.shape[2]
    slopes = jnp.exp2(-(8.0 / A_HEADS) * jnp.arange(1, A_HEADS + 1, dtype=F32))
    cross_w = (mem_prompt, cache_mem_k, cache_mem_v, norm_mem_w, norm_cross_w, x_w_q, x_w_k, x_w_v, x_w_o,
               x_q_norm, x_k_norm)
    dk_p, dv_p, dk_s, dv_s, rw_p, rw_s, sh_p, sh_s = [], [], [], [], [], [], [], []
    mk_l, mv_l, gd_p, gd_s, cv_p, cv_s, gl_p, gl_s = [], [], [], [], [], [], [], []
    for layer in range(depth):
        i = layer // 2
        if layer % 2 == 0:
            lam_init = 0.8 - 0.6 * math.exp(-0.3 * layer)
            w_in = w_in_even[i]
            hw = jnp.concatenate([jnp.tile(a_q_norm[i], A_Q // A_DH), jnp.tile(a_k_norm[i], A_Q // A_DH),
                                  jnp.ones((A_COLS - 2 * A_Q,), F32)]).reshape(1, A_COLS)
            pa = _norm_matmul(x, norm_mix_w[layer], w_in[:, :A_COLS].astype(BF16), tm=TM, tn=1024, head_w=hw,
                              n_norm_cols=2 * A_Q, gsize=A_DH)
            pb = _norm_matmul(x, norm_mix_w[layer], _rwkv_cols(w_in[:, A_COLS:]).astype(BF16), tm=TM,
                              tn=PB_COLS // 2)
            lam_stack = jnp.stack([a_lam_q1[i], a_lam_k1[i], a_lam_q2[i], a_lam_k2[i]])
            oa_p = _diff_attention_prompt(pa, n_p, t_p, slopes, lam_stack, a_subln[i], lam_init, tq=min(256, t_p))
            n_pool = cache_diff_k.shape[1]
            oa_s = _diff_attention_sample(pa, mp, n_s, t_s, cache_diff_k[i].reshape(n_pool, page, A_Q),
                                          cache_diff_v[i].reshape(n_pool, page, A_Q), page_table, lam_stack,
                                          a_subln[i], lam_init)
            ob, st_p, st_s = _rwkv_mixer(pb, n_p, t_p, n_s, t_s, state_rwkv_shift[i], state_rwkv[i], b_mu[i],
                                         b_w0[i], b_w2[i], b_a0[i], b_a2[i], b_g2[i], b_k_k[i], b_k_a[i], b_r_k[i],
                                         b_lnx_w[i], b_lnx_b[i], tm=TM_SMALL, tb=min(256, t_p))
            dk_p.append(pa[:mp, A_Q:2 * A_Q].reshape(n_p, t_p, A_HEADS, 2 * A_DH))
            dv_p.append(pa[:mp, 2 * A_Q:].reshape(n_p, t_p, A_HEADS, 2 * A_DH))
            dk_s.append(pa[mp:, A_Q:2 * A_Q].reshape(n_s, t_s, A_HEADS, 2 * A_DH))
            dv_s.append(pa[mp:, 2 * A_Q:].reshape(n_s, t_s, A_HEADS, 2 * A_DH))
            rw_p.append(st_p)
            rw_s.append(st_s)
            sh_p.append(_rwkv_cols_inv(pb[:mp].reshape(n_p, t_p, PB_COLS)[:, -1:]))
            sh_s.append(_rwkv_cols_inv(pb[mp:].reshape(n_s, t_s, PB_COLS)[:, -1:]))
            oa = jnp.concatenate([oa_p, oa_s], axis=0)
            w_out = w_out_even[i].astype(BF16)
            x = _matmul_residual([oa, ob], [w_out[:A_Q], w_out[A_Q:]], x, tm=TM, tn=1024)
        else:
            w_in = w_in_odd[i]
            c_cols = C_CONV_CH + C_V + 2 * C_V_HEADS
            zc = jnp.zeros((d, LANES - C_V_HEADS), F32)
            wc = jnp.concatenate([w_in[:, :PC_BETA], w_in[:, PC_BETA:PC_BETA + C_V_HEADS], zc,
                                  w_in[:, PC_BETA + C_V_HEADS:c_cols], zc], axis=1).astype(BF16)
            wd_ = jnp.concatenate([w_in[:, c_cols:], jnp.zeros((d, LANES - D_GATE_LORA), F32)], axis=1).astype(BF16)
            pc = _norm_matmul(x, norm_mix_w[layer], wc, tm=TM, tn=PC_COLS // 2)
            pd = _norm_matmul(x, norm_mix_w[layer], wd_, tm=TM, tn=PD_COLS // 5)
            gdn_w = (c_conv_w[i], c_a_log[i], c_dt_bias[i], c_onorm[i])
            oc_p, gs_p = _gdn_group(pc, jnp.zeros((n_p, C_CONV - 1, C_CONV_CH), F32),
                                    jnp.zeros((n_p, C_V_HEADS, C_DH, C_DH), F32), 0, n_p, t_p, *gdn_w,
                                    tm=min(512, t_p), tb=min(256, t_p))
            oc_s, gs_s = _gdn_group(pc, state_gdn_conv[i], state_gdn[i], mp, n_s, t_s, *gdn_w, tm=t_s, tb=t_s)
            gla_w = (d_w_gk2[i], d_b_gk[i], d_onorm[i])
            od_p, ls_p = _gla_group(pd, jnp.zeros((n_p, D_HEADS, D_DK, D_DV), F32), 0, n_p, t_p, *gla_w,
                                    tb=min(128, t_p))
            od_s, ls_s = _gla_group(pd, state_gla[i], mp, n_s, t_s, *gla_w, tb=t_s)
            gd_p.append(gs_p)
            gd_s.append(gs_s)
            cv_p.append(pc[:mp].reshape(n_p, t_p, PC_COLS)[:, -(C_CONV - 1):, :C_CONV_CH])
            cv_s.append(pc[mp:].reshape(n_s, t_s, PC_COLS)[:, -(C_CONV - 1):, :C_CONV_CH])
            gl_p.append(ls_p)
            gl_s.append(ls_s)
            oc = jnp.concatenate([oc_p, oc_s], axis=0)
            od = jnp.concatenate([od_p, od_s], axis=0)
            w_out = w_out_odd[i].astype(BF16)
            x = _matmul_residual([oc, od], [w_out[:C_V], w_out[C_V:]], x, tm=TM, tn=1024)
        x, mk_p, mv_p = _cross_block(x, layer, n_p, t_p, n_s, t_s, *cross_w)
        mk_l.append(mk_p)
        mv_l.append(mv_p)
        if layer % 2 == 0:
            x = _swiglu_ffn(x, norm_ffn_w[layer], ffd_w_gate[i].astype(BF16), ffd_w_up[i].astype(BF16),
                            ffd_w_down[i].astype(BF16), tm=TM, tf=512)
        else:
            x = _moe_ffn(x, norm_ffn_w[layer], moe_w_router[i], moe_w_gate[i].astype(BF16),
                         moe_w_up[i].astype(BF16), moe_w_down[i].astype(BF16), tm=TM, tf=256)
    st = jnp.stack
    return (x[:mp].reshape(n_p, t_p, d), x[mp:].reshape(n_s, t_s, d), st(dk_p), st(dv_p), st(dk_s), st(dv_s),
            st(rw_p), st(rw_s), st(sh_p), st(sh_s), st(mk_l), st(mv_l), st(gd_p), st(gd_s), st(cv_p), st(cv_s),
            st(gl_p), st(gl_s))
```

```python
import functools
import math

import jax
import jax.numpy as jnp
from jax import lax
from jax.experimental import pallas as pl
from jax.experimental.pallas import tpu as pltpu

F32 = jnp.float32
BF16 = jnp.bfloat16
NEG_INF = float("-inf")

D_MODEL = 2048
A_HEADS = 8
A_DH = 64
A_Q = 1024
A_COLS = 3072
B_HEADS = 16
B_DH = 64
B_W = 1024
B_DECAY_LORA = 96
B_AAA_LORA = 96
B_GATE_LORA = 256
B_COLS = 3520
B_GN_EPS = 64e-5
C_QK_HEADS = 4
C_V_HEADS = 8
C_DH = 128
C_CONV = 4
C_QK = 512
C_V = 1024
C_CONV_CH = 2048
C_CHUNK = 64
D_HEADS = 4
D_DK = 128
D_DV = 256
D_K = 512
D_V = 1024
D_GATE_LORA = 16
D_GATE_NORM = 16.0
D_CHUNK = 16
N_MEM = 256
X_HEADS = 4
X_DH = 128
X_W = 512
FF_DENSE = 5632
N_EXPERTS = 8
FF_EXPERT = 2816
EPS = 1e-6

LANES = 128
VMEM_LIMIT = 56 * 1024 * 1024


def _cparams(sem):
    return pltpu.CompilerParams(dimension_semantics=sem, vmem_limit_bytes=VMEM_LIMIT)


def _sigmoid(x):
    return 1.0 / (1.0 + jnp.exp(-x))


def _silu(x):
    return x * _sigmoid(x)


def _softplus(x):
    return jnp.maximum(x, 0.0) + jnp.log(1.0 + jnp.exp(-jnp.abs(x)))


def _rms_rows(x, w):
    ms = jnp.mean(x * x, axis=-1, keepdims=True)
    return x * lax.rsqrt(ms + EPS) * w


def _group_sum(x, gsize):
    lane = lax.broadcasted_iota(jnp.int32, (1, LANES), 1)
    outs = []
    for c in range(x.shape[1] // LANES):
        xc = x[:, c * LANES:(c + 1) * LANES]
        if gsize == LANES:
            ss = jnp.broadcast_to(jnp.sum(xc, axis=-1, keepdims=True), xc.shape)
        else:
            lo = jnp.sum(jnp.where(lane < 64, xc, 0.0), axis=-1, keepdims=True)
            hi = jnp.sum(jnp.where(lane >= 64, xc, 0.0), axis=-1, keepdims=True)
            ss = jnp.where(lane < 64, lo, hi)
        outs.append(ss)
    return outs[0] if len(outs) == 1 else jnp.concatenate(outs, axis=1)


def _group_norm_scale(x, gsize, inv_n, eps):
    return x * lax.rsqrt(_group_sum(x * x, gsize) * inv_n + eps)


def _split3(a):
    a1 = a.astype(BF16)
    r1 = a - a1.astype(F32)
    a2 = r1.astype(BF16)
    a3 = (r1 - a2.astype(F32)).astype(BF16)
    return a1, a2, a3


def _dot_f32(a, b):
    a1, a2, a3 = _split3(a)
    b1, b2, b3 = _split3(b)
    d = lambda x, y: jnp.dot(x, y, preferred_element_type=F32)
    return d(a1, b3) + d(a3, b1) + d(a2, b2) + d(a1, b2) + d(a2, b1) + d(a1, b1)


def _norm_mm_kernel(x_ref, nw_ref, w_ref, hw_ref, o_ref, h_scr, *, n_norm_tiles, gsize):
    j = pl.program_id(1)

    @pl.when(j == 0)
    def _():
        h_scr[...] = _rms_rows(x_ref[...], nw_ref[...]).astype(BF16)

    acc = jnp.dot(h_scr[...], w_ref[...], preferred_element_type=F32)
    if n_norm_tiles == 0:
        o_ref[...] = acc
    else:
        @pl.when(j < n_norm_tiles)
        def _():
            o_ref[...] = _group_norm_scale(acc, gsize, 1.0 / gsize, EPS) * hw_ref[...]

        @pl.when(j >= n_norm_tiles)
        def _():
            o_ref[...] = acc


def _norm_matmul(x, nw, w, *, tm, tn, head_w=None, n_norm_cols=0, gsize=LANES):
    m, k = x.shape
    n = w.shape[1]
    assert m % tm == 0 and n % tn == 0 and n_norm_cols % tn == 0
    if head_w is None:
        head_w = jnp.ones((1, n), F32)
    kern = functools.partial(_norm_mm_kernel, n_norm_tiles=n_norm_cols // tn, gsize=gsize)
    return pl.pallas_call(
        kern,
        grid=(m // tm, n // tn),
        in_specs=[
            pl.BlockSpec((tm, k), lambda i, j: (i, 0)),
            pl.BlockSpec((1, k), lambda i, j: (0, 0)),
            pl.BlockSpec((k, tn), lambda i, j: (0, j)),
            pl.BlockSpec((1, tn), lambda i, j: (0, j)),
        ],
        out_specs=pl.BlockSpec((tm, tn), lambda i, j: (i, j)),
        out_shape=jax.ShapeDtypeStruct((m, n), F32),
        scratch_shapes=[pltpu.VMEM((tm, k), BF16)],
        compiler_params=_cparams(("parallel", "arbitrary")),
        name="norm_matmul",
    )(x, nw.reshape(1, k), w, head_w)


def _mm_res_kernel(*refs, n_in):
    x_ref = refs[2 * n_in]
    o_ref = refs[2 * n_in + 1]
    acc = x_ref[...]
    for t in range(n_in):
        acc = acc + jnp.dot(refs[t][...], refs[n_in + t][...], preferred_element_type=F32)
    o_ref[...] = acc


def _matmul_residual(acts, weights, x, *, tm, tn):
    m, n = x.shape
    n_in = len(acts)
    in_specs = [pl.BlockSpec((tm, a.shape[1]), lambda i, j: (i, 0)) for a in acts]
    in_specs += [pl.BlockSpec((w.shape[0], tn), lambda i, j: (0, j)) for w in weights]
    in_specs += [pl.BlockSpec((tm, tn), lambda i, j: (i, j))]
    return pl.pallas_call(
        functools.partial(_mm_res_kernel, n_in=n_in),
        grid=(m // tm, n // tn),
        in_specs=in_specs,
        out_specs=pl.BlockSpec((tm, tn), lambda i, j: (i, j)),
        out_shape=jax.ShapeDtypeStruct((m, n), F32),
        compiler_params=_cparams(("parallel", "arbitrary")),
        name="matmul_residual",
    )(*acts, *weights, x)


def _swiglu_kernel(x_ref, nw_ref, wg_ref, wu_ref, wd_ref, o_ref, h_scr):
    f = pl.program_id(1)

    @pl.when(f == 0)
    def _():
        x = x_ref[...]
        h_scr[...] = _rms_rows(x, nw_ref[...]).astype(BF16)
        o_ref[...] = x

    h = h_scr[...]
    g = jnp.dot(h, wg_ref[...], preferred_element_type=F32)
    u = jnp.dot(h, wu_ref[...], preferred_element_type=F32)
    a = (_silu(g) * u).astype(BF16)
    o_ref[...] += jnp.dot(a, wd_ref[...], preferred_element_type=F32)


def _swiglu_ffn(x, nw, wg, wu, wd, *, tm, tf):
    m, d = x.shape
    ff = wg.shape[1]
    return pl.pallas_call(
        _swiglu_kernel,
        grid=(m // tm, ff // tf),
        in_specs=[
            pl.BlockSpec((tm, d), lambda i, f: (i, 0)),
            pl.BlockSpec((1, d), lambda i, f: (0, 0)),
            pl.BlockSpec((d, tf), lambda i, f: (0, f)),
            pl.BlockSpec((d, tf), lambda i, f: (0, f)),
            pl.BlockSpec((tf, d), lambda i, f: (f, 0)),
        ],
        out_specs=pl.BlockSpec((tm, d), lambda i, f: (i, 0)),
        out_shape=jax.ShapeDtypeStruct((m, d), F32),
        scratch_shapes=[pltpu.VMEM((tm, d), BF16)],
        compiler_params=_cparams(("parallel", "arbitrary")),
        name="swiglu_ffn",
    )(x, nw.reshape(1, d), wg, wu, wd)


def _moe_kernel(x_ref, nw_ref, wr_ref, wg_ref, wu_ref, wd_ref, o_ref, h_scr, comb_scr):
    e = pl.program_id(1)
    f = pl.program_id(2)
    tm = x_ref.shape[0]
    lane = lax.broadcasted_iota(jnp.int32, (tm, LANES), 1)

    @pl.when((e == 0) & (f == 0))
    def _():
        x = x_ref[...]
        hf = _rms_rows(x, nw_ref[...])
        h_scr[...] = hf.astype(BF16)
        o_ref[...] = x
        logits = jnp.where(lane < N_EXPERTS, _dot_f32(hf, wr_ref[...]), NEG_INF)
        m1 = jnp.max(logits, axis=-1, keepdims=True)
        i1 = jnp.min(jnp.where(logits == m1, lane, LANES), axis=-1, keepdims=True)
        rest = jnp.where(lane == i1, NEG_INF, logits)
        m2 = jnp.max(rest, axis=-1, keepdims=True)
        i2 = jnp.min(jnp.where(rest == m2, lane, LANES), axis=-1, keepdims=True)
        e2 = jnp.exp(m2 - m1)
        g1 = 1.0 / (1.0 + e2)
        g2 = e2 / (1.0 + e2)
        comb_scr[...] = jnp.where(lane == i1, g1, 0.0) + jnp.where(lane == i2, g2, 0.0)

    c = jnp.sum(jnp.where(lane == e, comb_scr[...], 0.0), axis=-1, keepdims=True)
    h = h_scr[...]
    g = jnp.dot(h, wg_ref[...], preferred_element_type=F32)
    u = jnp.dot(h, wu_ref[...], preferred_element_type=F32)
    a = (_silu(g) * u * c).astype(BF16)
    o_ref[...] += jnp.dot(a, wd_ref[...], preferred_element_type=F32)


def _moe_ffn(x, nw, w_router, wg, wu, wd, *, tm, tf):
    m, d = x.shape
    ne, _, ff = wg.shape
    wr = jnp.zeros((d, LANES), F32).at[:, :ne].set(w_router)
    return pl.pallas_call(
        _moe_kernel,
        grid=(m // tm, ne, ff // tf),
        in_specs=[
            pl.BlockSpec((tm, d), lambda i, e, f: (i, 0)),
            pl.BlockSpec((1, d), lambda i, e, f: (0, 0)),
            pl.BlockSpec((d, LANES), lambda i, e, f: (0, 0)),
            pl.BlockSpec((None, d, tf), lambda i, e, f: (e, 0, f)),
            pl.BlockSpec((None, d, tf), lambda i, e, f: (e, 0, f)),
            pl.BlockSpec((None, tf, d), lambda i, e, f: (e, f, 0)),
        ],
        out_specs=pl.BlockSpec((tm, d), lambda i, e, f: (i, 0)),
        out_shape=jax.ShapeDtypeStruct((m, d), F32),
        scratch_shapes=[pltpu.VMEM((tm, d), BF16), pltpu.VMEM((tm, LANES), F32)],
        compiler_params=_cparams(("parallel", "arbitrary", "arbitrary")),
        name="moe_ffn",
    )(x, nw.reshape(1, d), wr, wg, wu, wd)


def _diff_lambda_vec(lam_ref, lam_init):
    lv = lam_ref[...]
    d1 = jnp.sum(lv[0:1, :] * lv[1:2, :], axis=-1, keepdims=True)
    d2 = jnp.sum(lv[2:3, :] * lv[3:4, :], axis=-1, keepdims=True)
    return jnp.exp(d1) - jnp.exp(d2) + lam_init


def _softmax_rows(s):
    m = jnp.max(s, axis=-1, keepdims=True)
    e = jnp.exp(s - m)
    return e, jnp.sum(e, axis=-1, keepdims=True)


def _diffattn_prompt_kernel(slope_ref, q_ref, k_ref, v_ref, lam_ref, subln_ref, o_ref, *, tq, lam_init):
    t = q_ref.shape[0]
    h = pl.program_id(1)
    slope = slope_ref[h]
    lane = lax.broadcasted_iota(jnp.int32, (1, LANES), 1)
    lam = _diff_lambda_vec(lam_ref, lam_init)
    kb = k_ref[...].astype(BF16)
    vb = v_ref[...].astype(BF16)
    nt = (((1,), (1,)), ((), ()))
    for qi in range(t // tq):
        n_keys = (qi + 1) * tq
        q = q_ref[qi * tq:(qi + 1) * tq, :] * (A_DH ** -0.5)
        q1 = jnp.where(lane < A_DH, q, 0.0).astype(BF16)
        q2 = jnp.where(lane >= A_DH, q, 0.0).astype(BF16)
        kk = kb[:n_keys]
        row = lax.broadcasted_iota(jnp.int32, (tq, n_keys), 0) + qi * tq
        col = lax.broadcasted_iota(jnp.int32, (tq, n_keys), 1)
        dist = (row - col).astype(F32)
        bias = jnp.where(dist >= 0.0, -slope * dist, NEG_INF)
        e1, l1 = _softmax_rows(lax.dot_general(q1, kk, nt, preferred_element_type=F32) + bias)
        e2, l2 = _softmax_rows(lax.dot_general(q2, kk, nt, preferred_element_type=F32) + bias)
        p = e1 * (1.0 / l1) - e2 * (lam / l2)
        o = jnp.dot(p.astype(BF16), vb[:n_keys], preferred_element_type=F32)
        o = _rms_rows(o, subln_ref[...]) * (1.0 - lam_init)
        o_ref[qi * tq:(qi + 1) * tq, :] = o.astype(BF16)


def _diff_attention_prompt(pa, n_batch, t, slopes, lam_stack, subln, lam_init, *, tq):
    kern = functools.partial(_diffattn_prompt_kernel, tq=tq, lam_init=lam_init)
    return pl.pallas_call(
        kern,
        grid=(n_batch, A_HEADS),
        in_specs=[
            pl.BlockSpec(memory_space=pltpu.SMEM),
            pl.BlockSpec((t, LANES), lambda b, h: (b, h)),
            pl.BlockSpec((t, LANES), lambda b, h: (b, A_HEADS + h)),
            pl.BlockSpec((t, LANES), lambda b, h: (b, 2 * A_HEADS + h)),
            pl.BlockSpec((4, A_DH), lambda b, h: (0, 0)),
            pl.BlockSpec((1, LANES), lambda b, h: (0, 0)),
        ],
        out_specs=pl.BlockSpec((t, LANES), lambda b, h: (b, h)),
        out_shape=jax.ShapeDtypeStruct((n_batch * t, A_HEADS * LANES), BF16),
        compiler_params=_cparams(("parallel", "parallel")),
        name="diff_attention_prompt",
    )(slopes, pa, pa, pa, lam_stack, subln.reshape(1, LANES))


PAGES_PER_STEP = 4


def _diffattn_sample_kernel(pt_ref, q_ref, kn_ref, vn_ref, *rest, past, lam_init):
    del pt_ref
    npg = PAGES_PER_STEP
    k_refs, v_refs = rest[:npg], rest[npg:2 * npg]
    lam_ref, subln_ref, o_ref, qq_scr, m_scr, l_scr, acc_scr = rest[2 * npg:]
    g = pl.program_id(1)
    ts = q_ref.shape[0]
    page = k_refs[0].shape[0]
    lane = lax.broadcasted_iota(jnp.int32, (1, LANES), 1)
    hsub = lax.broadcasted_iota(jnp.int32, (A_HEADS, LANES), 0)
    col = lax.broadcasted_iota(jnp.int32, (A_HEADS, LANES), 1)
    valid = (col // (2 * ts)) == hsub
    pos_q = past + jnp.bitwise_and(col, ts - 1)
    slope = jnp.exp2(-(8.0 / A_HEADS) * (hsub + 1).astype(F32))
    nt = (((1,), (1,)), ((), ()))
    tn = (((0,), (0,)), ((), ()))

    @pl.when(g == 0)
    def _():
        rows = []
        for h in range(A_HEADS):
            q = q_ref[:, h * LANES:(h + 1) * LANES] * (A_DH ** -0.5)
            rows += [jnp.where(lane < A_DH, q, 0.0), jnp.where(lane >= A_DH, q, 0.0)]
        qq_scr[...] = jnp.concatenate(rows, axis=0).astype(BF16)
        m_scr[...] = jnp.where(valid, NEG_INF, 0.0)
        l_scr[...] = jnp.zeros(l_scr.shape, F32)
        acc_scr[...] = jnp.zeros(acc_scr.shape, F32)

    def absorb(k3, v3, key0, causal):
        nk = k3.shape[0]
        k2 = k3.reshape(nk * A_HEADS, LANES).astype(BF16)
        v2 = v3.reshape(nk * A_HEADS, LANES).astype(BF16)
        s = lax.dot_general(k2, qq_scr[...], nt, preferred_element_type=F32).reshape(nk, A_HEADS, LANES)
        kidx = lax.broadcasted_iota(jnp.int32, (nk * A_HEADS, LANES), 0) // A_HEADS
        dist = (pos_q[None] - (key0 + kidx).reshape(nk, A_HEADS, LANES)).astype(F32)
        keep = valid[None] & (dist >= 0.0) if causal else valid[None]
        s = jnp.where(keep, s - slope[None] * dist, NEG_INF)
        m_old = m_scr[...]
        m_new = jnp.maximum(m_old, jnp.max(s, axis=0))
        alpha = jnp.exp(m_old - m_new)
        pe = jnp.exp(s - m_new[None])
        l_scr[...] = alpha * l_scr[...] + jnp.sum(pe, axis=0)
        m_scr[...] = m_new
        alpha_row = jnp.sum(jnp.where(valid, alpha, 0.0), axis=0, keepdims=True)
        pv = lax.dot_general(v2, pe.reshape(nk * A_HEADS, LANES).astype(BF16), tn, preferred_element_type=F32)
        acc_scr[...] = alpha_row * acc_scr[...] + pv

    for j in range(npg):
        absorb(k_refs[j][...], v_refs[j][...], (g * npg + j) * page, False)

    @pl.when(g == pl.num_programs(1) - 1)
    def _():
        absorb(kn_ref[...], vn_ref[...], past, True)
        lam = _diff_lambda_vec(lam_ref, lam_init)
        l_row = jnp.sum(jnp.where(valid, l_scr[...], 0.0), axis=0, keepdims=True)
        ot = acc_scr[...] / l_row
        ot = ot - lam * pltpu.roll(ot, LANES - ts, 1)
        o = ot.T
        for h in range(A_HEADS):
            oh = _rms_rows(o[h * 2 * ts:h * 2 * ts + ts, :], subln_ref[...]) * (1.0 - lam_init)
            o_ref[:, h * LANES:(h + 1) * LANES] = oh.astype(BF16)


def _diff_attention_sample(pa, row0, n_batch, ts, cache_k, cache_v, layer_idx, page_table, lam_stack, subln,
                           lam_init):
    n_pages = page_table.shape[1]
    page = cache_k.shape[2]
    npg = PAGES_PER_STEP
    assert row0 % ts == 0 and n_pages % npg == 0 and 2 * ts * A_HEADS == LANES
    rb = row0 // ts
    w = A_HEADS * LANES
    new_rows = lambda c0: pa[row0:, c0:c0 + w].reshape(n_batch, ts, A_HEADS, LANES)
    kern = functools.partial(_diffattn_sample_kernel, past=n_pages * page, lam_init=lam_init)
    page_spec = lambda j: pl.BlockSpec((None, None, page, A_HEADS, LANES),
                                       lambda b, g, pt: (layer_idx, pt[b, g * npg + j], 0, 0, 0))
    new_spec = pl.BlockSpec((None, ts, A_HEADS, LANES), lambda b, g, pt: (b, 0, 0, 0))
    grid_spec = pltpu.PrefetchScalarGridSpec(
        num_scalar_prefetch=1,
        grid=(n_batch, n_pages // npg),
        in_specs=[pl.BlockSpec((ts, w), lambda b, g, pt: (rb + b, 0)), new_spec, new_spec]
        + [page_spec(j) for j in range(npg)] * 2
        + [pl.BlockSpec((4, A_DH), lambda b, g, pt: (0, 0)), pl.BlockSpec((1, LANES), lambda b, g, pt: (0, 0))],
        out_specs=pl.BlockSpec((ts, w), lambda b, g, pt: (b, 0)),
        scratch_shapes=[
            pltpu.VMEM((LANES, LANES), BF16),
            pltpu.VMEM((A_HEADS, LANES), F32),
            pltpu.VMEM((A_HEADS, LANES), F32),
            pltpu.VMEM((LANES, LANES), F32),
        ],
    )
    return pl.pallas_call(
        kern,
        grid_spec=grid_spec,
        out_shape=jax.ShapeDtypeStruct((n_batch * ts, w), BF16),
        compiler_params=_cparams(("parallel", "arbitrary")),
        name="diff_attention_sample",
    )(page_table, pa, new_rows(w), new_rows(2 * w), *([cache_k] * npg), *([cache_v] * npg), lam_stack,
      subln.reshape(1, LANES))


def _cross_attn_kernel(q_ref, k_ref, v_ref, qn_ref, o_ref):
    nt = (((1,), (1,)), ((), ()))
    for h in range(X_HEADS):
        sl = slice(h * LANES, (h + 1) * LANES)
        q = _rms_rows(q_ref[:, sl], qn_ref[...]) * (X_DH ** -0.5)
        s = lax.dot_general(q.astype(BF16), k_ref[:, sl].astype(BF16), nt, preferred_element_type=F32)
        e, l = _softmax_rows(s)
        p = (e * (1.0 / l)).astype(BF16)
        o_ref[:, sl] = jnp.dot(p, v_ref[:, sl].astype(BF16), preferred_element_type=F32).astype(BF16)


def _cross_attention(q, row0, n_batch, t, mk, mv, q_norm, *, tq):
    assert t % tq == 0 and row0 % tq == 0
    nq = t // tq
    rb = row0 // tq
    return pl.pallas_call(
        _cross_attn_kernel,
        grid=(n_batch, nq),
        in_specs=[
            pl.BlockSpec((tq, X_W), lambda b, i: (rb + b * nq + i, 0)),
            pl.BlockSpec((None, N_MEM, X_W), lambda b, i: (b, 0, 0)),
            pl.BlockSpec((None, N_MEM, X_W), lambda b, i: (b, 0, 0)),
            pl.BlockSpec((1, LANES), lambda b, i: (0, 0)),
        ],
        out_specs=pl.BlockSpec((tq, X_W), lambda b, i: (b * nq + i, 0)),
        out_shape=jax.ShapeDtypeStruct((n_batch * t, X_W), BF16),
        compiler_params=_cparams(("parallel", "parallel")),
        name="cross_attention",
    )(q, mk, mv, q_norm.reshape(1, LANES))


PB_LG = 3 * B_W
PB_LW = PB_LG + B_GATE_LORA
PB_LA = PB_LW + LANES
PB_COLS = PB_LA + LANES


def _swap_halves(x):
    tiles = [pltpu.roll(x[:, c * LANES:(c + 1) * LANES], LANES // 2, 1) for c in range(x.shape[1] // LANES)]
    return tiles[0] if len(tiles) == 1 else jnp.concatenate(tiles, axis=1)


def _rwkv_prep_kernel(pb_ref, ps_ref, mu_ref, w0_ref, w2_ref, a0_ref, a2_ref, g2_ref, kk_ref, ka_ref,
                      r_ref, kf_ref, v_ref, g_ref, rt_ref, at_ref, bt_ref, kt_ref, bh_ref, kh_ref, pc_ref,
                      *, n_prompt_tiles, chunk_p, chunk_s):
    tm = pb_ref.shape[0]
    pb = pb_ref[...]
    xs = pb + (ps_ref[...] - pb) * mu_ref[...]
    r = xs[:, :B_W]
    k = xs[:, B_W:2 * B_W]
    lg = xs[:, PB_LG:PB_LW]
    lw_in = xs[:, PB_LW:PB_LA]
    la = xs[:, PB_LA:PB_COLS]
    wl = w0_ref[...] + jnp.dot(jnp.tanh(lw_in).astype(BF16), w2_ref[...], preferred_element_type=F32)
    lw = -jnp.exp(-_softplus(-wl) - 0.5)
    a = _sigmoid(a0_ref[...] + jnp.dot(la.astype(BF16), a2_ref[...], preferred_element_type=F32))
    kk = _group_norm_scale(k * kk_ref[...], B_DH, 1.0, EPS)
    kf = k * (1.0 + (a - 1.0) * ka_ref[...])
    b = kk * a
    r_ref[...] = r
    kf_ref[...] = kf
    v_ref[...] = xs[:, 2 * B_W:3 * B_W]
    g_ref[...] = jnp.dot(_sigmoid(lg).astype(BF16), g2_ref[...], preferred_element_type=F32)

    chunk = jnp.where(pl.program_id(0) < n_prompt_tiles, chunk_p, chunk_s)
    rowc = jnp.bitwise_and(lax.broadcasted_iota(jnp.int32, (tm, B_W), 0), chunk - 1)
    lp = lw
    sfx = lw
    step = 1
    while step < max(chunk_p, chunk_s):
        lp = lp + jnp.where(rowc >= step, pltpu.roll(lp, step, 0), 0.0)
        sfx = sfx + jnp.where(rowc + step < chunk, pltpu.roll(sfx, tm - step, 0), 0.0)
        step *= 2
    sfx = sfx - lw
    e_neg = jnp.exp(-lp)
    e_sfx = jnp.exp(sfx)
    sw = lambda z: _swap_halves(z).astype(BF16)
    rt_ref[...] = sw(r * jnp.exp(lp))
    at_ref[...] = sw(kk * jnp.exp(lp - lw))
    bt_ref[...] = sw(b * e_neg)
    kt_ref[...] = sw(kf * e_neg)
    bh_ref[...] = sw(b * e_sfx)
    kh_ref[...] = sw(kf * e_sfx)
    pc_ref[...] = _swap_halves(jnp.exp(lp + sfx))


def _rwkv_prep(pb, ps, mu, w0, w2, a0, a2, g2, k_k, k_a, *, tm, n_prompt_tiles, chunk_p, chunk_s):
    m = pb.shape[0]
    assert tm % chunk_p == 0 and tm % chunk_s == 0
    row = lambda n: pl.BlockSpec((1, n), lambda i: (0, 0))
    full = lambda a: pl.BlockSpec(a.shape, lambda i: (0, 0))
    tok = lambda n: pl.BlockSpec((tm, n), lambda i: (i, 0))
    f32o = jax.ShapeDtypeStruct((m, B_W), F32)
    b16o = jax.ShapeDtypeStruct((m, B_W), BF16)
    kern = functools.partial(_rwkv_prep_kernel, n_prompt_tiles=n_prompt_tiles, chunk_p=chunk_p, chunk_s=chunk_s)
    return pl.pallas_call(
        kern,
        grid=(m // tm,),
        in_specs=[tok(PB_COLS), tok(PB_COLS), row(PB_COLS), row(B_W), full(w2), row(B_W), full(a2), full(g2),
                  row(B_W), row(B_W)],
        out_specs=[tok(B_W)] * 11,
        out_shape=[f32o] * 4 + [b16o] * 6 + [f32o],
        compiler_params=_cparams(("parallel",)),
        name="rwkv_prep",
    )(pb, ps, mu, w0.reshape(1, B_W), w2, a0.reshape(1, B_W), a2, g2, k_k.reshape(1, B_W), k_a.reshape(1, B_W))


SOLVE_BLOCK = 16


def _split2(a):
    hi = a.astype(BF16)
    return hi, (a - hi.astype(F32)).astype(BF16)


def _solve_unit_lower(lmat, rhs):
    c = lmat.shape[0]
    bs = min(SOLVE_BLOCK, c)
    col = lax.broadcasted_iota(jnp.int32, (bs, c), 1)
    d = lambda x, y: jnp.dot(x, y, preferred_element_type=F32)
    done = []
    for blk in range(c // bs):
        rows = slice(blk * bs, (blk + 1) * bs)
        r = rhs[rows, :]
        lrow = lmat[rows, :]
        if blk > 0:
            xs = jnp.concatenate(done + [jnp.zeros((c - blk * bs, rhs.shape[1]), F32)], axis=0)
            lh, ll = _split2(jnp.where(col < blk * bs, lrow, 0.0))
            xh, xl = _split2(xs)
            r = r - (d(lh, xl) + d(ll, xh) + d(lh, xh))
        for j in range(bs - 1):
            r = r - lrow[:, blk * bs + j:blk * bs + j + 1] * r[j:j + 1, :]
        done.append(r)
    return done[0] if len(done) == 1 else jnp.concatenate(done, axis=0)


def _rwkv_chunk_kernel(rt_ref, at_ref, bt_ref, kt_ref, bh_ref, kh_ref, v_ref, pc_ref, z0_ref, y_ref, zo_ref, z_scr,
                       *, chunk):
    i = pl.program_id(2)
    tb = rt_ref.shape[0]

    @pl.when(i == 0)
    def _():
        z_scr[...] = z0_ref[...]

    lo = lax.broadcasted_iota(jnp.int32, (1, LANES), 1) < B_DH
    zr = lax.broadcasted_iota(jnp.int32, (LANES, LANES), 0) < B_DH
    zc = lax.broadcasted_iota(jnp.int32, (LANES, LANES), 1) < B_DH
    offdiag = zr != zc
    ri = lax.broadcasted_iota(jnp.int32, (chunk, chunk), 0)
    ci = lax.broadcasted_iota(jnp.int32, (chunk, chunk), 1)
    nt = (((1,), (1,)), ((), ()))
    tn = (((0,), (0,)), ((), ()))
    d = lambda x, y: jnp.dot(x, y, preferred_element_type=F32)
    zero = jnp.zeros((), BF16)
    for c in range(tb // chunk):
        rs = slice(c * chunk, (c + 1) * chunk)
        rt, at, bt, kt = rt_ref[rs, :], at_ref[rs, :], bt_ref[rs, :], kt_ref[rs, :]
        v = v_ref[rs, :].astype(BF16)
        xs, rkv, rbs = [], [], []
        for head in range(2):
            key_lo = head == 1
            kmask = lo if key_lo else jnp.logical_not(lo)
            a_h, r_h = jnp.where(kmask, at, zero), jnp.where(kmask, rt, zero)
            b_h, k_h = jnp.where(kmask, bt, zero), jnp.where(kmask, kt, zero)
            v_h = jnp.where(kmask, zero, v)
            lhs = jnp.concatenate([a_h, r_h], axis=0)
            gb = lax.dot_general(lhs, b_h, nt, preferred_element_type=F32)
            gk = lax.dot_general(lhs, k_h, nt, preferred_element_type=F32)
            lmat = jnp.where(ri > ci, gb[:chunk], 0.0)
            ak = jnp.where(ri > ci, gk[:chunk], 0.0).astype(BF16)
            rk = jnp.where(ri >= ci, gk[chunk:], 0.0).astype(BF16)
            rbs.append(jnp.where(ri >= ci, gb[chunk:], 0.0).astype(BF16))
            xs.append(_solve_unit_lower(lmat, a_h.astype(F32) + d(ak, v_h)))
            rkv.append(d(rk, v_h))
        z = z_scr[...]
        stacked = jnp.concatenate(xs + [jnp.where(jnp.logical_not(lo), rt, zero).astype(F32),
                                        jnp.where(lo, rt, zero).astype(F32)], axis=0).astype(BF16)
        big = lax.dot_general(stacked, z.astype(BF16), nt, preferred_element_type=F32)
        u0 = jnp.where(lo, big[:chunk], 0.0)
        u1 = jnp.where(lo, 0.0, big[chunk:2 * chunk])
        y = jnp.where(lo, big[2 * chunk:3 * chunk], big[3 * chunk:]) + rkv[0] + rkv[1]
        y = y - d(rbs[0], u0.astype(BF16)) - d(rbs[1], u1.astype(BF16))
        y_ref[rs, :] = y
        upd = lax.dot_general(jnp.concatenate([v, (u0 + u1).astype(BF16)], axis=0),
                              jnp.concatenate([kh_ref[rs, :], -bh_ref[rs, :]], axis=0), tn,
                              preferred_element_type=F32)
        pc = pc_ref[c * chunk:c * chunk + 1, :]
        z_scr[...] = jnp.where(offdiag, z * pc + upd, z)

    @pl.when(i == pl.num_programs(2) - 1)
    def _():
        zo_ref[...] = z_scr[...]


def _rwkv_chunks(rt, at, bt, kt, bh, kh, v, pc, z0, row0, n_seq, t, *, tb, chunk):
    assert t % tb == 0 and row0 % tb == 0 and tb % chunk == 0
    nb = t // tb
    rb = row0 // tb
    n_pairs = B_HEADS // 2
    tok = pl.BlockSpec((tb, LANES), lambda s, p, i: (rb + s * nb + i, p))
    st = pl.BlockSpec((None, None, LANES, LANES), lambda s, p, i: (s, p, 0, 0))
    return pl.pallas_call(
        functools.partial(_rwkv_chunk_kernel, chunk=chunk),
        grid=(n_seq, n_pairs, nb),
        in_specs=[tok] * 8 + [st],
        out_specs=[pl.BlockSpec((tb, LANES), lambda s, p, i: (s * nb + i, p)), st],
        out_shape=[jax.ShapeDtypeStruct((n_seq * t, B_W), F32),
                   jax.ShapeDtypeStruct((n_seq, n_pairs, LANES, LANES), F32)],
        scratch_shapes=[pltpu.VMEM((LANES, LANES), F32)],
        compiler_params=_cparams(("parallel", "parallel", "arbitrary")),
        name="rwkv_chunks",
    )(rt, at, bt, kt, bh, kh, v, pc, z0)


def _rwkv_post_kernel(y_ref, r_ref, kf_ref, v_ref, g_ref, lw_ref, lb_ref, rk_ref, o_ref):
    y = y_ref[...]
    mean = _group_sum(y, B_DH) * (1.0 / B_DH)
    yc = y - mean
    var = _group_sum(yc * yc, B_DH) * (1.0 / B_DH)
    yn = yc * lax.rsqrt(var + B_GN_EPS) * lw_ref[...] + lb_ref[...]
    bonus = _group_sum(r_ref[...] * kf_ref[...] * rk_ref[...], B_DH)
    o_ref[...] = ((yn + bonus * v_ref[...]) * g_ref[...]).astype(BF16)


def _rwkv_post(y, r, kf, v, g, lnx_w, lnx_b, r_k, *, tm):
    m = y.shape[0]
    tok = pl.BlockSpec((tm, B_W), lambda i: (i, 0))
    row = pl.BlockSpec((1, B_W), lambda i: (0, 0))
    return pl.pallas_call(
        _rwkv_post_kernel,
        grid=(m // tm,),
        in_specs=[tok] * 5 + [row] * 3,
        out_specs=tok,
        out_shape=jax.ShapeDtypeStruct((m, B_W), BF16),
        compiler_params=_cparams(("parallel",)),
        name="rwkv_post",
    )(y, r, kf, v, g, lnx_w.reshape(1, B_W), lnx_b.reshape(1, B_W), r_k.reshape(1, B_W))


def _rwkv_cols(a):
    lw0 = 3 * B_W
    la0 = lw0 + B_DECAY_LORA
    lg0 = la0 + B_AAA_LORA
    z = jnp.zeros(a.shape[:-1] + (LANES - B_DECAY_LORA,), a.dtype)
    return jnp.concatenate([a[..., :lw0], a[..., lg0:], a[..., lw0:la0], z, a[..., la0:lg0], z], axis=-1)


def _rwkv_cols_inv(a):
    return jnp.concatenate([a[..., :PB_LG], a[..., PB_LW:PB_LW + B_DECAY_LORA],
                            a[..., PB_LA:PB_LA + B_AAA_LORA], a[..., PB_LG:PB_LW]], axis=-1)


def _pad_rows(a, n):
    return jnp.pad(a, ((0, n - a.shape[0]), (0, 0)))


def _rwkv_state_to_pairs(s):
    n = s.shape[0]
    s = s.reshape(n, B_HEADS // 2, 2, B_DH, B_DH)
    eye = jnp.broadcast_to(jnp.eye(B_DH, dtype=F32), (n, B_HEADS // 2, B_DH, B_DH))
    top = jnp.concatenate([eye, s[:, :, 0]], axis=-1)
    bot = jnp.concatenate([s[:, :, 1], eye], axis=-1)
    return jnp.concatenate([top, bot], axis=-2)


def _rwkv_state_from_pairs(z):
    n = z.shape[0]
    s = jnp.stack([z[:, :, :B_DH, B_DH:], z[:, :, B_DH:, :B_DH]], axis=2)
    return s.reshape(n, B_HEADS, B_DH, B_DH)


def _rwkv_mixer(pb, n_p, t_p, n_s, t_s, prev_s, state_s, mu, w0, w2, a0, a2, g2, k_k, k_a, r_k, lnx_w, lnx_b,
                *, tm, tb):
    mp = n_p * t_p
    chunk_p = min(64, t_p)
    chunk_s = min(64, t_s)
    assert mp % tm == 0 and t_p % chunk_p == 0 and t_s % chunk_s == 0
    pbp = pb[:mp].reshape(n_p, t_p, PB_COLS)
    pbs = pb[mp:].reshape(n_s, t_s, PB_COLS)
    sh_p = jnp.concatenate([jnp.zeros((n_p, 1, PB_COLS), F32), pbp[:, :-1]], axis=1)
    sh_s = jnp.concatenate([_rwkv_cols(prev_s), pbs[:, :-1]], axis=1)
    ps = jnp.concatenate([sh_p.reshape(mp, PB_COLS), sh_s.reshape(n_s * t_s, PB_COLS)], axis=0)
    r, kf, v, g, rt, at, bt, kt, bh, kh, pc = _rwkv_prep(
        pb, ps, _rwkv_cols(mu.reshape(1, B_COLS)), w0, _pad_rows(w2, LANES).astype(BF16), a0,
        _pad_rows(a2, LANES).astype(BF16), g2.astype(BF16), k_k, k_a, tm=tm, n_prompt_tiles=mp // tm,
        chunk_p=chunk_p, chunk_s=chunk_s)
    z0_p = _rwkv_state_to_pairs(jnp.zeros((n_p, B_HEADS, B_DH, B_DH), F32))
    y_p, z_p = _rwkv_chunks(rt, at, bt, kt, bh, kh, v, pc, z0_p, 0, n_p, t_p, tb=tb, chunk=chunk_p)
    y_s, z_s = _rwkv_chunks(rt, at, bt, kt, bh, kh, v, pc, _rwkv_state_to_pairs(state_s), mp, n_s, t_s,
                            tb=t_s, chunk=chunk_s)
    y = jnp.concatenate([y_p, y_s], axis=0)
    ob = _rwkv_post(y, r, kf, v, g, lnx_w, lnx_b, r_k.reshape(B_W), tm=tm)
    return ob, _rwkv_state_from_pairs(z_p), _rwkv_state_from_pairs(z_s)


PC_Z = C_CONV_CH
PC_BETA = PC_Z + C_V
PC_COLS = PC_BETA + 2 * LANES
HALO = 8


def _chunk_cumsum(g, chunk):
    rowc = jnp.bitwise_and(lax.broadcasted_iota(jnp.int32, g.shape, 0), chunk - 1)
    k = 1
    while k < chunk:
        g = g + jnp.where(rowc >= k, pltpu.roll(g, k, 0), 0.0)
        k *= 2
    return g


def _transpose_rows(a):
    c = a.shape[0]
    if c < LANES:
        a = jnp.concatenate([a, jnp.zeros((LANES - c, LANES), F32)], axis=0)
    return a.T


def _pad_chunk_rows(a):
    c = a.shape[0]
    return a if c == LANES else jnp.concatenate([a, jnp.zeros((LANES - c, a.shape[1]), a.dtype)], axis=0)


def _gdn_prep_kernel(x_ref, halo_ref, st_ref, cw_ref, sm_ref, alog_ref, dtb_ref,
                     q_ref, k_ref, v_ref, beta_ref, gc_ref, *, chunk):
    i = pl.program_id(1)
    x = x_ref[...]
    tm = x.shape[0]
    halo = jnp.where(i == 0, st_ref[...], halo_ref[...])
    row8 = lax.broadcasted_iota(jnp.int32, (HALO, x.shape[1]), 0)
    acc = x * cw_ref[C_CONV - 1:C_CONV, :]
    for s in range(1, C_CONV):
        xs = pltpu.roll(x, s, 0)
        first = jnp.where(row8 < s, pltpu.roll(halo, s, 0), xs[:HALO])
        xs = first if tm == HALO else jnp.concatenate([first, xs[HALO:]], axis=0)
        acc = acc + xs * cw_ref[C_CONV - 1 - s:C_CONV - s, :]
    y = _silu(acc)
    q_ref[...] = _group_norm_scale(y[:, :C_QK], C_DH, 1.0, EPS) * (C_DH ** -0.5)
    k_ref[...] = _group_norm_scale(y[:, C_QK:2 * C_QK], C_DH, 1.0, EPS)
    v_ref[...] = y[:, 2 * C_QK:]
    sm = sm_ref[...]
    beta_ref[...] = _sigmoid(sm[:, :LANES])
    g = -jnp.exp(alog_ref[...]) * _softplus(sm[:, LANES:] + dtb_ref[...])
    gc_ref[...] = _chunk_cumsum(g, chunk)


def _gdn_prep(pc, conv_state, row0, n_seq, t, conv_w, a_log, dt_bias, *, tm, chunk):
    assert t % tm == 0 and row0 % tm == 0 and tm % HALO == 0 and tm % chunk == 0 and chunk & (chunk - 1) == 0
    nb = t // tm
    rb = row0 // tm
    hb = tm // HALO
    tok_in = lambda n, cb: pl.BlockSpec((tm, n), lambda s, i: (rb + s * nb + i, cb))
    tok_out = lambda n: pl.BlockSpec((tm, n), lambda s, i: (s * nb + i, 0))
    row = lambda n: pl.BlockSpec((1, n), lambda s, i: (0, 0))
    sds = lambda n: jax.ShapeDtypeStruct((n_seq * t, n), F32)
    lanes8 = lambda a: jnp.zeros((1, LANES), F32).at[0, :C_V_HEADS].set(a)
    return pl.pallas_call(
        functools.partial(_gdn_prep_kernel, chunk=chunk),
        grid=(n_seq, nb),
        in_specs=[
            tok_in(C_CONV_CH, 0),
            pl.BlockSpec((HALO, C_CONV_CH), lambda s, i: (jnp.maximum((rb + s * nb + i) * hb - 1, 0), 0)),
            pl.BlockSpec((None, HALO, C_CONV_CH), lambda s, i: (s, 0, 0)),
            pl.BlockSpec((C_CONV, C_CONV_CH), lambda s, i: (0, 0)),
            tok_in(2 * LANES, PC_BETA // (2 * LANES)),
            row(LANES), row(LANES),
        ],
        out_specs=[tok_out(C_QK), tok_out(C_QK), tok_out(C_V), tok_out(LANES), tok_out(LANES)],
        out_shape=[sds(C_QK), sds(C_QK), sds(C_V), sds(LANES), sds(LANES)],
        compiler_params=_cparams(("parallel", "arbitrary")),
        name="gdn_prep",
    )(pc, pc, conv_state, conv_w, pc, lanes8(a_log), lanes8(dt_bias))


def _gdn_chunk_kernel(q_ref, k_ref, v_ref, beta_ref, gc_ref, z_ref, on_ref, s0_ref, o_ref, so_ref, s_scr, *, chunk):
    h = pl.program_id(1)
    i = pl.program_id(2)
    tb = q_ref.shape[0]

    @pl.when(i == 0)
    def _():
        s_scr[...] = s0_ref[...]

    lane = lax.broadcasted_iota(jnp.int32, (chunk, LANES), 1)
    ri = lax.broadcasted_iota(jnp.int32, (chunk, chunk), 0)
    ci = lax.broadcasted_iota(jnp.int32, (chunk, chunk), 1)
    nt = (((1,), (1,)), ((), ()))
    ones = jnp.ones((chunk, LANES), BF16)
    for c in range(tb // chunk):
        rs = slice(c * chunk, (c + 1) * chunk)
        q, k, v = q_ref[rs, :], k_ref[rs, :], v_ref[rs, :]
        beta = jnp.sum(jnp.where(lane == h, beta_ref[rs, :], 0.0), axis=-1, keepdims=True)
        gcol = jnp.sum(jnp.where(lane == h, gc_ref[rs, :], 0.0), axis=-1, keepdims=True)
        g0 = jnp.where(lane == 0, gcol, 0.0)
        grow = sum(lax.dot_general(ones, part, nt, preferred_element_type=F32) for part in _split3(g0))
        dec = jnp.exp(jnp.where(ri >= ci, gcol - grow, NEG_INF))
        kb = k * beta
        kbf = k.astype(BF16)
        lmat = jnp.where(ri > ci, lax.dot_general(kb.astype(BF16), kbf, nt, preferred_element_type=F32) * dec, 0.0)
        x = _solve_unit_lower(lmat, jnp.concatenate([v * beta, kb * jnp.exp(gcol)], axis=1))
        s = s_scr[...]
        sb = s.astype(BF16)
        v_new = x[:, :C_DH] - jnp.dot(x[:, C_DH:].astype(BF16), sb, preferred_element_type=F32)
        attn = lax.dot_general(q.astype(BF16), kbf, nt, preferred_element_type=F32) * dec
        o = jnp.dot((q * jnp.exp(gcol)).astype(BF16), sb, preferred_element_type=F32)
        o = o + jnp.dot(attn.astype(BF16), v_new.astype(BF16), preferred_element_type=F32)
        glast = gcol[chunk - 1:chunk, :]
        kdt = _transpose_rows(k * jnp.exp(glast - gcol)).astype(BF16)
        s_scr[...] = s * jnp.exp(glast) + jnp.dot(kdt, _pad_chunk_rows(v_new).astype(BF16),
                                                  preferred_element_type=F32)
        o = _rms_rows(o, on_ref[...]) * _silu(z_ref[rs, :])
        o_ref[rs, :] = o.astype(BF16)

    @pl.when(i == pl.num_programs(2) - 1)
    def _():
        so_ref[...] = s_scr[...]


def _gdn_chunks(q, k, v, beta, gc, pc, row0, n_seq, t, onorm, s0, *, tb, chunk):
    assert t % tb == 0 and row0 % tb == 0 and tb % chunk == 0
    nb = t // tb
    rb = row0 // tb
    rep = C_V_HEADS // C_QK_HEADS
    tok = lambda cb: pl.BlockSpec((tb, LANES), cb)
    st = pl.BlockSpec((None, None, C_DH, C_DH), lambda s, h, i: (s, h, 0, 0))
    return pl.pallas_call(
        functools.partial(_gdn_chunk_kernel, chunk=chunk),
        grid=(n_seq, C_V_HEADS, nb),
        in_specs=[
            tok(lambda s, h, i: (s * nb + i, h // rep)),
            tok(lambda s, h, i: (s * nb + i, h // rep)),
            tok(lambda s, h, i: (s * nb + i, h)),
            tok(lambda s, h, i: (s * nb + i, 0)),
            tok(lambda s, h, i: (s * nb + i, 0)),
            tok(lambda s, h, i: (rb + s * nb + i, PC_Z // LANES + h)),
            pl.BlockSpec((1, LANES), lambda s, h, i: (0, 0)),
            st,
        ],
        out_specs=[tok(lambda s, h, i: (s * nb + i, h)), st],
        out_shape=[jax.ShapeDtypeStruct((n_seq * t, C_V), BF16),
                   jax.ShapeDtypeStruct((n_seq, C_V_HEADS, C_DH, C_DH), F32)],
        scratch_shapes=[pltpu.VMEM((C_DH, C_DH), F32)],
        compiler_params=_cparams(("parallel", "parallel", "arbitrary")),
        name="gdn_chunks",
    )(q, k, v, beta, gc, pc, onorm.reshape(1, LANES), s0)


def _gdn_group(pc, conv_state3, s0, row0, n_seq, t, conv_w, a_log, dt_bias, onorm, *, tm, tb):
    chunk = C_CHUNK if t % C_CHUNK == 0 else t
    conv_state = jnp.pad(conv_state3, ((0, 0), (HALO - (C_CONV - 1), 0), (0, 0)))
    q, k, v, beta, gc = _gdn_prep(pc, conv_state, row0, n_seq, t, conv_w, a_log, dt_bias, tm=tm, chunk=chunk)
    return _gdn_chunks(q, k, v, beta, gc, pc, row0, n_seq, t, onorm, s0, tb=tb, chunk=chunk)


PD_V = 2 * D_K
PD_GATE = PD_V + D_V
PD_GLR = PD_GATE + D_V
PD_COLS = PD_GLR + LANES


def _gla_kernel(q_ref, k_ref, v_ref, gate_ref, glr_ref, w2_ref, bgk_ref, on_ref, s0_ref, o_ref, so_ref, s_scr,
                *, chunk):
    i = pl.program_id(2)
    tb = q_ref.shape[0]

    @pl.when(i == 0)
    def _():
        s_scr[...] = s0_ref[...]

    pre = jnp.dot(glr_ref[...].astype(BF16), w2_ref[...], preferred_element_type=F32) + bgk_ref[...]
    bcum = _chunk_cumsum(-_softplus(-pre) * (1.0 / D_GATE_NORM), chunk)
    ri = lax.broadcasted_iota(jnp.int32, (chunk, 1), 0)
    ci = lax.broadcasted_iota(jnp.int32, (chunk, chunk), 1)
    for c in range(tb // chunk):
        rs = slice(c * chunk, (c + 1) * chunk)
        q = q_ref[rs, :] * (D_DK ** -0.5)
        k, v, bc = k_ref[rs, :], v_ref[rs, :], bcum[rs, :]
        attn = jnp.zeros((chunk, chunk), F32)
        for j in range(chunk):
            e = jnp.exp(jnp.where(ri >= j, bc - bc[j:j + 1, :], NEG_INF))
            col = jnp.sum(q * k[j:j + 1, :] * e, axis=-1, keepdims=True)
            attn = jnp.where(ci == j, col, attn)
        s = s_scr[...]
        o = jnp.dot((q * jnp.exp(bc)).astype(BF16), s.astype(BF16), preferred_element_type=F32)
        o = o + jnp.dot(attn.astype(BF16), v.astype(BF16), preferred_element_type=F32)
        blast = bc[chunk - 1:chunk, :]
        ebt = jnp.broadcast_to(jnp.exp(blast), (LANES, LANES)).T
        kdt = _transpose_rows(k * jnp.exp(blast - bc)).astype(BF16)
        s_scr[...] = s * jnp.concatenate([ebt, ebt], axis=1) + jnp.dot(
            kdt, _pad_chunk_rows(v).astype(BF16), preferred_element_type=F32)
        o = _rms_rows(o, on_ref[...]) * _silu(gate_ref[rs, :])
        o_ref[rs, :] = o.astype(BF16)

    @pl.when(i == pl.num_programs(2) - 1)
    def _():
        so_ref[...] = s_scr[...]


def _gla_group(pd, s0, row0, n_seq, t, w_gk2, b_gk, onorm, *, tb):
    chunk = D_CHUNK if t % D_CHUNK == 0 else t
    assert t % tb == 0 and row0 % tb == 0 and tb % chunk == 0 and chunk & (chunk - 1) == 0
    nb = t // tb
    rb = row0 // tb
    w2 = _pad_rows(w_gk2, LANES).astype(BF16)
    tok = lambda n, cb: pl.BlockSpec((tb, n), cb)
    st = pl.BlockSpec((None, None, D_DK, D_DV), lambda s, h, i: (s, h, 0, 0))
    return pl.pallas_call(
        functools.partial(_gla_kernel, chunk=chunk),
        grid=(n_seq, D_HEADS, nb),
        in_specs=[
            tok(D_DK, lambda s, h, i: (rb + s * nb + i, h)),
            tok(D_DK, lambda s, h, i: (rb + s * nb + i, D_K // D_DK + h)),
            tok(D_DV, lambda s, h, i: (rb + s * nb + i, PD_V // D_DV + h)),
            tok(D_DV, lambda s, h, i: (rb + s * nb + i, PD_GATE // D_DV + h)),
            tok(LANES, lambda s, h, i: (rb + s * nb + i, PD_GLR // LANES)),
            pl.BlockSpec((LANES, D_DK), lambda s, h, i: (0, h)),
            pl.BlockSpec((1, D_DK), lambda s, h, i: (0, h)),
            pl.BlockSpec((1, D_DV), lambda s, h, i: (0, 0)),
            st,
        ],
        out_specs=[tok(D_DV, lambda s, h, i: (s * nb + i, h)), st],
        out_shape=[jax.ShapeDtypeStruct((n_seq * t, D_V), BF16),
                   jax.ShapeDtypeStruct((n_seq, D_HEADS, D_DK, D_DV), F32)],
        scratch_shapes=[pltpu.VMEM((D_DK, D_DV), F32)],
        compiler_params=_cparams(("parallel", "parallel", "arbitrary")),
        name="gla_chunks",
    )(pd, pd, pd, pd, pd, w2, b_gk.reshape(1, D_K), onorm.reshape(1, D_DV), s0)


TM = 768
TM_SMALL = 256


def _cross_block(x, layer, n_p, t_p, n_s, t_s, mem_prompt, cache_mem_k, cache_mem_v, norm_mem_w, norm_cross_w,
                 x_w_q, x_w_k, x_w_v, x_w_o, x_q_norm, x_k_norm):
    d = x.shape[1]
    mp = n_p * t_p
    wkv = jnp.concatenate([x_w_k[layer], x_w_v[layer]], axis=1).astype(BF16)
    hw = jnp.concatenate([jnp.tile(x_k_norm[layer], X_HEADS), jnp.ones((X_W,), F32)]).reshape(1, 2 * X_W)
    mem = mem_prompt.reshape(n_p * N_MEM, d)
    kv = _norm_matmul(mem, norm_mem_w[layer], wkv, tm=min(512, n_p * N_MEM), tn=X_W, head_w=hw,
                      n_norm_cols=X_W, gsize=X_DH)
    mk_p = kv[:, :X_W].reshape(n_p, N_MEM, X_W)
    mv_p = kv[:, X_W:].reshape(n_p, N_MEM, X_W)
    qx = _norm_matmul(x, norm_cross_w[layer], x_w_q[layer].astype(BF16), tm=TM, tn=X_W)
    ca_p = _cross_attention(qx, 0, n_p, t_p, mk_p, mv_p, x_q_norm[layer], tq=min(512, t_p))
    ca_s = _cross_attention(qx, mp, n_s, t_s, cache_mem_k[layer].reshape(n_s, N_MEM, X_W),
                            cache_mem_v[layer].reshape(n_s, N_MEM, X_W), x_q_norm[layer], tq=t_s)
    ca = jnp.concatenate([ca_p, ca_s], axis=0)
    x = _matmul_residual([ca], [x_w_o[layer].astype(BF16)], x, tm=TM, tn=1024)
    return x, mk_p.reshape(n_p, N_MEM, X_HEADS, X_DH), mv_p.reshape(n_p, N_MEM, X_HEADS, X_DH)


def kernel(x_prompt, x_sample, cache_diff_k, cache_diff_v, state_rwkv, state_rwkv_shift, cache_mem_k, cache_mem_v, state_gdn, state_gdn_conv, state_gla, page_table, mem_prompt, norm_mix_w, norm_cross_w, norm_mem_w, norm_ffn_w, w_in_even, w_out_even, a_q_norm, a_k_norm, a_lam_q1, a_lam_k1, a_lam_q2, a_lam_k2, a_subln, b_mu, b_w0, b_w2, b_a0, b_a2, b_g2, b_k_k, b_k_a, b_r_k, b_lnx_w, b_lnx_b, ffd_w_gate, ffd_w_up, ffd_w_down, w_in_odd, w_out_odd, c_conv_w, c_a_log, c_dt_bias, c_onorm, d_w_gk2, d_b_gk, d_onorm, moe_w_router, moe_w_gate, moe_w_up, moe_w_down, x_w_q, x_w_k, x_w_v, x_w_o, x_q_norm, x_k_norm):
    n_p, t_p, d = x_prompt.shape
    n_s, t_s, _ = x_sample.shape
    mp, ms = n_p * t_p, n_s * t_s
    depth = norm_mix_w.shape[0]
    x = jnp.concatenate([x_prompt.reshape(mp, d), x_sample.reshape(ms, d)], axis=0)
    assert (mp + ms) % TM == 0 and (mp + ms) % TM_SMALL == 0
    page = cache_diff_k.shape[2]
    slopes = jnp.exp2(-(8.0 / A_HEADS) * jnp.arange(1, A_HEADS + 1, dtype=F32))
    cross_w = (mem_prompt, cache_mem_k, cache_mem_v, norm_mem_w, norm_cross_w, x_w_q, x_w_k, x_w_v, x_w_o,
               x_q_norm, x_k_norm)
    dk_p, dv_p, dk_s, dv_s, rw_p, rw_s, sh_p, sh_s = [], [], [], [], [], [], [], []
    mk_l, mv_l, gd_p, gd_s, cv_p, cv_s, gl_p, gl_s = [], [], [], [], [], [], [], []
    for layer in range(depth):
        i = layer // 2
        if layer % 2 == 0:
            lam_init = 0.8 - 0.6 * math.exp(-0.3 * layer)
            w_in = w_in_even[i]
            hw = jnp.concatenate([jnp.tile(a_q_norm[i], A_Q // A_DH), jnp.tile(a_k_norm[i], A_Q // A_DH),
                                  jnp.ones((A_COLS - 2 * A_Q,), F32)]).reshape(1, A_COLS)
            pa = _norm_matmul(x, norm_mix_w[layer], w_in[:, :A_COLS].astype(BF16), tm=TM, tn=1024, head_w=hw,
                              n_norm_cols=2 * A_Q, gsize=A_DH)
            pb = _norm_matmul(x, norm_mix_w[layer], _rwkv_cols(w_in[:, A_COLS:]).astype(BF16), tm=TM,
                              tn=PB_COLS // 2)
            lam_stack = jnp.stack([a_lam_q1[i], a_lam_k1[i], a_lam_q2[i], a_lam_k2[i]])
            oa_p = _diff_attention_prompt(pa, n_p, t_p, slopes, lam_stack, a_subln[i], lam_init, tq=min(256, t_p))
            oa_s = _diff_attention_sample(pa, mp, n_s, t_s, cache_diff_k, cache_diff_v, i, page_table, lam_stack,
                                          a_subln[i], lam_init)
            ob, st_p, st_s = _rwkv_mixer(pb, n_p, t_p, n_s, t_s, state_rwkv_shift[i], state_rwkv[i], b_mu[i],
                                         b_w0[i], b_w2[i], b_a0[i], b_a2[i], b_g2[i], b_k_k[i], b_k_a[i], b_r_k[i],
                                         b_lnx_w[i], b_lnx_b[i], tm=TM_SMALL, tb=min(256, t_p))
            dk_p.append(pa[:mp, A_Q:2 * A_Q].reshape(n_p, t_p, A_HEADS, 2 * A_DH))
            dv_p.append(pa[:mp, 2 * A_Q:].reshape(n_p, t_p, A_HEADS, 2 * A_DH))
            dk_s.append(pa[mp:, A_Q:2 * A_Q].reshape(n_s, t_s, A_HEADS, 2 * A_DH))
            dv_s.append(pa[mp:, 2 * A_Q:].reshape(n_s, t_s, A_HEADS, 2 * A_DH))
            rw_p.append(st_p)
            rw_s.append(st_s)
            sh_p.append(_rwkv_cols_inv(pb[:mp].reshape(n_p, t_p, PB_COLS)[:, -1:]))
            sh_s.append(_rwkv_cols_inv(pb[mp:].reshape(n_s, t_s, PB_COLS)[:, -1:]))
            oa = jnp.concatenate([oa_p, oa_s], axis=0)
            w_out = w_out_even[i].astype(BF16)
            x = _matmul_residual([oa, ob], [w_out[:A_Q], w_out[A_Q:]], x, tm=TM, tn=1024)
        else:
            w_in = w_in_odd[i]
            c_cols = C_CONV_CH + C_V + 2 * C_V_HEADS
            zc = jnp.zeros((d, LANES - C_V_HEADS), F32)
            wc = jnp.concatenate([w_in[:, :PC_BETA], w_in[:, PC_BETA:PC_BETA + C_V_HEADS], zc,
                                  w_in[:, PC_BETA + C_V_HEADS:c_cols], zc], axis=1).astype(BF16)
            wd_ = jnp.concatenate([w_in[:, c_cols:], jnp.zeros((d, LANES - D_GATE_LORA), F32)], axis=1).astype(BF16)
            pc = _norm_matmul(x, norm_mix_w[layer], wc, tm=TM, tn=PC_COLS // 2)
            pd = _norm_matmul(x, norm_mix_w[layer], wd_, tm=TM, tn=PD_COLS // 5)
            gdn_w = (c_conv_w[i], c_a_log[i], c_dt_bias[i], c_onorm[i])
            oc_p, gs_p = _gdn_group(pc, jnp.zeros((n_p, C_CONV - 1, C_CONV_CH), F32),
                                    jnp.zeros((n_p, C_V_HEADS, C_DH, C_DH), F32), 0, n_p, t_p, *gdn_w,
                                    tm=min(512, t_p), tb=min(256, t_p))
            oc_s, gs_s = _gdn_group(pc, state_gdn_conv[i], state_gdn[i], mp, n_s, t_s, *gdn_w, tm=t_s, tb=t_s)
            gla_w = (d_w_gk2[i], d_b_gk[i], d_onorm[i])
            od_p, ls_p = _gla_group(pd, jnp.zeros((n_p, D_HEADS, D_DK, D_DV), F32), 0, n_p, t_p, *gla_w,
                                    tb=min(128, t_p))
            od_s, ls_s = _gla_group(pd, state_gla[i], mp, n_s, t_s, *gla_w, tb=t_s)
            gd_p.append(gs_p)
            gd_s.append(gs_s)
            cv_p.append(pc[:mp].reshape(n_p, t_p, PC_COLS)[:, -(C_CONV - 1):, :C_CONV_CH])
            cv_s.append(pc[mp:].reshape(n_s, t_s, PC_COLS)[:, -(C_CONV - 1):, :C_CONV_CH])
            gl_p.append(ls_p)
            gl_s.append(ls_s)
            oc = jnp.concatenate([oc_p, oc_s], axis=0)
            od = jnp.concatenate([od_p, od_s], axis=0)
            w_out = w_out_odd[i].astype(BF16)
            x = _matmul_residual([oc, od], [w_out[:C_V], w_out[C_V:]], x, tm=TM, tn=1024)
        x, mk_p, mv_p = _cross_block(x, layer, n_p, t_p, n_s, t_s, *cross_w)
        mk_l.append(mk_p)
        mv_l.append(mv_p)
        if layer % 2 == 0:
            x = _swiglu_ffn(x, norm_ffn_w[layer], ffd_w_gate[i].astype(BF16), ffd_w_up[i].astype(BF16),
                            ffd_w_down[i].astype(BF16), tm=TM, tf=512)
        else:
            x = _moe_ffn(x, norm_ffn_w[layer], moe_w_router[i], moe_w_gate[i].astype(BF16),
                         moe_w_up[i].astype(BF16), moe_w_down[i].astype(BF16), tm=TM, tf=256)
    st = jnp.stack
    return (x[:mp].reshape(n_p, t_p, d), x[mp:].reshape(n_s, t_s, d), st(dk_p), st(dv_p), st(dk_s), st(dv_s),
            st(rw_p), st(rw_s), st(sh_p), st(sh_s), st(mk_l), st(mv_l), st(gd_p), st(gd_s), st(cv_p), st(cv_s),
            st(gl_p), st(gl_s))
```

```python
import functools
import math

import jax
import jax.numpy as jnp
from jax import lax
from jax.experimental import pallas as pl
from jax.experimental.pallas import tpu as pltpu

F32 = jnp.float32
BF16 = jnp.bfloat16
NEG_INF = float("-inf")

D_MODEL = 2048
A_HEADS = 8
A_DH = 64
A_Q = 1024
A_COLS = 3072
B_HEADS = 16
B_DH = 64
B_W = 1024
B_DECAY_LORA = 96
B_AAA_LORA = 96
B_GATE_LORA = 256
B_COLS = 3520
B_GN_EPS = 64e-5
C_QK_HEADS = 4
C_V_HEADS = 8
C_DH = 128
C_CONV = 4
C_QK = 512
C_V = 1024
C_CONV_CH = 2048
C_CHUNK = 64
D_HEADS = 4
D_DK = 128
D_DV = 256
D_K = 512
D_V = 1024
D_GATE_LORA = 16
D_GATE_NORM = 16.0
D_CHUNK = 16
N_MEM = 256
X_HEADS = 4
X_DH = 128
X_W = 512
FF_DENSE = 5632
N_EXPERTS = 8
FF_EXPERT = 2816
EPS = 1e-6

LANES = 128
VMEM_LIMIT = 56 * 1024 * 1024


def _cparams(sem):
    return pltpu.CompilerParams(dimension_semantics=sem, vmem_limit_bytes=VMEM_LIMIT)


def _sigmoid(x):
    return 1.0 / (1.0 + jnp.exp(-x))


def _silu(x):
    return x * _sigmoid(x)


def _softplus(x):
    return jnp.maximum(x, 0.0) + jnp.log(1.0 + jnp.exp(-jnp.abs(x)))


def _rms_rows(x, w):
    ms = jnp.mean(x * x, axis=-1, keepdims=True)
    return x * lax.rsqrt(ms + EPS) * w


def _group_sum(x, gsize):
    lane = lax.broadcasted_iota(jnp.int32, (1, LANES), 1)
    outs = []
    for c in range(x.shape[1] // LANES):
        xc = x[:, c * LANES:(c + 1) * LANES]
        if gsize == LANES:
            ss = jnp.broadcast_to(jnp.sum(xc, axis=-1, keepdims=True), xc.shape)
        else:
            lo = jnp.sum(jnp.where(lane < 64, xc, 0.0), axis=-1, keepdims=True)
            hi = jnp.sum(jnp.where(lane >= 64, xc, 0.0), axis=-1, keepdims=True)
            ss = jnp.where(lane < 64, lo, hi)
        outs.append(ss)
    return outs[0] if len(outs) == 1 else jnp.concatenate(outs, axis=1)


def _group_norm_scale(x, gsize, inv_n, eps):
    return x * lax.rsqrt(_group_sum(x * x, gsize) * inv_n + eps)


def _split3(a):
    a1 = a.astype(BF16)
    r1 = a - a1.astype(F32)
    a2 = r1.astype(BF16)
    a3 = (r1 - a2.astype(F32)).astype(BF16)
    return a1, a2, a3


def _dot_f32(a, b):
    a1, a2, a3 = _split3(a)
    b1, b2, b3 = _split3(b)
    d = lambda x, y: jnp.dot(x, y, preferred_element_type=F32)
    return d(a1, b3) + d(a3, b1) + d(a2, b2) + d(a1, b2) + d(a2, b1) + d(a1, b1)


def _norm_mm_kernel(x_ref, nw_ref, w_ref, hw_ref, o_ref, h_scr, *, n_norm_tiles, gsize):
    j = pl.program_id(1)

    @pl.when(j == 0)
    def _():
        h_scr[...] = _rms_rows(x_ref[...], nw_ref[...]).astype(BF16)

    acc = jnp.dot(h_scr[...], w_ref[...], preferred_element_type=F32)
    if n_norm_tiles == 0:
        o_ref[...] = acc
    else:
        @pl.when(j < n_norm_tiles)
        def _():
            o_ref[...] = _group_norm_scale(acc, gsize, 1.0 / gsize, EPS) * hw_ref[...]

        @pl.when(j >= n_norm_tiles)
        def _():
            o_ref[...] = acc


def _norm_matmul(x, nw, w, *, tm, tn, head_w=None, n_norm_cols=0, gsize=LANES):
    m, k = x.shape
    n = w.shape[1]
    assert m % tm == 0 and n % tn == 0 and n_norm_cols % tn == 0
    if head_w is None:
        head_w = jnp.ones((1, n), F32)
    kern = functools.partial(_norm_mm_kernel, n_norm_tiles=n_norm_cols // tn, gsize=gsize)
    return pl.pallas_call(
        kern,
        grid=(m // tm, n // tn),
        in_specs=[
            pl.BlockSpec((tm, k), lambda i, j: (i, 0)),
            pl.BlockSpec((1, k), lambda i, j: (0, 0)),
            pl.BlockSpec((k, tn), lambda i, j: (0, j)),
            pl.BlockSpec((1, tn), lambda i, j: (0, j)),
        ],
        out_specs=pl.BlockSpec((tm, tn), lambda i, j: (i, j)),
        out_shape=jax.ShapeDtypeStruct((m, n), F32),
        scratch_shapes=[pltpu.VMEM((tm, k), BF16)],
        compiler_params=_cparams(("parallel", "arbitrary")),
        name="norm_matmul",
    )(x, nw.reshape(1, k), w, head_w)


def _mm_res_kernel(*refs, n_in):
    x_ref = refs[2 * n_in]
    o_ref = refs[2 * n_in + 1]
    acc = x_ref[...]
    for t in range(n_in):
        acc = acc + jnp.dot(refs[t][...], refs[n_in + t][...], preferred_element_type=F32)
    o_ref[...] = acc


def _matmul_residual(acts, weights, x, *, tm, tn):
    m, n = x.shape
    n_in = len(acts)
    in_specs = [pl.BlockSpec((tm, a.shape[1]), lambda i, j: (i, 0)) for a in acts]
    in_specs += [pl.BlockSpec((w.shape[0], tn), lambda i, j: (0, j)) for w in weights]
    in_specs += [pl.BlockSpec((tm, tn), lambda i, j: (i, j))]
    return pl.pallas_call(
        functools.partial(_mm_res_kernel, n_in=n_in),
        grid=(m // tm, n // tn),
        in_specs=in_specs,
        out_specs=pl.BlockSpec((tm, tn), lambda i, j: (i, j)),
        out_shape=jax.ShapeDtypeStruct((m, n), F32),
        compiler_params=_cparams(("parallel", "arbitrary")),
        name="matmul_residual",
    )(*acts, *weights, x)


def _swiglu_kernel(x_ref, nw_ref, wg_ref, wu_ref, wd_ref, o_ref, h_scr):
    f = pl.program_id(1)

    @pl.when(f == 0)
    def _():
        x = x_ref[...]
        h_scr[...] = _rms_rows(x, nw_ref[...]).astype(BF16)
        o_ref[...] = x

    h = h_scr[...]
    g = jnp.dot(h, wg_ref[...], preferred_element_type=F32)
    u = jnp.dot(h, wu_ref[...], preferred_element_type=F32)
    a = (_silu(g) * u).astype(BF16)
    o_ref[...] += jnp.dot(a, wd_ref[...], preferred_element_type=F32)


def _swiglu_ffn(x, nw, wg, wu, wd, *, tm, tf):
    m, d = x.shape
    ff = wg.shape[1]
    return pl.pallas_call(
        _swiglu_kernel,
        grid=(m // tm, ff // tf),
        in_specs=[
            pl.BlockSpec((tm, d), lambda i, f: (i, 0)),
            pl.BlockSpec((1, d), lambda i, f: (0, 0)),
            pl.BlockSpec((d, tf), lambda i, f: (0, f)),
            pl.BlockSpec((d, tf), lambda i, f: (0, f)),
            pl.BlockSpec((tf, d), lambda i, f: (f, 0)),
        ],
        out_specs=pl.BlockSpec((tm, d), lambda i, f: (i, 0)),
        out_shape=jax.ShapeDtypeStruct((m, d), F32),
        scratch_shapes=[pltpu.VMEM((tm, d), BF16)],
        compiler_params=_cparams(("parallel", "arbitrary")),
        name="swiglu_ffn",
    )(x, nw.reshape(1, d), wg, wu, wd)


MOE_TILE = 512
MOE_DMA_WINDOW = 16
ROW_SLABS = D_MODEL // LANES


def _rows_to_slabs(ref, x):
    for j in range(ROW_SLABS):
        ref[:, j, :] = x[:, j * LANES:(j + 1) * LANES]


def _slabs_to_rows(ref):
    return jnp.concatenate([ref[:, j, :] for j in range(ROW_SLABS)], axis=1)


def _moe_router_kernel(x_ref, nw_ref, wr_ref, h_ref, info_ref):
    tm = x_ref.shape[0]
    lane = lax.broadcasted_iota(jnp.int32, (tm, LANES), 1)
    hf = _rms_rows(x_ref[...], nw_ref[...])
    _rows_to_slabs(h_ref, hf)
    logits = jnp.where(lane < N_EXPERTS, _dot_f32(hf, wr_ref[...]), NEG_INF)
    m1 = jnp.max(logits, axis=-1, keepdims=True)
    i1 = jnp.min(jnp.where(logits == m1, lane, LANES), axis=-1, keepdims=True)
    rest = jnp.where(lane == i1, NEG_INF, logits)
    m2 = jnp.max(rest, axis=-1, keepdims=True)
    i2 = jnp.min(jnp.where(rest == m2, lane, LANES), axis=-1, keepdims=True)
    e2 = jnp.exp(m2 - m1)
    vals = [i1.astype(F32), i2.astype(F32), 1.0 / (1.0 + e2), e2 / (1.0 + e2)]
    info = jnp.zeros((tm, LANES), F32)
    for k, val in enumerate(vals):
        info = jnp.where(lane == k, val, info)
    info_ref[...] = info


def _moe_gather_kernel(dest_ref, h_ref, zero_ref, xs_ref, sem):
    del zero_ref
    n2 = dest_ref.shape[0]
    n = n2 // 2
    win = MOE_DMA_WINDOW

    def copy(a):
        tok = jnp.where(a >= n, a - n, a)
        return pltpu.make_async_copy(h_ref.at[tok], xs_ref.at[dest_ref[a]], sem.at[a % win])

    def issue(a, carry):
        @pl.when(a >= win)
        def _():
            copy(a - win).wait()
        copy(a).start()
        return carry

    def drain(a, carry):
        copy(a).wait()
        return carry

    lax.fori_loop(0, n2, issue, 0)
    lax.fori_loop(n2 - win, n2, drain, 0)


def _moe_expert_kernel(te_ref, nu_ref, xs_ref, wg_ref, wu_ref, wd_ref, y_ref, xb_scr, acc_scr):
    del te_ref
    t = pl.program_id(0)
    f = pl.program_id(1)

    @pl.when(f == 0)
    def _():
        xb_scr[...] = _slabs_to_rows(xs_ref).astype(BF16)
        acc_scr[...] = jnp.zeros(acc_scr.shape, F32)

    @pl.when(t < nu_ref[0])
    def _():
        xb = xb_scr[...]
        g = jnp.dot(xb, wg_ref[...], preferred_element_type=F32)
        u = jnp.dot(xb, wu_ref[...], preferred_element_type=F32)
        acc_scr[...] += jnp.dot((_silu(g) * u).astype(BF16), wd_ref[...], preferred_element_type=F32)

    @pl.when(f == pl.num_programs(1) - 1)
    def _():
        _rows_to_slabs(y_ref, acc_scr[...])


def _moe_combine_kernel(dest_ref, x_ref, info_ref, y_ref, o_ref, ybuf, sem):
    i = pl.program_id(0)
    tc = x_ref.shape[0]
    n = dest_ref.shape[0] // 2

    def copy(r, slot):
        return pltpu.make_async_copy(y_ref.at[dest_ref[slot * n + i * tc + r]], ybuf.at[slot, r], sem.at[slot])

    def issue(r, carry):
        copy(r, 0).start()
        copy(r, 1).start()
        return carry

    def drain(r, carry):
        copy(r, 0).wait()
        copy(r, 1).wait()
        return carry

    lax.fori_loop(0, tc, issue, 0)
    lax.fori_loop(0, tc, drain, 0)
    info = info_ref[...]
    y1 = _slabs_to_rows(ybuf.at[0])
    y2 = _slabs_to_rows(ybuf.at[1])
    o_ref[...] = x_ref[...] + info[:, 2:3] * y1 + info[:, 3:4] * y2


def _moe_ffn(x, nw, w_router, wg, wu, wd, *, tm, tf, tc):
    m, d = x.shape
    ne, _, ff = wg.shape
    tg = MOE_TILE
    assert d == D_MODEL and m % tm == 0 and m % tc == 0 and ff % tf == 0
    n_tiles = -(-(2 * m + ne * (tg - 1)) // tg)
    p_rows = n_tiles * tg
    wr = jnp.zeros((d, LANES), F32).at[:, :ne].set(w_router)
    h3, info = pl.pallas_call(
        _moe_router_kernel,
        grid=(m // tm,),
        in_specs=[pl.BlockSpec((tm, d), lambda i: (i, 0)), pl.BlockSpec((1, d), lambda i: (0, 0)),
                  pl.BlockSpec((d, LANES), lambda i: (0, 0))],
        out_specs=[pl.BlockSpec((tm, ROW_SLABS, LANES), lambda i: (i, 0, 0)),
                   pl.BlockSpec((tm, LANES), lambda i: (i, 0))],
        out_shape=[jax.ShapeDtypeStruct((m, ROW_SLABS, LANES), F32), jax.ShapeDtypeStruct((m, LANES), F32)],
        compiler_params=_cparams(("parallel",)),
        name="moe_router",
    )(x, nw.reshape(1, d), wr)

    e12 = info[:, :2].astype(jnp.int32)
    onehot = (e12[:, :1] == jnp.arange(ne)) | (e12[:, 1:2] == jnp.arange(ne))
    csum = jnp.cumsum(onehot.astype(jnp.int32), axis=0)
    rank = csum - onehot
    padded = (csum[-1] + tg - 1) // tg * tg
    gend = jnp.cumsum(padded)
    base = (gend - padded)[None, :] + rank
    dest = jnp.concatenate([jnp.take_along_axis(base, e12[:, :1], axis=1)[:, 0],
                            jnp.take_along_axis(base, e12[:, 1:2], axis=1)[:, 0]]).astype(jnp.int32)
    tile_expert = jnp.minimum(jnp.sum(jnp.arange(n_tiles)[:, None] * tg >= gend[None, :], axis=1),
                              ne - 1).astype(jnp.int32)
    n_used = (gend[-1:] // tg).astype(jnp.int32)

    any_spec = pl.BlockSpec(memory_space=pl.ANY)
    xs3 = pl.pallas_call(
        _moe_gather_kernel,
        grid_spec=pltpu.PrefetchScalarGridSpec(
            num_scalar_prefetch=1, grid=(1,), in_specs=[any_spec, any_spec], out_specs=any_spec,
            scratch_shapes=[pltpu.SemaphoreType.DMA((MOE_DMA_WINDOW,))]),
        out_shape=jax.ShapeDtypeStruct((p_rows, ROW_SLABS, LANES), F32),
        input_output_aliases={2: 0},
        name="moe_gather",
    )(dest, h3, jnp.zeros((p_rows, ROW_SLABS, LANES), F32))

    last = lambda t, nu: jnp.minimum(t, nu[0] - 1)
    y3 = pl.pallas_call(
        _moe_expert_kernel,
        grid_spec=pltpu.PrefetchScalarGridSpec(
            num_scalar_prefetch=2,
            grid=(n_tiles, ff // tf),
            in_specs=[
                pl.BlockSpec((tg, ROW_SLABS, LANES), lambda t, f, te, nu: (last(t, nu), 0, 0)),
                pl.BlockSpec((None, d, tf), lambda t, f, te, nu: (te[last(t, nu)], 0, f)),
                pl.BlockSpec((None, d, tf), lambda t, f, te, nu: (te[last(t, nu)], 0, f)),
                pl.BlockSpec((None, tf, d), lambda t, f, te, nu: (te[last(t, nu)], f, 0)),
            ],
            out_specs=pl.BlockSpec((tg, ROW_SLABS, LANES), lambda t, f, te, nu: (t, 0, 0)),
            scratch_shapes=[pltpu.VMEM((tg, d), BF16), pltpu.VMEM((tg, d), F32)]),
        out_shape=jax.ShapeDtypeStruct((p_rows, ROW_SLABS, LANES), F32),
        compiler_params=_cparams(("arbitrary", "arbitrary")),
        name="moe_experts",
    )(tile_expert, n_used, xs3, wg, wu, wd)

    return pl.pallas_call(
        _moe_combine_kernel,
        grid_spec=pltpu.PrefetchScalarGridSpec(
            num_scalar_prefetch=1,
            grid=(m // tc,),
            in_specs=[pl.BlockSpec((tc, d), lambda i, de: (i, 0)), pl.BlockSpec((tc, LANES), lambda i, de: (i, 0)),
                      any_spec],
            out_specs=pl.BlockSpec((tc, d), lambda i, de: (i, 0)),
            scratch_shapes=[pltpu.VMEM((2, tc, ROW_SLABS, LANES), F32), pltpu.SemaphoreType.DMA((2,))]),
        out_shape=jax.ShapeDtypeStruct((m, d), F32),
        compiler_params=_cparams(("arbitrary",)),
        name="moe_combine",
    )(dest, x, info, y3)


def _diff_lambda_vec(lam_ref, lam_init):
    lv = lam_ref[...]
    d1 = jnp.sum(lv[0:1, :] * lv[1:2, :], axis=-1, keepdims=True)
    d2 = jnp.sum(lv[2:3, :] * lv[3:4, :], axis=-1, keepdims=True)
    return jnp.exp(d1) - jnp.exp(d2) + lam_init


def _softmax_rows(s):
    m = jnp.max(s, axis=-1, keepdims=True)
    e = jnp.exp(s - m)
    return e, jnp.sum(e, axis=-1, keepdims=True)


def _diffattn_prompt_kernel(slope_ref, q_ref, k_ref, v_ref, lam_ref, subln_ref, o_ref, *, tq, lam_init):
    t = q_ref.shape[0]
    h = pl.program_id(1)
    slope = slope_ref[h]
    lane = lax.broadcasted_iota(jnp.int32, (1, LANES), 1)
    lam = _diff_lambda_vec(lam_ref, lam_init)
    kb = k_ref[...].astype(BF16)
    vb = v_ref[...].astype(BF16)
    nt = (((1,), (1,)), ((), ()))
    for qi in range(t // tq):
        n_keys = (qi + 1) * tq
        q = q_ref[qi * tq:(qi + 1) * tq, :] * (A_DH ** -0.5)
        q1 = jnp.where(lane < A_DH, q, 0.0).astype(BF16)
        q2 = jnp.where(lane >= A_DH, q, 0.0).astype(BF16)
        kk = kb[:n_keys]
        row = lax.broadcasted_iota(jnp.int32, (tq, n_keys), 0) + qi * tq
        col = lax.broadcasted_iota(jnp.int32, (tq, n_keys), 1)
        dist = (row - col).astype(F32)
        bias = jnp.where(dist >= 0.0, -slope * dist, NEG_INF)
        e1, l1 = _softmax_rows(lax.dot_general(q1, kk, nt, preferred_element_type=F32) + bias)
        e2, l2 = _softmax_rows(lax.dot_general(q2, kk, nt, preferred_element_type=F32) + bias)
        p = e1 * (1.0 / l1) - e2 * (lam / l2)
        o = jnp.dot(p.astype(BF16), vb[:n_keys], preferred_element_type=F32)
        o = _rms_rows(o, subln_ref[...]) * (1.0 - lam_init)
        o_ref[qi * tq:(qi + 1) * tq, :] = o.astype(BF16)


def _diff_attention_prompt(pa, n_batch, t, slopes, lam_stack, subln, lam_init, *, tq):
    kern = functools.partial(_diffattn_prompt_kernel, tq=tq, lam_init=lam_init)
    return pl.pallas_call(
        kern,
        grid=(n_batch, A_HEADS),
        in_specs=[
            pl.BlockSpec(memory_space=pltpu.SMEM),
            pl.BlockSpec((t, LANES), lambda b, h: (b, h)),
            pl.BlockSpec((t, LANES), lambda b, h: (b, A_HEADS + h)),
            pl.BlockSpec((t, LANES), lambda b, h: (b, 2 * A_HEADS + h)),
            pl.BlockSpec((4, A_DH), lambda b, h: (0, 0)),
            pl.BlockSpec((1, LANES), lambda b, h: (0, 0)),
        ],
        out_specs=pl.BlockSpec((t, LANES), lambda b, h: (b, h)),
        out_shape=jax.ShapeDtypeStruct((n_batch * t, A_HEADS * LANES), BF16),
        compiler_params=_cparams(("parallel", "parallel")),
        name="diff_attention_prompt",
    )(slopes, pa, pa, pa, lam_stack, subln.reshape(1, LANES))


PAGES_PER_STEP = 4


def _diffattn_sample_kernel(pt_ref, q_ref, kn_ref, vn_ref, *rest, past, lam_init):
    del pt_ref
    npg = PAGES_PER_STEP
    k_refs, v_refs = rest[:npg], rest[npg:2 * npg]
    lam_ref, subln_ref, o_ref, qq_scr, m_scr, l_scr, acc_scr = rest[2 * npg:]
    g = pl.program_id(1)
    ts = q_ref.shape[0]
    page = k_refs[0].shape[0]
    lane = lax.broadcasted_iota(jnp.int32, (1, LANES), 1)
    hsub = lax.broadcasted_iota(jnp.int32, (A_HEADS, LANES), 0)
    col = lax.broadcasted_iota(jnp.int32, (A_HEADS, LANES), 1)
    valid = (col // (2 * ts)) == hsub
    pos_q = past + jnp.bitwise_and(col, ts - 1)
    slope = jnp.exp2(-(8.0 / A_HEADS) * (hsub + 1).astype(F32))
    nt = (((1,), (1,)), ((), ()))
    tn = (((0,), (0,)), ((), ()))

    @pl.when(g == 0)
    def _():
        rows = []
        for h in range(A_HEADS):
            q = q_ref[:, h * LANES:(h + 1) * LANES] * (A_DH ** -0.5)
            rows += [jnp.where(lane < A_DH, q, 0.0), jnp.where(lane >= A_DH, q, 0.0)]
        qq_scr[...] = jnp.concatenate(rows, axis=0).astype(BF16)
        m_scr[...] = jnp.where(valid, NEG_INF, 0.0)
        l_scr[...] = jnp.zeros(l_scr.shape, F32)
        acc_scr[...] = jnp.zeros(acc_scr.shape, F32)

    def absorb(k3, v3, key0, causal):
        nk = k3.shape[0]
        k2 = k3.reshape(nk * A_HEADS, LANES).astype(BF16)
        v2 = v3.reshape(nk * A_HEADS, LANES).astype(BF16)
        s = lax.dot_general(k2, qq_scr[...], nt, preferred_element_type=F32).reshape(nk, A_HEADS, LANES)
        kidx = lax.broadcasted_iota(jnp.int32, (nk * A_HEADS, LANES), 0) // A_HEADS
        dist = (pos_q[None] - (key0 + kidx).reshape(nk, A_HEADS, LANES)).astype(F32)
        keep = valid[None] & (dist >= 0.0) if causal else valid[None]
        s = jnp.where(keep, s - slope[None] * dist, NEG_INF)
        m_old = m_scr[...]
        m_new = jnp.maximum(m_old, jnp.max(s, axis=0))
        alpha = jnp.exp(m_old - m_new)
        pe = jnp.exp(s - m_new[None])
        l_scr[...] = alpha * l_scr[...] + jnp.sum(pe, axis=0)
        m_scr[...] = m_new
        alpha_row = jnp.sum(jnp.where(valid, alpha, 0.0), axis=0, keepdims=True)
        pv = lax.dot_general(v2, pe.reshape(nk * A_HEADS, LANES).astype(BF16), tn, preferred_element_type=F32)
        acc_scr[...] = alpha_row * acc_scr[...] + pv

    for j in range(npg):
        absorb(k_refs[j][...], v_refs[j][...], (g * npg + j) * page, False)

    @pl.when(g == pl.num_programs(1) - 1)
    def _():
        absorb(kn_ref[...], vn_ref[...], past, True)
        lam = _diff_lambda_vec(lam_ref, lam_init)
        l_row = jnp.sum(jnp.where(valid, l_scr[...], 0.0), axis=0, keepdims=True)
        ot = acc_scr[...] / l_row
        ot = ot - lam * pltpu.roll(ot, LANES - ts, 1)
        o = ot.T
        for h in range(A_HEADS):
            oh = _rms_rows(o[h * 2 * ts:h * 2 * ts + ts, :], subln_ref[...]) * (1.0 - lam_init)
            o_ref[:, h * LANES:(h + 1) * LANES] = oh.astype(BF16)


def _diff_attention_sample(pa, row0, n_batch, ts, cache_k, cache_v, layer_idx, page_table, lam_stack, subln,
                           lam_init):
    n_pages = page_table.shape[1]
    page = cache_k.shape[2]
    npg = PAGES_PER_STEP
    assert row0 % ts == 0 and n_pages % npg == 0 and 2 * ts * A_HEADS == LANES
    rb = row0 // ts
    w = A_HEADS * LANES
    new_rows = lambda c0: pa[row0:, c0:c0 + w].reshape(n_batch, ts, A_HEADS, LANES)
    kern = functools.partial(_diffattn_sample_kernel, past=n_pages * page, lam_init=lam_init)
    page_spec = lambda j: pl.BlockSpec((None, None, page, A_HEADS, LANES),
                                       lambda b, g, pt: (layer_idx, pt[b, g * npg + j], 0, 0, 0))
    new_spec = pl.BlockSpec((None, ts, A_HEADS, LANES), lambda b, g, pt: (b, 0, 0, 0))
    grid_spec = pltpu.PrefetchScalarGridSpec(
        num_scalar_prefetch=1,
        grid=(n_batch, n_pages // npg),
        in_specs=[pl.BlockSpec((ts, w), lambda b, g, pt: (rb + b, 0)), new_spec, new_spec]
        + [page_spec(j) for j in range(npg)] * 2
        + [pl.BlockSpec((4, A_DH), lambda b, g, pt: (0, 0)), pl.BlockSpec((1, LANES), lambda b, g, pt: (0, 0))],
        out_specs=pl.BlockSpec((ts, w), lambda b, g, pt: (b, 0)),
        scratch_shapes=[
            pltpu.VMEM((LANES, LANES), BF16),
            pltpu.VMEM((A_HEADS, LANES), F32),
            pltpu.VMEM((A_HEADS, LANES), F32),
            pltpu.VMEM((LANES, LANES), F32),
        ],
    )
    return pl.pallas_call(
        kern,
        grid_spec=grid_spec,
        out_shape=jax.ShapeDtypeStruct((n_batch * ts, w), BF16),
        compiler_params=_cparams(("parallel", "arbitrary")),
        name="diff_attention_sample",
    )(page_table, pa, new_rows(w), new_rows(2 * w), *([cache_k] * npg), *([cache_v] * npg), lam_stack,
      subln.reshape(1, LANES))


def _cross_attn_kernel(q_ref, k_ref, v_ref, qn_ref, o_ref):
    nt = (((1,), (1,)), ((), ()))
    for h in range(X_HEADS):
        sl = slice(h * LANES, (h + 1) * LANES)
        q = _rms_rows(q_ref[:, sl], qn_ref[...]) * (X_DH ** -0.5)
        s = lax.dot_general(q.astype(BF16), k_ref[:, sl].astype(BF16), nt, preferred_element_type=F32)
        e, l = _softmax_rows(s)
        p = (e * (1.0 / l)).astype(BF16)
        o_ref[:, sl] = jnp.dot(p, v_ref[:, sl].astype(BF16), preferred_element_type=F32).astype(BF16)


def _cross_attention(q, row0, n_batch, t, mk, mv, q_norm, *, tq):
    assert t % tq == 0 and row0 % tq == 0
    nq = t // tq
    rb = row0 // tq
    return pl.pallas_call(
        _cross_attn_kernel,
        grid=(n_batch, nq),
        in_specs=[
            pl.BlockSpec((tq, X_W), lambda b, i: (rb + b * nq + i, 0)),
            pl.BlockSpec((None, N_MEM, X_W), lambda b, i: (b, 0, 0)),
            pl.BlockSpec((None, N_MEM, X_W), lambda b, i: (b, 0, 0)),
            pl.BlockSpec((1, LANES), lambda b, i: (0, 0)),
        ],
        out_specs=pl.BlockSpec((tq, X_W), lambda b, i: (b * nq + i, 0)),
        out_shape=jax.ShapeDtypeStruct((n_batch * t, X_W), BF16),
        compiler_params=_cparams(("parallel", "parallel")),
        name="cross_attention",
    )(q, mk, mv, q_norm.reshape(1, LANES))


PB_LG = 3 * B_W
PB_LW = PB_LG + B_GATE_LORA
PB_LA = PB_LW + LANES
PB_COLS = PB_LA + LANES


def _swap_halves(x):
    tiles = [pltpu.roll(x[:, c * LANES:(c + 1) * LANES], LANES // 2, 1) for c in range(x.shape[1] // LANES)]
    return tiles[0] if len(tiles) == 1 else jnp.concatenate(tiles, axis=1)


def _rwkv_prep_kernel(pb_ref, ps_ref, mu_ref, w0_ref, w2_ref, a0_ref, a2_ref, g2_ref, kk_ref, ka_ref,
                      r_ref, kf_ref, v_ref, g_ref, rt_ref, at_ref, bt_ref, kt_ref, bh_ref, kh_ref, pc_ref,
                      *, n_prompt_tiles, chunk_p, chunk_s):
    tm = pb_ref.shape[0]
    pb = pb_ref[...]
    xs = pb + (ps_ref[...] - pb) * mu_ref[...]
    r = xs[:, :B_W]
    k = xs[:, B_W:2 * B_W]
    lg = xs[:, PB_LG:PB_LW]
    lw_in = xs[:, PB_LW:PB_LA]
    la = xs[:, PB_LA:PB_COLS]
    wl = w0_ref[...] + jnp.dot(jnp.tanh(lw_in).astype(BF16), w2_ref[...], preferred_element_type=F32)
    lw = -jnp.exp(-_softplus(-wl) - 0.5)
    a = _sigmoid(a0_ref[...] + jnp.dot(la.astype(BF16), a2_ref[...], preferred_element_type=F32))
    kk = _group_norm_scale(k * kk_ref[...], B_DH, 1.0, EPS)
    kf = k * (1.0 + (a - 1.0) * ka_ref[...])
    b = kk * a
    r_ref[...] = r
    kf_ref[...] = kf
    v_ref[...] = xs[:, 2 * B_W:3 * B_W]
    g_ref[...] = jnp.dot(_sigmoid(lg).astype(BF16), g2_ref[...], preferred_element_type=F32)

    chunk = jnp.where(pl.program_id(0) < n_prompt_tiles, chunk_p, chunk_s)
    rowc = jnp.bitwise_and(lax.broadcasted_iota(jnp.int32, (tm, B_W), 0), chunk - 1)
    lp = lw
    sfx = lw
    step = 1
    while step < max(chunk_p, chunk_s):
        lp = lp + jnp.where(rowc >= step, pltpu.roll(lp, step, 0), 0.0)
        sfx = sfx + jnp.where(rowc + step < chunk, pltpu.roll(sfx, tm - step, 0), 0.0)
        step *= 2
    sfx = sfx - lw
    e_neg = jnp.exp(-lp)
    e_sfx = jnp.exp(sfx)
    sw = lambda z: _swap_halves(z).astype(BF16)
    rt_ref[...] = sw(r * jnp.exp(lp))
    at_ref[...] = sw(kk * jnp.exp(lp - lw))
    bt_ref[...] = sw(b * e_neg)
    kt_ref[...] = sw(kf * e_neg)
    bh_ref[...] = sw(b * e_sfx)
    kh_ref[...] = sw(kf * e_sfx)
    pc_ref[...] = _swap_halves(jnp.exp(lp + sfx))


def _rwkv_prep(pb, ps, mu, w0, w2, a0, a2, g2, k_k, k_a, *, tm, n_prompt_tiles, chunk_p, chunk_s):
    m = pb.shape[0]
    assert tm % chunk_p == 0 and tm % chunk_s == 0
    row = lambda n: pl.BlockSpec((1, n), lambda i: (0, 0))
    full = lambda a: pl.BlockSpec(a.shape, lambda i: (0, 0))
    tok = lambda n: pl.BlockSpec((tm, n), lambda i: (i, 0))
    f32o = jax.ShapeDtypeStruct((m, B_W), F32)
    b16o = jax.ShapeDtypeStruct((m, B_W), BF16)
    kern = functools.partial(_rwkv_prep_kernel, n_prompt_tiles=n_prompt_tiles, chunk_p=chunk_p, chunk_s=chunk_s)
    return pl.pallas_call(
        kern,
        grid=(m // tm,),
        in_specs=[tok(PB_COLS), tok(PB_COLS), row(PB_COLS), row(B_W), full(w2), row(B_W), full(a2), full(g2),
                  row(B_W), row(B_W)],
        out_specs=[tok(B_W)] * 11,
        out_shape=[f32o] * 4 + [b16o] * 6 + [f32o],
        compiler_params=_cparams(("parallel",)),
        name="rwkv_prep",
    )(pb, ps, mu, w0.reshape(1, B_W), w2, a0.reshape(1, B_W), a2, g2, k_k.reshape(1, B_W), k_a.reshape(1, B_W))


SOLVE_BLOCK = 16


def _split2(a):
    hi = a.astype(BF16)
    return hi, (a - hi.astype(F32)).astype(BF16)


def _solve_unit_lower_many(lmats, rhss):
    ns = len(lmats)
    c = lmats[0].shape[0]
    n = rhss[0].shape[1]
    bs = min(SOLVE_BLOCK, c)
    col = lax.broadcasted_iota(jnp.int32, (bs, c), 1)
    d = lambda x, y: jnp.dot(x, y, preferred_element_type=F32)
    done = [[] for _ in range(ns)]
    for blk in range(c // bs):
        rows = slice(blk * bs, (blk + 1) * bs)
        lrows = [lm[rows, :] for lm in lmats]
        rs = [rh[rows, :] for rh in rhss]
        if blk > 0:
            for s in range(ns):
                xs = jnp.concatenate(done[s] + [jnp.zeros((c - blk * bs, n), F32)], axis=0)
                lh, ll = _split2(jnp.where(col < blk * bs, lrows[s], 0.0))
                xh, xl = _split2(xs)
                rs[s] = rs[s] - (d(lh, xl) + d(ll, xh) + d(lh, xh))
        for j in range(bs - 1):
            for s in range(ns):
                rs[s] = rs[s] - lrows[s][:, blk * bs + j:blk * bs + j + 1] * rs[s][j:j + 1, :]
        for s in range(ns):
            done[s].append(rs[s])
    return [dn[0] if len(dn) == 1 else jnp.concatenate(dn, axis=0) for dn in done]


def _solve_unit_lower(lmat, rhs):
    return _solve_unit_lower_many([lmat], [rhs])[0]


def _rwkv_chunk_kernel(rt_ref, at_ref, bt_ref, kt_ref, bh_ref, kh_ref, v_ref, pc_ref, z0_ref, y_ref, zo_ref, z_scr,
                       *, chunk):
    i = pl.program_id(2)
    tb = rt_ref.shape[0]

    @pl.when(i == 0)
    def _():
        z_scr[...] = z0_ref[...]

    lo = lax.broadcasted_iota(jnp.int32, (1, LANES), 1) < B_DH
    zr = lax.broadcasted_iota(jnp.int32, (LANES, LANES), 0) < B_DH
    zc = lax.broadcasted_iota(jnp.int32, (LANES, LANES), 1) < B_DH
    offdiag = zr != zc
    ri = lax.broadcasted_iota(jnp.int32, (chunk, chunk), 0)
    ci = lax.broadcasted_iota(jnp.int32, (chunk, chunk), 1)
    nt = (((1,), (1,)), ((), ()))
    tn = (((0,), (0,)), ((), ()))
    d = lambda x, y: jnp.dot(x, y, preferred_element_type=F32)
    zero = jnp.zeros((), BF16)
    n_chunks = tb // chunk
    n_pp = rt_ref.shape[1] // LANES
    lmats, rhss, rkvs, rbs = [], [], [], []
    for p in range(n_pp):
        ls = slice(p * LANES, (p + 1) * LANES)
        for c in range(n_chunks):
            rs = slice(c * chunk, (c + 1) * chunk)
            rt, at, bt, kt = rt_ref[rs, ls], at_ref[rs, ls], bt_ref[rs, ls], kt_ref[rs, ls]
            v = v_ref[rs, ls].astype(BF16)
            for head in range(2):
                kmask = lo if head == 1 else jnp.logical_not(lo)
                a_h, r_h = jnp.where(kmask, at, zero), jnp.where(kmask, rt, zero)
                b_h, k_h = jnp.where(kmask, bt, zero), jnp.where(kmask, kt, zero)
                v_h = jnp.where(kmask, zero, v)
                lhs = jnp.concatenate([a_h, r_h], axis=0)
                gb = lax.dot_general(lhs, b_h, nt, preferred_element_type=F32)
                gk = lax.dot_general(lhs, k_h, nt, preferred_element_type=F32)
                lmats.append(jnp.where(ri > ci, gb[:chunk], 0.0))
                rbs.append(jnp.where(ri >= ci, gb[chunk:], 0.0).astype(BF16))
                ak = jnp.where(ri > ci, gk[:chunk], 0.0).astype(BF16)
                rk = jnp.where(ri >= ci, gk[chunk:], 0.0).astype(BF16)
                rhss.append(a_h.astype(F32) + d(ak, v_h))
                rkvs.append(d(rk, v_h))
    xs = _solve_unit_lower_many(lmats, rhss)
    zs = [z_scr[p] for p in range(n_pp)]
    for c in range(n_chunks):
        rs = slice(c * chunk, (c + 1) * chunk)
        for p in range(n_pp):
            ls = slice(p * LANES, (p + 1) * LANES)
            s0 = 2 * (p * n_chunks + c)
            rt = rt_ref[rs, ls]
            stacked = jnp.concatenate([xs[s0].astype(BF16), xs[s0 + 1].astype(BF16),
                                       jnp.where(lo, zero, rt), jnp.where(lo, rt, zero)], axis=0)
            big = lax.dot_general(stacked, zs[p].astype(BF16), nt, preferred_element_type=F32)
            u0 = jnp.where(lo, big[:chunk], 0.0)
            u1 = jnp.where(lo, 0.0, big[chunk:2 * chunk])
            y = jnp.where(lo, big[2 * chunk:3 * chunk], big[3 * chunk:]) + rkvs[s0] + rkvs[s0 + 1]
            y = y - d(rbs[s0], u0.astype(BF16)) - d(rbs[s0 + 1], u1.astype(BF16))
            y_ref[rs, ls] = y
            upd = lax.dot_general(jnp.concatenate([v_ref[rs, ls].astype(BF16), (u0 + u1).astype(BF16)], axis=0),
                                  jnp.concatenate([kh_ref[rs, ls], -bh_ref[rs, ls]], axis=0), tn,
                                  preferred_element_type=F32)
            zs[p] = jnp.where(offdiag, zs[p] * pc_ref[c * chunk:c * chunk + 1, ls] + upd, zs[p])
    for p in range(n_pp):
        z_scr[p] = zs[p]

    @pl.when(i == pl.num_programs(2) - 1)
    def _():
        zo_ref[...] = z_scr[...]


RWKV_PAIRS_PER_STEP = 4


def _rwkv_chunks(rt, at, bt, kt, bh, kh, v, pc, z0, row0, n_seq, t, *, tb, chunk):
    assert t % tb == 0 and row0 % tb == 0 and tb % chunk == 0
    nb = t // tb
    rb = row0 // tb
    n_pairs = B_HEADS // 2
    pps = RWKV_PAIRS_PER_STEP
    tok = pl.BlockSpec((tb, pps * LANES), lambda s, p, i: (rb + s * nb + i, p))
    st = pl.BlockSpec((None, pps, LANES, LANES), lambda s, p, i: (s, p, 0, 0))
    return pl.pallas_call(
        functools.partial(_rwkv_chunk_kernel, chunk=chunk),
        grid=(n_seq, n_pairs // pps, nb),
        in_specs=[tok] * 8 + [st],
        out_specs=[pl.BlockSpec((tb, pps * LANES), lambda s, p, i: (s * nb + i, p)), st],
        out_shape=[jax.ShapeDtypeStruct((n_seq * t, B_W), F32),
                   jax.ShapeDtypeStruct((n_seq, n_pairs, LANES, LANES), F32)],
        scratch_shapes=[pltpu.VMEM((pps, LANES, LANES), F32)],
        compiler_params=_cparams(("parallel", "parallel", "arbitrary")),
        name="rwkv_chunks",
    )(rt, at, bt, kt, bh, kh, v, pc, z0)


def _rwkv_post_kernel(y_ref, r_ref, kf_ref, v_ref, g_ref, lw_ref, lb_ref, rk_ref, o_ref):
    y = y_ref[...]
    mean = _group_sum(y, B_DH) * (1.0 / B_DH)
    yc = y - mean
    var = _group_sum(yc * yc, B_DH) * (1.0 / B_DH)
    yn = yc * lax.rsqrt(var + B_GN_EPS) * lw_ref[...] + lb_ref[...]
    bonus = _group_sum(r_ref[...] * kf_ref[...] * rk_ref[...], B_DH)
    o_ref[...] = ((yn + bonus * v_ref[...]) * g_ref[...]).astype(BF16)


def _rwkv_post(y, r, kf, v, g, lnx_w, lnx_b, r_k, *, tm):
    m = y.shape[0]
    tok = pl.BlockSpec((tm, B_W), lambda i: (i, 0))
    row = pl.BlockSpec((1, B_W), lambda i: (0, 0))
    return pl.pallas_call(
        _rwkv_post_kernel,
        grid=(m // tm,),
        in_specs=[tok] * 5 + [row] * 3,
        out_specs=tok,
        out_shape=jax.ShapeDtypeStruct((m, B_W), BF16),
        compiler_params=_cparams(("parallel",)),
        name="rwkv_post",
    )(y, r, kf, v, g, lnx_w.reshape(1, B_W), lnx_b.reshape(1, B_W), r_k.reshape(1, B_W))


def _rwkv_cols(a):
    lw0 = 3 * B_W
    la0 = lw0 + B_DECAY_LORA
    lg0 = la0 + B_AAA_LORA
    z = jnp.zeros(a.shape[:-1] + (LANES - B_DECAY_LORA,), a.dtype)
    return jnp.concatenate([a[..., :lw0], a[..., lg0:], a[..., lw0:la0], z, a[..., la0:lg0], z], axis=-1)


def _rwkv_cols_inv(a):
    return jnp.concatenate([a[..., :PB_LG], a[..., PB_LW:PB_LW + B_DECAY_LORA],
                            a[..., PB_LA:PB_LA + B_AAA_LORA], a[..., PB_LG:PB_LW]], axis=-1)


def _pad_rows(a, n):
    return jnp.pad(a, ((0, n - a.shape[0]), (0, 0)))


def _rwkv_state_to_pairs(s):
    n = s.shape[0]
    s = s.reshape(n, B_HEADS // 2, 2, B_DH, B_DH)
    eye = jnp.broadcast_to(jnp.eye(B_DH, dtype=F32), (n, B_HEADS // 2, B_DH, B_DH))
    top = jnp.concatenate([eye, s[:, :, 0]], axis=-1)
    bot = jnp.concatenate([s[:, :, 1], eye], axis=-1)
    return jnp.concatenate([top, bot], axis=-2)


def _rwkv_state_from_pairs(z):
    n = z.shape[0]
    s = jnp.stack([z[:, :, :B_DH, B_DH:], z[:, :, B_DH:, :B_DH]], axis=2)
    return s.reshape(n, B_HEADS, B_DH, B_DH)


def _rwkv_mixer(pb, n_p, t_p, n_s, t_s, prev_s, state_s, mu, w0, w2, a0, a2, g2, k_k, k_a, r_k, lnx_w, lnx_b,
                *, tm, tb):
    mp = n_p * t_p
    chunk_p = min(64, t_p)
    chunk_s = min(64, t_s)
    assert mp % tm == 0 and t_p % chunk_p == 0 and t_s % chunk_s == 0
    pbp = pb[:mp].reshape(n_p, t_p, PB_COLS)
    pbs = pb[mp:].reshape(n_s, t_s, PB_COLS)
    sh_p = jnp.concatenate([jnp.zeros((n_p, 1, PB_COLS), F32), pbp[:, :-1]], axis=1)
    sh_s = jnp.concatenate([_rwkv_cols(prev_s), pbs[:, :-1]], axis=1)
    ps = jnp.concatenate([sh_p.reshape(mp, PB_COLS), sh_s.reshape(n_s * t_s, PB_COLS)], axis=0)
    r, kf, v, g, rt, at, bt, kt, bh, kh, pc = _rwkv_prep(
        pb, ps, _rwkv_cols(mu.reshape(1, B_COLS)), w0, _pad_rows(w2, LANES).astype(BF16), a0,
        _pad_rows(a2, LANES).astype(BF16), g2.astype(BF16), k_k, k_a, tm=tm, n_prompt_tiles=mp // tm,
        chunk_p=chunk_p, chunk_s=chunk_s)
    z0_p = _rwkv_state_to_pairs(jnp.zeros((n_p, B_HEADS, B_DH, B_DH), F32))
    y_p, z_p = _rwkv_chunks(rt, at, bt, kt, bh, kh, v, pc, z0_p, 0, n_p, t_p, tb=tb, chunk=chunk_p)
    y_s, z_s = _rwkv_chunks(rt, at, bt, kt, bh, kh, v, pc, _rwkv_state_to_pairs(state_s), mp, n_s, t_s,
                            tb=t_s, chunk=chunk_s)
    y = jnp.concatenate([y_p, y_s], axis=0)
    ob = _rwkv_post(y, r, kf, v, g, lnx_w, lnx_b, r_k.reshape(B_W), tm=tm)
    return ob, _rwkv_state_from_pairs(z_p), _rwkv_state_from_pairs(z_s)


PC_Z = C_CONV_CH
PC_BETA = PC_Z + C_V
PC_COLS = PC_BETA + 2 * LANES
HALO = 8


def _chunk_cumsum(g, chunk):
    rowc = jnp.bitwise_and(lax.broadcasted_iota(jnp.int32, g.shape, 0), chunk - 1)
    k = 1
    while k < chunk:
        g = g + jnp.where(rowc >= k, pltpu.roll(g, k, 0), 0.0)
        k *= 2
    return g


def _transpose_rows(a):
    c = a.shape[0]
    if c < LANES:
        a = jnp.concatenate([a, jnp.zeros((LANES - c, LANES), F32)], axis=0)
    return a.T


def _pad_chunk_rows(a):
    c = a.shape[0]
    return a if c == LANES else jnp.concatenate([a, jnp.zeros((LANES - c, a.shape[1]), a.dtype)], axis=0)


def _gdn_prep_kernel(x_ref, halo_ref, st_ref, cw_ref, sm_ref, alog_ref, dtb_ref,
                     q_ref, k_ref, v_ref, beta_ref, gc_ref, *, chunk):
    i = pl.program_id(1)
    x = x_ref[...]
    tm = x.shape[0]
    halo = jnp.where(i == 0, st_ref[...], halo_ref[...])
    row8 = lax.broadcasted_iota(jnp.int32, (HALO, x.shape[1]), 0)
    acc = x * cw_ref[C_CONV - 1:C_CONV, :]
    for s in range(1, C_CONV):
        xs = pltpu.roll(x, s, 0)
        first = jnp.where(row8 < s, pltpu.roll(halo, s, 0), xs[:HALO])
        xs = first if tm == HALO else jnp.concatenate([first, xs[HALO:]], axis=0)
        acc = acc + xs * cw_ref[C_CONV - 1 - s:C_CONV - s, :]
    y = _silu(acc)
    q_ref[...] = _group_norm_scale(y[:, :C_QK], C_DH, 1.0, EPS) * (C_DH ** -0.5)
    k_ref[...] = _group_norm_scale(y[:, C_QK:2 * C_QK], C_DH, 1.0, EPS)
    v_ref[...] = y[:, 2 * C_QK:]
    sm = sm_ref[...]
    beta_ref[...] = _sigmoid(sm[:, :LANES])
    g = -jnp.exp(alog_ref[...]) * _softplus(sm[:, LANES:] + dtb_ref[...])
    gc_ref[...] = _chunk_cumsum(g, chunk)


def _gdn_prep(pc, conv_state, row0, n_seq, t, conv_w, a_log, dt_bias, *, tm, chunk):
    assert t % tm == 0 and row0 % tm == 0 and tm % HALO == 0 and tm % chunk == 0 and chunk & (chunk - 1) == 0
    nb = t // tm
    rb = row0 // tm
    hb = tm // HALO
    tok_in = lambda n, cb: pl.BlockSpec((tm, n), lambda s, i: (rb + s * nb + i, cb))
    tok_out = lambda n: pl.BlockSpec((tm, n), lambda s, i: (s * nb + i, 0))
    row = lambda n: pl.BlockSpec((1, n), lambda s, i: (0, 0))
    sds = lambda n: jax.ShapeDtypeStruct((n_seq * t, n), F32)
    lanes8 = lambda a: jnp.zeros((1, LANES), F32).at[0, :C_V_HEADS].set(a)
    return pl.pallas_call(
        functools.partial(_gdn_prep_kernel, chunk=chunk),
        grid=(n_seq, nb),
        in_specs=[
            tok_in(C_CONV_CH, 0),
            pl.BlockSpec((HALO, C_CONV_CH), lambda s, i: (jnp.maximum((rb + s * nb + i) * hb - 1, 0), 0)),
            pl.BlockSpec((None, HALO, C_CONV_CH), lambda s, i: (s, 0, 0)),
            pl.BlockSpec((C_CONV, C_CONV_CH), lambda s, i: (0, 0)),
            tok_in(2 * LANES, PC_BETA // (2 * LANES)),
            row(LANES), row(LANES),
        ],
        out_specs=[tok_out(C_QK), tok_out(C_QK), tok_out(C_V), tok_out(LANES), tok_out(LANES)],
        out_shape=[sds(C_QK), sds(C_QK), sds(C_V), sds(LANES), sds(LANES)],
        compiler_params=_cparams(("parallel", "arbitrary")),
        name="gdn_prep",
    )(pc, pc, conv_state, conv_w, pc, lanes8(a_log), lanes8(dt_bias))


def _gdn_chunk_kernel(q_ref, k_ref, v_ref, beta_ref, gc_ref, z_ref, on_ref, s0_ref, o_ref, so_ref, s_scr, *, chunk):
    hq = pl.program_id(1)
    i = pl.program_id(2)
    tb = q_ref.shape[0]
    rep = v_ref.shape[1] // C_DH

    @pl.when(i == 0)
    def _():
        s_scr[...] = s0_ref[...]

    lane = lax.broadcasted_iota(jnp.int32, (chunk, LANES), 1)
    ri = lax.broadcasted_iota(jnp.int32, (chunk, chunk), 0)
    ci = lax.broadcasted_iota(jnp.int32, (chunk, chunk), 1)
    nt = (((1,), (1,)), ((), ()))
    ones = jnp.ones((chunk, LANES), BF16)
    n_chunks = tb // chunk
    lmats, rhss, attns, qgs, kds, eglast = [], [], [], [], [], []
    for c in range(n_chunks):
        rs = slice(c * chunk, (c + 1) * chunk)
        q, k = q_ref[rs, :], k_ref[rs, :]
        qb, kbf = q.astype(BF16), k.astype(BF16)
        kkt = lax.dot_general(kbf, kbf, nt, preferred_element_type=F32)
        qkt = lax.dot_general(qb, kbf, nt, preferred_element_type=F32)
        for j in range(rep):
            hsel = lane == hq * rep + j
            beta = jnp.sum(jnp.where(hsel, beta_ref[rs, :], 0.0), axis=-1, keepdims=True)
            gcol = jnp.sum(jnp.where(hsel, gc_ref[rs, :], 0.0), axis=-1, keepdims=True)
            g0 = jnp.where(lane == 0, gcol, 0.0)
            grow = sum(lax.dot_general(ones, part, nt, preferred_element_type=F32) for part in _split3(g0))
            dec = jnp.exp(jnp.where(ri >= ci, gcol - grow, NEG_INF))
            lmats.append(jnp.where(ri > ci, beta * kkt * dec, 0.0))
            rhss.append(jnp.concatenate([v_ref[rs, j * C_DH:(j + 1) * C_DH] * beta, k * (beta * jnp.exp(gcol))],
                                        axis=1))
            attns.append((qkt * dec).astype(BF16))
            qgs.append((q * jnp.exp(gcol)).astype(BF16))
            glast = gcol[chunk - 1:chunk, :]
            kds.append(_transpose_rows(k * jnp.exp(glast - gcol)).astype(BF16))
            eglast.append(jnp.exp(glast))
    xs = _solve_unit_lower_many(lmats, rhss)
    ss = [s_scr[j] for j in range(rep)]
    for c in range(n_chunks):
        rs = slice(c * chunk, (c + 1) * chunk)
        for j in range(rep):
            n = c * rep + j
            sb = ss[j].astype(BF16)
            v_new = xs[n][:, :C_DH] - jnp.dot(xs[n][:, C_DH:].astype(BF16), sb, preferred_element_type=F32)
            vb = v_new.astype(BF16)
            o = jnp.dot(qgs[n], sb, preferred_element_type=F32) + jnp.dot(attns[n], vb, preferred_element_type=F32)
            ss[j] = ss[j] * eglast[n] + jnp.dot(kds[n], _pad_chunk_rows(vb), preferred_element_type=F32)
            o = _rms_rows(o, on_ref[...]) * _silu(z_ref[rs, j * C_DH:(j + 1) * C_DH])
            o_ref[rs, j * C_DH:(j + 1) * C_DH] = o.astype(BF16)
    for j in range(rep):
        s_scr[j] = ss[j]

    @pl.when(i == pl.num_programs(2) - 1)
    def _():
        so_ref[...] = s_scr[...]


def _gdn_chunks(q, k, v, beta, gc, pc, row0, n_seq, t, onorm, s0, *, tb, chunk):
    assert t % tb == 0 and row0 % tb == 0 and tb % chunk == 0
    nb = t // tb
    rb = row0 // tb
    rep = C_V_HEADS // C_QK_HEADS
    tok = lambda n, cb: pl.BlockSpec((tb, n), cb)
    st = pl.BlockSpec((None, rep, C_DH, C_DH), lambda s, h, i: (s, h, 0, 0))
    return pl.pallas_call(
        functools.partial(_gdn_chunk_kernel, chunk=chunk),
        grid=(n_seq, C_QK_HEADS, nb),
        in_specs=[
            tok(C_DH, lambda s, h, i: (s * nb + i, h)),
            tok(C_DH, lambda s, h, i: (s * nb + i, h)),
            tok(rep * C_DH, lambda s, h, i: (s * nb + i, h)),
            tok(LANES, lambda s, h, i: (s * nb + i, 0)),
            tok(LANES, lambda s, h, i: (s * nb + i, 0)),
            tok(rep * C_DH, lambda s, h, i: (rb + s * nb + i, PC_Z // (rep * C_DH) + h)),
            pl.BlockSpec((1, LANES), lambda s, h, i: (0, 0)),
            st,
        ],
        out_specs=[tok(rep * C_DH, lambda s, h, i: (s * nb + i, h)), st],
        out_shape=[jax.ShapeDtypeStruct((n_seq * t, C_V), BF16),
                   jax.ShapeDtypeStruct((n_seq, C_V_HEADS, C_DH, C_DH), F32)],
        scratch_shapes=[pltpu.VMEM((rep, C_DH, C_DH), F32)],
        compiler_params=_cparams(("parallel", "parallel", "arbitrary")),
        name="gdn_chunks",
    )(q, k, v, beta, gc, pc, onorm.reshape(1, LANES), s0)


def _gdn_group(pc, conv_state3, s0, row0, n_seq, t, conv_w, a_log, dt_bias, onorm, *, tm, tb):
    chunk = C_CHUNK if t % C_CHUNK == 0 else t
    conv_state = jnp.pad(conv_state3, ((0, 0), (HALO - (C_CONV - 1), 0), (0, 0)))
    q, k, v, beta, gc = _gdn_prep(pc, conv_state, row0, n_seq, t, conv_w, a_log, dt_bias, tm=tm, chunk=chunk)
    return _gdn_chunks(q, k, v, beta, gc, pc, row0, n_seq, t, onorm, s0, tb=tb, chunk=chunk)


PD_V = 2 * D_K
PD_GATE = PD_V + D_V
PD_GLR = PD_GATE + D_V
PD_COLS = PD_GLR + LANES


def _gla_kernel(q_ref, k_ref, v_ref, gate_ref, glr_ref, w2_ref, bgk_ref, on_ref, s0_ref, o_ref, so_ref, s_scr,
                *, chunk):
    i = pl.program_id(2)
    tb = q_ref.shape[0]

    @pl.when(i == 0)
    def _():
        s_scr[...] = s0_ref[...]

    pre = jnp.dot(glr_ref[...].astype(BF16), w2_ref[...], preferred_element_type=F32) + bgk_ref[...]
    bcum = _chunk_cumsum(-_softplus(-pre) * (1.0 / D_GATE_NORM), chunk)
    ri = lax.broadcasted_iota(jnp.int32, (chunk, 1), 0)
    ci = lax.broadcasted_iota(jnp.int32, (chunk, chunk), 1)
    intra, qgs, decays, upds = [], [], [], []
    for c in range(tb // chunk):
        rs = slice(c * chunk, (c + 1) * chunk)
        q = q_ref[rs, :] * (D_DK ** -0.5)
        k, bc = k_ref[rs, :], bcum[rs, :]
        vb = v_ref[rs, :].astype(BF16)
        attn = jnp.zeros((chunk, chunk), F32)
        for j in range(chunk):
            e = jnp.exp(jnp.where(ri >= j, bc - bc[j:j + 1, :], NEG_INF))
            col = jnp.sum(q * k[j:j + 1, :] * e, axis=-1, keepdims=True)
            attn = jnp.where(ci == j, col, attn)
        intra.append(jnp.dot(attn.astype(BF16), vb, preferred_element_type=F32))
        qgs.append((q * jnp.exp(bc)).astype(BF16))
        blast = bc[chunk - 1:chunk, :]
        ebt = jnp.broadcast_to(jnp.exp(blast), (LANES, LANES)).T
        decays.append(jnp.concatenate([ebt, ebt], axis=1))
        kdt = _transpose_rows(k * jnp.exp(blast - bc)).astype(BF16)
        upds.append(jnp.dot(kdt, _pad_chunk_rows(vb), preferred_element_type=F32))
    s = s_scr[...]
    for c in range(tb // chunk):
        rs = slice(c * chunk, (c + 1) * chunk)
        o = jnp.dot(qgs[c], s.astype(BF16), preferred_element_type=F32) + intra[c]
        s = s * decays[c] + upds[c]
        o = _rms_rows(o, on_ref[...]) * _silu(gate_ref[rs, :])
        o_ref[rs, :] = o.astype(BF16)
    s_scr[...] = s

    @pl.when(i == pl.num_programs(2) - 1)
    def _():
        so_ref[...] = s_scr[...]


def _gla_group(pd, s0, row0, n_seq, t, w_gk2, b_gk, onorm, *, tb):
    chunk = D_CHUNK if t % D_CHUNK == 0 else t
    assert t % tb == 0 and row0 % tb == 0 and tb % chunk == 0 and chunk & (chunk - 1) == 0
    nb = t // tb
    rb = row0 // tb
    w2 = _pad_rows(w_gk2, LANES).astype(BF16)
    tok = lambda n, cb: pl.BlockSpec((tb, n), cb)
    st = pl.BlockSpec((None, None, D_DK, D_DV), lambda s, h, i: (s, h, 0, 0))
    return pl.pallas_call(
        functools.partial(_gla_kernel, chunk=chunk),
        grid=(n_seq, D_HEADS, nb),
        in_specs=[
            tok(D_DK, lambda s, h, i: (rb + s * nb + i, h)),
            tok(D_DK, lambda s, h, i: (rb + s * nb + i, D_K // D_DK + h)),
            tok(D_DV, lambda s, h, i: (rb + s * nb + i, PD_V // D_DV + h)),
            tok(D_DV, lambda s, h, i: (rb + s * nb + i, PD_GATE // D_DV + h)),
            tok(LANES, lambda s, h, i: (rb + s * nb + i, PD_GLR // LANES)),
            pl.BlockSpec((LANES, D_DK), lambda s, h, i: (0, h)),
            pl.BlockSpec((1, D_DK), lambda s, h, i: (0, h)),
            pl.BlockSpec((1, D_DV), lambda s, h, i: (0, 0)),
            st,
        ],
        out_specs=[tok(D_DV, lambda s, h, i: (s * nb + i, h)), st],
        out_shape=[jax.ShapeDtypeStruct((n_seq * t, D_V), BF16),
                   jax.ShapeDtypeStruct((n_seq, D_HEADS, D_DK, D_DV), F32)],
        scratch_shapes=[pltpu.VMEM((D_DK, D_DV), F32)],
        compiler_params=_cparams(("parallel", "parallel", "arbitrary")),
        name="gla_chunks",
    )(pd, pd, pd, pd, pd, w2, b_gk.reshape(1, D_K), onorm.reshape(1, D_DV), s0)


TM = 768
TM_SMALL = 256


def _cross_block(x, layer, n_p, t_p, n_s, t_s, mem_prompt, cache_mem_k, cache_mem_v, norm_mem_w, norm_cross_w,
                 x_w_q, x_w_k, x_w_v, x_w_o, x_q_norm, x_k_norm):
    d = x.shape[1]
    mp = n_p * t_p
    wkv = jnp.concatenate([x_w_k[layer], x_w_v[layer]], axis=1).astype(BF16)
    hw = jnp.concatenate([jnp.tile(x_k_norm[layer], X_HEADS), jnp.ones((X_W,), F32)]).reshape(1, 2 * X_W)
    mem = mem_prompt.reshape(n_p * N_MEM, d)
    kv = _norm_matmul(mem, norm_mem_w[layer], wkv, tm=min(512, n_p * N_MEM), tn=X_W, head_w=hw,
                      n_norm_cols=X_W, gsize=X_DH)
    mk_p = kv[:, :X_W].reshape(n_p, N_MEM, X_W)
    mv_p = kv[:, X_W:].reshape(n_p, N_MEM, X_W)
    qx = _norm_matmul(x, norm_cross_w[layer], x_w_q[layer].astype(BF16), tm=TM, tn=X_W)
    ca_p = _cross_attention(qx, 0, n_p, t_p, mk_p, mv_p, x_q_norm[layer], tq=min(512, t_p))
    ca_s = _cross_attention(qx, mp, n_s, t_s, cache_mem_k[layer].reshape(n_s, N_MEM, X_W),
                            cache_mem_v[layer].reshape(n_s, N_MEM, X_W), x_q_norm[layer], tq=t_s)
    ca = jnp.concatenate([ca_p, ca_s], axis=0)
    x = _matmul_residual([ca], [x_w_o[layer].astype(BF16)], x, tm=TM, tn=1024)
    return x, mk_p.reshape(n_p, N_MEM, X_HEADS, X_DH), mv_p.reshape(n_p, N_MEM, X_HEADS, X_DH)


def kernel(x_prompt, x_sample, cache_diff_k, cache_diff_v, state_rwkv, state_rwkv_shift, cache_mem_k, cache_mem_v, state_gdn, state_gdn_conv, state_gla, page_table, mem_prompt, norm_mix_w, norm_cross_w, norm_mem_w, norm_ffn_w, w_in_even, w_out_even, a_q_norm, a_k_norm, a_lam_q1, a_lam_k1, a_lam_q2, a_lam_k2, a_subln, b_mu, b_w0, b_w2, b_a0, b_a2, b_g2, b_k_k, b_k_a, b_r_k, b_lnx_w, b_lnx_b, ffd_w_gate, ffd_w_up, ffd_w_down, w_in_odd, w_out_odd, c_conv_w, c_a_log, c_dt_bias, c_onorm, d_w_gk2, d_b_gk, d_onorm, moe_w_router, moe_w_gate, moe_w_up, moe_w_down, x_w_q, x_w_k, x_w_v, x_w_o, x_q_norm, x_k_norm):
    n_p, t_p, d = x_prompt.shape
    n_s, t_s, _ = x_sample.shape
    mp, ms = n_p * t_p, n_s * t_s
    depth = norm_mix_w.shape[0]
    x = jnp.concatenate([x_prompt.reshape(mp, d), x_sample.reshape(ms, d)], axis=0)
    assert (mp + ms) % TM == 0 and (mp + ms) % TM_SMALL == 0
    page = cache_diff_k.shape[2]
    slopes = jnp.exp2(-(8.0 / A_HEADS) * jnp.arange(1, A_HEADS + 1, dtype=F32))
    cross_w = (mem_prompt, cache_mem_k, cache_mem_v, norm_mem_w, norm_cross_w, x_w_q, x_w_k, x_w_v, x_w_o,
               x_q_norm, x_k_norm)
    dk_p, dv_p, dk_s, dv_s, rw_p, rw_s, sh_p, sh_s = [], [], [], [], [], [], [], []
    mk_l, mv_l, gd_p, gd_s, cv_p, cv_s, gl_p, gl_s = [], [], [], [], [], [], [], []
    for layer in range(depth):
        i = layer // 2
        if layer % 2 == 0:
            lam_init = 0.8 - 0.6 * math.exp(-0.3 * layer)
            w_in = w_in_even[i]
            hw = jnp.concatenate([jnp.tile(a_q_norm[i], A_Q // A_DH), jnp.tile(a_k_norm[i], A_Q // A_DH),
                                  jnp.ones((A_COLS - 2 * A_Q,), F32)]).reshape(1, A_COLS)
            pa = _norm_matmul(x, norm_mix_w[layer], w_in[:, :A_COLS].astype(BF16), tm=TM, tn=1024, head_w=hw,
                              n_norm_cols=2 * A_Q, gsize=A_DH)
            pb = _norm_matmul(x, norm_mix_w[layer], _rwkv_cols(w_in[:, A_COLS:]).astype(BF16), tm=TM,
                              tn=PB_COLS // 2)
            lam_stack = jnp.stack([a_lam_q1[i], a_lam_k1[i], a_lam_q2[i], a_lam_k2[i]])
            oa_p = _diff_attention_prompt(pa, n_p, t_p, slopes, lam_stack, a_subln[i], lam_init, tq=min(256, t_p))
            oa_s = _diff_attention_sample(pa, mp, n_s, t_s, cache_diff_k, cache_diff_v, i, page_table, lam_stack,
                                          a_subln[i], lam_init)
            ob, st_p, st_s = _rwkv_mixer(pb, n_p, t_p, n_s, t_s, state_rwkv_shift[i], state_rwkv[i], b_mu[i],
                                         b_w0[i], b_w2[i], b_a0[i], b_a2[i], b_g2[i], b_k_k[i], b_k_a[i], b_r_k[i],
                                         b_lnx_w[i], b_lnx_b[i], tm=TM_SMALL, tb=min(256, t_p))
            dk_p.append(pa[:mp, A_Q:2 * A_Q].reshape(n_p, t_p, A_HEADS, 2 * A_DH))
            dv_p.append(pa[:mp, 2 * A_Q:].reshape(n_p, t_p, A_HEADS, 2 * A_DH))
            dk_s.append(pa[mp:, A_Q:2 * A_Q].reshape(n_s, t_s, A_HEADS, 2 * A_DH))
            dv_s.append(pa[mp:, 2 * A_Q:].reshape(n_s, t_s, A_HEADS, 2 * A_DH))
            rw_p.append(st_p)
            rw_s.append(st_s)
            sh_p.append(_rwkv_cols_inv(pb[:mp].reshape(n_p, t_p, PB_COLS)[:, -1:]))
            sh_s.append(_rwkv_cols_inv(pb[mp:].reshape(n_s, t_s, PB_COLS)[:, -1:]))
            oa = jnp.concatenate([oa_p, oa_s], axis=0)
            w_out = w_out_even[i].astype(BF16)
            x = _matmul_residual([oa, ob], [w_out[:A_Q], w_out[A_Q:]], x, tm=TM, tn=1024)
        else:
            w_in = w_in_odd[i]
            c_cols = C_CONV_CH + C_V + 2 * C_V_HEADS
            zc = jnp.zeros((d, LANES - C_V_HEADS), F32)
            wc = jnp.concatenate([w_in[:, :PC_BETA], w_in[:, PC_BETA:PC_BETA + C_V_HEADS], zc,
                                  w_in[:, PC_BETA + C_V_HEADS:c_cols], zc], axis=1).astype(BF16)
            wd_ = jnp.concatenate([w_in[:, c_cols:], jnp.zeros((d, LANES - D_GATE_LORA), F32)], axis=1).astype(BF16)
            pc = _norm_matmul(x, norm_mix_w[layer], wc, tm=TM, tn=PC_COLS // 2)
            pd = _norm_matmul(x, norm_mix_w[layer], wd_, tm=TM, tn=PD_COLS // 5)
            gdn_w = (c_conv_w[i], c_a_log[i], c_dt_bias[i], c_onorm[i])
            oc_p, gs_p = _gdn_group(pc, jnp.zeros((n_p, C_CONV - 1, C_CONV_CH), F32),
                                    jnp.zeros((n_p, C_V_HEADS, C_DH, C_DH), F32), 0, n_p, t_p, *gdn_w,
                                    tm=min(512, t_p), tb=min(512, t_p))
            oc_s, gs_s = _gdn_group(pc, state_gdn_conv[i], state_gdn[i], mp, n_s, t_s, *gdn_w, tm=t_s, tb=t_s)
            gla_w = (d_w_gk2[i], d_b_gk[i], d_onorm[i])
            od_p, ls_p = _gla_group(pd, jnp.zeros((n_p, D_HEADS, D_DK, D_DV), F32), 0, n_p, t_p, *gla_w,
                                    tb=min(256, t_p))
            od_s, ls_s = _gla_group(pd, state_gla[i], mp, n_s, t_s, *gla_w, tb=t_s)
            gd_p.append(gs_p)
            gd_s.append(gs_s)
            cv_p.append(pc[:mp].reshape(n_p, t_p, PC_COLS)[:, -(C_CONV - 1):, :C_CONV_CH])
            cv_s.append(pc[mp:].reshape(n_s, t_s, PC_COLS)[:, -(C_CONV - 1):, :C_CONV_CH])
            gl_p.append(ls_p)
            gl_s.append(ls_s)
            oc = jnp.concatenate([oc_p, oc_s], axis=0)
            od = jnp.concatenate([od_p, od_s], axis=0)
            w_out = w_out_odd[i].astype(BF16)
            x = _matmul_residual([oc, od], [w_out[:C_V], w_out[C_V:]], x, tm=TM, tn=1024)
        x, mk_p, mv_p = _cross_block(x, layer, n_p, t_p, n_s, t_s, *cross_w)
        mk_l.append(mk_p)
        mv_l.append(mv_p)
        if layer % 2 == 0:
            x = _swiglu_ffn(x, norm_ffn_w[layer], ffd_w_gate[i].astype(BF16), ffd_w_up[i].astype(BF16),
                            ffd_w_down[i].astype(BF16), tm=TM, tf=512)
        else:
            x = _moe_ffn(x, norm_ffn_w[layer], moe_w_router[i], moe_w_gate[i].astype(BF16),
                         moe_w_up[i].astype(BF16), moe_w_down[i].astype(BF16), tm=TM, tf=256, tc=TM_SMALL)
    st = jnp.stack
    return (x[:mp].reshape(n_p, t_p, d), x[mp:].reshape(n_s, t_s, d), st(dk_p), st(dv_p), st(dk_s), st(dv_s),
            st(rw_p), st(rw_s), st(sh_p), st(sh_s), st(mk_l), st(mv_l), st(gd_p), st(gd_s), st(cv_p), st(cv_s),
            st(gl_p), st(gl_s))
```

```python
import functools
import math

import jax
import jax.numpy as jnp
from jax import lax
from jax.experimental import pallas as pl
from jax.experimental.pallas import tpu as pltpu

F32 = jnp.float32
BF16 = jnp.bfloat16
NEG_INF = float("-inf")

D_MODEL = 2048
A_HEADS = 8
A_DH = 64
A_Q = 1024
A_COLS = 3072
B_HEADS = 16
B_DH = 64
B_W = 1024
B_DECAY_LORA = 96
B_AAA_LORA = 96
B_GATE_LORA = 256
B_COLS = 3520
B_GN_EPS = 64e-5
C_QK_HEADS = 4
C_V_HEADS = 8
C_DH = 128
C_CONV = 4
C_QK = 512
C_V = 1024
C_CONV_CH = 2048
C_CHUNK = 64
D_HEADS = 4
D_DK = 128
D_DV = 256
D_K = 512
D_V = 1024
D_GATE_LORA = 16
D_GATE_NORM = 16.0
D_CHUNK = 16
N_MEM = 256
X_HEADS = 4
X_DH = 128
X_W = 512
FF_DENSE = 5632
N_EXPERTS = 8
FF_EXPERT = 2816
EPS = 1e-6

LANES = 128
VMEM_LIMIT = 56 * 1024 * 1024


def _cparams(sem):
    return pltpu.CompilerParams(dimension_semantics=sem, vmem_limit_bytes=VMEM_LIMIT)


def _sigmoid(x):
    return 1.0 / (1.0 + jnp.exp(-x))


def _silu(x):
    return x * _sigmoid(x)


def _softplus(x):
    return jnp.maximum(x, 0.0) + jnp.log(1.0 + jnp.exp(-jnp.abs(x)))


def _rms_rows(x, w):
    ms = jnp.mean(x * x, axis=-1, keepdims=True)
    return x * lax.rsqrt(ms + EPS) * w


def _group_sum(x, gsize):
    lane = lax.broadcasted_iota(jnp.int32, (1, LANES), 1)
    outs = []
    for c in range(x.shape[1] // LANES):
        xc = x[:, c * LANES:(c + 1) * LANES]
        if gsize == LANES:
            ss = jnp.broadcast_to(jnp.sum(xc, axis=-1, keepdims=True), xc.shape)
        else:
            lo = jnp.sum(jnp.where(lane < 64, xc, 0.0), axis=-1, keepdims=True)
            hi = jnp.sum(jnp.where(lane >= 64, xc, 0.0), axis=-1, keepdims=True)
            ss = jnp.where(lane < 64, lo, hi)
        outs.append(ss)
    return outs[0] if len(outs) == 1 else jnp.concatenate(outs, axis=1)


def _group_norm_scale(x, gsize, inv_n, eps):
    return x * lax.rsqrt(_group_sum(x * x, gsize) * inv_n + eps)


def _split3(a):
    a1 = a.astype(BF16)
    r1 = a - a1.astype(F32)
    a2 = r1.astype(BF16)
    a3 = (r1 - a2.astype(F32)).astype(BF16)
    return a1, a2, a3


def _dot_f32(a, b):
    a1, a2, a3 = _split3(a)
    b1, b2, b3 = _split3(b)
    d = lambda x, y: jnp.dot(x, y, preferred_element_type=F32)
    return d(a1, b3) + d(a3, b1) + d(a2, b2) + d(a1, b2) + d(a2, b1) + d(a1, b1)


def _norm_mm_kernel(x_ref, nw_ref, w_ref, hw_ref, o_ref, h_scr, *, n_norm_tiles, gsize):
    j = pl.program_id(1)

    @pl.when(j == 0)
    def _():
        h_scr[...] = _rms_rows(x_ref[...], nw_ref[...]).astype(BF16)

    acc = jnp.dot(h_scr[...], w_ref[...], preferred_element_type=F32)
    if n_norm_tiles == 0:
        o_ref[...] = acc
    else:
        @pl.when(j < n_norm_tiles)
        def _():
            o_ref[...] = _group_norm_scale(acc, gsize, 1.0 / gsize, EPS) * hw_ref[...]

        @pl.when(j >= n_norm_tiles)
        def _():
            o_ref[...] = acc


def _norm_matmul(x, nw, w, *, tm, tn, head_w=None, n_norm_cols=0, gsize=LANES):
    m, k = x.shape
    n = w.shape[1]
    assert m % tm == 0 and n % tn == 0 and n_norm_cols % tn == 0
    if head_w is None:
        head_w = jnp.ones((1, n), F32)
    kern = functools.partial(_norm_mm_kernel, n_norm_tiles=n_norm_cols // tn, gsize=gsize)
    return pl.pallas_call(
        kern,
        grid=(m // tm, n // tn),
        in_specs=[
            pl.BlockSpec((tm, k), lambda i, j: (i, 0)),
            pl.BlockSpec((1, k), lambda i, j: (0, 0)),
            pl.BlockSpec((k, tn), lambda i, j: (0, j)),
            pl.BlockSpec((1, tn), lambda i, j: (0, j)),
        ],
        out_specs=pl.BlockSpec((tm, tn), lambda i, j: (i, j)),
        out_shape=jax.ShapeDtypeStruct((m, n), F32),
        scratch_shapes=[pltpu.VMEM((tm, k), BF16)],
        compiler_params=_cparams(("parallel", "arbitrary")),
        name="norm_matmul",
    )(x, nw.reshape(1, k), w, head_w)


def _mm_res_kernel(*refs, n_in):
    x_ref = refs[2 * n_in]
    o_ref = refs[2 * n_in + 1]
    acc = x_ref[...]
    for t in range(n_in):
        acc = acc + jnp.dot(refs[t][...], refs[n_in + t][...], preferred_element_type=F32)
    o_ref[...] = acc


def _matmul_residual(acts, weights, x, *, tm, tn):
    m, n = x.shape
    n_in = len(acts)
    in_specs = [pl.BlockSpec((tm, a.shape[1]), lambda i, j: (i, 0)) for a in acts]
    in_specs += [pl.BlockSpec((w.shape[0], tn), lambda i, j: (0, j)) for w in weights]
    in_specs += [pl.BlockSpec((tm, tn), lambda i, j: (i, j))]
    return pl.pallas_call(
        functools.partial(_mm_res_kernel, n_in=n_in),
        grid=(m // tm, n // tn),
        in_specs=in_specs,
        out_specs=pl.BlockSpec((tm, tn), lambda i, j: (i, j)),
        out_shape=jax.ShapeDtypeStruct((m, n), F32),
        compiler_params=_cparams(("parallel", "arbitrary")),
        name="matmul_residual",
    )(*acts, *weights, x)


def _swiglu_kernel(x_ref, nw_ref, wg_ref, wu_ref, wd_ref, o_ref, h_scr):
    f = pl.program_id(1)

    @pl.when(f == 0)
    def _():
        x = x_ref[...]
        h_scr[...] = _rms_rows(x, nw_ref[...]).astype(BF16)
        o_ref[...] = x

    h = h_scr[...]
    g = jnp.dot(h, wg_ref[...], preferred_element_type=F32)
    u = jnp.dot(h, wu_ref[...], preferred_element_type=F32)
    a = (_silu(g) * u).astype(BF16)
    o_ref[...] += jnp.dot(a, wd_ref[...], preferred_element_type=F32)


def _swiglu_ffn(x, nw, wg, wu, wd, *, tm, tf):
    m, d = x.shape
    ff = wg.shape[1]
    return pl.pallas_call(
        _swiglu_kernel,
        grid=(m // tm, ff // tf),
        in_specs=[
            pl.BlockSpec((tm, d), lambda i, f: (i, 0)),
            pl.BlockSpec((1, d), lambda i, f: (0, 0)),
            pl.BlockSpec((d, tf), lambda i, f: (0, f)),
            pl.BlockSpec((d, tf), lambda i, f: (0, f)),
            pl.BlockSpec((tf, d), lambda i, f: (f, 0)),
        ],
        out_specs=pl.BlockSpec((tm, d), lambda i, f: (i, 0)),
        out_shape=jax.ShapeDtypeStruct((m, d), F32),
        scratch_shapes=[pltpu.VMEM((tm, d), BF16)],
        compiler_params=_cparams(("parallel", "arbitrary")),
        name="swiglu_ffn",
    )(x, nw.reshape(1, d), wg, wu, wd)


MOE_TILE = 768
MOE_DMA_WINDOW = 512
ROW_SLABS = D_MODEL // LANES


def _rows_to_slabs(ref, x):
    for j in range(ROW_SLABS):
        ref[:, j, :] = x[:, j * LANES:(j + 1) * LANES]


def _slabs_to_rows(ref):
    return jnp.concatenate([ref[:, j, :] for j in range(ROW_SLABS)], axis=1)


def _moe_router_kernel(x_ref, nw_ref, wr_ref, h_ref, info_ref):
    tm = x_ref.shape[0]
    lane = lax.broadcasted_iota(jnp.int32, (tm, LANES), 1)
    hf = _rms_rows(x_ref[...], nw_ref[...])
    _rows_to_slabs(h_ref, hf)
    logits = jnp.where(lane < N_EXPERTS, _dot_f32(hf, wr_ref[...]), NEG_INF)
    m1 = jnp.max(logits, axis=-1, keepdims=True)
    i1 = jnp.min(jnp.where(logits == m1, lane, LANES), axis=-1, keepdims=True)
    rest = jnp.where(lane == i1, NEG_INF, logits)
    m2 = jnp.max(rest, axis=-1, keepdims=True)
    i2 = jnp.min(jnp.where(rest == m2, lane, LANES), axis=-1, keepdims=True)
    e2 = jnp.exp(m2 - m1)
    vals = [i1.astype(F32), i2.astype(F32), 1.0 / (1.0 + e2), e2 / (1.0 + e2)]
    info = jnp.zeros((tm, LANES), F32)
    for k, val in enumerate(vals):
        info = jnp.where(lane == k, val, info)
    info_ref[...] = info


def _moe_gather_kernel(dest_ref, h_ref, zero_ref, xs_ref, sem):
    del zero_ref
    n2 = dest_ref.shape[0]
    n = n2 // 2
    win = sem.shape[0]

    def copy(a):
        tok = jnp.where(a >= n, a - n, a)
        return pltpu.make_async_copy(h_ref.at[tok], xs_ref.at[dest_ref[a]], sem.at[a % win])

    def issue(a, carry):
        @pl.when(a >= win)
        def _():
            copy(a - win).wait()
        copy(a).start()
        return carry

    def drain(a, carry):
        copy(a).wait()
        return carry

    lax.fori_loop(0, n2, issue, 0)
    lax.fori_loop(n2 - win, n2, drain, 0)


def _moe_expert_kernel(te_ref, nu_ref, xs_ref, wg_ref, wu_ref, wd_ref, y_ref, xb_scr, acc_scr):
    del te_ref
    t = pl.program_id(0)
    f = pl.program_id(1)

    @pl.when(f == 0)
    def _():
        xb_scr[...] = _slabs_to_rows(xs_ref).astype(BF16)
        acc_scr[...] = jnp.zeros(acc_scr.shape, F32)

    @pl.when(t < nu_ref[0])
    def _():
        xb = xb_scr[...]
        g = jnp.dot(xb, wg_ref[...].astype(BF16), preferred_element_type=F32)
        u = jnp.dot(xb, wu_ref[...].astype(BF16), preferred_element_type=F32)
        acc_scr[...] += jnp.dot((_silu(g) * u).astype(BF16), wd_ref[...].astype(BF16),
                                preferred_element_type=F32)

    @pl.when(f == pl.num_programs(1) - 1)
    def _():
        _rows_to_slabs(y_ref, acc_scr[...])


def _moe_combine_kernel(dest_ref, x_ref, info_ref, y_ref, o1_ref, o2_ref, ybuf, sem, *, n_first):
    i = pl.program_id(0)
    tc = x_ref.shape[0]
    n = dest_ref.shape[0] // 2

    def copy(r, slot):
        return pltpu.make_async_copy(y_ref.at[dest_ref[slot * n + i * tc + r]], ybuf.at[slot, r], sem.at[slot])

    def issue(r, carry):
        copy(r, 0).start()
        copy(r, 1).start()
        return carry

    def drain(r, carry):
        copy(r, 0).wait()
        copy(r, 1).wait()
        return carry

    lax.fori_loop(0, tc, issue, 0)
    lax.fori_loop(0, tc, drain, 0)
    info = info_ref[...]
    y1 = _slabs_to_rows(ybuf.at[0])
    y2 = _slabs_to_rows(ybuf.at[1])
    res = x_ref[...] + info[:, 2:3] * y1 + info[:, 3:4] * y2

    @pl.when(i < n_first)
    def _():
        o1_ref[...] = res

    @pl.when(i >= n_first)
    def _():
        o2_ref[...] = res


def _moe_ffn(x, nw, w_router, wg, wu, wd, *, tm, tf, tc, split):
    m, d = x.shape
    ne, _, ff = wg.shape
    tg = MOE_TILE
    assert d == D_MODEL and m % tm == 0 and m % tc == 0 and ff % tf == 0
    n_tiles = -(-(2 * m + ne * (tg - 1)) // tg)
    p_rows = n_tiles * tg
    wr = jnp.zeros((d, LANES), F32).at[:, :ne].set(w_router)
    h3, info = pl.pallas_call(
        _moe_router_kernel,
        grid=(m // tm,),
        in_specs=[pl.BlockSpec((tm, d), lambda i: (i, 0)), pl.BlockSpec((1, d), lambda i: (0, 0)),
                  pl.BlockSpec((d, LANES), lambda i: (0, 0))],
        out_specs=[pl.BlockSpec((tm, ROW_SLABS, LANES), lambda i: (i, 0, 0)),
                   pl.BlockSpec((tm, LANES), lambda i: (i, 0))],
        out_shape=[jax.ShapeDtypeStruct((m, ROW_SLABS, LANES), F32), jax.ShapeDtypeStruct((m, LANES), F32)],
        compiler_params=_cparams(("parallel",)),
        name="moe_router",
    )(x, nw.reshape(1, d), wr)

    e12 = info[:, :2].astype(jnp.int32)
    onehot = (e12[:, :1] == jnp.arange(ne)) | (e12[:, 1:2] == jnp.arange(ne))
    csum = jnp.cumsum(onehot.astype(jnp.int32), axis=0)
    rank = csum - onehot
    padded = (csum[-1] + tg - 1) // tg * tg
    gend = jnp.cumsum(padded)
    base = (gend - padded)[None, :] + rank
    dest = jnp.concatenate([jnp.take_along_axis(base, e12[:, :1], axis=1)[:, 0],
                            jnp.take_along_axis(base, e12[:, 1:2], axis=1)[:, 0]]).astype(jnp.int32)
    tile_expert = jnp.minimum(jnp.sum(jnp.arange(n_tiles)[:, None] * tg >= gend[None, :], axis=1),
                              ne - 1).astype(jnp.int32)
    n_used = (gend[-1:] // tg).astype(jnp.int32)

    any_spec = pl.BlockSpec(memory_space=pl.ANY)
    xs3 = pl.pallas_call(
        _moe_gather_kernel,
        grid_spec=pltpu.PrefetchScalarGridSpec(
            num_scalar_prefetch=1, grid=(1,), in_specs=[any_spec, any_spec], out_specs=any_spec,
            scratch_shapes=[pltpu.SemaphoreType.DMA((min(MOE_DMA_WINDOW, 2 * m),))]),
        out_shape=jax.ShapeDtypeStruct((p_rows, ROW_SLABS, LANES), F32),
        input_output_aliases={2: 0},
        name="moe_gather",
    )(dest, h3, jnp.zeros((p_rows, ROW_SLABS, LANES), F32))

    last = lambda t, nu: jnp.minimum(t, nu[0] - 1)
    y3 = pl.pallas_call(
        _moe_expert_kernel,
        grid_spec=pltpu.PrefetchScalarGridSpec(
            num_scalar_prefetch=2,
            grid=(n_tiles, ff // tf),
            in_specs=[
                pl.BlockSpec((tg, ROW_SLABS, LANES), lambda t, f, te, nu: (last(t, nu), 0, 0)),
                pl.BlockSpec((None, d, tf), lambda t, f, te, nu: (te[last(t, nu)], 0, f)),
                pl.BlockSpec((None, d, tf), lambda t, f, te, nu: (te[last(t, nu)], 0, f)),
                pl.BlockSpec((None, tf, d), lambda t, f, te, nu: (te[last(t, nu)], f, 0)),
            ],
            out_specs=pl.BlockSpec((tg, ROW_SLABS, LANES), lambda t, f, te, nu: (t, 0, 0)),
            scratch_shapes=[pltpu.VMEM((tg, d), BF16), pltpu.VMEM((tg, d), F32)]),
        out_shape=jax.ShapeDtypeStruct((p_rows, ROW_SLABS, LANES), F32),
        compiler_params=_cparams(("arbitrary", "arbitrary")),
        name="moe_experts",
    )(tile_expert, n_used, xs3, wg, wu, wd)

    assert split % tc == 0 and 0 < split < m
    n_first = split // tc
    return pl.pallas_call(
        functools.partial(_moe_combine_kernel, n_first=n_first),
        grid_spec=pltpu.PrefetchScalarGridSpec(
            num_scalar_prefetch=1,
            grid=(m // tc,),
            in_specs=[pl.BlockSpec((tc, d), lambda i, de: (i, 0)), pl.BlockSpec((tc, LANES), lambda i, de: (i, 0)),
                      any_spec],
            out_specs=[pl.BlockSpec((tc, d), lambda i, de: (jnp.minimum(i, n_first - 1), 0)),
                       pl.BlockSpec((tc, d), lambda i, de: (jnp.maximum(i - n_first, 0), 0))],
            scratch_shapes=[pltpu.VMEM((2, tc, ROW_SLABS, LANES), F32), pltpu.SemaphoreType.DMA((2,))]),
        out_shape=[jax.ShapeDtypeStruct((split, d), F32), jax.ShapeDtypeStruct((m - split, d), F32)],
        compiler_params=_cparams(("arbitrary",)),
        name="moe_combine",
    )(dest, x, info, y3)


def _diff_lambda_vec(lam_ref, lam_init):
    lv = lam_ref[...]
    d1 = jnp.sum(lv[0:1, :] * lv[1:2, :], axis=-1, keepdims=True)
    d2 = jnp.sum(lv[2:3, :] * lv[3:4, :], axis=-1, keepdims=True)
    return jnp.exp(d1) - jnp.exp(d2) + lam_init


def _softmax_rows(s):
    m = jnp.max(s, axis=-1, keepdims=True)
    e = jnp.exp(s - m)
    return e, jnp.sum(e, axis=-1, keepdims=True)


def _diffattn_prompt_kernel(slope_ref, q_ref, k_ref, v_ref, lam_ref, subln_ref, o_ref, *, tq, lam_init):
    t = q_ref.shape[0]
    h = pl.program_id(1)
    slope = slope_ref[h]
    lane = lax.broadcasted_iota(jnp.int32, (1, LANES), 1)
    lam = _diff_lambda_vec(lam_ref, lam_init)
    kb = k_ref[...].astype(BF16)
    vb = v_ref[...].astype(BF16)
    nt = (((1,), (1,)), ((), ()))
    for qi in range(t // tq):
        n_keys = (qi + 1) * tq
        q = q_ref[qi * tq:(qi + 1) * tq, :] * (A_DH ** -0.5)
        q1 = jnp.where(lane < A_DH, q, 0.0).astype(BF16)
        q2 = jnp.where(lane >= A_DH, q, 0.0).astype(BF16)
        kk = kb[:n_keys]
        row = lax.broadcasted_iota(jnp.int32, (tq, n_keys), 0) + qi * tq
        col = lax.broadcasted_iota(jnp.int32, (tq, n_keys), 1)
        dist = (row - col).astype(F32)
        bias = jnp.where(dist >= 0.0, -slope * dist, NEG_INF)
        e1, l1 = _softmax_rows(lax.dot_general(q1, kk, nt, preferred_element_type=F32) + bias)
        e2, l2 = _softmax_rows(lax.dot_general(q2, kk, nt, preferred_element_type=F32) + bias)
        p = e1 * (1.0 / l1) - e2 * (lam / l2)
        o = jnp.dot(p.astype(BF16), vb[:n_keys], preferred_element_type=F32)
        o = _rms_rows(o, subln_ref[...]) * (1.0 - lam_init)
        o_ref[qi * tq:(qi + 1) * tq, :] = o.astype(BF16)


def _diff_attention_prompt(pa, n_batch, t, slopes, lam_stack, subln, lam_init, *, tq):
    kern = functools.partial(_diffattn_prompt_kernel, tq=tq, lam_init=lam_init)
    return pl.pallas_call(
        kern,
        grid=(n_batch, A_HEADS),
        in_specs=[
            pl.BlockSpec(memory_space=pltpu.SMEM),
            pl.BlockSpec((t, LANES), lambda b, h: (b, h)),
            pl.BlockSpec((t, LANES), lambda b, h: (b, A_HEADS + h)),
            pl.BlockSpec((t, LANES), lambda b, h: (b, 2 * A_HEADS + h)),
            pl.BlockSpec((4, A_DH), lambda b, h: (0, 0)),
            pl.BlockSpec((1, LANES), lambda b, h: (0, 0)),
        ],
        out_specs=pl.BlockSpec((t, LANES), lambda b, h: (b, h)),
        out_shape=jax.ShapeDtypeStruct((n_batch * t, A_HEADS * LANES), BF16),
        compiler_params=_cparams(("parallel", "parallel")),
        name="diff_attention_prompt",
    )(slopes, pa, pa, pa, lam_stack, subln.reshape(1, LANES))


PAGES_PER_STEP = 4


def _diffattn_sample_kernel(pt_ref, q_ref, kn_ref, vn_ref, *rest, past, lam_init):
    del pt_ref
    npg = PAGES_PER_STEP
    k_refs, v_refs = rest[:npg], rest[npg:2 * npg]
    lam_ref, subln_ref, o_ref, qq_scr, m_scr, l_scr, acc_scr, bias_scr = rest[2 * npg:]
    g = pl.program_id(1)
    ts = q_ref.shape[0]
    page = k_refs[0].shape[0]
    lane = lax.broadcasted_iota(jnp.int32, (1, LANES), 1)
    hsub = lax.broadcasted_iota(jnp.int32, (A_HEADS, LANES), 0)
    col = lax.broadcasted_iota(jnp.int32, (A_HEADS, LANES), 1)
    valid = (col // (2 * ts)) == hsub
    pos_q = past + jnp.bitwise_and(col, ts - 1)
    slope = jnp.exp2(-(8.0 / A_HEADS) * (hsub + 1).astype(F32))
    nt = (((1,), (1,)), ((), ()))
    tn = (((0,), (0,)), ((), ()))

    @pl.when(g == 0)
    def _():
        rows = []
        for h in range(A_HEADS):
            q = q_ref[:, h * LANES:(h + 1) * LANES] * (A_DH ** -0.5)
            rows += [jnp.where(lane < A_DH, q, 0.0), jnp.where(lane >= A_DH, q, 0.0)]
        qq_scr[...] = jnp.concatenate(rows, axis=0).astype(BF16)
        m_scr[...] = jnp.where(valid, NEG_INF, 0.0)
        l_scr[...] = jnp.zeros(l_scr.shape, F32)
        acc_scr[...] = jnp.zeros(acc_scr.shape, F32)

    def key_bias(nk, key0, causal):
        kidx = lax.broadcasted_iota(jnp.int32, (nk * A_HEADS, LANES), 0) // A_HEADS
        dist = (pos_q[None] - (key0 + kidx).reshape(nk, A_HEADS, LANES)).astype(F32)
        keep = valid[None] & (dist >= 0.0) if causal else valid[None]
        return jnp.where(keep, -slope[None] * dist, NEG_INF)

    @pl.when(g == 0)
    def _():
        bias_scr[...] = key_bias(page, 0, False)

    def scores(k3):
        nk = k3.shape[0]
        k2 = k3.reshape(nk * A_HEADS, LANES).astype(BF16)
        return lax.dot_general(k2, qq_scr[...], nt, preferred_element_type=F32).reshape(nk, A_HEADS, LANES)

    def absorb(ss, v3s):
        m_old = m_scr[...]
        m_new = m_old
        for s in ss:
            m_new = jnp.maximum(m_new, jnp.max(s, axis=0))
        alpha = jnp.exp(m_old - m_new)
        l_new = alpha * l_scr[...]
        acc = jnp.sum(jnp.where(valid, alpha, 0.0), axis=0, keepdims=True) * acc_scr[...]
        for s, v3 in zip(ss, v3s):
            nk = s.shape[0]
            pe = jnp.exp(s - m_new[None])
            l_new = l_new + jnp.sum(pe, axis=0)
            v2 = v3.reshape(nk * A_HEADS, LANES).astype(BF16)
            acc = acc + lax.dot_general(v2, pe.reshape(nk * A_HEADS, LANES).astype(BF16), tn,
                                        preferred_element_type=F32)
        m_scr[...] = m_new
        l_scr[...] = l_new
        acc_scr[...] = acc

    absorb([scores(k_refs[j][...]) + bias_scr[...] + (slope * ((g * npg + j) * page).astype(F32))[None]
            for j in range(npg)], [v_refs[j][...] for j in range(npg)])

    @pl.when(g == pl.num_programs(1) - 1)
    def _():
        absorb([scores(kn_ref[...]) + key_bias(ts, past, True)], [vn_ref[...]])
        lam = _diff_lambda_vec(lam_ref, lam_init)
        l_row = jnp.sum(jnp.where(valid, l_scr[...], 0.0), axis=0, keepdims=True)
        ot = acc_scr[...] / l_row
        ot = ot - lam * pltpu.roll(ot, LANES - ts, 1)
        o = ot.T
        for h in range(A_HEADS):
            oh = _rms_rows(o[h * 2 * ts:h * 2 * ts + ts, :], subln_ref[...]) * (1.0 - lam_init)
            o_ref[:, h * LANES:(h + 1) * LANES] = oh.astype(BF16)


def _diff_attention_sample(pa, row0, n_batch, ts, cache_k, cache_v, layer_idx, page_table, lam_stack, subln,
                           lam_init):
    n_pages = page_table.shape[1]
    page = cache_k.shape[2]
    npg = PAGES_PER_STEP
    assert row0 % ts == 0 and n_pages % npg == 0 and 2 * ts * A_HEADS == LANES
    rb = row0 // ts
    w = A_HEADS * LANES
    new_rows = lambda c0: pa[row0:, c0:c0 + w].reshape(n_batch, ts, A_HEADS, LANES)
    kern = functools.partial(_diffattn_sample_kernel, past=n_pages * page, lam_init=lam_init)
    page_spec = lambda j: pl.BlockSpec((None, None, page, A_HEADS, LANES),
                                       lambda b, g, pt: (layer_idx, pt[b, g * npg + j], 0, 0, 0))
    new_spec = pl.BlockSpec((None, ts, A_HEADS, LANES), lambda b, g, pt: (b, 0, 0, 0))
    grid_spec = pltpu.PrefetchScalarGridSpec(
        num_scalar_prefetch=1,
        grid=(n_batch, n_pages // npg),
        in_specs=[pl.BlockSpec((ts, w), lambda b, g, pt: (rb + b, 0)), new_spec, new_spec]
        + [page_spec(j) for j in range(npg)] * 2
        + [pl.BlockSpec((4, A_DH), lambda b, g, pt: (0, 0)), pl.BlockSpec((1, LANES), lambda b, g, pt: (0, 0))],
        out_specs=pl.BlockSpec((ts, w), lambda b, g, pt: (b, 0)),
        scratch_shapes=[
            pltpu.VMEM((LANES, LANES), BF16),
            pltpu.VMEM((A_HEADS, LANES), F32),
            pltpu.VMEM((A_HEADS, LANES), F32),
            pltpu.VMEM((LANES, LANES), F32),
            pltpu.VMEM((page, A_HEADS, LANES), F32),
        ],
    )
    return pl.pallas_call(
        kern,
        grid_spec=grid_spec,
        out_shape=jax.ShapeDtypeStruct((n_batch * ts, w), BF16),
        compiler_params=_cparams(("parallel", "arbitrary")),
        name="diff_attention_sample",
    )(page_table, pa, new_rows(w), new_rows(2 * w), *([cache_k] * npg), *([cache_v] * npg), lam_stack,
      subln.reshape(1, LANES))


def _cross_attn_kernel(q_ref, k_ref, v_ref, qn_ref, o_ref):
    nt = (((1,), (1,)), ((), ()))
    for h in range(X_HEADS):
        sl = slice(h * LANES, (h + 1) * LANES)
        q = _rms_rows(q_ref[:, sl], qn_ref[...]) * (X_DH ** -0.5)
        s = lax.dot_general(q.astype(BF16), k_ref[:, sl].astype(BF16), nt, preferred_element_type=F32)
        e, l = _softmax_rows(s)
        p = (e * (1.0 / l)).astype(BF16)
        o_ref[:, sl] = jnp.dot(p, v_ref[:, sl].astype(BF16), preferred_element_type=F32).astype(BF16)


def _cross_attention(q, row0, n_batch, t, mk, mv, q_norm, *, tq):
    assert t % tq == 0 and row0 % tq == 0
    nq = t // tq
    rb = row0 // tq
    return pl.pallas_call(
        _cross_attn_kernel,
        grid=(n_batch, nq),
        in_specs=[
            pl.BlockSpec((tq, X_W), lambda b, i: (rb + b * nq + i, 0)),
            pl.BlockSpec((None, N_MEM, X_W), lambda b, i: (b, 0, 0)),
            pl.BlockSpec((None, N_MEM, X_W), lambda b, i: (b, 0, 0)),
            pl.BlockSpec((1, LANES), lambda b, i: (0, 0)),
        ],
        out_specs=pl.BlockSpec((tq, X_W), lambda b, i: (b * nq + i, 0)),
        out_shape=jax.ShapeDtypeStruct((n_batch * t, X_W), BF16),
        compiler_params=_cparams(("parallel", "parallel")),
        name="cross_attention",
    )(q, mk, mv, q_norm.reshape(1, LANES))


PB_LG = 3 * B_W
PB_LW = PB_LG + B_GATE_LORA
PB_LA = PB_LW + LANES
PB_COLS = PB_LA + LANES


def _swap_halves(x):
    tiles = [pltpu.roll(x[:, c * LANES:(c + 1) * LANES], LANES // 2, 1) for c in range(x.shape[1] // LANES)]
    return tiles[0] if len(tiles) == 1 else jnp.concatenate(tiles, axis=1)


def _rwkv_prep_kernel(pb_ref, halo_ref, start_ref, mu_ref, w0_ref, w2_ref, a0_ref, a2_ref, g2_ref, kk_ref, ka_ref,
                      r_ref, kf_ref, v_ref, g_ref, rt_ref, at_ref, bt_ref, kt_ref, bh_ref, kh_ref, pc_ref,
                      *, n_prompt_tiles, chunk_p, chunk_s, t_p, t_s):
    tm = pb_ref.shape[0]
    i = pl.program_id(0)
    pb = pb_ref[...]
    row = lax.broadcasted_iota(jnp.int32, (tm, 1), 0)
    prev = jnp.where(row == 0, halo_ref[HALO - 1:HALO, :], pltpu.roll(pb, 1, 0))
    seq_len = jnp.where(i < n_prompt_tiles, t_p, t_s)
    is_start = jnp.bitwise_and(i * tm + row, seq_len - 1) == 0
    ps = jnp.where(is_start, start_ref[...], prev)
    xs = pb + (ps - pb) * mu_ref[...]
    r = xs[:, :B_W]
    k = xs[:, B_W:2 * B_W]
    lg = xs[:, PB_LG:PB_LW]
    lw_in = xs[:, PB_LW:PB_LA]
    la = xs[:, PB_LA:PB_COLS]
    wl = w0_ref[...] + jnp.dot(jnp.tanh(lw_in).astype(BF16), w2_ref[...], preferred_element_type=F32)
    lw = -jnp.exp(-_softplus(-wl) - 0.5)
    a = _sigmoid(a0_ref[...] + jnp.dot(la.astype(BF16), a2_ref[...], preferred_element_type=F32))
    kk = _group_norm_scale(k * kk_ref[...], B_DH, 1.0, EPS)
    kf = k * (1.0 + (a - 1.0) * ka_ref[...])
    b = kk * a
    r_ref[...] = r
    kf_ref[...] = kf
    v_ref[...] = xs[:, 2 * B_W:3 * B_W]
    g_ref[...] = jnp.dot(_sigmoid(lg).astype(BF16), g2_ref[...], preferred_element_type=F32)

    chunk = jnp.where(pl.program_id(0) < n_prompt_tiles, chunk_p, chunk_s)
    rowc = jnp.bitwise_and(lax.broadcasted_iota(jnp.int32, (tm, B_W), 0), chunk - 1)
    lp = lw
    sfx = lw
    step = 1
    while step < max(chunk_p, chunk_s):
        lp = lp + jnp.where(rowc >= step, pltpu.roll(lp, step, 0), 0.0)
        sfx = sfx + jnp.where(rowc + step < chunk, pltpu.roll(sfx, tm - step, 0), 0.0)
        step *= 2
    sfx = sfx - lw
    e_neg = jnp.exp(-lp)
    e_sfx = jnp.exp(sfx)
    sw = lambda z: _swap_halves(z).astype(BF16)
    rt_ref[...] = sw(r * jnp.exp(lp))
    at_ref[...] = sw(kk * jnp.exp(lp - lw))
    bt_ref[...] = sw(b * e_neg)
    kt_ref[...] = sw(kf * e_neg)
    bh_ref[...] = sw(b * e_sfx)
    kh_ref[...] = sw(kf * e_sfx)
    pc_ref[...] = _swap_halves(jnp.exp(lp + sfx))


def _rwkv_prep(pb, starts, mu, w0, w2, a0, a2, g2, k_k, k_a, *, tm, n_prompt_tiles, chunk_p, chunk_s, t_p, t_s):
    m = pb.shape[0]
    assert tm % chunk_p == 0 and tm % chunk_s == 0 and tm % HALO == 0
    assert t_p & (t_p - 1) == 0 and t_s & (t_s - 1) == 0 and (n_prompt_tiles * tm) % t_s == 0
    npt = n_prompt_tiles
    row = lambda n: pl.BlockSpec((1, n), lambda i: (0, 0))
    full = lambda a: pl.BlockSpec(a.shape, lambda i: (0, 0))
    tok = lambda n: pl.BlockSpec((tm, n), lambda i: (i, 0))
    f32o = jax.ShapeDtypeStruct((m, B_W), F32)
    b16o = jax.ShapeDtypeStruct((m, B_W), BF16)
    kern = functools.partial(_rwkv_prep_kernel, n_prompt_tiles=npt, chunk_p=chunk_p, chunk_s=chunk_s, t_p=t_p,
                             t_s=t_s)
    return pl.pallas_call(
        kern,
        grid=(m // tm,),
        in_specs=[tok(PB_COLS),
                  pl.BlockSpec((HALO, PB_COLS), lambda i: (jnp.maximum(i * (tm // HALO) - 1, 0), 0)),
                  pl.BlockSpec((tm, PB_COLS), lambda i: (jnp.maximum(i - npt + 1, 0), 0)),
                  row(PB_COLS), row(B_W), full(w2), row(B_W), full(a2), full(g2), row(B_W), row(B_W)],
        out_specs=[tok(B_W)] * 11,
        out_shape=[f32o] * 4 + [b16o] * 6 + [f32o],
        compiler_params=_cparams(("parallel",)),
        name="rwkv_prep",
    )(pb, pb, starts, mu, w0.reshape(1, B_W), w2, a0.reshape(1, B_W), a2, g2, k_k.reshape(1, B_W),
      k_a.reshape(1, B_W))


SOLVE_BLOCK = 16


def _split2(a):
    hi = a.astype(BF16)
    return hi, (a - hi.astype(F32)).astype(BF16)


def _solve_unit_lower_many(lmats, rhss):
    ns = len(lmats)
    c = lmats[0].shape[0]
    n = rhss[0].shape[1]
    bs = min(SOLVE_BLOCK, c)
    col = lax.broadcasted_iota(jnp.int32, (bs, c), 1)
    d = lambda x, y: jnp.dot(x, y, preferred_element_type=F32)
    done = [[] for _ in range(ns)]
    for blk in range(c // bs):
        rows = slice(blk * bs, (blk + 1) * bs)
        lrows = [lm[rows, :] for lm in lmats]
        rs = [rh[rows, :] for rh in rhss]
        if blk > 0:
            for s in range(ns):
                xs = jnp.concatenate(done[s] + [jnp.zeros((c - blk * bs, n), F32)], axis=0)
                lh, ll = _split2(jnp.where(col < blk * bs, lrows[s], 0.0))
                xh, xl = _split2(xs)
                rs[s] = rs[s] - (d(lh, xl) + d(ll, xh) + d(lh, xh))
        for j in range(bs - 1):
            for s in range(ns):
                rs[s] = rs[s] - lrows[s][:, blk * bs + j:blk * bs + j + 1] * rs[s][j:j + 1, :]
        for s in range(ns):
            done[s].append(rs[s])
    return [dn[0] if len(dn) == 1 else jnp.concatenate(dn, axis=0) for dn in done]


def _solve_unit_lower(lmat, rhs):
    return _solve_unit_lower_many([lmat], [rhs])[0]


def _rwkv_chunk_kernel(rt_ref, at_ref, bt_ref, kt_ref, bh_ref, kh_ref, v_ref, pc_ref, z0_ref, y_ref, zo_ref, z_scr,
                       *, chunk):
    i = pl.program_id(2)
    tb = rt_ref.shape[0]

    @pl.when(i == 0)
    def _():
        z_scr[...] = z0_ref[...]

    lo = lax.broadcasted_iota(jnp.int32, (1, LANES), 1) < B_DH
    zr = lax.broadcasted_iota(jnp.int32, (LANES, LANES), 0) < B_DH
    zc = lax.broadcasted_iota(jnp.int32, (LANES, LANES), 1) < B_DH
    offdiag = zr != zc
    ri = lax.broadcasted_iota(jnp.int32, (chunk, chunk), 0)
    ci = lax.broadcasted_iota(jnp.int32, (chunk, chunk), 1)
    nt = (((1,), (1,)), ((), ()))
    tn = (((0,), (0,)), ((), ()))
    d = lambda x, y: jnp.dot(x, y, preferred_element_type=F32)
    zero = jnp.zeros((), BF16)
    n_chunks = tb // chunk
    n_pp = rt_ref.shape[1] // LANES
    lmats, rhss, rkvs, rbs = [], [], [], []
    for p in range(n_pp):
        ls = slice(p * LANES, (p + 1) * LANES)
        for c in range(n_chunks):
            rs = slice(c * chunk, (c + 1) * chunk)
            rt, at, bt, kt = rt_ref[rs, ls], at_ref[rs, ls], bt_ref[rs, ls], kt_ref[rs, ls]
            v = v_ref[rs, ls].astype(BF16)
            for head in range(2):
                kmask = lo if head == 1 else jnp.logical_not(lo)
                a_h, r_h = jnp.where(kmask, at, zero), jnp.where(kmask, rt, zero)
                b_h, k_h = jnp.where(kmask, bt, zero), jnp.where(kmask, kt, zero)
                v_h = jnp.where(kmask, zero, v)
                lhs = jnp.concatenate([a_h, r_h], axis=0)
                gb = lax.dot_general(lhs, b_h, nt, preferred_element_type=F32)
                gk = lax.dot_general(lhs, k_h, nt, preferred_element_type=F32)
                lmats.append(jnp.where(ri > ci, gb[:chunk], 0.0))
                rbs.append(jnp.where(ri >= ci, gb[chunk:], 0.0).astype(BF16))
                ak = jnp.where(ri > ci, gk[:chunk], 0.0).astype(BF16)
                rk = jnp.where(ri >= ci, gk[chunk:], 0.0).astype(BF16)
                rhss.append(a_h.astype(F32) + d(ak, v_h))
                rkvs.append(d(rk, v_h))
    xs = _solve_unit_lower_many(lmats, rhss)
    zs = [z_scr[p] for p in range(n_pp)]
    for c in range(n_chunks):
        rs = slice(c * chunk, (c + 1) * chunk)
        for p in range(n_pp):
            ls = slice(p * LANES, (p + 1) * LANES)
            s0 = 2 * (p * n_chunks + c)
            rt = rt_ref[rs, ls]
            stacked = jnp.concatenate([xs[s0].astype(BF16), xs[s0 + 1].astype(BF16),
                                       jnp.where(lo, zero, rt), jnp.where(lo, rt, zero)], axis=0)
            big = lax.dot_general(stacked, zs[p].astype(BF16), nt, preferred_element_type=F32)
            u0 = jnp.where(lo, big[:chunk], 0.0)
            u1 = jnp.where(lo, 0.0, big[chunk:2 * chunk])
            y = jnp.where(lo, big[2 * chunk:3 * chunk], big[3 * chunk:]) + rkvs[s0] + rkvs[s0 + 1]
            y = y - d(rbs[s0], u0.astype(BF16)) - d(rbs[s0 + 1], u1.astype(BF16))
            y_ref[rs, ls] = y
            upd = lax.dot_general(jnp.concatenate([v_ref[rs, ls].astype(BF16), (u0 + u1).astype(BF16)], axis=0),
                                  jnp.concatenate([kh_ref[rs, ls], -bh_ref[rs, ls]], axis=0), tn,
                                  preferred_element_type=F32)
            zs[p] = jnp.where(offdiag, zs[p] * pc_ref[c * chunk:c * chunk + 1, ls] + upd, zs[p])
    for p in range(n_pp):
        z_scr[p] = zs[p]

    @pl.when(i == pl.num_programs(2) - 1)
    def _():
        zo_ref[...] = z_scr[...]


RWKV_PAIRS_PER_STEP = 4


def _rwkv_chunks(rt, at, bt, kt, bh, kh, v, pc, z0, row0, n_seq, t, *, tb, chunk):
    assert t % tb == 0 and row0 % tb == 0 and tb % chunk == 0
    nb = t // tb
    rb = row0 // tb
    n_pairs = B_HEADS // 2
    pps = RWKV_PAIRS_PER_STEP
    tok = pl.BlockSpec((tb, pps * LANES), lambda s, p, i: (rb + s * nb + i, p))
    st = pl.BlockSpec((None, pps, LANES, LANES), lambda s, p, i: (s, p, 0, 0))
    return pl.pallas_call(
        functools.partial(_rwkv_chunk_kernel, chunk=chunk),
        grid=(n_seq, n_pairs // pps, nb),
        in_specs=[tok] * 8 + [st],
        out_specs=[pl.BlockSpec((tb, pps * LANES), lambda s, p, i: (s * nb + i, p)), st],
        out_shape=[jax.ShapeDtypeStruct((n_seq * t, B_W), F32),
                   jax.ShapeDtypeStruct((n_seq, n_pairs, LANES, LANES), F32)],
        scratch_shapes=[pltpu.VMEM((pps, LANES, LANES), F32)],
        compiler_params=_cparams(("parallel", "parallel", "arbitrary")),
        name="rwkv_chunks",
    )(rt, at, bt, kt, bh, kh, v, pc, z0)


def _rwkv_post_kernel(y_ref, r_ref, kf_ref, v_ref, g_ref, lw_ref, lb_ref, rk_ref, o_ref):
    y = y_ref[...]
    mean = _group_sum(y, B_DH) * (1.0 / B_DH)
    yc = y - mean
    var = _group_sum(yc * yc, B_DH) * (1.0 / B_DH)
    yn = yc * lax.rsqrt(var + B_GN_EPS) * lw_ref[...] + lb_ref[...]
    bonus = _group_sum(r_ref[...] * kf_ref[...] * rk_ref[...], B_DH)
    o_ref[...] = ((yn + bonus * v_ref[...]) * g_ref[...]).astype(BF16)


def _rwkv_post(y, r, kf, v, g, lnx_w, lnx_b, r_k, *, tm):
    m = y.shape[0]
    tok = pl.BlockSpec((tm, B_W), lambda i: (i, 0))
    row = pl.BlockSpec((1, B_W), lambda i: (0, 0))
    return pl.pallas_call(
        _rwkv_post_kernel,
        grid=(m // tm,),
        in_specs=[tok] * 5 + [row] * 3,
        out_specs=tok,
        out_shape=jax.ShapeDtypeStruct((m, B_W), BF16),
        compiler_params=_cparams(("parallel",)),
        name="rwkv_post",
    )(y, r, kf, v, g, lnx_w.reshape(1, B_W), lnx_b.reshape(1, B_W), r_k.reshape(1, B_W))


def _rwkv_cols(a):
    lw0 = 3 * B_W
    la0 = lw0 + B_DECAY_LORA
    lg0 = la0 + B_AAA_LORA
    z = jnp.zeros(a.shape[:-1] + (LANES - B_DECAY_LORA,), a.dtype)
    return jnp.concatenate([a[..., :lw0], a[..., lg0:], a[..., lw0:la0], z, a[..., la0:lg0], z], axis=-1)


def _rwkv_cols_inv(a):
    return jnp.concatenate([a[..., :PB_LG], a[..., PB_LW:PB_LW + B_DECAY_LORA],
                            a[..., PB_LA:PB_LA + B_AAA_LORA], a[..., PB_LG:PB_LW]], axis=-1)


def _pad_rows(a, n):
    return jnp.pad(a, ((0, n - a.shape[0]), (0, 0)))


def _rwkv_state_to_pairs(s):
    n = s.shape[0]
    s = s.reshape(n, B_HEADS // 2, 2, B_DH, B_DH)
    eye = jnp.broadcast_to(jnp.eye(B_DH, dtype=F32), (n, B_HEADS // 2, B_DH, B_DH))
    top = jnp.concatenate([eye, s[:, :, 0]], axis=-1)
    bot = jnp.concatenate([s[:, :, 1], eye], axis=-1)
    return jnp.concatenate([top, bot], axis=-2)


def _rwkv_state_from_pairs(z):
    n = z.shape[0]
    s = jnp.stack([z[:, :, :B_DH, B_DH:], z[:, :, B_DH:, :B_DH]], axis=2)
    return s.reshape(n, B_HEADS, B_DH, B_DH)


def _rwkv_mixer(pb, n_p, t_p, n_s, t_s, prev_s, state_s, mu, w0, w2, a0, a2, g2, k_k, k_a, r_k, lnx_w, lnx_b,
                *, tm, tb):
    mp = n_p * t_p
    chunk_p = min(64, t_p)
    chunk_s = min(64, t_s)
    assert mp % tm == 0 and (n_s * t_s) % tm == 0 and t_p % chunk_p == 0 and t_s % chunk_s == 0
    starts = jnp.concatenate([
        jnp.zeros((tm, PB_COLS), F32),
        jnp.broadcast_to(_rwkv_cols(prev_s), (n_s, t_s, PB_COLS)).reshape(n_s * t_s, PB_COLS)], axis=0)
    r, kf, v, g, rt, at, bt, kt, bh, kh, pc = _rwkv_prep(
        pb, starts, _rwkv_cols(mu.reshape(1, B_COLS)), w0, _pad_rows(w2, LANES).astype(BF16), a0,
        _pad_rows(a2, LANES).astype(BF16), g2.astype(BF16), k_k, k_a, tm=tm, n_prompt_tiles=mp // tm,
        chunk_p=chunk_p, chunk_s=chunk_s, t_p=t_p, t_s=t_s)
    z0_p = _rwkv_state_to_pairs(jnp.zeros((n_p, B_HEADS, B_DH, B_DH), F32))
    y_p, z_p = _rwkv_chunks(rt, at, bt, kt, bh, kh, v, pc, z0_p, 0, n_p, t_p, tb=tb, chunk=chunk_p)
    y_s, z_s = _rwkv_chunks(rt, at, bt, kt, bh, kh, v, pc, _rwkv_state_to_pairs(state_s), mp, n_s, t_s,
                            tb=t_s, chunk=chunk_s)
    y = jnp.concatenate([y_p, y_s], axis=0)
    ob = _rwkv_post(y, r, kf, v, g, lnx_w, lnx_b, r_k.reshape(B_W), tm=tm)
    return ob, _rwkv_state_from_pairs(z_p), _rwkv_state_from_pairs(z_s)


PC_Z = C_CONV_CH
PC_BETA = PC_Z + C_V
PC_COLS = PC_BETA + 2 * LANES
HALO = 8


def _chunk_cumsum(g, chunk):
    rowc = jnp.bitwise_and(lax.broadcasted_iota(jnp.int32, g.shape, 0), chunk - 1)
    k = 1
    while k < chunk:
        g = g + jnp.where(rowc >= k, pltpu.roll(g, k, 0), 0.0)
        k *= 2
    return g


def _transpose_rows(a):
    c = a.shape[0]
    if c < LANES:
        a = jnp.concatenate([a, jnp.zeros((LANES - c, LANES), F32)], axis=0)
    return a.T


def _pad_chunk_rows(a):
    c = a.shape[0]
    return a if c == LANES else jnp.concatenate([a, jnp.zeros((LANES - c, a.shape[1]), a.dtype)], axis=0)


def _gdn_prep_kernel(x_ref, halo_ref, st_ref, cw_ref, sm_ref, alog_ref, dtb_ref,
                     q_ref, k_ref, v_ref, beta_ref, gc_ref, *, chunk):
    i = pl.program_id(1)
    x = x_ref[...]
    tm = x.shape[0]
    halo = jnp.where(i == 0, st_ref[...], halo_ref[...])
    row8 = lax.broadcasted_iota(jnp.int32, (HALO, x.shape[1]), 0)
    acc = x * cw_ref[C_CONV - 1:C_CONV, :]
    for s in range(1, C_CONV):
        xs = pltpu.roll(x, s, 0)
        first = jnp.where(row8 < s, pltpu.roll(halo, s, 0), xs[:HALO])
        xs = first if tm == HALO else jnp.concatenate([first, xs[HALO:]], axis=0)
        acc = acc + xs * cw_ref[C_CONV - 1 - s:C_CONV - s, :]
    y = _silu(acc)
    q_ref[...] = _group_norm_scale(y[:, :C_QK], C_DH, 1.0, EPS) * (C_DH ** -0.5)
    k_ref[...] = _group_norm_scale(y[:, C_QK:2 * C_QK], C_DH, 1.0, EPS)
    v_ref[...] = y[:, 2 * C_QK:]
    sm = sm_ref[...]
    beta_ref[...] = _sigmoid(sm[:, :LANES])
    g = -jnp.exp(alog_ref[...]) * _softplus(sm[:, LANES:] + dtb_ref[...])
    gc_ref[...] = _chunk_cumsum(g, chunk)


def _gdn_prep(pc, conv_state, row0, n_seq, t, conv_w, a_log, dt_bias, *, tm, chunk):
    assert t % tm == 0 and row0 % tm == 0 and tm % HALO == 0 and tm % chunk == 0 and chunk & (chunk - 1) == 0
    nb = t // tm
    rb = row0 // tm
    hb = tm // HALO
    tok_in = lambda n, cb: pl.BlockSpec((tm, n), lambda s, i: (rb + s * nb + i, cb))
    tok_out = lambda n: pl.BlockSpec((tm, n), lambda s, i: (s * nb + i, 0))
    row = lambda n: pl.BlockSpec((1, n), lambda s, i: (0, 0))
    sds = lambda n: jax.ShapeDtypeStruct((n_seq * t, n), F32)
    lanes8 = lambda a: jnp.zeros((1, LANES), F32).at[0, :C_V_HEADS].set(a)
    return pl.pallas_call(
        functools.partial(_gdn_prep_kernel, chunk=chunk),
        grid=(n_seq, nb),
        in_specs=[
            tok_in(C_CONV_CH, 0),
            pl.BlockSpec((HALO, C_CONV_CH), lambda s, i: (jnp.maximum((rb + s * nb + i) * hb - 1, 0), 0)),
            pl.BlockSpec((None, HALO, C_CONV_CH), lambda s, i: (s, 0, 0)),
            pl.BlockSpec((C_CONV, C_CONV_CH), lambda s, i: (0, 0)),
            tok_in(2 * LANES, PC_BETA // (2 * LANES)),
            row(LANES), row(LANES),
        ],
        out_specs=[tok_out(C_QK), tok_out(C_QK), tok_out(C_V), tok_out(LANES), tok_out(LANES)],
        out_shape=[sds(C_QK), sds(C_QK), sds(C_V), sds(LANES), sds(LANES)],
        compiler_params=_cparams(("parallel", "arbitrary")),
        name="gdn_prep",
    )(pc, pc, conv_state, conv_w, pc, lanes8(a_log), lanes8(dt_bias))


def _gdn_chunk_kernel(q_ref, k_ref, v_ref, beta_ref, gc_ref, z_ref, on_ref, s0_ref, o_ref, so_ref, s_scr, *, chunk):
    hq = pl.program_id(1)
    i = pl.program_id(2)
    tb = q_ref.shape[0]
    rep = v_ref.shape[1] // C_DH

    @pl.when(i == 0)
    def _():
        s_scr[...] = s0_ref[...]

    lane = lax.broadcasted_iota(jnp.int32, (chunk, LANES), 1)
    ri = lax.broadcasted_iota(jnp.int32, (chunk, chunk), 0)
    ci = lax.broadcasted_iota(jnp.int32, (chunk, chunk), 1)
    nt = (((1,), (1,)), ((), ()))
    ones = jnp.ones((chunk, LANES), BF16)
    n_chunks = tb // chunk
    lmats, rhss, attns, qgs, kds, eglast = [], [], [], [], [], []
    for c in range(n_chunks):
        rs = slice(c * chunk, (c + 1) * chunk)
        q, k = q_ref[rs, :], k_ref[rs, :]
        qb, kbf = q.astype(BF16), k.astype(BF16)
        kkt = lax.dot_general(kbf, kbf, nt, preferred_element_type=F32)
        qkt = lax.dot_general(qb, kbf, nt, preferred_element_type=F32)
        for j in range(rep):
            hsel = lane == hq * rep + j
            beta = jnp.sum(jnp.where(hsel, beta_ref[rs, :], 0.0), axis=-1, keepdims=True)
            gcol = jnp.sum(jnp.where(hsel, gc_ref[rs, :], 0.0), axis=-1, keepdims=True)
            g0 = jnp.where(lane == 0, gcol, 0.0)
            grow = sum(lax.dot_general(ones, part, nt, preferred_element_type=F32) for part in _split3(g0))
            dec = jnp.exp(jnp.where(ri >= ci, gcol - grow, NEG_INF))
            lmats.append(jnp.where(ri > ci, beta * kkt * dec, 0.0))
            rhss.append(jnp.concatenate([v_ref[rs, j * C_DH:(j + 1) * C_DH] * beta, k * (beta * jnp.exp(gcol))],
                                        axis=1))
            attns.append((qkt * dec).astype(BF16))
            qgs.append((q * jnp.exp(gcol)).astype(BF16))
            glast = gcol[chunk - 1:chunk, :]
            kds.append(_transpose_rows(k * jnp.exp(glast - gcol)).astype(BF16))
            eglast.append(jnp.exp(glast))
    xs = _solve_unit_lower_many(lmats, rhss)
    ss = [s_scr[j] for j in range(rep)]
    for c in range(n_chunks):
        rs = slice(c * chunk, (c + 1) * chunk)
        for j in range(rep):
            n = c * rep + j
            sb = ss[j].astype(BF16)
            v_new = xs[n][:, :C_DH] - jnp.dot(xs[n][:, C_DH:].astype(BF16), sb, preferred_element_type=F32)
            vb = v_new.astype(BF16)
            o = jnp.dot(qgs[n], sb, preferred_element_type=F32) + jnp.dot(attns[n], vb, preferred_element_type=F32)
            ss[j] = ss[j] * eglast[n] + jnp.dot(kds[n], _pad_chunk_rows(vb), preferred_element_type=F32)
            o = _rms_rows(o, on_ref[...]) * _silu(z_ref[rs, j * C_DH:(j + 1) * C_DH])
            o_ref[rs, j * C_DH:(j + 1) * C_DH] = o.astype(BF16)
    for j in range(rep):
        s_scr[j] = ss[j]

    @pl.when(i == pl.num_programs(2) - 1)
    def _():
        so_ref[...] = s_scr[...]


def _gdn_chunks(q, k, v, beta, gc, pc, row0, n_seq, t, onorm, s0, *, tb, chunk):
    assert t % tb == 0 and row0 % tb == 0 and tb % chunk == 0
    nb = t // tb
    rb = row0 // tb
    rep = C_V_HEADS // C_QK_HEADS
    tok = lambda n, cb: pl.BlockSpec((tb, n), cb)
    st = pl.BlockSpec((None, rep, C_DH, C_DH), lambda s, h, i: (s, h, 0, 0))
    return pl.pallas_call(
        functools.partial(_gdn_chunk_kernel, chunk=chunk),
        grid=(n_seq, C_QK_HEADS, nb),
        in_specs=[
            tok(C_DH, lambda s, h, i: (s * nb + i, h)),
            tok(C_DH, lambda s, h, i: (s * nb + i, h)),
            tok(rep * C_DH, lambda s, h, i: (s * nb + i, h)),
            tok(LANES, lambda s, h, i: (s * nb + i, 0)),
            tok(LANES, lambda s, h, i: (s * nb + i, 0)),
            tok(rep * C_DH, lambda s, h, i: (rb + s * nb + i, PC_Z // (rep * C_DH) + h)),
            pl.BlockSpec((1, LANES), lambda s, h, i: (0, 0)),
            st,
        ],
        out_specs=[tok(rep * C_DH, lambda s, h, i: (s * nb + i, h)), st],
        out_shape=[jax.ShapeDtypeStruct((n_seq * t, C_V), BF16),
                   jax.ShapeDtypeStruct((n_seq, C_V_HEADS, C_DH, C_DH), F32)],
        scratch_shapes=[pltpu.VMEM((rep, C_DH, C_DH), F32)],
        compiler_params=_cparams(("parallel", "parallel", "arbitrary")),
        name="gdn_chunks",
    )(q, k, v, beta, gc, pc, onorm.reshape(1, LANES), s0)


def _gdn_group(pc, conv_state3, s0, row0, n_seq, t, conv_w, a_log, dt_bias, onorm, *, tm, tb):
    chunk = C_CHUNK if t % C_CHUNK == 0 else t
    conv_state = jnp.pad(conv_state3, ((0, 0), (HALO - (C_CONV - 1), 0), (0, 0)))
    q, k, v, beta, gc = _gdn_prep(pc, conv_state, row0, n_seq, t, conv_w, a_log, dt_bias, tm=tm, chunk=chunk)
    return _gdn_chunks(q, k, v, beta, gc, pc, row0, n_seq, t, onorm, s0, tb=tb, chunk=chunk)


PD_V = 2 * D_K
PD_GATE = PD_V + D_V
PD_GLR = PD_GATE + D_V
PD_COLS = PD_GLR + LANES


def _gla_kernel(q_ref, k_ref, v_ref, gate_ref, glr_ref, w2_ref, bgk_ref, on_ref, s0_ref, o_ref, so_ref, s_scr,
                *, chunk):
    i = pl.program_id(2)
    tb = q_ref.shape[0]

    @pl.when(i == 0)
    def _():
        s_scr[...] = s0_ref[...]

    pre = jnp.dot(glr_ref[...].astype(BF16), w2_ref[...], preferred_element_type=F32) + bgk_ref[...]
    bcum = _chunk_cumsum(-_softplus(-pre) * (1.0 / D_GATE_NORM), chunk)
    ri = lax.broadcasted_iota(jnp.int32, (chunk, 1), 0)
    ci = lax.broadcasted_iota(jnp.int32, (chunk, chunk), 1)
    intra, qgs, decays, upds = [], [], [], []
    for c in range(tb // chunk):
        rs = slice(c * chunk, (c + 1) * chunk)
        q = q_ref[rs, :] * (D_DK ** -0.5)
        k, bc = k_ref[rs, :], bcum[rs, :]
        vb = v_ref[rs, :].astype(BF16)
        attn = jnp.zeros((chunk, chunk), F32)
        for j in range(chunk):
            e = jnp.exp(jnp.where(ri >= j, bc - bc[j:j + 1, :], NEG_INF))
            col = jnp.sum(q * k[j:j + 1, :] * e, axis=-1, keepdims=True)
            attn = jnp.where(ci == j, col, attn)
        intra.append(jnp.dot(attn.astype(BF16), vb, preferred_element_type=F32))
        qgs.append((q * jnp.exp(bc)).astype(BF16))
        blast = bc[chunk - 1:chunk, :]
        ebt = jnp.broadcast_to(jnp.exp(blast), (LANES, LANES)).T
        decays.append(jnp.concatenate([ebt, ebt], axis=1))
        kdt = _transpose_rows(k * jnp.exp(blast - bc)).astype(BF16)
        upds.append(jnp.dot(kdt, _pad_chunk_rows(vb), preferred_element_type=F32))
    s = s_scr[...]
    for c in range(tb // chunk):
        rs = slice(c * chunk, (c + 1) * chunk)
        o = jnp.dot(qgs[c], s.astype(BF16), preferred_element_type=F32) + intra[c]
        s = s * decays[c] + upds[c]
        o = _rms_rows(o, on_ref[...]) * _silu(gate_ref[rs, :])
        o_ref[rs, :] = o.astype(BF16)
    s_scr[...] = s

    @pl.when(i == pl.num_programs(2) - 1)
    def _():
        so_ref[...] = s_scr[...]


def _gla_group(pd, s0, row0, n_seq, t, w_gk2, b_gk, onorm, *, tb):
    chunk = D_CHUNK if t % D_CHUNK == 0 else t
    assert t % tb == 0 and row0 % tb == 0 and tb % chunk == 0 and chunk & (chunk - 1) == 0
    nb = t // tb
    rb = row0 // tb
    w2 = _pad_rows(w_gk2, LANES).astype(BF16)
    tok = lambda n, cb: pl.BlockSpec((tb, n), cb)
    st = pl.BlockSpec((None, None, D_DK, D_DV), lambda s, h, i: (s, h, 0, 0))
    return pl.pallas_call(
        functools.partial(_gla_kernel, chunk=chunk),
        grid=(n_seq, D_HEADS, nb),
        in_specs=[
            tok(D_DK, lambda s, h, i: (rb + s * nb + i, h)),
            tok(D_DK, lambda s, h, i: (rb + s * nb + i, D_K // D_DK + h)),
            tok(D_DV, lambda s, h, i: (rb + s * nb + i, PD_V // D_DV + h)),
            tok(D_DV, lambda s, h, i: (rb + s * nb + i, PD_GATE // D_DV + h)),
            tok(LANES, lambda s, h, i: (rb + s * nb + i, PD_GLR // LANES)),
            pl.BlockSpec((LANES, D_DK), lambda s, h, i: (0, h)),
            pl.BlockSpec((1, D_DK), lambda s, h, i: (0, h)),
            pl.BlockSpec((1, D_DV), lambda s, h, i: (0, 0)),
            st,
        ],
        out_specs=[tok(D_DV, lambda s, h, i: (s * nb + i, h)), st],
        out_shape=[jax.ShapeDtypeStruct((n_seq * t, D_V), BF16),
                   jax.ShapeDtypeStruct((n_seq, D_HEADS, D_DK, D_DV), F32)],
        scratch_shapes=[pltpu.VMEM((D_DK, D_DV), F32)],
        compiler_params=_cparams(("parallel", "parallel", "arbitrary")),
        name="gla_chunks",
    )(pd, pd, pd, pd, pd, w2, b_gk.reshape(1, D_K), onorm.reshape(1, D_DV), s0)


TM = 768
TM_SMALL = 256


def _cross_block(x, layer, n_p, t_p, n_s, t_s, mem_prompt, cache_mem_k, cache_mem_v, norm_mem_w, norm_cross_w,
                 x_w_q, x_w_k, x_w_v, x_w_o, x_q_norm, x_k_norm):
    d = x.shape[1]
    mp = n_p * t_p
    wkv = jnp.concatenate([x_w_k[layer], x_w_v[layer]], axis=1).astype(BF16)
    hw = jnp.concatenate([jnp.tile(x_k_norm[layer], X_HEADS), jnp.ones((X_W,), F32)]).reshape(1, 2 * X_W)
    mem = mem_prompt.reshape(n_p * N_MEM, d)
    kv = _norm_matmul(mem, norm_mem_w[layer], wkv, tm=min(512, n_p * N_MEM), tn=X_W, head_w=hw,
                      n_norm_cols=X_W, gsize=X_DH)
    mk_p = kv[:, :X_W].reshape(n_p, N_MEM, X_W)
    mv_p = kv[:, X_W:].reshape(n_p, N_MEM, X_W)
    qx = _norm_matmul(x, norm_cross_w[layer], x_w_q[layer].astype(BF16), tm=TM, tn=X_W)
    ca_p = _cross_attention(qx, 0, n_p, t_p, mk_p, mv_p, x_q_norm[layer], tq=min(512, t_p))
    ca_s = _cross_attention(qx, mp, n_s, t_s, cache_mem_k[layer].reshape(n_s, N_MEM, X_W),
                            cache_mem_v[layer].reshape(n_s, N_MEM, X_W), x_q_norm[layer], tq=t_s)
    ca = jnp.concatenate([ca_p, ca_s], axis=0)
    x = _matmul_residual([ca], [x_w_o[layer].astype(BF16)], x, tm=TM, tn=1024)
    return x, mk_p.reshape(n_p, N_MEM, X_HEADS, X_DH), mv_p.reshape(n_p, N_MEM, X_HEADS, X_DH)


def kernel(x_prompt, x_sample, cache_diff_k, cache_diff_v, state_rwkv, state_rwkv_shift, cache_mem_k, cache_mem_v, state_gdn, state_gdn_conv, state_gla, page_table, mem_prompt, norm_mix_w, norm_cross_w, norm_mem_w, norm_ffn_w, w_in_even, w_out_even, a_q_norm, a_k_norm, a_lam_q1, a_lam_k1, a_lam_q2, a_lam_k2, a_subln, b_mu, b_w0, b_w2, b_a0, b_a2, b_g2, b_k_k, b_k_a, b_r_k, b_lnx_w, b_lnx_b, ffd_w_gate, ffd_w_up, ffd_w_down, w_in_odd, w_out_odd, c_conv_w, c_a_log, c_dt_bias, c_onorm, d_w_gk2, d_b_gk, d_onorm, moe_w_router, moe_w_gate, moe_w_up, moe_w_down, x_w_q, x_w_k, x_w_v, x_w_o, x_q_norm, x_k_norm):
    n_p, t_p, d = x_prompt.shape
    n_s, t_s, _ = x_sample.shape
    mp, ms = n_p * t_p, n_s * t_s
    depth = norm_mix_w.shape[0]
    x = jnp.concatenate([x_prompt.reshape(mp, d), x_sample.reshape(ms, d)], axis=0)
    assert (mp + ms) % TM == 0 and (mp + ms) % TM_SMALL == 0
    page = cache_diff_k.shape[2]
    slopes = jnp.exp2(-(8.0 / A_HEADS) * jnp.arange(1, A_HEADS + 1, dtype=F32))
    cross_w = (mem_prompt, cache_mem_k, cache_mem_v, norm_mem_w, norm_cross_w, x_w_q, x_w_k, x_w_v, x_w_o,
               x_q_norm, x_k_norm)
    dk_p, dv_p, dk_s, dv_s, rw_p, rw_s, sh_p, sh_s = [], [], [], [], [], [], [], []
    mk_l, mv_l, gd_p, gd_s, cv_p, cv_s, gl_p, gl_s = [], [], [], [], [], [], [], []
    for layer in range(depth):
        i = layer // 2
        if layer % 2 == 0:
            lam_init = 0.8 - 0.6 * math.exp(-0.3 * layer)
            w_in = w_in_even[i]
            hw = jnp.concatenate([jnp.tile(a_q_norm[i], A_Q // A_DH), jnp.tile(a_k_norm[i], A_Q // A_DH),
                                  jnp.ones((A_COLS - 2 * A_Q,), F32)]).reshape(1, A_COLS)
            pa = _norm_matmul(x, norm_mix_w[layer], w_in[:, :A_COLS].astype(BF16), tm=TM, tn=1024, head_w=hw,
                              n_norm_cols=2 * A_Q, gsize=A_DH)
            pb = _norm_matmul(x, norm_mix_w[layer], _rwkv_cols(w_in[:, A_COLS:]).astype(BF16), tm=TM,
                              tn=PB_COLS // 2)
            lam_stack = jnp.stack([a_lam_q1[i], a_lam_k1[i], a_lam_q2[i], a_lam_k2[i]])
            oa_p = _diff_attention_prompt(pa, n_p, t_p, slopes, lam_stack, a_subln[i], lam_init, tq=min(256, t_p))
            oa_s = _diff_attention_sample(pa, mp, n_s, t_s, cache_diff_k, cache_diff_v, i, page_table, lam_stack,
                                          a_subln[i], lam_init)
            ob, st_p, st_s = _rwkv_mixer(pb, n_p, t_p, n_s, t_s, state_rwkv_shift[i], state_rwkv[i], b_mu[i],
                                         b_w0[i], b_w2[i], b_a0[i], b_a2[i], b_g2[i], b_k_k[i], b_k_a[i], b_r_k[i],
                                         b_lnx_w[i], b_lnx_b[i], tm=TM_SMALL, tb=min(256, t_p))
            dk_p.append(pa[:mp, A_Q:2 * A_Q].reshape(n_p, t_p, A_HEADS, 2 * A_DH))
            dv_p.append(pa[:mp, 2 * A_Q:].reshape(n_p, t_p, A_HEADS, 2 * A_DH))
            dk_s.append(pa[mp:, A_Q:2 * A_Q].reshape(n_s, t_s, A_HEADS, 2 * A_DH))
            dv_s.append(pa[mp:, 2 * A_Q:].reshape(n_s, t_s, A_HEADS, 2 * A_DH))
            rw_p.append(st_p)
            rw_s.append(st_s)
            sh_p.append(_rwkv_cols_inv(pb[:mp].reshape(n_p, t_p, PB_COLS)[:, -1:]))
            sh_s.append(_rwkv_cols_inv(pb[mp:].reshape(n_s, t_s, PB_COLS)[:, -1:]))
            oa = jnp.concatenate([oa_p, oa_s], axis=0)
            w_out = w_out_even[i].astype(BF16)
            x = _matmul_residual([oa, ob], [w_out[:A_Q], w_out[A_Q:]], x, tm=TM, tn=1024)
        else:
            w_in = w_in_odd[i]
            c_cols = C_CONV_CH + C_V + 2 * C_V_HEADS
            zc = jnp.zeros((d, LANES - C_V_HEADS), F32)
            wc = jnp.concatenate([w_in[:, :PC_BETA], w_in[:, PC_BETA:PC_BETA + C_V_HEADS], zc,
                                  w_in[:, PC_BETA + C_V_HEADS:c_cols], zc], axis=1).astype(BF16)
            wd_ = jnp.concatenate([w_in[:, c_cols:], jnp.zeros((d, LANES - D_GATE_LORA), F32)], axis=1).astype(BF16)
            pc = _norm_matmul(x, norm_mix_w[layer], wc, tm=TM, tn=PC_COLS // 2)
            pd = _norm_matmul(x, norm_mix_w[layer], wd_, tm=TM, tn=PD_COLS // 5)
            gdn_w = (c_conv_w[i], c_a_log[i], c_dt_bias[i], c_onorm[i])
            oc_p, gs_p = _gdn_group(pc, jnp.zeros((n_p, C_CONV - 1, C_CONV_CH), F32),
                                    jnp.zeros((n_p, C_V_HEADS, C_DH, C_DH), F32), 0, n_p, t_p, *gdn_w,
                                    tm=min(512, t_p), tb=min(512, t_p))
            oc_s, gs_s = _gdn_group(pc, state_gdn_conv[i], state_gdn[i], mp, n_s, t_s, *gdn_w, tm=t_s, tb=t_s)
            gla_w = (d_w_gk2[i], d_b_gk[i], d_onorm[i])
            od_p, ls_p = _gla_group(pd, jnp.zeros((n_p, D_HEADS, D_DK, D_DV), F32), 0, n_p, t_p, *gla_w,
                                    tb=min(256, t_p))
            od_s, ls_s = _gla_group(pd, state_gla[i], mp, n_s, t_s, *gla_w, tb=t_s)
            gd_p.append(gs_p)
            gd_s.append(gs_s)
            cv_p.append(pc[:mp].reshape(n_p, t_p, PC_COLS)[:, -(C_CONV - 1):, :C_CONV_CH])
            cv_s.append(pc[mp:].reshape(n_s, t_s, PC_COLS)[:, -(C_CONV - 1):, :C_CONV_CH])
            gl_p.append(ls_p)
            gl_s.append(ls_s)
            oc = jnp.concatenate([oc_p, oc_s], axis=0)
            od = jnp.concatenate([od_p, od_s], axis=0)
            w_out = w_out_odd[i].astype(BF16)
            x = _matmul_residual([oc, od], [w_out[:C_V], w_out[C_V:]], x, tm=TM, tn=1024)
        x, mk_p, mv_p = _cross_block(x, layer, n_p, t_p, n_s, t_s, *cross_w)
        mk_l.append(mk_p)
        mv_l.append(mv_p)
        if layer % 2 == 0:
            x = _swiglu_ffn(x, norm_ffn_w[layer], ffd_w_gate[i].astype(BF16), ffd_w_up[i].astype(BF16),
                            ffd_w_down[i].astype(BF16), tm=TM, tf=512)
        else:
            y_p, y_s = _moe_ffn(x, norm_ffn_w[layer], moe_w_router[i], moe_w_gate[i], moe_w_up[i], moe_w_down[i],
                                tm=TM, tf=256, tc=TM_SMALL, split=mp)
            if layer + 1 < depth:
                x = jnp.concatenate([y_p, y_s], axis=0)
    if depth % 2 == 1:
        y_p, y_s = x[:mp], x[mp:]
    st = jnp.stack
    return (y_p.reshape(n_p, t_p, d), y_s.reshape(n_s, t_s, d), st(dk_p), st(dv_p), st(dk_s), st(dv_s),
            st(rw_p), st(rw_s), st(sh_p), st(sh_s), st(mk_l), st(mv_l), st(gd_p), st(gd_s), st(cv_p), st(cv_s),
            st(gl_p), st(gl_s))
```

```python
import functools
import math

import jax
import jax.numpy as jnp
from jax import lax
from jax.experimental import pallas as pl
from jax.experimental.pallas import tpu as pltpu

F32 = jnp.float32
BF16 = jnp.bfloat16
NEG_INF = float("-inf")

D_MODEL = 2048
A_HEADS = 8
A_DH = 64
A_Q = 1024
A_COLS = 3072
B_HEADS = 16
B_DH = 64
B_W = 1024
B_DECAY_LORA = 96
B_AAA_LORA = 96
B_GATE_LORA = 256
B_COLS = 3520
B_GN_EPS = 64e-5
C_QK_HEADS = 4
C_V_HEADS = 8
C_DH = 128
C_CONV = 4
C_QK = 512
C_V = 1024
C_CONV_CH = 2048
C_CHUNK = 64
D_HEADS = 4
D_DK = 128
D_DV = 256
D_K = 512
D_V = 1024
D_GATE_LORA = 16
D_GATE_NORM = 16.0
D_CHUNK = 16
N_MEM = 256
X_HEADS = 4
X_DH = 128
X_W = 512
FF_DENSE = 5632
N_EXPERTS = 8
FF_EXPERT = 2816
EPS = 1e-6

LANES = 128
VMEM_LIMIT = 56 * 1024 * 1024


def _cparams(sem):
    return pltpu.CompilerParams(dimension_semantics=sem, vmem_limit_bytes=VMEM_LIMIT)


def _sigmoid(x):
    return 1.0 / (1.0 + jnp.exp(-x))


def _silu(x):
    return x * _sigmoid(x)


def _softplus(x):
    return jnp.maximum(x, 0.0) + jnp.log(1.0 + jnp.exp(-jnp.abs(x)))


def _rms_rows(x, w):
    ms = jnp.mean(x * x, axis=-1, keepdims=True)
    return x * lax.rsqrt(ms + EPS) * w


def _group_sum(x, gsize):
    lane = lax.broadcasted_iota(jnp.int32, (1, LANES), 1)
    outs = []
    for c in range(x.shape[1] // LANES):
        xc = x[:, c * LANES:(c + 1) * LANES]
        if gsize == LANES:
            ss = jnp.broadcast_to(jnp.sum(xc, axis=-1, keepdims=True), xc.shape)
        else:
            lo = jnp.sum(jnp.where(lane < 64, xc, 0.0), axis=-1, keepdims=True)
            hi = jnp.sum(jnp.where(lane >= 64, xc, 0.0), axis=-1, keepdims=True)
            ss = jnp.where(lane < 64, lo, hi)
        outs.append(ss)
    return outs[0] if len(outs) == 1 else jnp.concatenate(outs, axis=1)


def _group_norm_scale(x, gsize, inv_n, eps):
    return x * lax.rsqrt(_group_sum(x * x, gsize) * inv_n + eps)


def _split3(a):
    a1 = a.astype(BF16)
    r1 = a - a1.astype(F32)
    a2 = r1.astype(BF16)
    a3 = (r1 - a2.astype(F32)).astype(BF16)
    return a1, a2, a3


def _dot_f32(a, b):
    a1, a2, a3 = _split3(a)
    b1, b2, b3 = _split3(b)
    d = lambda x, y: jnp.dot(x, y, preferred_element_type=F32)
    return d(a1, b3) + d(a3, b1) + d(a2, b2) + d(a1, b2) + d(a2, b1) + d(a1, b1)


def _norm_mm_kernel(x_ref, nw_ref, w_ref, hw_ref, o_ref, h_scr, *, n_norm_tiles, gsize):
    j = pl.program_id(1)

    @pl.when(j == 0)
    def _():
        h_scr[...] = _rms_rows(x_ref[...], nw_ref[...]).astype(BF16)

    acc = jnp.dot(h_scr[...], w_ref[...], preferred_element_type=F32)
    if n_norm_tiles == 0:
        o_ref[...] = acc
    else:
        @pl.when(j < n_norm_tiles)
        def _():
            o_ref[...] = _group_norm_scale(acc, gsize, 1.0 / gsize, EPS) * hw_ref[...]

        @pl.when(j >= n_norm_tiles)
        def _():
            o_ref[...] = acc


def _norm_matmul(x, nw, w, *, tm, tn, head_w=None, n_norm_cols=0, gsize=LANES):
    m, k = x.shape
    n = w.shape[1]
    assert m % tm == 0 and n % tn == 0 and n_norm_cols % tn == 0
    if head_w is None:
        head_w = jnp.ones((1, n), F32)
    kern = functools.partial(_norm_mm_kernel, n_norm_tiles=n_norm_cols // tn, gsize=gsize)
    return pl.pallas_call(
        kern,
        grid=(m // tm, n // tn),
        in_specs=[
            pl.BlockSpec((tm, k), lambda i, j: (i, 0)),
            pl.BlockSpec((1, k), lambda i, j: (0, 0)),
            pl.BlockSpec((k, tn), lambda i, j: (0, j)),
            pl.BlockSpec((1, tn), lambda i, j: (0, j)),
        ],
        out_specs=pl.BlockSpec((tm, tn), lambda i, j: (i, j)),
        out_shape=jax.ShapeDtypeStruct((m, n), F32),
        scratch_shapes=[pltpu.VMEM((tm, k), BF16)],
        compiler_params=_cparams(("parallel", "arbitrary")),
        name="norm_matmul",
    )(x, nw.reshape(1, k), w, head_w)


def _mm_res_kernel(*refs, n_in):
    x_ref = refs[2 * n_in]
    o_ref = refs[2 * n_in + 1]
    acc = x_ref[...]
    for t in range(n_in):
        acc = acc + jnp.dot(refs[t][...], refs[n_in + t][...], preferred_element_type=F32)
    o_ref[...] = acc


def _matmul_residual(acts, weights, x, *, tm, tn):
    m, n = x.shape
    n_in = len(acts)
    in_specs = [pl.BlockSpec((tm, a.shape[1]), lambda i, j: (i, 0)) for a in acts]
    in_specs += [pl.BlockSpec((w.shape[0], tn), lambda i, j: (0, j)) for w in weights]
    in_specs += [pl.BlockSpec((tm, tn), lambda i, j: (i, j))]
    return pl.pallas_call(
        functools.partial(_mm_res_kernel, n_in=n_in),
        grid=(m // tm, n // tn),
        in_specs=in_specs,
        out_specs=pl.BlockSpec((tm, tn), lambda i, j: (i, j)),
        out_shape=jax.ShapeDtypeStruct((m, n), F32),
        compiler_params=_cparams(("parallel", "arbitrary")),
        name="matmul_residual",
    )(*acts, *weights, x)


def _swiglu_kernel(x_ref, nw_ref, wg_ref, wu_ref, wd_ref, o_ref, h_scr):
    f = pl.program_id(1)

    @pl.when(f == 0)
    def _():
        x = x_ref[...]
        h_scr[...] = _rms_rows(x, nw_ref[...]).astype(BF16)
        o_ref[...] = x

    h = h_scr[...]
    g = jnp.dot(h, wg_ref[...], preferred_element_type=F32)
    u = jnp.dot(h, wu_ref[...], preferred_element_type=F32)
    a = (_silu(g) * u).astype(BF16)
    o_ref[...] += jnp.dot(a, wd_ref[...], preferred_element_type=F32)


def _swiglu_ffn(x, nw, wg, wu, wd, *, tm, tf):
    m, d = x.shape
    ff = wg.shape[1]
    return pl.pallas_call(
        _swiglu_kernel,
        grid=(m // tm, ff // tf),
        in_specs=[
            pl.BlockSpec((tm, d), lambda i, f: (i, 0)),
            pl.BlockSpec((1, d), lambda i, f: (0, 0)),
            pl.BlockSpec((d, tf), lambda i, f: (0, f)),
            pl.BlockSpec((d, tf), lambda i, f: (0, f)),
            pl.BlockSpec((tf, d), lambda i, f: (f, 0)),
        ],
        out_specs=pl.BlockSpec((tm, d), lambda i, f: (i, 0)),
        out_shape=jax.ShapeDtypeStruct((m, d), F32),
        scratch_shapes=[pltpu.VMEM((tm, d), BF16)],
        compiler_params=_cparams(("parallel", "arbitrary")),
        name="swiglu_ffn",
    )(x, nw.reshape(1, d), wg, wu, wd)


MOE_TILE = 768
ROW_SLABS = D_MODEL // LANES


def _rows_to_slabs(ref, x):
    for j in range(ROW_SLABS):
        ref[:, j, :] = x[:, j * LANES:(j + 1) * LANES]


def _slabs_to_rows(ref):
    return jnp.concatenate([ref[:, j, :] for j in range(ROW_SLABS)], axis=1)


def _moe_router_kernel(x_ref, nw_ref, wr_ref, h_ref, info_ref):
    tm = x_ref.shape[0]
    lane = lax.broadcasted_iota(jnp.int32, (tm, LANES), 1)
    hf = _rms_rows(x_ref[...], nw_ref[...])
    _rows_to_slabs(h_ref, hf)
    logits = jnp.where(lane < N_EXPERTS, _dot_f32(hf, wr_ref[...]), NEG_INF)
    m1 = jnp.max(logits, axis=-1, keepdims=True)
    i1 = jnp.min(jnp.where(logits == m1, lane, LANES), axis=-1, keepdims=True)
    rest = jnp.where(lane == i1, NEG_INF, logits)
    m2 = jnp.max(rest, axis=-1, keepdims=True)
    i2 = jnp.min(jnp.where(rest == m2, lane, LANES), axis=-1, keepdims=True)
    e2 = jnp.exp(m2 - m1)
    vals = [i1.astype(F32), i2.astype(F32), 1.0 / (1.0 + e2), e2 / (1.0 + e2)]
    info = jnp.zeros((tm, LANES), F32)
    for k, val in enumerate(vals):
        info = jnp.where(lane == k, val, info)
    info_ref[...] = info


def _moe_gather_kernel(src_ref, h_ref, xs_ref, sem):
    t = pl.program_id(0)
    tg = xs_ref.shape[0]

    def copy(r):
        return pltpu.make_async_copy(h_ref.at[src_ref[t * tg + r]], xs_ref.at[r], sem.at[0])

    def issue(r, carry):
        @pl.when(src_ref[t * tg + r] >= 0)
        def _():
            copy(r).start()

        @pl.when(src_ref[t * tg + r] < 0)
        def _():
            xs_ref[r] = jnp.zeros(xs_ref.shape[1:], F32)
        return carry

    def drain(r, carry):
        @pl.when(src_ref[t * tg + r] >= 0)
        def _():
            copy(r).wait()
        return carry

    lax.fori_loop(0, tg, issue, 0)
    lax.fori_loop(0, tg, drain, 0)


def _moe_expert_kernel(te_ref, nu_ref, xs_ref, wg_ref, wu_ref, wd_ref, y_ref, xb_scr, acc_scr):
    del te_ref
    t = pl.program_id(0)
    f = pl.program_id(1)

    @pl.when(f == 0)
    def _():
        xb_scr[...] = _slabs_to_rows(xs_ref).astype(BF16)
        acc_scr[...] = jnp.zeros(acc_scr.shape, F32)

    @pl.when(t < nu_ref[0])
    def _():
        xb = xb_scr[...]
        g = jnp.dot(xb, wg_ref[...].astype(BF16), preferred_element_type=F32)
        u = jnp.dot(xb, wu_ref[...].astype(BF16), preferred_element_type=F32)
        acc_scr[...] += jnp.dot((_silu(g) * u).astype(BF16), wd_ref[...].astype(BF16),
                                preferred_element_type=F32)

    @pl.when(f == pl.num_programs(1) - 1)
    def _():
        _rows_to_slabs(y_ref, acc_scr[...])


def _moe_combine_kernel(dest_ref, x_ref, info_ref, y_ref, o1_ref, o2_ref, ybuf, sem, *, n_first):
    i = pl.program_id(0)
    tc = x_ref.shape[0]
    n = dest_ref.shape[0] // 2

    def copy(r, slot):
        return pltpu.make_async_copy(y_ref.at[dest_ref[slot * n + i * tc + r]], ybuf.at[slot, r], sem.at[slot])

    def issue(r, carry):
        copy(r, 0).start()
        copy(r, 1).start()
        return carry

    def drain(r, carry):
        copy(r, 0).wait()
        copy(r, 1).wait()
        return carry

    lax.fori_loop(0, tc, issue, 0)
    lax.fori_loop(0, tc, drain, 0)
    info = info_ref[...]
    y1 = _slabs_to_rows(ybuf.at[0])
    y2 = _slabs_to_rows(ybuf.at[1])
    res = x_ref[...] + info[:, 2:3] * y1 + info[:, 3:4] * y2

    @pl.when(i < n_first)
    def _():
        o1_ref[...] = res

    @pl.when(i >= n_first)
    def _():
        o2_ref[...] = res


def _moe_ffn(x, nw, w_router, wg, wu, wd, w_layer, *, tm, tf, tc, split):
    m, d = x.shape
    _, ne, _, ff = wg.shape
    tg = MOE_TILE
    assert d == D_MODEL and m % tm == 0 and m % tc == 0 and ff % tf == 0
    n_tiles = -(-(2 * m + ne * (tg - 1)) // tg)
    p_rows = n_tiles * tg
    wr = jnp.zeros((d, LANES), F32).at[:, :ne].set(w_router)
    h3, info = pl.pallas_call(
        _moe_router_kernel,
        grid=(m // tm,),
        in_specs=[pl.BlockSpec((tm, d), lambda i: (i, 0)), pl.BlockSpec((1, d), lambda i: (0, 0)),
                  pl.BlockSpec((d, LANES), lambda i: (0, 0))],
        out_specs=[pl.BlockSpec((tm, ROW_SLABS, LANES), lambda i: (i, 0, 0)),
                   pl.BlockSpec((tm, LANES), lambda i: (i, 0))],
        out_shape=[jax.ShapeDtypeStruct((m, ROW_SLABS, LANES), F32), jax.ShapeDtypeStruct((m, LANES), F32)],
        compiler_params=_cparams(("parallel",)),
        name="moe_router",
    )(x, nw.reshape(1, d), wr)

    e12 = info[:, :2].astype(jnp.int32)
    onehot = (e12[:, :1] == jnp.arange(ne)) | (e12[:, 1:2] == jnp.arange(ne))
    csum = jnp.cumsum(onehot.astype(jnp.int32), axis=0)
    rank = csum - onehot
    padded = (csum[-1] + tg - 1) // tg * tg
    gend = jnp.cumsum(padded)
    base = (gend - padded)[None, :] + rank
    dest = jnp.concatenate([jnp.take_along_axis(base, e12[:, :1], axis=1)[:, 0],
                            jnp.take_along_axis(base, e12[:, 1:2], axis=1)[:, 0]]).astype(jnp.int32)
    tile_expert = jnp.minimum(jnp.sum(jnp.arange(n_tiles)[:, None] * tg >= gend[None, :], axis=1),
                              ne - 1).astype(jnp.int32)
    n_used = (gend[-1:] // tg).astype(jnp.int32)

    any_spec = pl.BlockSpec(memory_space=pl.ANY)
    tok_ids = jnp.tile(jnp.arange(m, dtype=jnp.int32), 2)
    src = jnp.full((p_rows,), -1, jnp.int32).at[dest].set(tok_ids, unique_indices=True)
    xs3 = pl.pallas_call(
        _moe_gather_kernel,
        grid_spec=pltpu.PrefetchScalarGridSpec(
            num_scalar_prefetch=1, grid=(n_tiles,), in_specs=[any_spec],
            out_specs=pl.BlockSpec((tg, ROW_SLABS, LANES), lambda t, sr: (t, 0, 0)),
            scratch_shapes=[pltpu.SemaphoreType.DMA((1,))]),
        out_shape=jax.ShapeDtypeStruct((p_rows, ROW_SLABS, LANES), F32),
        compiler_params=_cparams(("arbitrary",)),
        name="moe_gather",
    )(src, h3)

    last = lambda t, nu: jnp.minimum(t, nu[0] - 1)
    y3 = pl.pallas_call(
        _moe_expert_kernel,
        grid_spec=pltpu.PrefetchScalarGridSpec(
            num_scalar_prefetch=2,
            grid=(n_tiles, ff // tf),
            in_specs=[
                pl.BlockSpec((tg, ROW_SLABS, LANES), lambda t, f, te, nu: (last(t, nu), 0, 0)),
                pl.BlockSpec((None, None, d, tf), lambda t, f, te, nu: (w_layer, te[last(t, nu)], 0, f)),
                pl.BlockSpec((None, None, d, tf), lambda t, f, te, nu: (w_layer, te[last(t, nu)], 0, f)),
                pl.BlockSpec((None, None, tf, d), lambda t, f, te, nu: (w_layer, te[last(t, nu)], f, 0)),
            ],
            out_specs=pl.BlockSpec((tg, ROW_SLABS, LANES), lambda t, f, te, nu: (t, 0, 0)),
            scratch_shapes=[pltpu.VMEM((tg, d), BF16), pltpu.VMEM((tg, d), F32)]),
        out_shape=jax.ShapeDtypeStruct((p_rows, ROW_SLABS, LANES), F32),
        compiler_params=_cparams(("arbitrary", "arbitrary")),
        name="moe_experts",
    )(tile_expert, n_used, xs3, wg, wu, wd)

    assert split % tc == 0 and 0 < split < m
    n_first = split // tc
    return pl.pallas_call(
        functools.partial(_moe_combine_kernel, n_first=n_first),
        grid_spec=pltpu.PrefetchScalarGridSpec(
            num_scalar_prefetch=1,
            grid=(m // tc,),
            in_specs=[pl.BlockSpec((tc, d), lambda i, de: (i, 0)), pl.BlockSpec((tc, LANES), lambda i, de: (i, 0)),
                      any_spec],
            out_specs=[pl.BlockSpec((tc, d), lambda i, de: (jnp.minimum(i, n_first - 1), 0)),
                       pl.BlockSpec((tc, d), lambda i, de: (jnp.maximum(i - n_first, 0), 0))],
            scratch_shapes=[pltpu.VMEM((2, tc, ROW_SLABS, LANES), F32), pltpu.SemaphoreType.DMA((2,))]),
        out_shape=[jax.ShapeDtypeStruct((split, d), F32), jax.ShapeDtypeStruct((m - split, d), F32)],
        compiler_params=_cparams(("arbitrary",)),
        name="moe_combine",
    )(dest, x, info, y3)


def _diff_lambda_vec(lam_ref, lam_init):
    lv = lam_ref[...]
    d1 = jnp.sum(lv[0:1, :] * lv[1:2, :], axis=-1, keepdims=True)
    d2 = jnp.sum(lv[2:3, :] * lv[3:4, :], axis=-1, keepdims=True)
    return jnp.exp(d1) - jnp.exp(d2) + lam_init


def _softmax_rows(s):
    m = jnp.max(s, axis=-1, keepdims=True)
    e = jnp.exp(s - m)
    return e, jnp.sum(e, axis=-1, keepdims=True)


def _diffattn_prompt_kernel(slope_ref, q_ref, k_ref, v_ref, lam_ref, subln_ref, o_ref, *, tq, lam_init):
    t = q_ref.shape[0]
    h = pl.program_id(1)
    slope = slope_ref[h]
    lane = lax.broadcasted_iota(jnp.int32, (1, LANES), 1)
    lam = _diff_lambda_vec(lam_ref, lam_init)
    kb = k_ref[...].astype(BF16)
    vb = v_ref[...].astype(BF16)
    nt = (((1,), (1,)), ((), ()))
    for qi in range(t // tq):
        n_keys = (qi + 1) * tq
        q = q_ref[qi * tq:(qi + 1) * tq, :] * (A_DH ** -0.5)
        q1 = jnp.where(lane < A_DH, q, 0.0).astype(BF16)
        q2 = jnp.where(lane >= A_DH, q, 0.0).astype(BF16)
        kk = kb[:n_keys]
        row = lax.broadcasted_iota(jnp.int32, (tq, n_keys), 0) + qi * tq
        col = lax.broadcasted_iota(jnp.int32, (tq, n_keys), 1)
        dist = (row - col).astype(F32)
        bias = jnp.where(dist >= 0.0, -slope * dist, NEG_INF)
        e1, l1 = _softmax_rows(lax.dot_general(q1, kk, nt, preferred_element_type=F32) + bias)
        e2, l2 = _softmax_rows(lax.dot_general(q2, kk, nt, preferred_element_type=F32) + bias)
        p = e1 * (1.0 / l1) - e2 * (lam / l2)
        o = jnp.dot(p.astype(BF16), vb[:n_keys], preferred_element_type=F32)
        o = _rms_rows(o, subln_ref[...]) * (1.0 - lam_init)
        o_ref[qi * tq:(qi + 1) * tq, :] = o.astype(BF16)


def _diff_attention_prompt(pa, n_batch, t, slopes, lam_stack, subln, lam_init, *, tq):
    kern = functools.partial(_diffattn_prompt_kernel, tq=tq, lam_init=lam_init)
    return pl.pallas_call(
        kern,
        grid=(n_batch, A_HEADS),
        in_specs=[
            pl.BlockSpec(memory_space=pltpu.SMEM),
            pl.BlockSpec((t, LANES), lambda b, h: (b, h)),
            pl.BlockSpec((t, LANES), lambda b, h: (b, A_HEADS + h)),
            pl.BlockSpec((t, LANES), lambda b, h: (b, 2 * A_HEADS + h)),
            pl.BlockSpec((4, A_DH), lambda b, h: (0, 0)),
            pl.BlockSpec((1, LANES), lambda b, h: (0, 0)),
        ],
        out_specs=pl.BlockSpec((t, LANES), lambda b, h: (b, h)),
        out_shape=jax.ShapeDtypeStruct((n_batch * t, A_HEADS * LANES), BF16),
        compiler_params=_cparams(("parallel", "parallel")),
        name="diff_attention_prompt",
    )(slopes, pa, pa, pa, lam_stack, subln.reshape(1, LANES))


PAGES_PER_STEP = 4


def _diffattn_sample_kernel(pt_ref, q_ref, kn_ref, vn_ref, *rest, past, lam_init):
    del pt_ref
    npg = PAGES_PER_STEP
    k_refs, v_refs = rest[:npg], rest[npg:2 * npg]
    lam_ref, subln_ref, o_ref, qq_scr, m_scr, l_scr, acc_scr, bias_scr = rest[2 * npg:]
    g = pl.program_id(1)
    ts = q_ref.shape[0]
    page = k_refs[0].shape[0]
    lane = lax.broadcasted_iota(jnp.int32, (1, LANES), 1)
    hsub = lax.broadcasted_iota(jnp.int32, (A_HEADS, LANES), 0)
    col = lax.broadcasted_iota(jnp.int32, (A_HEADS, LANES), 1)
    valid = (col // (2 * ts)) == hsub
    pos_q = past + jnp.bitwise_and(col, ts - 1)
    slope = jnp.exp2(-(8.0 / A_HEADS) * (hsub + 1).astype(F32))
    nt = (((1,), (1,)), ((), ()))
    tn = (((0,), (0,)), ((), ()))

    @pl.when(g == 0)
    def _():
        rows = []
        for h in range(A_HEADS):
            q = q_ref[:, h * LANES:(h + 1) * LANES] * (A_DH ** -0.5)
            rows += [jnp.where(lane < A_DH, q, 0.0), jnp.where(lane >= A_DH, q, 0.0)]
        qq_scr[...] = jnp.concatenate(rows, axis=0).astype(BF16)
        m_scr[...] = jnp.where(valid, NEG_INF, 0.0)
        l_scr[...] = jnp.zeros(l_scr.shape, F32)
        acc_scr[...] = jnp.zeros(acc_scr.shape, F32)

    def key_bias(nk, key0, causal):
        kidx = lax.broadcasted_iota(jnp.int32, (nk * A_HEADS, LANES), 0) // A_HEADS
        dist = (pos_q[None] - (key0 + kidx).reshape(nk, A_HEADS, LANES)).astype(F32)
        keep = valid[None] & (dist >= 0.0) if causal else valid[None]
        return jnp.where(keep, -slope[None] * dist, NEG_INF)

    @pl.when(g == 0)
    def _():
        bias_scr[...] = key_bias(page, 0, False)

    def scores(k3):
        nk = k3.shape[0]
        k2 = k3.reshape(nk * A_HEADS, LANES).astype(BF16)
        return lax.dot_general(k2, qq_scr[...], nt, preferred_element_type=F32).reshape(nk, A_HEADS, LANES)

    def absorb(ss, v3s):
        m_old = m_scr[...]
        m_new = m_old
        for s in ss:
            m_new = jnp.maximum(m_new, jnp.max(s, axis=0))
        alpha = jnp.exp(m_old - m_new)
        l_new = alpha * l_scr[...]
        acc = jnp.sum(jnp.where(valid, alpha, 0.0), axis=0, keepdims=True) * acc_scr[...]
        for s, v3 in zip(ss, v3s):
            nk = s.shape[0]
            pe = jnp.exp(s - m_new[None])
            l_new = l_new + jnp.sum(pe, axis=0)
            v2 = v3.reshape(nk * A_HEADS, LANES).astype(BF16)
            acc = acc + lax.dot_general(v2, pe.reshape(nk * A_HEADS, LANES).astype(BF16), tn,
                                        preferred_element_type=F32)
        m_scr[...] = m_new
        l_scr[...] = l_new
        acc_scr[...] = acc

    absorb([scores(k_refs[j][...]) + bias_scr[...] + (slope * ((g * npg + j) * page).astype(F32))[None]
            for j in range(npg)], [v_refs[j][...] for j in range(npg)])

    @pl.when(g == pl.num_programs(1) - 1)
    def _():
        absorb([scores(kn_ref[...]) + key_bias(ts, past, True)], [vn_ref[...]])
        lam = _diff_lambda_vec(lam_ref, lam_init)
        l_row = jnp.sum(jnp.where(valid, l_scr[...], 0.0), axis=0, keepdims=True)
        ot = acc_scr[...] / l_row
        ot = ot - lam * pltpu.roll(ot, LANES - ts, 1)
        o = ot.T
        for h in range(A_HEADS):
            oh = _rms_rows(o[h * 2 * ts:h * 2 * ts + ts, :], subln_ref[...]) * (1.0 - lam_init)
            o_ref[:, h * LANES:(h + 1) * LANES] = oh.astype(BF16)


def _diff_attention_sample(pa, row0, n_batch, ts, cache_k, cache_v, layer_idx, page_table, lam_stack, subln,
                           lam_init):
    n_pages = page_table.shape[1]
    page = cache_k.shape[2]
    npg = PAGES_PER_STEP
    assert row0 % ts == 0 and n_pages % npg == 0 and 2 * ts * A_HEADS == LANES
    rb = row0 // ts
    w = A_HEADS * LANES
    new_rows = lambda c0: pa[row0:, c0:c0 + w].reshape(n_batch, ts, A_HEADS, LANES)
    kern = functools.partial(_diffattn_sample_kernel, past=n_pages * page, lam_init=lam_init)
    page_spec = lambda j: pl.BlockSpec((None, None, page, A_HEADS, LANES),
                                       lambda b, g, pt: (layer_idx, pt[b, g * npg + j], 0, 0, 0))
    new_spec = pl.BlockSpec((None, ts, A_HEADS, LANES), lambda b, g, pt: (b, 0, 0, 0))
    grid_spec = pltpu.PrefetchScalarGridSpec(
        num_scalar_prefetch=1,
        grid=(n_batch, n_pages // npg),
        in_specs=[pl.BlockSpec((ts, w), lambda b, g, pt: (rb + b, 0)), new_spec, new_spec]
        + [page_spec(j) for j in range(npg)] * 2
        + [pl.BlockSpec((4, A_DH), lambda b, g, pt: (0, 0)), pl.BlockSpec((1, LANES), lambda b, g, pt: (0, 0))],
        out_specs=pl.BlockSpec((ts, w), lambda b, g, pt: (b, 0)),
        scratch_shapes=[
            pltpu.VMEM((LANES, LANES), BF16),
            pltpu.VMEM((A_HEADS, LANES), F32),
            pltpu.VMEM((A_HEADS, LANES), F32),
            pltpu.VMEM((LANES, LANES), F32),
            pltpu.VMEM((page, A_HEADS, LANES), F32),
        ],
    )
    return pl.pallas_call(
        kern,
        grid_spec=grid_spec,
        out_shape=jax.ShapeDtypeStruct((n_batch * ts, w), BF16),
        compiler_params=_cparams(("parallel", "arbitrary")),
        name="diff_attention_sample",
    )(page_table, pa, new_rows(w), new_rows(2 * w), *([cache_k] * npg), *([cache_v] * npg), lam_stack,
      subln.reshape(1, LANES))


def _cross_attn_kernel(q_ref, k_ref, v_ref, qn_ref, o_ref):
    nt = (((1,), (1,)), ((), ()))
    for h in range(X_HEADS):
        sl = slice(h * LANES, (h + 1) * LANES)
        q = _rms_rows(q_ref[:, sl], qn_ref[...]) * (X_DH ** -0.5)
        s = lax.dot_general(q.astype(BF16), k_ref[:, sl].astype(BF16), nt, preferred_element_type=F32)
        e, l = _softmax_rows(s)
        p = (e * (1.0 / l)).astype(BF16)
        o_ref[:, sl] = jnp.dot(p, v_ref[:, sl].astype(BF16), preferred_element_type=F32).astype(BF16)


def _cross_attention(q, row0, n_batch, t, mk, mv, q_norm, *, tq):
    assert t % tq == 0 and row0 % tq == 0
    nq = t // tq
    rb = row0 // tq
    return pl.pallas_call(
        _cross_attn_kernel,
        grid=(n_batch, nq),
        in_specs=[
            pl.BlockSpec((tq, X_W), lambda b, i: (rb + b * nq + i, 0)),
            pl.BlockSpec((None, N_MEM, X_W), lambda b, i: (b, 0, 0)),
            pl.BlockSpec((None, N_MEM, X_W), lambda b, i: (b, 0, 0)),
            pl.BlockSpec((1, LANES), lambda b, i: (0, 0)),
        ],
        out_specs=pl.BlockSpec((tq, X_W), lambda b, i: (b * nq + i, 0)),
        out_shape=jax.ShapeDtypeStruct((n_batch * t, X_W), BF16),
        compiler_params=_cparams(("parallel", "parallel")),
        name="cross_attention",
    )(q, mk, mv, q_norm.reshape(1, LANES))


PB_LG = 3 * B_W
PB_LW = PB_LG + B_GATE_LORA
PB_LA = PB_LW + LANES
PB_COLS = PB_LA + LANES


def _swap_halves(x):
    tiles = [pltpu.roll(x[:, c * LANES:(c + 1) * LANES], LANES // 2, 1) for c in range(x.shape[1] // LANES)]
    return tiles[0] if len(tiles) == 1 else jnp.concatenate(tiles, axis=1)


def _rwkv_prep_kernel(pb_ref, halo_ref, start_ref, mu_ref, w0_ref, w2_ref, a0_ref, a2_ref, g2_ref, kk_ref, ka_ref,
                      r_ref, kf_ref, v_ref, g_ref, rt_ref, at_ref, bt_ref, kt_ref, bh_ref, kh_ref, pc_ref,
                      *, n_prompt_tiles, chunk_p, chunk_s, t_p, t_s):
    tm = pb_ref.shape[0]
    i = pl.program_id(0)
    pb = pb_ref[...]
    row = lax.broadcasted_iota(jnp.int32, (tm, 1), 0)
    prev = jnp.where(row == 0, halo_ref[HALO - 1:HALO, :], pltpu.roll(pb, 1, 0))
    seq_len = jnp.where(i < n_prompt_tiles, t_p, t_s)
    is_start = jnp.bitwise_and(i * tm + row, seq_len - 1) == 0
    ps = jnp.where(is_start, start_ref[...], prev)
    xs = pb + (ps - pb) * mu_ref[...]
    r = xs[:, :B_W]
    k = xs[:, B_W:2 * B_W]
    lg = xs[:, PB_LG:PB_LW]
    lw_in = xs[:, PB_LW:PB_LA]
    la = xs[:, PB_LA:PB_COLS]
    wl = w0_ref[...] + jnp.dot(jnp.tanh(lw_in).astype(BF16), w2_ref[...], preferred_element_type=F32)
    lw = -jnp.exp(-_softplus(-wl) - 0.5)
    a = _sigmoid(a0_ref[...] + jnp.dot(la.astype(BF16), a2_ref[...], preferred_element_type=F32))
    kk = _group_norm_scale(k * kk_ref[...], B_DH, 1.0, EPS)
    kf = k * (1.0 + (a - 1.0) * ka_ref[...])
    b = kk * a
    r_ref[...] = r
    kf_ref[...] = kf
    v_ref[...] = xs[:, 2 * B_W:3 * B_W]
    g_ref[...] = jnp.dot(_sigmoid(lg).astype(BF16), g2_ref[...], preferred_element_type=F32)

    chunk = jnp.where(pl.program_id(0) < n_prompt_tiles, chunk_p, chunk_s)
    rowc = jnp.bitwise_and(lax.broadcasted_iota(jnp.int32, (tm, B_W), 0), chunk - 1)
    lp = lw
    sfx = lw
    step = 1
    while step < max(chunk_p, chunk_s):
        lp = lp + jnp.where(rowc >= step, pltpu.roll(lp, step, 0), 0.0)
        sfx = sfx + jnp.where(rowc + step < chunk, pltpu.roll(sfx, tm - step, 0), 0.0)
        step *= 2
    sfx = sfx - lw
    e_neg = jnp.exp(-lp)
    e_sfx = jnp.exp(sfx)
    sw = lambda z: _swap_halves(z).astype(BF16)
    rt_ref[...] = sw(r * jnp.exp(lp))
    at_ref[...] = sw(kk * jnp.exp(lp - lw))
    bt_ref[...] = sw(b * e_neg)
    kt_ref[...] = sw(kf * e_neg)
    bh_ref[...] = sw(b * e_sfx)
    kh_ref[...] = sw(kf * e_sfx)
    pc_ref[...] = _swap_halves(jnp.exp(lp + sfx))


def _rwkv_prep(pb, starts, mu, w0, w2, a0, a2, g2, k_k, k_a, *, tm, n_prompt_tiles, chunk_p, chunk_s, t_p, t_s):
    m = pb.shape[0]
    assert tm % chunk_p == 0 and tm % chunk_s == 0 and tm % HALO == 0
    assert t_p & (t_p - 1) == 0 and t_s & (t_s - 1) == 0 and (n_prompt_tiles * tm) % t_s == 0
    npt = n_prompt_tiles
    row = lambda n: pl.BlockSpec((1, n), lambda i: (0, 0))
    full = lambda a: pl.BlockSpec(a.shape, lambda i: (0, 0))
    tok = lambda n: pl.BlockSpec((tm, n), lambda i: (i, 0))
    f32o = jax.ShapeDtypeStruct((m, B_W), F32)
    b16o = jax.ShapeDtypeStruct((m, B_W), BF16)
    kern = functools.partial(_rwkv_prep_kernel, n_prompt_tiles=npt, chunk_p=chunk_p, chunk_s=chunk_s, t_p=t_p,
                             t_s=t_s)
    return pl.pallas_call(
        kern,
        grid=(m // tm,),
        in_specs=[tok(PB_COLS),
                  pl.BlockSpec((HALO, PB_COLS), lambda i: (jnp.maximum(i * (tm // HALO) - 1, 0), 0)),
                  pl.BlockSpec((tm, PB_COLS), lambda i: (jnp.maximum(i - npt + 1, 0), 0)),
                  row(PB_COLS), row(B_W), full(w2), row(B_W), full(a2), full(g2), row(B_W), row(B_W)],
        out_specs=[tok(B_W)] * 11,
        out_shape=[f32o] * 4 + [b16o] * 6 + [f32o],
        compiler_params=_cparams(("parallel",)),
        name="rwkv_prep",
    )(pb, pb, starts, mu, w0.reshape(1, B_W), w2, a0.reshape(1, B_W), a2, g2, k_k.reshape(1, B_W),
      k_a.reshape(1, B_W))


SOLVE_BLOCK = 16


def _split2(a):
    hi = a.astype(BF16)
    return hi, (a - hi.astype(F32)).astype(BF16)


def _solve_unit_lower_many(lmats, rhss):
    ns = len(lmats)
    c = lmats[0].shape[0]
    n = rhss[0].shape[1]
    bs = min(SOLVE_BLOCK, c)
    col = lax.broadcasted_iota(jnp.int32, (bs, c), 1)
    d = lambda x, y: jnp.dot(x, y, preferred_element_type=F32)
    done = [[] for _ in range(ns)]
    for blk in range(c // bs):
        rows = slice(blk * bs, (blk + 1) * bs)
        lrows = [lm[rows, :] for lm in lmats]
        rs = [rh[rows, :] for rh in rhss]
        if blk > 0:
            for s in range(ns):
                xs = jnp.concatenate(done[s] + [jnp.zeros((c - blk * bs, n), F32)], axis=0)
                lh, ll = _split2(jnp.where(col < blk * bs, lrows[s], 0.0))
                xh, xl = _split2(xs)
                rs[s] = rs[s] - (d(lh, xl) + d(ll, xh) + d(lh, xh))
        for j in range(bs - 1):
            for s in range(ns):
                rs[s] = rs[s] - lrows[s][:, blk * bs + j:blk * bs + j + 1] * rs[s][j:j + 1, :]
        for s in range(ns):
            done[s].append(rs[s])
    return [dn[0] if len(dn) == 1 else jnp.concatenate(dn, axis=0) for dn in done]


def _solve_unit_lower(lmat, rhs):
    return _solve_unit_lower_many([lmat], [rhs])[0]


def _rwkv_chunk_kernel(rt_ref, at_ref, bt_ref, kt_ref, bh_ref, kh_ref, v_ref, pc_ref, z0_ref, y_ref, zo_ref, z_scr,
                       *, chunk):
    i = pl.program_id(2)
    tb = rt_ref.shape[0]

    @pl.when(i == 0)
    def _():
        z_scr[...] = z0_ref[...]

    lo = lax.broadcasted_iota(jnp.int32, (1, LANES), 1) < B_DH
    zr = lax.broadcasted_iota(jnp.int32, (LANES, LANES), 0) < B_DH
    zc = lax.broadcasted_iota(jnp.int32, (LANES, LANES), 1) < B_DH
    offdiag = zr != zc
    ri = lax.broadcasted_iota(jnp.int32, (chunk, chunk), 0)
    ci = lax.broadcasted_iota(jnp.int32, (chunk, chunk), 1)
    nt = (((1,), (1,)), ((), ()))
    tn = (((0,), (0,)), ((), ()))
    d = lambda x, y: jnp.dot(x, y, preferred_element_type=F32)
    zero = jnp.zeros((), BF16)
    n_chunks = tb // chunk
    n_pp = rt_ref.shape[1] // LANES
    lmats, rhss, rkvs, rbs = [], [], [], []
    for p in range(n_pp):
        ls = slice(p * LANES, (p + 1) * LANES)
        for c in range(n_chunks):
            rs = slice(c * chunk, (c + 1) * chunk)
            rt, at, bt, kt = rt_ref[rs, ls], at_ref[rs, ls], bt_ref[rs, ls], kt_ref[rs, ls]
            v = v_ref[rs, ls].astype(BF16)
            for head in range(2):
                kmask = lo if head == 1 else jnp.logical_not(lo)
                a_h, r_h = jnp.where(kmask, at, zero), jnp.where(kmask, rt, zero)
                b_h, k_h = jnp.where(kmask, bt, zero), jnp.where(kmask, kt, zero)
                v_h = jnp.where(kmask, zero, v)
                lhs = jnp.concatenate([a_h, r_h], axis=0)
                gb = lax.dot_general(lhs, b_h, nt, preferred_element_type=F32)
                gk = lax.dot_general(lhs, k_h, nt, preferred_element_type=F32)
                lmats.append(jnp.where(ri > ci, gb[:chunk], 0.0))
                rbs.append(jnp.where(ri >= ci, gb[chunk:], 0.0).astype(BF16))
                ak = jnp.where(ri > ci, gk[:chunk], 0.0).astype(BF16)
                rk = jnp.where(ri >= ci, gk[chunk:], 0.0).astype(BF16)
                rhss.append(a_h.astype(F32) + d(ak, v_h))
                rkvs.append(d(rk, v_h))
    xs = _solve_unit_lower_many(lmats, rhss)
    zs = [z_scr[p] for p in range(n_pp)]
    for c in range(n_chunks):
        rs = slice(c * chunk, (c + 1) * chunk)
        for p in range(n_pp):
            ls = slice(p * LANES, (p + 1) * LANES)
            s0 = 2 * (p * n_chunks + c)
            rt = rt_ref[rs, ls]
            stacked = jnp.concatenate([xs[s0].astype(BF16), xs[s0 + 1].astype(BF16),
                                       jnp.where(lo, zero, rt), jnp.where(lo, rt, zero)], axis=0)
            big = lax.dot_general(stacked, zs[p].astype(BF16), nt, preferred_element_type=F32)
            u0 = jnp.where(lo, big[:chunk], 0.0)
            u1 = jnp.where(lo, 0.0, big[chunk:2 * chunk])
            y = jnp.where(lo, big[2 * chunk:3 * chunk], big[3 * chunk:]) + rkvs[s0] + rkvs[s0 + 1]
            y = y - d(rbs[s0], u0.astype(BF16)) - d(rbs[s0 + 1], u1.astype(BF16))
            y_ref[rs, ls] = y
            upd = lax.dot_general(jnp.concatenate([v_ref[rs, ls].astype(BF16), (u0 + u1).astype(BF16)], axis=0),
                                  jnp.concatenate([kh_ref[rs, ls], -bh_ref[rs, ls]], axis=0), tn,
                                  preferred_element_type=F32)
            zs[p] = jnp.where(offdiag, zs[p] * pc_ref[c * chunk:c * chunk + 1, ls] + upd, zs[p])
    for p in range(n_pp):
        z_scr[p] = zs[p]

    @pl.when(i == pl.num_programs(2) - 1)
    def _():
        zo_ref[...] = z_scr[...]


RWKV_PAIRS_PER_STEP = 4


def _rwkv_chunks(rt, at, bt, kt, bh, kh, v, pc, z0, row0, n_seq, t, *, tb, chunk):
    assert t % tb == 0 and row0 % tb == 0 and tb % chunk == 0
    nb = t // tb
    rb = row0 // tb
    n_pairs = B_HEADS // 2
    pps = RWKV_PAIRS_PER_STEP
    tok = pl.BlockSpec((tb, pps * LANES), lambda s, p, i: (rb + s * nb + i, p))
    st = pl.BlockSpec((None, pps, LANES, LANES), lambda s, p, i: (s, p, 0, 0))
    return pl.pallas_call(
        functools.partial(_rwkv_chunk_kernel, chunk=chunk),
        grid=(n_seq, n_pairs // pps, nb),
        in_specs=[tok] * 8 + [st],
        out_specs=[pl.BlockSpec((tb, pps * LANES), lambda s, p, i: (s * nb + i, p)), st],
        out_shape=[jax.ShapeDtypeStruct((n_seq * t, B_W), F32),
                   jax.ShapeDtypeStruct((n_seq, n_pairs, LANES, LANES), F32)],
        scratch_shapes=[pltpu.VMEM((pps, LANES, LANES), F32)],
        compiler_params=_cparams(("parallel", "parallel", "arbitrary")),
        name="rwkv_chunks",
    )(rt, at, bt, kt, bh, kh, v, pc, z0)


def _rwkv_post_kernel(y_ref, r_ref, kf_ref, v_ref, g_ref, lw_ref, lb_ref, rk_ref, o_ref):
    y = y_ref[...]
    mean = _group_sum(y, B_DH) * (1.0 / B_DH)
    yc = y - mean
    var = _group_sum(yc * yc, B_DH) * (1.0 / B_DH)
    yn = yc * lax.rsqrt(var + B_GN_EPS) * lw_ref[...] + lb_ref[...]
    bonus = _group_sum(r_ref[...] * kf_ref[...] * rk_ref[...], B_DH)
    o_ref[...] = ((yn + bonus * v_ref[...]) * g_ref[...]).astype(BF16)


def _rwkv_post(y, r, kf, v, g, lnx_w, lnx_b, r_k, *, tm):
    m = y.shape[0]
    tok = pl.BlockSpec((tm, B_W), lambda i: (i, 0))
    row = pl.BlockSpec((1, B_W), lambda i: (0, 0))
    return pl.pallas_call(
        _rwkv_post_kernel,
        grid=(m // tm,),
        in_specs=[tok] * 5 + [row] * 3,
        out_specs=tok,
        out_shape=jax.ShapeDtypeStruct((m, B_W), BF16),
        compiler_params=_cparams(("parallel",)),
        name="rwkv_post",
    )(y, r, kf, v, g, lnx_w.reshape(1, B_W), lnx_b.reshape(1, B_W), r_k.reshape(1, B_W))


def _rwkv_cols(a):
    lw0 = 3 * B_W
    la0 = lw0 + B_DECAY_LORA
    lg0 = la0 + B_AAA_LORA
    z = jnp.zeros(a.shape[:-1] + (LANES - B_DECAY_LORA,), a.dtype)
    return jnp.concatenate([a[..., :lw0], a[..., lg0:], a[..., lw0:la0], z, a[..., la0:lg0], z], axis=-1)


def _rwkv_cols_inv(a):
    return jnp.concatenate([a[..., :PB_LG], a[..., PB_LW:PB_LW + B_DECAY_LORA],
                            a[..., PB_LA:PB_LA + B_AAA_LORA], a[..., PB_LG:PB_LW]], axis=-1)


def _pad_rows(a, n):
    return jnp.pad(a, ((0, n - a.shape[0]), (0, 0)))


def _rwkv_state_to_pairs(s):
    n = s.shape[0]
    s = s.reshape(n, B_HEADS // 2, 2, B_DH, B_DH)
    eye = jnp.broadcast_to(jnp.eye(B_DH, dtype=F32), (n, B_HEADS // 2, B_DH, B_DH))
    top = jnp.concatenate([eye, s[:, :, 0]], axis=-1)
    bot = jnp.concatenate([s[:, :, 1], eye], axis=-1)
    return jnp.concatenate([top, bot], axis=-2)


def _rwkv_state_from_pairs(z):
    n = z.shape[0]
    s = jnp.stack([z[:, :, :B_DH, B_DH:], z[:, :, B_DH:, :B_DH]], axis=2)
    return s.reshape(n, B_HEADS, B_DH, B_DH)


def _rwkv_mixer(pb, n_p, t_p, n_s, t_s, prev_s, state_s, mu, w0, w2, a0, a2, g2, k_k, k_a, r_k, lnx_w, lnx_b,
                *, tm, tb):
    mp = n_p * t_p
    chunk_p = min(64, t_p)
    chunk_s = min(64, t_s)
    assert mp % tm == 0 and (n_s * t_s) % tm == 0 and t_p % chunk_p == 0 and t_s % chunk_s == 0
    starts = jnp.concatenate([
        jnp.zeros((tm, PB_COLS), F32),
        jnp.broadcast_to(_rwkv_cols(prev_s), (n_s, t_s, PB_COLS)).reshape(n_s * t_s, PB_COLS)], axis=0)
    r, kf, v, g, rt, at, bt, kt, bh, kh, pc = _rwkv_prep(
        pb, starts, _rwkv_cols(mu.reshape(1, B_COLS)), w0, _pad_rows(w2, LANES).astype(BF16), a0,
        _pad_rows(a2, LANES).astype(BF16), g2.astype(BF16), k_k, k_a, tm=tm, n_prompt_tiles=mp // tm,
        chunk_p=chunk_p, chunk_s=chunk_s, t_p=t_p, t_s=t_s)
    z0_p = _rwkv_state_to_pairs(jnp.zeros((n_p, B_HEADS, B_DH, B_DH), F32))
    y_p, z_p = _rwkv_chunks(rt, at, bt, kt, bh, kh, v, pc, z0_p, 0, n_p, t_p, tb=tb, chunk=chunk_p)
    y_s, z_s = _rwkv_chunks(rt, at, bt, kt, bh, kh, v, pc, _rwkv_state_to_pairs(state_s), mp, n_s, t_s,
                            tb=t_s, chunk=chunk_s)
    y = jnp.concatenate([y_p, y_s], axis=0)
    ob = _rwkv_post(y, r, kf, v, g, lnx_w, lnx_b, r_k.reshape(B_W), tm=tm)
    return ob, _rwkv_state_from_pairs(z_p), _rwkv_state_from_pairs(z_s)


PC_Z = C_CONV_CH
PC_BETA = PC_Z + C_V
PC_COLS = PC_BETA + 2 * LANES
HALO = 8


def _chunk_cumsum(g, chunk):
    rowc = jnp.bitwise_and(lax.broadcasted_iota(jnp.int32, g.shape, 0), chunk - 1)
    k = 1
    while k < chunk:
        g = g + jnp.where(rowc >= k, pltpu.roll(g, k, 0), 0.0)
        k *= 2
    return g


def _transpose_rows(a):
    c = a.shape[0]
    if c < LANES:
        a = jnp.concatenate([a, jnp.zeros((LANES - c, LANES), F32)], axis=0)
    return a.T


def _pad_chunk_rows(a):
    c = a.shape[0]
    return a if c == LANES else jnp.concatenate([a, jnp.zeros((LANES - c, a.shape[1]), a.dtype)], axis=0)


def _gdn_prep_kernel(x_ref, halo_ref, st_ref, cw_ref, sm_ref, alog_ref, dtb_ref,
                     q_ref, k_ref, v_ref, beta_ref, gc_ref, *, chunk):
    i = pl.program_id(1)
    x = x_ref[...]
    tm = x.shape[0]
    halo = jnp.where(i == 0, st_ref[...], halo_ref[...])
    row8 = lax.broadcasted_iota(jnp.int32, (HALO, x.shape[1]), 0)
    acc = x * cw_ref[C_CONV - 1:C_CONV, :]
    for s in range(1, C_CONV):
        xs = pltpu.roll(x, s, 0)
        first = jnp.where(row8 < s, pltpu.roll(halo, s, 0), xs[:HALO])
        xs = first if tm == HALO else jnp.concatenate([first, xs[HALO:]], axis=0)
        acc = acc + xs * cw_ref[C_CONV - 1 - s:C_CONV - s, :]
    y = _silu(acc)
    q_ref[...] = _group_norm_scale(y[:, :C_QK], C_DH, 1.0, EPS) * (C_DH ** -0.5)
    k_ref[...] = _group_norm_scale(y[:, C_QK:2 * C_QK], C_DH, 1.0, EPS)
    v_ref[...] = y[:, 2 * C_QK:]
    sm = sm_ref[...]
    beta_ref[...] = _sigmoid(sm[:, :LANES])
    g = -jnp.exp(alog_ref[...]) * _softplus(sm[:, LANES:] + dtb_ref[...])
    gc_ref[...] = _chunk_cumsum(g, chunk)


def _gdn_prep(pc, conv_state, row0, n_seq, t, conv_w, a_log, dt_bias, *, tm, chunk):
    assert t % tm == 0 and row0 % tm == 0 and tm % HALO == 0 and tm % chunk == 0 and chunk & (chunk - 1) == 0
    nb = t // tm
    rb = row0 // tm
    hb = tm // HALO
    tok_in = lambda n, cb: pl.BlockSpec((tm, n), lambda s, i: (rb + s * nb + i, cb))
    tok_out = lambda n: pl.BlockSpec((tm, n), lambda s, i: (s * nb + i, 0))
    row = lambda n: pl.BlockSpec((1, n), lambda s, i: (0, 0))
    sds = lambda n: jax.ShapeDtypeStruct((n_seq * t, n), F32)
    lanes8 = lambda a: jnp.zeros((1, LANES), F32).at[0, :C_V_HEADS].set(a)
    return pl.pallas_call(
        functools.partial(_gdn_prep_kernel, chunk=chunk),
        grid=(n_seq, nb),
        in_specs=[
            tok_in(C_CONV_CH, 0),
            pl.BlockSpec((HALO, C_CONV_CH), lambda s, i: (jnp.maximum((rb + s * nb + i) * hb - 1, 0), 0)),
            pl.BlockSpec((None, HALO, C_CONV_CH), lambda s, i: (s, 0, 0)),
            pl.BlockSpec((C_CONV, C_CONV_CH), lambda s, i: (0, 0)),
            tok_in(2 * LANES, PC_BETA // (2 * LANES)),
            row(LANES), row(LANES),
        ],
        out_specs=[tok_out(C_QK), tok_out(C_QK), tok_out(C_V), tok_out(LANES), tok_out(LANES)],
        out_shape=[sds(C_QK), sds(C_QK), sds(C_V), sds(LANES), sds(LANES)],
        compiler_params=_cparams(("parallel", "arbitrary")),
        name="gdn_prep",
    )(pc, pc, conv_state, conv_w, pc, lanes8(a_log), lanes8(dt_bias))


def _gdn_chunk_kernel(q_ref, k_ref, v_ref, beta_ref, gc_ref, z_ref, on_ref, s0_ref, o_ref, so_ref, s_scr, *, chunk):
    hq = pl.program_id(1)
    i = pl.program_id(2)
    tb = q_ref.shape[0]
    rep = v_ref.shape[1] // C_DH

    @pl.when(i == 0)
    def _():
        s_scr[...] = s0_ref[...]

    lane = lax.broadcasted_iota(jnp.int32, (chunk, LANES), 1)
    ri = lax.broadcasted_iota(jnp.int32, (chunk, chunk), 0)
    ci = lax.broadcasted_iota(jnp.int32, (chunk, chunk), 1)
    nt = (((1,), (1,)), ((), ()))
    ones = jnp.ones((chunk, LANES), BF16)
    n_chunks = tb // chunk
    lmats, rhss, attns, qgs, kds, eglast = [], [], [], [], [], []
    for c in range(n_chunks):
        rs = slice(c * chunk, (c + 1) * chunk)
        q, k = q_ref[rs, :], k_ref[rs, :]
        qb, kbf = q.astype(BF16), k.astype(BF16)
        kkt = lax.dot_general(kbf, kbf, nt, preferred_element_type=F32)
        qkt = lax.dot_general(qb, kbf, nt, preferred_element_type=F32)
        for j in range(rep):
            hsel = lane == hq * rep + j
            beta = jnp.sum(jnp.where(hsel, beta_ref[rs, :], 0.0), axis=-1, keepdims=True)
            gcol = jnp.sum(jnp.where(hsel, gc_ref[rs, :], 0.0), axis=-1, keepdims=True)
            g0 = jnp.where(lane == 0, gcol, 0.0)
            grow = sum(lax.dot_general(ones, part, nt, preferred_element_type=F32) for part in _split3(g0))
            dec = jnp.exp(jnp.where(ri >= ci, gcol - grow, NEG_INF))
            lmats.append(jnp.where(ri > ci, beta * kkt * dec, 0.0))
            rhss.append(jnp.concatenate([v_ref[rs, j * C_DH:(j + 1) * C_DH] * beta, k * (beta * jnp.exp(gcol))],
                                        axis=1))
            attns.append((qkt * dec).astype(BF16))
            qgs.append((q * jnp.exp(gcol)).astype(BF16))
            glast = gcol[chunk - 1:chunk, :]
            kds.append(_transpose_rows(k * jnp.exp(glast - gcol)).astype(BF16))
            eglast.append(jnp.exp(glast))
    xs = _solve_unit_lower_many(lmats, rhss)
    ss = [s_scr[j] for j in range(rep)]
    for c in range(n_chunks):
        rs = slice(c * chunk, (c + 1) * chunk)
        for j in range(rep):
            n = c * rep + j
            sb = ss[j].astype(BF16)
            v_new = xs[n][:, :C_DH] - jnp.dot(xs[n][:, C_DH:].astype(BF16), sb, preferred_element_type=F32)
            vb = v_new.astype(BF16)
            o = jnp.dot(qgs[n], sb, preferred_element_type=F32) + jnp.dot(attns[n], vb, preferred_element_type=F32)
            ss[j] = ss[j] * eglast[n] + jnp.dot(kds[n], _pad_chunk_rows(vb), preferred_element_type=F32)
            o = _rms_rows(o, on_ref[...]) * _silu(z_ref[rs, j * C_DH:(j + 1) * C_DH])
            o_ref[rs, j * C_DH:(j + 1) * C_DH] = o.astype(BF16)
    for j in range(rep):
        s_scr[j] = ss[j]

    @pl.when(i == pl.num_programs(2) - 1)
    def _():
        so_ref[...] = s_scr[...]


def _gdn_chunks(q, k, v, beta, gc, pc, row0, n_seq, t, onorm, s0, *, tb, chunk):
    assert t % tb == 0 and row0 % tb == 0 and tb % chunk == 0
    nb = t // tb
    rb = row0 // tb
    rep = C_V_HEADS // C_QK_HEADS
    tok = lambda n, cb: pl.BlockSpec((tb, n), cb)
    st = pl.BlockSpec((None, rep, C_DH, C_DH), lambda s, h, i: (s, h, 0, 0))
    return pl.pallas_call(
        functools.partial(_gdn_chunk_kernel, chunk=chunk),
        grid=(n_seq, C_QK_HEADS, nb),
        in_specs=[
            tok(C_DH, lambda s, h, i: (s * nb + i, h)),
            tok(C_DH, lambda s, h, i: (s * nb + i, h)),
            tok(rep * C_DH, lambda s, h, i: (s * nb + i, h)),
            tok(LANES, lambda s, h, i: (s * nb + i, 0)),
            tok(LANES, lambda s, h, i: (s * nb + i, 0)),
            tok(rep * C_DH, lambda s, h, i: (rb + s * nb + i, PC_Z // (rep * C_DH) + h)),
            pl.BlockSpec((1, LANES), lambda s, h, i: (0, 0)),
            st,
        ],
        out_specs=[tok(rep * C_DH, lambda s, h, i: (s * nb + i, h)), st],
        out_shape=[jax.ShapeDtypeStruct((n_seq * t, C_V), BF16),
                   jax.ShapeDtypeStruct((n_seq, C_V_HEADS, C_DH, C_DH), F32)],
        scratch_shapes=[pltpu.VMEM((rep, C_DH, C_DH), F32)],
        compiler_params=_cparams(("parallel", "parallel", "arbitrary")),
        name="gdn_chunks",
    )(q, k, v, beta, gc, pc, onorm.reshape(1, LANES), s0)


def _gdn_group(pc, conv_state3, s0, row0, n_seq, t, conv_w, a_log, dt_bias, onorm, *, tm, tb):
    chunk = C_CHUNK if t % C_CHUNK == 0 else t
    conv_state = jnp.pad(conv_state3, ((0, 0), (HALO - (C_CONV - 1), 0), (0, 0)))
    q, k, v, beta, gc = _gdn_prep(pc, conv_state, row0, n_seq, t, conv_w, a_log, dt_bias, tm=tm, chunk=chunk)
    return _gdn_chunks(q, k, v, beta, gc, pc, row0, n_seq, t, onorm, s0, tb=tb, chunk=chunk)


PD_V = 2 * D_K
PD_GATE = PD_V + D_V
PD_GLR = PD_GATE + D_V
PD_COLS = PD_GLR + LANES


def _gla_kernel(q_ref, k_ref, v_ref, gate_ref, glr_ref, w2_ref, bgk_ref, on_ref, s0_ref, o_ref, so_ref, s_scr,
                *, chunk):
    i = pl.program_id(2)
    tb = q_ref.shape[0]

    @pl.when(i == 0)
    def _():
        s_scr[...] = s0_ref[...]

    pre = jnp.dot(glr_ref[...].astype(BF16), w2_ref[...], preferred_element_type=F32) + bgk_ref[...]
    bcum = _chunk_cumsum(-_softplus(-pre) * (1.0 / D_GATE_NORM), chunk)
    ri = lax.broadcasted_iota(jnp.int32, (chunk, 1), 0)
    ci = lax.broadcasted_iota(jnp.int32, (chunk, chunk), 1)
    intra, qgs, decays, upds = [], [], [], []
    for c in range(tb // chunk):
        rs = slice(c * chunk, (c + 1) * chunk)
        q = q_ref[rs, :] * (D_DK ** -0.5)
        k, bc = k_ref[rs, :], bcum[rs, :]
        vb = v_ref[rs, :].astype(BF16)
        attn = jnp.zeros((chunk, chunk), F32)
        for j in range(chunk):
            e = jnp.exp(jnp.where(ri >= j, bc - bc[j:j + 1, :], NEG_INF))
            col = jnp.sum(q * k[j:j + 1, :] * e, axis=-1, keepdims=True)
            attn = jnp.where(ci == j, col, attn)
        intra.append(jnp.dot(attn.astype(BF16), vb, preferred_element_type=F32))
        qgs.append((q * jnp.exp(bc)).astype(BF16))
        blast = bc[chunk - 1:chunk, :]
        ebt = jnp.broadcast_to(jnp.exp(blast), (LANES, LANES)).T
        decays.append(jnp.concatenate([ebt, ebt], axis=1))
        kdt = _transpose_rows(k * jnp.exp(blast - bc)).astype(BF16)
        upds.append(jnp.dot(kdt, _pad_chunk_rows(vb), preferred_element_type=F32))
    s = s_scr[...]
    for c in range(tb // chunk):
        rs = slice(c * chunk, (c + 1) * chunk)
        o = jnp.dot(qgs[c], s.astype(BF16), preferred_element_type=F32) + intra[c]
        s = s * decays[c] + upds[c]
        o = _rms_rows(o, on_ref[...]) * _silu(gate_ref[rs, :])
        o_ref[rs, :] = o.astype(BF16)
    s_scr[...] = s

    @pl.when(i == pl.num_programs(2) - 1)
    def _():
        so_ref[...] = s_scr[...]


def _gla_group(pd, s0, row0, n_seq, t, w_gk2, b_gk, onorm, *, tb):
    chunk = D_CHUNK if t % D_CHUNK == 0 else t
    assert t % tb == 0 and row0 % tb == 0 and tb % chunk == 0 and chunk & (chunk - 1) == 0
    nb = t // tb
    rb = row0 // tb
    w2 = _pad_rows(w_gk2, LANES).astype(BF16)
    tok = lambda n, cb: pl.BlockSpec((tb, n), cb)
    st = pl.BlockSpec((None, None, D_DK, D_DV), lambda s, h, i: (s, h, 0, 0))
    return pl.pallas_call(
        functools.partial(_gla_kernel, chunk=chunk),
        grid=(n_seq, D_HEADS, nb),
        in_specs=[
            tok(D_DK, lambda s, h, i: (rb + s * nb + i, h)),
            tok(D_DK, lambda s, h, i: (rb + s * nb + i, D_K // D_DK + h)),
            tok(D_DV, lambda s, h, i: (rb + s * nb + i, PD_V // D_DV + h)),
            tok(D_DV, lambda s, h, i: (rb + s * nb + i, PD_GATE // D_DV + h)),
            tok(LANES, lambda s, h, i: (rb + s * nb + i, PD_GLR // LANES)),
            pl.BlockSpec((LANES, D_DK), lambda s, h, i: (0, h)),
            pl.BlockSpec((1, D_DK), lambda s, h, i: (0, h)),
            pl.BlockSpec((1, D_DV), lambda s, h, i: (0, 0)),
            st,
        ],
        out_specs=[tok(D_DV, lambda s, h, i: (s * nb + i, h)), st],
        out_shape=[jax.ShapeDtypeStruct((n_seq * t, D_V), BF16),
                   jax.ShapeDtypeStruct((n_seq, D_HEADS, D_DK, D_DV), F32)],
        scratch_shapes=[pltpu.VMEM((D_DK, D_DV), F32)],
        compiler_params=_cparams(("parallel", "parallel", "arbitrary")),
        name="gla_chunks",
    )(pd, pd, pd, pd, pd, w2, b_gk.reshape(1, D_K), onorm.reshape(1, D_DV), s0)


TM = 768
TM_SMALL = 256


def _cross_block(x, layer, n_p, t_p, n_s, t_s, mem_prompt, cache_mem_k, cache_mem_v, norm_mem_w, norm_cross_w,
                 x_w_q, x_w_k, x_w_v, x_w_o, x_q_norm, x_k_norm):
    d = x.shape[1]
    mp = n_p * t_p
    wkv = jnp.concatenate([x_w_k[layer], x_w_v[layer]], axis=1).astype(BF16)
    hw = jnp.concatenate([jnp.tile(x_k_norm[layer], X_HEADS), jnp.ones((X_W,), F32)]).reshape(1, 2 * X_W)
    mem = mem_prompt.reshape(n_p * N_MEM, d)
    kv = _norm_matmul(mem, norm_mem_w[layer], wkv, tm=min(512, n_p * N_MEM), tn=X_W, head_w=hw,
                      n_norm_cols=X_W, gsize=X_DH)
    mk_p = kv[:, :X_W].reshape(n_p, N_MEM, X_W)
    mv_p = kv[:, X_W:].reshape(n_p, N_MEM, X_W)
    qx = _norm_matmul(x, norm_cross_w[layer], x_w_q[layer].astype(BF16), tm=TM, tn=X_W)
    ca_p = _cross_attention(qx, 0, n_p, t_p, mk_p, mv_p, x_q_norm[layer], tq=min(512, t_p))
    ca_s = _cross_attention(qx, mp, n_s, t_s, cache_mem_k[layer].reshape(n_s, N_MEM, X_W),
                            cache_mem_v[layer].reshape(n_s, N_MEM, X_W), x_q_norm[layer], tq=t_s)
    ca = jnp.concatenate([ca_p, ca_s], axis=0)
    x = _matmul_residual([ca], [x_w_o[layer].astype(BF16)], x, tm=TM, tn=1024)
    return x, mk_p.reshape(n_p, N_MEM, X_HEADS, X_DH), mv_p.reshape(n_p, N_MEM, X_HEADS, X_DH)


def kernel(x_prompt, x_sample, cache_diff_k, cache_diff_v, state_rwkv, state_rwkv_shift, cache_mem_k, cache_mem_v, state_gdn, state_gdn_conv, state_gla, page_table, mem_prompt, norm_mix_w, norm_cross_w, norm_mem_w, norm_ffn_w, w_in_even, w_out_even, a_q_norm, a_k_norm, a_lam_q1, a_lam_k1, a_lam_q2, a_lam_k2, a_subln, b_mu, b_w0, b_w2, b_a0, b_a2, b_g2, b_k_k, b_k_a, b_r_k, b_lnx_w, b_lnx_b, ffd_w_gate, ffd_w_up, ffd_w_down, w_in_odd, w_out_odd, c_conv_w, c_a_log, c_dt_bias, c_onorm, d_w_gk2, d_b_gk, d_onorm, moe_w_router, moe_w_gate, moe_w_up, moe_w_down, x_w_q, x_w_k, x_w_v, x_w_o, x_q_norm, x_k_norm):
    n_p, t_p, d = x_prompt.shape
    n_s, t_s, _ = x_sample.shape
    mp, ms = n_p * t_p, n_s * t_s
    depth = norm_mix_w.shape[0]
    x = jnp.concatenate([x_prompt.reshape(mp, d), x_sample.reshape(ms, d)], axis=0)
    assert (mp + ms) % TM == 0 and (mp + ms) % TM_SMALL == 0
    page = cache_diff_k.shape[2]
    slopes = jnp.exp2(-(8.0 / A_HEADS) * jnp.arange(1, A_HEADS + 1, dtype=F32))
    cross_w = (mem_prompt, cache_mem_k, cache_mem_v, norm_mem_w, norm_cross_w, x_w_q, x_w_k, x_w_v, x_w_o,
               x_q_norm, x_k_norm)
    dk_p, dv_p, dk_s, dv_s, rw_p, rw_s, sh_p, sh_s = [], [], [], [], [], [], [], []
    mk_l, mv_l, gd_p, gd_s, cv_p, cv_s, gl_p, gl_s = [], [], [], [], [], [], [], []
    for layer in range(depth):
        i = layer // 2
        if layer % 2 == 0:
            lam_init = 0.8 - 0.6 * math.exp(-0.3 * layer)
            w_in = w_in_even[i]
            hw = jnp.concatenate([jnp.tile(a_q_norm[i], A_Q // A_DH), jnp.tile(a_k_norm[i], A_Q // A_DH),
                                  jnp.ones((A_COLS - 2 * A_Q,), F32)]).reshape(1, A_COLS)
            pa = _norm_matmul(x, norm_mix_w[layer], w_in[:, :A_COLS].astype(BF16), tm=TM, tn=1024, head_w=hw,
                              n_norm_cols=2 * A_Q, gsize=A_DH)
            pb = _norm_matmul(x, norm_mix_w[layer], _rwkv_cols(w_in[:, A_COLS:]).astype(BF16), tm=TM,
                              tn=PB_COLS // 2)
            lam_stack = jnp.stack([a_lam_q1[i], a_lam_k1[i], a_lam_q2[i], a_lam_k2[i]])
            oa_p = _diff_attention_prompt(pa, n_p, t_p, slopes, lam_stack, a_subln[i], lam_init, tq=min(256, t_p))
            oa_s = _diff_attention_sample(pa, mp, n_s, t_s, cache_diff_k, cache_diff_v, i, page_table, lam_stack,
                                          a_subln[i], lam_init)
            ob, st_p, st_s = _rwkv_mixer(pb, n_p, t_p, n_s, t_s, state_rwkv_shift[i], state_rwkv[i], b_mu[i],
                                         b_w0[i], b_w2[i], b_a0[i], b_a2[i], b_g2[i], b_k_k[i], b_k_a[i], b_r_k[i],
                                         b_lnx_w[i], b_lnx_b[i], tm=TM_SMALL, tb=min(256, t_p))
            dk_p.append(pa[:mp, A_Q:2 * A_Q].reshape(n_p, t_p, A_HEADS, 2 * A_DH))
            dv_p.append(pa[:mp, 2 * A_Q:].reshape(n_p, t_p, A_HEADS, 2 * A_DH))
            dk_s.append(pa[mp:, A_Q:2 * A_Q].reshape(n_s, t_s, A_HEADS, 2 * A_DH))
            dv_s.append(pa[mp:, 2 * A_Q:].reshape(n_s, t_s, A_HEADS, 2 * A_DH))
            rw_p.append(st_p)
            rw_s.append(st_s)
            sh_p.append(_rwkv_cols_inv(pb[:mp].reshape(n_p, t_p, PB_COLS)[:, -1:]))
            sh_s.append(_rwkv_cols_inv(pb[mp:].reshape(n_s, t_s, PB_COLS)[:, -1:]))
            oa = jnp.concatenate([oa_p, oa_s], axis=0)
            w_out = w_out_even[i].astype(BF16)
            x = _matmul_residual([oa, ob], [w_out[:A_Q], w_out[A_Q:]], x, tm=TM, tn=1024)
        else:
            w_in = w_in_odd[i]
            c_cols = C_CONV_CH + C_V + 2 * C_V_HEADS
            zc = jnp.zeros((d, LANES - C_V_HEADS), F32)
            wc = jnp.concatenate([w_in[:, :PC_BETA], w_in[:, PC_BETA:PC_BETA + C_V_HEADS], zc,
                                  w_in[:, PC_BETA + C_V_HEADS:c_cols], zc], axis=1).astype(BF16)
            wd_ = jnp.concatenate([w_in[:, c_cols:], jnp.zeros((d, LANES - D_GATE_LORA), F32)], axis=1).astype(BF16)
            pc = _norm_matmul(x, norm_mix_w[layer], wc, tm=TM, tn=PC_COLS // 2)
            pd = _norm_matmul(x, norm_mix_w[layer], wd_, tm=TM, tn=PD_COLS // 5)
            gdn_w = (c_conv_w[i], c_a_log[i], c_dt_bias[i], c_onorm[i])
            oc_p, gs_p = _gdn_group(pc, jnp.zeros((n_p, C_CONV - 1, C_CONV_CH), F32),
                                    jnp.zeros((n_p, C_V_HEADS, C_DH, C_DH), F32), 0, n_p, t_p, *gdn_w,
                                    tm=min(512, t_p), tb=min(512, t_p))
            oc_s, gs_s = _gdn_group(pc, state_gdn_conv[i], state_gdn[i], mp, n_s, t_s, *gdn_w, tm=t_s, tb=t_s)
            gla_w = (d_w_gk2[i], d_b_gk[i], d_onorm[i])
            od_p, ls_p = _gla_group(pd, jnp.zeros((n_p, D_HEADS, D_DK, D_DV), F32), 0, n_p, t_p, *gla_w,
                                    tb=min(256, t_p))
            od_s, ls_s = _gla_group(pd, state_gla[i], mp, n_s, t_s, *gla_w, tb=t_s)
            gd_p.append(gs_p)
            gd_s.append(gs_s)
            cv_p.append(pc[:mp].reshape(n_p, t_p, PC_COLS)[:, -(C_CONV - 1):, :C_CONV_CH])
            cv_s.append(pc[mp:].reshape(n_s, t_s, PC_COLS)[:, -(C_CONV - 1):, :C_CONV_CH])
            gl_p.append(ls_p)
            gl_s.append(ls_s)
            oc = jnp.concatenate([oc_p, oc_s], axis=0)
            od = jnp.concatenate([od_p, od_s], axis=0)
            w_out = w_out_odd[i].astype(BF16)
            x = _matmul_residual([oc, od], [w_out[:C_V], w_out[C_V:]], x, tm=TM, tn=1024)
        x, mk_p, mv_p = _cross_block(x, layer, n_p, t_p, n_s, t_s, *cross_w)
        mk_l.append(mk_p)
        mv_l.append(mv_p)
        if layer % 2 == 0:
            x = _swiglu_ffn(x, norm_ffn_w[layer], ffd_w_gate[i].astype(BF16), ffd_w_up[i].astype(BF16),
                            ffd_w_down[i].astype(BF16), tm=TM, tf=512)
        else:
            y_p, y_s = _moe_ffn(x, norm_ffn_w[layer], moe_w_router[i], moe_w_gate, moe_w_up, moe_w_down, i,
                                tm=TM, tf=256, tc=TM_SMALL, split=mp)
            if layer + 1 < depth:
                x = jnp.concatenate([y_p, y_s], axis=0)
    if depth % 2 == 1:
        y_p, y_s = x[:mp], x[mp:]
    st = jnp.stack
    return (y_p.reshape(n_p, t_p, d), y_s.reshape(n_s, t_s, d), st(dk_p), st(dv_p), st(dk_s), st(dv_s),
            st(rw_p), st(rw_s), st(sh_p), st(sh_s), st(mk_l), st(mv_l), st(gd_p), st(gd_s), st(cv_p), st(cv_s),
            st(gl_p), st(gl_s))
```

```python
import functools
import math

import jax
import jax.numpy as jnp
from jax import lax
from jax.experimental import pallas as pl
from jax.experimental.pallas import tpu as pltpu

F32 = jnp.float32
BF16 = jnp.bfloat16
NEG_INF = float("-inf")

D_MODEL = 2048
A_HEADS = 8
A_DH = 64
A_Q = 1024
A_COLS = 3072
B_HEADS = 16
B_DH = 64
B_W = 1024
B_DECAY_LORA = 96
B_AAA_LORA = 96
B_GATE_LORA = 256
B_COLS = 3520
B_GN_EPS = 64e-5
C_QK_HEADS = 4
C_V_HEADS = 8
C_DH = 128
C_CONV = 4
C_QK = 512
C_V = 1024
C_CONV_CH = 2048
C_CHUNK = 64
D_HEADS = 4
D_DK = 128
D_DV = 256
D_K = 512
D_V = 1024
D_GATE_LORA = 16
D_GATE_NORM = 16.0
D_CHUNK = 16
N_MEM = 256
X_HEADS = 4
X_DH = 128
X_W = 512
FF_DENSE = 5632
N_EXPERTS = 8
FF_EXPERT = 2816
EPS = 1e-6

LANES = 128
VMEM_LIMIT = 56 * 1024 * 1024


def _cparams(sem):
    return pltpu.CompilerParams(dimension_semantics=sem, vmem_limit_bytes=VMEM_LIMIT)


def _sigmoid(x):
    return 1.0 / (1.0 + jnp.exp(-x))


def _silu(x):
    return x * _sigmoid(x)


def _softplus(x):
    return jnp.maximum(x, 0.0) + jnp.log(1.0 + jnp.exp(-jnp.abs(x)))


def _rms_rows(x, w):
    ms = jnp.mean(x * x, axis=-1, keepdims=True)
    return x * lax.rsqrt(ms + EPS) * w


def _group_sum(x, gsize):
    lane = lax.broadcasted_iota(jnp.int32, (1, LANES), 1)
    outs = []
    for c in range(x.shape[1] // LANES):
        xc = x[:, c * LANES:(c + 1) * LANES]
        if gsize == LANES:
            ss = jnp.broadcast_to(jnp.sum(xc, axis=-1, keepdims=True), xc.shape)
        else:
            lo = jnp.sum(jnp.where(lane < 64, xc, 0.0), axis=-1, keepdims=True)
            hi = jnp.sum(jnp.where(lane >= 64, xc, 0.0), axis=-1, keepdims=True)
            ss = jnp.where(lane < 64, lo, hi)
        outs.append(ss)
    return outs[0] if len(outs) == 1 else jnp.concatenate(outs, axis=1)


def _group_norm_scale(x, gsize, inv_n, eps):
    return x * lax.rsqrt(_group_sum(x * x, gsize) * inv_n + eps)


def _split3(a):
    a1 = a.astype(BF16)
    r1 = a - a1.astype(F32)
    a2 = r1.astype(BF16)
    a3 = (r1 - a2.astype(F32)).astype(BF16)
    return a1, a2, a3


def _dot_f32(a, b):
    a1, a2, a3 = _split3(a)
    b1, b2, b3 = _split3(b)
    d = lambda x, y: jnp.dot(x, y, preferred_element_type=F32)
    return d(a1, b3) + d(a3, b1) + d(a2, b2) + d(a1, b2) + d(a2, b1) + d(a1, b1)


def _norm_mm_kernel(x_ref, nw_ref, w_ref, hw_ref, o_ref, h_scr, *, n_norm_tiles, gsize):
    j = pl.program_id(1)

    @pl.when(j == 0)
    def _():
        h_scr[...] = _rms_rows(x_ref[...], nw_ref[...]).astype(BF16)

    acc = jnp.dot(h_scr[...], w_ref[...], preferred_element_type=F32)
    if n_norm_tiles == 0:
        o_ref[...] = acc
    else:
        @pl.when(j < n_norm_tiles)
        def _():
            o_ref[...] = _group_norm_scale(acc, gsize, 1.0 / gsize, EPS) * hw_ref[...]

        @pl.when(j >= n_norm_tiles)
        def _():
            o_ref[...] = acc


def _norm_matmul(x, nw, w, *, tm, tn, head_w=None, n_norm_cols=0, gsize=LANES):
    m, k = x.shape
    n = w.shape[1]
    assert m % tm == 0 and n % tn == 0 and n_norm_cols % tn == 0
    if head_w is None:
        head_w = jnp.ones((1, n), F32)
    kern = functools.partial(_norm_mm_kernel, n_norm_tiles=n_norm_cols // tn, gsize=gsize)
    return pl.pallas_call(
        kern,
        grid=(m // tm, n // tn),
        in_specs=[
            pl.BlockSpec((tm, k), lambda i, j: (i, 0)),
            pl.BlockSpec((1, k), lambda i, j: (0, 0)),
            pl.BlockSpec((k, tn), lambda i, j: (0, j)),
            pl.BlockSpec((1, tn), lambda i, j: (0, j)),
        ],
        out_specs=pl.BlockSpec((tm, tn), lambda i, j: (i, j)),
        out_shape=jax.ShapeDtypeStruct((m, n), F32),
        scratch_shapes=[pltpu.VMEM((tm, k), BF16)],
        compiler_params=_cparams(("parallel", "arbitrary")),
        name="norm_matmul",
    )(x, nw.reshape(1, k), w, head_w)


def _mm_res_kernel(*refs, n_in):
    x_ref = refs[2 * n_in]
    o_ref = refs[2 * n_in + 1]
    acc = x_ref[...]
    for t in range(n_in):
        acc = acc + jnp.dot(refs[t][...], refs[n_in + t][...], preferred_element_type=F32)
    o_ref[...] = acc


def _matmul_residual(acts, weights, x, *, tm, tn):
    m, n = x.shape
    n_in = len(acts)
    in_specs = [pl.BlockSpec((tm, a.shape[1]), lambda i, j: (i, 0)) for a in acts]
    in_specs += [pl.BlockSpec((w.shape[0], tn), lambda i, j: (0, j)) for w in weights]
    in_specs += [pl.BlockSpec((tm, tn), lambda i, j: (i, j))]
    return pl.pallas_call(
        functools.partial(_mm_res_kernel, n_in=n_in),
        grid=(m // tm, n // tn),
        in_specs=in_specs,
        out_specs=pl.BlockSpec((tm, tn), lambda i, j: (i, j)),
        out_shape=jax.ShapeDtypeStruct((m, n), F32),
        compiler_params=_cparams(("parallel", "arbitrary")),
        name="matmul_residual",
    )(*acts, *weights, x)


def _swiglu_kernel(x_ref, nw_ref, wg_ref, wu_ref, wd_ref, o_ref, h_scr):
    f = pl.program_id(1)

    @pl.when(f == 0)
    def _():
        x = x_ref[...]
        h_scr[...] = _rms_rows(x, nw_ref[...]).astype(BF16)
        o_ref[...] = x

    h = h_scr[...]
    g = jnp.dot(h, wg_ref[...], preferred_element_type=F32)
    u = jnp.dot(h, wu_ref[...], preferred_element_type=F32)
    a = (_silu(g) * u).astype(BF16)
    o_ref[...] += jnp.dot(a, wd_ref[...], preferred_element_type=F32)


def _swiglu_ffn(x, nw, wg, wu, wd, *, tm, tf):
    m, d = x.shape
    ff = wg.shape[1]
    return pl.pallas_call(
        _swiglu_kernel,
        grid=(m // tm, ff // tf),
        in_specs=[
            pl.BlockSpec((tm, d), lambda i, f: (i, 0)),
            pl.BlockSpec((1, d), lambda i, f: (0, 0)),
            pl.BlockSpec((d, tf), lambda i, f: (0, f)),
            pl.BlockSpec((d, tf), lambda i, f: (0, f)),
            pl.BlockSpec((tf, d), lambda i, f: (f, 0)),
        ],
        out_specs=pl.BlockSpec((tm, d), lambda i, f: (i, 0)),
        out_shape=jax.ShapeDtypeStruct((m, d), F32),
        scratch_shapes=[pltpu.VMEM((tm, d), BF16)],
        compiler_params=_cparams(("parallel", "arbitrary")),
        name="swiglu_ffn",
    )(x, nw.reshape(1, d), wg, wu, wd)


MOE_TILE = 768
ROW_SLABS = D_MODEL // LANES


def _rows_to_slabs(ref, x):
    ref[...] = jnp.swapaxes(jnp.stack([x[:, j * LANES:(j + 1) * LANES] for j in range(ROW_SLABS)], axis=0), 0, 1)


def _slabs_to_rows(ref):
    xt = jnp.swapaxes(ref[...], 0, 1)
    return jnp.concatenate([xt[j] for j in range(ROW_SLABS)], axis=1)


def _moe_router_kernel(x_ref, nw_ref, wr_ref, h_ref, info_ref):
    tm = x_ref.shape[0]
    lane = lax.broadcasted_iota(jnp.int32, (tm, LANES), 1)
    hf = _rms_rows(x_ref[...], nw_ref[...])
    _rows_to_slabs(h_ref, hf)
    logits = jnp.where(lane < N_EXPERTS, _dot_f32(hf, wr_ref[...]), NEG_INF)
    m1 = jnp.max(logits, axis=-1, keepdims=True)
    i1 = jnp.min(jnp.where(logits == m1, lane, LANES), axis=-1, keepdims=True)
    rest = jnp.where(lane == i1, NEG_INF, logits)
    m2 = jnp.max(rest, axis=-1, keepdims=True)
    i2 = jnp.min(jnp.where(rest == m2, lane, LANES), axis=-1, keepdims=True)
    e2 = jnp.exp(m2 - m1)
    vals = [i1.astype(F32), i2.astype(F32), 1.0 / (1.0 + e2), e2 / (1.0 + e2)]
    info = jnp.zeros((tm, LANES), F32)
    for k, val in enumerate(vals):
        info = jnp.where(lane == k, val, info)
    info_ref[...] = info


def _moe_gather_kernel(src_ref, nu_ref, h_ref, xs_ref, sem):
    t = pl.program_id(0)
    tg = xs_ref.shape[0]

    def copy(r):
        return pltpu.make_async_copy(h_ref.at[src_ref[t * tg + r]], xs_ref.at[r], sem.at[0])

    def issue(r, carry):
        copy(r).start()
        return carry

    def drain(r, carry):
        copy(r).wait()
        return carry

    @pl.when(t < nu_ref[0])
    def _():
        lax.fori_loop(0, tg, issue, 0, unroll=8)
        lax.fori_loop(0, tg, drain, 0, unroll=8)

    @pl.when(t >= nu_ref[0])
    def _():
        xs_ref[...] = jnp.zeros(xs_ref.shape, F32)


def _moe_expert_kernel(te_ref, nu_ref, xs_ref, wg_ref, wu_ref, wd_ref, y_ref, xb_scr, acc_scr):
    del te_ref
    t = pl.program_id(0)
    f = pl.program_id(1)

    @pl.when(f == 0)
    def _():
        xb_scr[...] = _slabs_to_rows(xs_ref).astype(BF16)
        acc_scr[...] = jnp.zeros(acc_scr.shape, F32)

    @pl.when(t < nu_ref[0])
    def _():
        xb = xb_scr[...]
        g = jnp.dot(xb, wg_ref[...].astype(BF16), preferred_element_type=F32)
        u = jnp.dot(xb, wu_ref[...].astype(BF16), preferred_element_type=F32)
        acc_scr[...] += jnp.dot((_silu(g) * u).astype(BF16), wd_ref[...].astype(BF16),
                                preferred_element_type=F32)

    @pl.when(f == pl.num_programs(1) - 1)
    def _():
        _rows_to_slabs(y_ref, acc_scr[...])


def _moe_combine_kernel(dest_ref, x_ref, info_ref, y_ref, o1_ref, o2_ref, ybuf, sem, *, n_first):
    i = pl.program_id(0)
    tc = x_ref.shape[0]
    n = dest_ref.shape[0] // 2

    def copy(r, slot):
        return pltpu.make_async_copy(y_ref.at[dest_ref[slot * n + i * tc + r]], ybuf.at[slot, r], sem.at[slot])

    def issue(r, carry):
        copy(r, 0).start()
        copy(r, 1).start()
        return carry

    def drain(r, carry):
        copy(r, 0).wait()
        copy(r, 1).wait()
        return carry

    lax.fori_loop(0, tc, issue, 0, unroll=8)
    lax.fori_loop(0, tc, drain, 0, unroll=8)
    info = info_ref[...]
    y1 = _slabs_to_rows(ybuf.at[0])
    y2 = _slabs_to_rows(ybuf.at[1])
    res = x_ref[...] + info[:, 2:3] * y1 + info[:, 3:4] * y2

    @pl.when(i < n_first)
    def _():
        o1_ref[...] = res

    @pl.when(i >= n_first)
    def _():
        o2_ref[...] = res


def _moe_ffn(x, nw, w_router, wg, wu, wd, w_layer, *, tm, tf, tc, split):
    m, d = x.shape
    _, ne, _, ff = wg.shape
    tg = MOE_TILE
    assert d == D_MODEL and m % tm == 0 and m % tc == 0 and ff % tf == 0
    n_tiles = -(-(2 * m + ne * (tg - 1)) // tg)
    p_rows = n_tiles * tg
    wr = jnp.zeros((d, LANES), F32).at[:, :ne].set(w_router)
    h3, info = pl.pallas_call(
        _moe_router_kernel,
        grid=(m // tm,),
        in_specs=[pl.BlockSpec((tm, d), lambda i: (i, 0)), pl.BlockSpec((1, d), lambda i: (0, 0)),
                  pl.BlockSpec((d, LANES), lambda i: (0, 0))],
        out_specs=[pl.BlockSpec((tm, ROW_SLABS, LANES), lambda i: (i, 0, 0)),
                   pl.BlockSpec((tm, LANES), lambda i: (i, 0))],
        out_shape=[jax.ShapeDtypeStruct((m, ROW_SLABS, LANES), F32), jax.ShapeDtypeStruct((m, LANES), F32)],
        compiler_params=_cparams(("parallel",)),
        name="moe_router",
    )(x, nw.reshape(1, d), wr)

    e12 = info[:, :2].astype(jnp.int32)
    onehot = (e12[:, :1] == jnp.arange(ne)) | (e12[:, 1:2] == jnp.arange(ne))
    csum = jnp.cumsum(onehot.astype(jnp.int32), axis=0)
    rank = csum - onehot
    padded = (csum[-1] + tg - 1) // tg * tg
    gend = jnp.cumsum(padded)
    base = (gend - padded)[None, :] + rank
    dest = jnp.concatenate([jnp.take_along_axis(base, e12[:, :1], axis=1)[:, 0],
                            jnp.take_along_axis(base, e12[:, 1:2], axis=1)[:, 0]]).astype(jnp.int32)
    tile_expert = jnp.minimum(jnp.sum(jnp.arange(n_tiles)[:, None] * tg >= gend[None, :], axis=1),
                              ne - 1).astype(jnp.int32)
    n_used = (gend[-1:] // tg).astype(jnp.int32)

    any_spec = pl.BlockSpec(memory_space=pl.ANY)
    tok_ids = jnp.tile(jnp.arange(m, dtype=jnp.int32), 2)
    src = jnp.zeros((p_rows,), jnp.int32).at[dest].set(tok_ids, unique_indices=True, mode="promise_in_bounds")
    xs3 = pl.pallas_call(
        _moe_gather_kernel,
        grid_spec=pltpu.PrefetchScalarGridSpec(
            num_scalar_prefetch=2, grid=(n_tiles,), in_specs=[any_spec],
            out_specs=pl.BlockSpec((tg, ROW_SLABS, LANES), lambda t, sr, nu: (t, 0, 0)),
            scratch_shapes=[pltpu.SemaphoreType.DMA((1,))]),
        out_shape=jax.ShapeDtypeStruct((p_rows, ROW_SLABS, LANES), F32),
        compiler_params=_cparams(("arbitrary",)),
        name="moe_gather",
    )(src, n_used, h3)

    last = lambda t, nu: jnp.minimum(t, nu[0] - 1)
    y3 = pl.pallas_call(
        _moe_expert_kernel,
        grid_spec=pltpu.PrefetchScalarGridSpec(
            num_scalar_prefetch=2,
            grid=(n_tiles, ff // tf),
            in_specs=[
                pl.BlockSpec((tg, ROW_SLABS, LANES), lambda t, f, te, nu: (last(t, nu), 0, 0)),
                pl.BlockSpec((None, None, d, tf), lambda t, f, te, nu: (w_layer, te[last(t, nu)], 0, f)),
                pl.BlockSpec((None, None, d, tf), lambda t, f, te, nu: (w_layer, te[last(t, nu)], 0, f)),
                pl.BlockSpec((None, None, tf, d), lambda t, f, te, nu: (w_layer, te[last(t, nu)], f, 0)),
            ],
            out_specs=pl.BlockSpec((tg, ROW_SLABS, LANES), lambda t, f, te, nu: (t, 0, 0)),
            scratch_shapes=[pltpu.VMEM((tg, d), BF16), pltpu.VMEM((tg, d), F32)]),
        out_shape=jax.ShapeDtypeStruct((p_rows, ROW_SLABS, LANES), F32),
        compiler_params=_cparams(("arbitrary", "arbitrary")),
        name="moe_experts",
    )(tile_expert, n_used, xs3, wg, wu, wd)

    assert split % tc == 0 and 0 < split < m
    n_first = split // tc
    return pl.pallas_call(
        functools.partial(_moe_combine_kernel, n_first=n_first),
        grid_spec=pltpu.PrefetchScalarGridSpec(
            num_scalar_prefetch=1,
            grid=(m // tc,),
            in_specs=[pl.BlockSpec((tc, d), lambda i, de: (i, 0)), pl.BlockSpec((tc, LANES), lambda i, de: (i, 0)),
                      any_spec],
            out_specs=[pl.BlockSpec((tc, d), lambda i, de: (jnp.minimum(i, n_first - 1), 0)),
                       pl.BlockSpec((tc, d), lambda i, de: (jnp.maximum(i - n_first, 0), 0))],
            scratch_shapes=[pltpu.VMEM((2, tc, ROW_SLABS, LANES), F32), pltpu.SemaphoreType.DMA((2,))]),
        out_shape=[jax.ShapeDtypeStruct((split, d), F32), jax.ShapeDtypeStruct((m - split, d), F32)],
        compiler_params=_cparams(("arbitrary",)),
        name="moe_combine",
    )(dest, x, info, y3)


def _diff_lambda_vec(lam_ref, lam_init):
    lv = lam_ref[...]
    d1 = jnp.sum(lv[0:1, :] * lv[1:2, :], axis=-1, keepdims=True)
    d2 = jnp.sum(lv[2:3, :] * lv[3:4, :], axis=-1, keepdims=True)
    return jnp.exp(d1) - jnp.exp(d2) + lam_init


def _softmax_rows(s):
    m = jnp.max(s, axis=-1, keepdims=True)
    e = jnp.exp(s - m)
    return e, jnp.sum(e, axis=-1, keepdims=True)


def _diffattn_prompt_kernel(slope_ref, q_ref, k_ref, v_ref, lam_ref, subln_ref, o_ref, *, tq, lam_init):
    t = q_ref.shape[0]
    h = pl.program_id(1)
    slope = slope_ref[h]
    lane = lax.broadcasted_iota(jnp.int32, (1, LANES), 1)
    lam = _diff_lambda_vec(lam_ref, lam_init)
    kb = k_ref[...].astype(BF16)
    vb = v_ref[...].astype(BF16)
    nt = (((1,), (1,)), ((), ()))
    for qi in range(t // tq):
        n_keys = (qi + 1) * tq
        q = q_ref[qi * tq:(qi + 1) * tq, :] * (A_DH ** -0.5)
        q1 = jnp.where(lane < A_DH, q, 0.0).astype(BF16)
        q2 = jnp.where(lane >= A_DH, q, 0.0).astype(BF16)
        kk = kb[:n_keys]
        row = lax.broadcasted_iota(jnp.int32, (tq, n_keys), 0) + qi * tq
        col = lax.broadcasted_iota(jnp.int32, (tq, n_keys), 1)
        dist = (row - col).astype(F32)
        bias = jnp.where(dist >= 0.0, -slope * dist, NEG_INF)
        e1, l1 = _softmax_rows(lax.dot_general(q1, kk, nt, preferred_element_type=F32) + bias)
        e2, l2 = _softmax_rows(lax.dot_general(q2, kk, nt, preferred_element_type=F32) + bias)
        p = e1 * (1.0 / l1) - e2 * (lam / l2)
        o = jnp.dot(p.astype(BF16), vb[:n_keys], preferred_element_type=F32)
        o = _rms_rows(o, subln_ref[...]) * (1.0 - lam_init)
        o_ref[qi * tq:(qi + 1) * tq, :] = o.astype(BF16)


def _diff_attention_prompt(pa, n_batch, t, slopes, lam_stack, subln, lam_init, *, tq):
    kern = functools.partial(_diffattn_prompt_kernel, tq=tq, lam_init=lam_init)
    return pl.pallas_call(
        kern,
        grid=(n_batch, A_HEADS),
        in_specs=[
            pl.BlockSpec(memory_space=pltpu.SMEM),
            pl.BlockSpec((t, LANES), lambda b, h: (b, h)),
            pl.BlockSpec((t, LANES), lambda b, h: (b, A_HEADS + h)),
            pl.BlockSpec((t, LANES), lambda b, h: (b, 2 * A_HEADS + h)),
            pl.BlockSpec((4, A_DH), lambda b, h: (0, 0)),
            pl.BlockSpec((1, LANES), lambda b, h: (0, 0)),
        ],
        out_specs=pl.BlockSpec((t, LANES), lambda b, h: (b, h)),
        out_shape=jax.ShapeDtypeStruct((n_batch * t, A_HEADS * LANES), BF16),
        compiler_params=_cparams(("parallel", "parallel")),
        name="diff_attention_prompt",
    )(slopes, pa, pa, pa, lam_stack, subln.reshape(1, LANES))


PAGES_PER_STEP = 16


def _diffattn_sample_kernel(pt_ref, q_ref, kn_ref, vn_ref, *rest, past, lam_init):
    del pt_ref
    npg = PAGES_PER_STEP
    k_refs, v_refs = rest[:npg], rest[npg:2 * npg]
    lam_ref, subln_ref, o_ref, qq_scr, m_scr, l_scr, acc_scr, bias_scr = rest[2 * npg:]
    g = pl.program_id(1)
    ts = q_ref.shape[0]
    page = k_refs[0].shape[0]
    lane = lax.broadcasted_iota(jnp.int32, (1, LANES), 1)
    hsub = lax.broadcasted_iota(jnp.int32, (A_HEADS, LANES), 0)
    col = lax.broadcasted_iota(jnp.int32, (A_HEADS, LANES), 1)
    valid = (col // (2 * ts)) == hsub
    pos_q = past + jnp.bitwise_and(col, ts - 1)
    slope = jnp.exp2(-(8.0 / A_HEADS) * (hsub + 1).astype(F32))
    nt = (((1,), (1,)), ((), ()))
    tn = (((0,), (0,)), ((), ()))

    @pl.when(g == 0)
    def _():
        rows = []
        for h in range(A_HEADS):
            q = q_ref[:, h * LANES:(h + 1) * LANES] * (A_DH ** -0.5)
            rows += [jnp.where(lane < A_DH, q, 0.0), jnp.where(lane >= A_DH, q, 0.0)]
        qq_scr[...] = jnp.concatenate(rows, axis=0).astype(BF16)
        m_scr[...] = jnp.where(valid, NEG_INF, 0.0)
        l_scr[...] = jnp.zeros(l_scr.shape, F32)
        acc_scr[...] = jnp.zeros(acc_scr.shape, F32)

    def key_bias(nk, key0, causal):
        kidx = lax.broadcasted_iota(jnp.int32, (nk * A_HEADS, LANES), 0) // A_HEADS
        dist = (pos_q[None] - (key0 + kidx).reshape(nk, A_HEADS, LANES)).astype(F32)
        keep = valid[None] & (dist >= 0.0) if causal else valid[None]
        return jnp.where(keep, -slope[None] * dist, NEG_INF)

    @pl.when(g == 0)
    def _():
        bias_scr[...] = key_bias(page, 0, False)

    def scores(k3):
        nk = k3.shape[0]
        k2 = k3.reshape(nk * A_HEADS, LANES).astype(BF16)
        return lax.dot_general(k2, qq_scr[...], nt, preferred_element_type=F32).reshape(nk, A_HEADS, LANES)

    def absorb(ss, v3s):
        m_old = m_scr[...]
        m_new = m_old
        for s in ss:
            m_new = jnp.maximum(m_new, jnp.max(s, axis=0))
        alpha = jnp.exp(m_old - m_new)
        l_new = alpha * l_scr[...]
        acc = jnp.sum(jnp.where(valid, alpha, 0.0), axis=0, keepdims=True) * acc_scr[...]
        for s, v3 in zip(ss, v3s):
            nk = s.shape[0]
            pe = jnp.exp(s - m_new[None])
            l_new = l_new + jnp.sum(pe, axis=0)
            v2 = v3.reshape(nk * A_HEADS, LANES).astype(BF16)
            acc = acc + lax.dot_general(v2, pe.reshape(nk * A_HEADS, LANES).astype(BF16), tn,
                                        preferred_element_type=F32)
        m_scr[...] = m_new
        l_scr[...] = l_new
        acc_scr[...] = acc

    absorb([scores(k_refs[j][...]) + bias_scr[...] + (slope * ((g * npg + j) * page).astype(F32))[None]
            for j in range(npg)], [v_refs[j][...] for j in range(npg)])

    @pl.when(g == pl.num_programs(1) - 1)
    def _():
        absorb([scores(kn_ref[...]) + key_bias(ts, past, True)], [vn_ref[...]])
        lam = _diff_lambda_vec(lam_ref, lam_init)
        l_row = jnp.sum(jnp.where(valid, l_scr[...], 0.0), axis=0, keepdims=True)
        ot = acc_scr[...] / l_row
        ot = ot - lam * pltpu.roll(ot, LANES - ts, 1)
        o = ot.T
        for h in range(A_HEADS):
            oh = _rms_rows(o[h * 2 * ts:h * 2 * ts + ts, :], subln_ref[...]) * (1.0 - lam_init)
            o_ref[:, h * LANES:(h + 1) * LANES] = oh.astype(BF16)


def _diff_attention_sample(pa, row0, n_batch, ts, cache_k, cache_v, layer_idx, page_table, lam_stack, subln,
                           lam_init):
    n_pages = page_table.shape[1]
    page = cache_k.shape[2]
    npg = PAGES_PER_STEP
    assert row0 % ts == 0 and n_pages % npg == 0 and 2 * ts * A_HEADS == LANES
    rb = row0 // ts
    w = A_HEADS * LANES
    new_rows = lambda c0: pa[row0:, c0:c0 + w].reshape(n_batch, ts, A_HEADS, LANES)
    kern = functools.partial(_diffattn_sample_kernel, past=n_pages * page, lam_init=lam_init)
    page_spec = lambda j: pl.BlockSpec((None, None, page, A_HEADS, LANES),
                                       lambda b, g, pt: (layer_idx, pt[b, g * npg + j], 0, 0, 0))
    new_spec = pl.BlockSpec((None, ts, A_HEADS, LANES), lambda b, g, pt: (b, 0, 0, 0))
    grid_spec = pltpu.PrefetchScalarGridSpec(
        num_scalar_prefetch=1,
        grid=(n_batch, n_pages // npg),
        in_specs=[pl.BlockSpec((ts, w), lambda b, g, pt: (rb + b, 0)), new_spec, new_spec]
        + [page_spec(j) for j in range(npg)] * 2
        + [pl.BlockSpec((4, A_DH), lambda b, g, pt: (0, 0)), pl.BlockSpec((1, LANES), lambda b, g, pt: (0, 0))],
        out_specs=pl.BlockSpec((ts, w), lambda b, g, pt: (b, 0)),
        scratch_shapes=[
            pltpu.VMEM((LANES, LANES), BF16),
            pltpu.VMEM((A_HEADS, LANES), F32),
            pltpu.VMEM((A_HEADS, LANES), F32),
            pltpu.VMEM((LANES, LANES), F32),
            pltpu.VMEM((page, A_HEADS, LANES), F32),
        ],
    )
    return pl.pallas_call(
        kern,
        grid_spec=grid_spec,
        out_shape=jax.ShapeDtypeStruct((n_batch * ts, w), BF16),
        compiler_params=_cparams(("parallel", "arbitrary")),
        name="diff_attention_sample",
    )(page_table, pa, new_rows(w), new_rows(2 * w), *([cache_k] * npg), *([cache_v] * npg), lam_stack,
      subln.reshape(1, LANES))


def _cross_attn_kernel(q_ref, k_ref, v_ref, qn_ref, o_ref):
    nt = (((1,), (1,)), ((), ()))
    for h in range(X_HEADS):
        sl = slice(h * LANES, (h + 1) * LANES)
        q = _rms_rows(q_ref[:, sl], qn_ref[...]) * (X_DH ** -0.5)
        s = lax.dot_general(q.astype(BF16), k_ref[:, sl].astype(BF16), nt, preferred_element_type=F32)
        e, l = _softmax_rows(s)
        p = (e * (1.0 / l)).astype(BF16)
        o_ref[:, sl] = jnp.dot(p, v_ref[:, sl].astype(BF16), preferred_element_type=F32).astype(BF16)


def _cross_attention(q, row0, n_batch, t, mk, mv, q_norm, *, tq):
    assert t % tq == 0 and row0 % tq == 0
    nq = t // tq
    rb = row0 // tq
    return pl.pallas_call(
        _cross_attn_kernel,
        grid=(n_batch, nq),
        in_specs=[
            pl.BlockSpec((tq, X_W), lambda b, i: (rb + b * nq + i, 0)),
            pl.BlockSpec((None, N_MEM, X_W), lambda b, i: (b, 0, 0)),
            pl.BlockSpec((None, N_MEM, X_W), lambda b, i: (b, 0, 0)),
            pl.BlockSpec((1, LANES), lambda b, i: (0, 0)),
        ],
        out_specs=pl.BlockSpec((tq, X_W), lambda b, i: (b * nq + i, 0)),
        out_shape=jax.ShapeDtypeStruct((n_batch * t, X_W), BF16),
        compiler_params=_cparams(("parallel", "parallel")),
        name="cross_attention",
    )(q, mk, mv, q_norm.reshape(1, LANES))


PB_LG = 3 * B_W
PB_LW = PB_LG + B_GATE_LORA
PB_LA = PB_LW + LANES
PB_COLS = PB_LA + LANES


def _swap_halves(x):
    tiles = [pltpu.roll(x[:, c * LANES:(c + 1) * LANES], LANES // 2, 1) for c in range(x.shape[1] // LANES)]
    return tiles[0] if len(tiles) == 1 else jnp.concatenate(tiles, axis=1)


def _rwkv_prep_kernel(pb_ref, halo_ref, start_ref, mu_ref, w0_ref, w2_ref, a0_ref, a2_ref, g2_ref, kk_ref, ka_ref,
                      r_ref, kf_ref, v_ref, g_ref, rt_ref, at_ref, bt_ref, kt_ref, bh_ref, kh_ref, pc_ref,
                      *, n_prompt_tiles, chunk_p, chunk_s, t_p, t_s):
    tm = pb_ref.shape[0]
    i = pl.program_id(0)
    pb = pb_ref[...]
    row = lax.broadcasted_iota(jnp.int32, (tm, 1), 0)
    prev = jnp.where(row == 0, halo_ref[HALO - 1:HALO, :], pltpu.roll(pb, 1, 0))
    seq_len = jnp.where(i < n_prompt_tiles, t_p, t_s)
    is_start = jnp.bitwise_and(i * tm + row, seq_len - 1) == 0
    ps = jnp.where(is_start, start_ref[...], prev)
    xs = pb + (ps - pb) * mu_ref[...]
    r = xs[:, :B_W]
    k = xs[:, B_W:2 * B_W]
    lg = xs[:, PB_LG:PB_LW]
    lw_in = xs[:, PB_LW:PB_LA]
    la = xs[:, PB_LA:PB_COLS]
    wl = w0_ref[...] + jnp.dot(jnp.tanh(lw_in).astype(BF16), w2_ref[...], preferred_element_type=F32)
    lw = -jnp.exp(-_softplus(-wl) - 0.5)
    a = _sigmoid(a0_ref[...] + jnp.dot(la.astype(BF16), a2_ref[...], preferred_element_type=F32))
    kk = _group_norm_scale(k * kk_ref[...], B_DH, 1.0, EPS)
    kf = k * (1.0 + (a - 1.0) * ka_ref[...])
    b = kk * a
    r_ref[...] = r
    kf_ref[...] = kf
    v_ref[...] = xs[:, 2 * B_W:3 * B_W]
    g_ref[...] = jnp.dot(_sigmoid(lg).astype(BF16), g2_ref[...], preferred_element_type=F32)

    chunk = jnp.where(pl.program_id(0) < n_prompt_tiles, chunk_p, chunk_s)
    rowc = jnp.bitwise_and(lax.broadcasted_iota(jnp.int32, (tm, B_W), 0), chunk - 1)
    lp = lw
    sfx = lw
    step = 1
    while step < max(chunk_p, chunk_s):
        lp = lp + jnp.where(rowc >= step, pltpu.roll(lp, step, 0), 0.0)
        sfx = sfx + jnp.where(rowc + step < chunk, pltpu.roll(sfx, tm - step, 0), 0.0)
        step *= 2
    sfx = sfx - lw
    e_neg = jnp.exp(-lp)
    e_sfx = jnp.exp(sfx)
    sw = lambda z: _swap_halves(z).astype(BF16)
    rt_ref[...] = sw(r * jnp.exp(lp))
    at_ref[...] = sw(kk * jnp.exp(lp - lw))
    bt_ref[...] = sw(b * e_neg)
    kt_ref[...] = sw(kf * e_neg)
    bh_ref[...] = sw(b * e_sfx)
    kh_ref[...] = sw(kf * e_sfx)
    pc_ref[...] = _swap_halves(jnp.exp(lp + sfx))


def _rwkv_prep(pb, starts, mu, w0, w2, a0, a2, g2, k_k, k_a, *, tm, n_prompt_tiles, chunk_p, chunk_s, t_p, t_s):
    m = pb.shape[0]
    assert tm % chunk_p == 0 and tm % chunk_s == 0 and tm % HALO == 0
    assert t_p & (t_p - 1) == 0 and t_s & (t_s - 1) == 0 and (n_prompt_tiles * tm) % t_s == 0
    npt = n_prompt_tiles
    row = lambda n: pl.BlockSpec((1, n), lambda i: (0, 0))
    full = lambda a: pl.BlockSpec(a.shape, lambda i: (0, 0))
    tok = lambda n: pl.BlockSpec((tm, n), lambda i: (i, 0))
    f32o = jax.ShapeDtypeStruct((m, B_W), F32)
    b16o = jax.ShapeDtypeStruct((m, B_W), BF16)
    kern = functools.partial(_rwkv_prep_kernel, n_prompt_tiles=npt, chunk_p=chunk_p, chunk_s=chunk_s, t_p=t_p,
                             t_s=t_s)
    return pl.pallas_call(
        kern,
        grid=(m // tm,),
        in_specs=[tok(PB_COLS),
                  pl.BlockSpec((HALO, PB_COLS), lambda i: (jnp.maximum(i * (tm // HALO) - 1, 0), 0)),
                  pl.BlockSpec((tm, PB_COLS), lambda i: (jnp.maximum(i - npt + 1, 0), 0)),
                  row(PB_COLS), row(B_W), full(w2), row(B_W), full(a2), full(g2), row(B_W), row(B_W)],
        out_specs=[tok(B_W)] * 11,
        out_shape=[f32o] * 4 + [b16o] * 6 + [f32o],
        compiler_params=_cparams(("parallel",)),
        name="rwkv_prep",
    )(pb, pb, starts, mu, w0.reshape(1, B_W), w2, a0.reshape(1, B_W), a2, g2, k_k.reshape(1, B_W),
      k_a.reshape(1, B_W))


SOLVE_BLOCK = 16


def _split2(a):
    hi = a.astype(BF16)
    return hi, (a - hi.astype(F32)).astype(BF16)


def _solve_unit_lower_many(lmats, rhss):
    ns = len(lmats)
    c = lmats[0].shape[0]
    n = rhss[0].shape[1]
    bs = min(SOLVE_BLOCK, c)
    col = lax.broadcasted_iota(jnp.int32, (bs, c), 1)
    d = lambda x, y: jnp.dot(x, y, preferred_element_type=F32)
    done = [[] for _ in range(ns)]
    for blk in range(c // bs):
        rows = slice(blk * bs, (blk + 1) * bs)
        lrows = [lm[rows, :] for lm in lmats]
        rs = [rh[rows, :] for rh in rhss]
        if blk > 0:
            for s in range(ns):
                xs = jnp.concatenate(done[s] + [jnp.zeros((c - blk * bs, n), F32)], axis=0)
                lh, ll = _split2(jnp.where(col < blk * bs, lrows[s], 0.0))
                xh, xl = _split2(xs)
                rs[s] = rs[s] - (d(lh, xl) + d(ll, xh) + d(lh, xh))
        for j in range(bs - 1):
            for s in range(ns):
                rs[s] = rs[s] - lrows[s][:, blk * bs + j:blk * bs + j + 1] * rs[s][j:j + 1, :]
        for s in range(ns):
            done[s].append(rs[s])
    return [dn[0] if len(dn) == 1 else jnp.concatenate(dn, axis=0) for dn in done]


def _solve_unit_lower(lmat, rhs):
    return _solve_unit_lower_many([lmat], [rhs])[0]


def _rwkv_chunk_kernel(rt_ref, at_ref, bt_ref, kt_ref, bh_ref, kh_ref, v_ref, pc_ref, z0_ref, y_ref, zo_ref, z_scr,
                       *, chunk):
    i = pl.program_id(2)
    tb = rt_ref.shape[0]

    @pl.when(i == 0)
    def _():
        z_scr[...] = z0_ref[...]

    lo = lax.broadcasted_iota(jnp.int32, (1, LANES), 1) < B_DH
    zr = lax.broadcasted_iota(jnp.int32, (LANES, LANES), 0) < B_DH
    zc = lax.broadcasted_iota(jnp.int32, (LANES, LANES), 1) < B_DH
    offdiag = zr != zc
    ri = lax.broadcasted_iota(jnp.int32, (chunk, chunk), 0)
    ci = lax.broadcasted_iota(jnp.int32, (chunk, chunk), 1)
    nt = (((1,), (1,)), ((), ()))
    tn = (((0,), (0,)), ((), ()))
    d = lambda x, y: jnp.dot(x, y, preferred_element_type=F32)
    zero = jnp.zeros((), BF16)
    n_chunks = tb // chunk
    n_pp = rt_ref.shape[1] // LANES
    lmats, rhss, rkvs, rbs = [], [], [], []
    for p in range(n_pp):
        ls = slice(p * LANES, (p + 1) * LANES)
        for c in range(n_chunks):
            rs = slice(c * chunk, (c + 1) * chunk)
            rt, at, bt, kt = rt_ref[rs, ls], at_ref[rs, ls], bt_ref[rs, ls], kt_ref[rs, ls]
            v = v_ref[rs, ls].astype(BF16)
            for head in range(2):
                kmask = lo if head == 1 else jnp.logical_not(lo)
                a_h, r_h = jnp.where(kmask, at, zero), jnp.where(kmask, rt, zero)
                b_h, k_h = jnp.where(kmask, bt, zero), jnp.where(kmask, kt, zero)
                v_h = jnp.where(kmask, zero, v)
                lhs = jnp.concatenate([a_h, r_h], axis=0)
                gb = lax.dot_general(lhs, b_h, nt, preferred_element_type=F32)
                gk = lax.dot_general(lhs, k_h, nt, preferred_element_type=F32)
                lmats.append(jnp.where(ri > ci, gb[:chunk], 0.0))
                rbs.append(jnp.where(ri >= ci, gb[chunk:], 0.0).astype(BF16))
                ak = jnp.where(ri > ci, gk[:chunk], 0.0).astype(BF16)
                rk = jnp.where(ri >= ci, gk[chunk:], 0.0).astype(BF16)
                rhss.append(a_h.astype(F32) + d(ak, v_h))
                rkvs.append(d(rk, v_h))
    xs = _solve_unit_lower_many(lmats, rhss)
    zs = [z_scr[p] for p in range(n_pp)]
    for c in range(n_chunks):
        rs = slice(c * chunk, (c + 1) * chunk)
        for p in range(n_pp):
            ls = slice(p * LANES, (p + 1) * LANES)
            s0 = 2 * (p * n_chunks + c)
            rt = rt_ref[rs, ls]
            stacked = jnp.concatenate([xs[s0].astype(BF16), xs[s0 + 1].astype(BF16),
                                       jnp.where(lo, zero, rt), jnp.where(lo, rt, zero)], axis=0)
            big = lax.dot_general(stacked, zs[p].astype(BF16), nt, preferred_element_type=F32)
            u0 = jnp.where(lo, big[:chunk], 0.0)
            u1 = jnp.where(lo, 0.0, big[chunk:2 * chunk])
            y = jnp.where(lo, big[2 * chunk:3 * chunk], big[3 * chunk:]) + rkvs[s0] + rkvs[s0 + 1]
            y = y - d(rbs[s0], u0.astype(BF16)) - d(rbs[s0 + 1], u1.astype(BF16))
            y_ref[rs, ls] = y
            upd = lax.dot_general(jnp.concatenate([v_ref[rs, ls].astype(BF16), (u0 + u1).astype(BF16)], axis=0),
                                  jnp.concatenate([kh_ref[rs, ls], -bh_ref[rs, ls]], axis=0), tn,
                                  preferred_element_type=F32)
            zs[p] = jnp.where(offdiag, zs[p] * pc_ref[c * chunk:c * chunk + 1, ls] + upd, zs[p])
    for p in range(n_pp):
        z_scr[p] = zs[p]

    @pl.when(i == pl.num_programs(2) - 1)
    def _():
        zo_ref[...] = z_scr[...]


RWKV_PAIRS_PER_STEP = 4


def _rwkv_chunks(rt, at, bt, kt, bh, kh, v, pc, z0, row0, n_seq, t, *, tb, chunk, pps=RWKV_PAIRS_PER_STEP):
    assert t % tb == 0 and row0 % tb == 0 and tb % chunk == 0
    nb = t // tb
    rb = row0 // tb
    n_pairs = B_HEADS // 2
    tok = pl.BlockSpec((tb, pps * LANES), lambda s, p, i: (rb + s * nb + i, p))
    st = pl.BlockSpec((None, pps, LANES, LANES), lambda s, p, i: (s, p, 0, 0))
    return pl.pallas_call(
        functools.partial(_rwkv_chunk_kernel, chunk=chunk),
        grid=(n_seq, n_pairs // pps, nb),
        in_specs=[tok] * 8 + [st],
        out_specs=[pl.BlockSpec((tb, pps * LANES), lambda s, p, i: (s * nb + i, p)), st],
        out_shape=[jax.ShapeDtypeStruct((n_seq * t, B_W), F32),
                   jax.ShapeDtypeStruct((n_seq, n_pairs, LANES, LANES), F32)],
        scratch_shapes=[pltpu.VMEM((pps, LANES, LANES), F32)],
        compiler_params=_cparams(("parallel", "parallel", "arbitrary")),
        name="rwkv_chunks",
    )(rt, at, bt, kt, bh, kh, v, pc, z0)


def _rwkv_post_kernel(y_ref, r_ref, kf_ref, v_ref, g_ref, lw_ref, lb_ref, rk_ref, o_ref):
    y = y_ref[...]
    mean = _group_sum(y, B_DH) * (1.0 / B_DH)
    yc = y - mean
    var = _group_sum(yc * yc, B_DH) * (1.0 / B_DH)
    yn = yc * lax.rsqrt(var + B_GN_EPS) * lw_ref[...] + lb_ref[...]
    bonus = _group_sum(r_ref[...] * kf_ref[...] * rk_ref[...], B_DH)
    o_ref[...] = ((yn + bonus * v_ref[...]) * g_ref[...]).astype(BF16)


def _rwkv_post(y, r, kf, v, g, lnx_w, lnx_b, r_k, *, tm):
    m = y.shape[0]
    tok = pl.BlockSpec((tm, B_W), lambda i: (i, 0))
    row = pl.BlockSpec((1, B_W), lambda i: (0, 0))
    return pl.pallas_call(
        _rwkv_post_kernel,
        grid=(m // tm,),
        in_specs=[tok] * 5 + [row] * 3,
        out_specs=tok,
        out_shape=jax.ShapeDtypeStruct((m, B_W), BF16),
        compiler_params=_cparams(("parallel",)),
        name="rwkv_post",
    )(y, r, kf, v, g, lnx_w.reshape(1, B_W), lnx_b.reshape(1, B_W), r_k.reshape(1, B_W))


def _rwkv_cols(a):
    lw0 = 3 * B_W
    la0 = lw0 + B_DECAY_LORA
    lg0 = la0 + B_AAA_LORA
    z = jnp.zeros(a.shape[:-1] + (LANES - B_DECAY_LORA,), a.dtype)
    return jnp.concatenate([a[..., :lw0], a[..., lg0:], a[..., lw0:la0], z, a[..., la0:lg0], z], axis=-1)


def _rwkv_cols_inv(a):
    return jnp.concatenate([a[..., :PB_LG], a[..., PB_LW:PB_LW + B_DECAY_LORA],
                            a[..., PB_LA:PB_LA + B_AAA_LORA], a[..., PB_LG:PB_LW]], axis=-1)


def _pad_rows(a, n):
    return jnp.pad(a, ((0, n - a.shape[0]), (0, 0)))


def _rwkv_state_to_pairs(s):
    n = s.shape[0]
    s = s.reshape(n, B_HEADS // 2, 2, B_DH, B_DH)
    eye = jnp.broadcast_to(jnp.eye(B_DH, dtype=F32), (n, B_HEADS // 2, B_DH, B_DH))
    top = jnp.concatenate([eye, s[:, :, 0]], axis=-1)
    bot = jnp.concatenate([s[:, :, 1], eye], axis=-1)
    return jnp.concatenate([top, bot], axis=-2)


def _rwkv_state_from_pairs(z):
    n = z.shape[0]
    s = jnp.stack([z[:, :, :B_DH, B_DH:], z[:, :, B_DH:, :B_DH]], axis=2)
    return s.reshape(n, B_HEADS, B_DH, B_DH)


def _rwkv_mixer(pb, n_p, t_p, n_s, t_s, prev_s, state_s, mu, w0, w2, a0, a2, g2, k_k, k_a, r_k, lnx_w, lnx_b,
                *, tm, tb):
    mp = n_p * t_p
    chunk_p = min(64, t_p)
    chunk_s = min(64, t_s)
    assert mp % tm == 0 and (n_s * t_s) % tm == 0 and t_p % chunk_p == 0 and t_s % chunk_s == 0
    starts = jnp.concatenate([
        jnp.zeros((tm, PB_COLS), F32),
        jnp.broadcast_to(_rwkv_cols(prev_s), (n_s, t_s, PB_COLS)).reshape(n_s * t_s, PB_COLS)], axis=0)
    r, kf, v, g, rt, at, bt, kt, bh, kh, pc = _rwkv_prep(
        pb, starts, _rwkv_cols(mu.reshape(1, B_COLS)), w0, _pad_rows(w2, LANES).astype(BF16), a0,
        _pad_rows(a2, LANES).astype(BF16), g2.astype(BF16), k_k, k_a, tm=tm, n_prompt_tiles=mp // tm,
        chunk_p=chunk_p, chunk_s=chunk_s, t_p=t_p, t_s=t_s)
    z0_p = _rwkv_state_to_pairs(jnp.zeros((n_p, B_HEADS, B_DH, B_DH), F32))
    y_p, z_p = _rwkv_chunks(rt, at, bt, kt, bh, kh, v, pc, z0_p, 0, n_p, t_p, tb=tb, chunk=chunk_p)
    y_s, z_s = _rwkv_chunks(rt, at, bt, kt, bh, kh, v, pc, _rwkv_state_to_pairs(state_s), mp, n_s, t_s,
                            tb=t_s, chunk=chunk_s, pps=B_HEADS // 2)
    y = jnp.concatenate([y_p, y_s], axis=0)
    ob = _rwkv_post(y, r, kf, v, g, lnx_w, lnx_b, r_k.reshape(B_W), tm=tm)
    return ob, _rwkv_state_from_pairs(z_p), _rwkv_state_from_pairs(z_s)


PC_Z = C_CONV_CH
PC_BETA = PC_Z + C_V
PC_COLS = PC_BETA + 2 * LANES
HALO = 8


def _chunk_cumsum(g, chunk):
    rowc = jnp.bitwise_and(lax.broadcasted_iota(jnp.int32, g.shape, 0), chunk - 1)
    k = 1
    while k < chunk:
        g = g + jnp.where(rowc >= k, pltpu.roll(g, k, 0), 0.0)
        k *= 2
    return g


def _transpose_rows(a):
    c = a.shape[0]
    if c < LANES:
        a = jnp.concatenate([a, jnp.zeros((LANES - c, LANES), F32)], axis=0)
    return a.T


def _pad_chunk_rows(a):
    c = a.shape[0]
    return a if c == LANES else jnp.concatenate([a, jnp.zeros((LANES - c, a.shape[1]), a.dtype)], axis=0)


def _gdn_prep_kernel(x_ref, halo_ref, st_ref, cw_ref, sm_ref, alog_ref, dtb_ref,
                     q_ref, k_ref, v_ref, beta_ref, gc_ref, *, chunk):
    i = pl.program_id(1)
    x = x_ref[...]
    tm = x.shape[0]
    halo = jnp.where(i == 0, st_ref[...], halo_ref[...])
    row8 = lax.broadcasted_iota(jnp.int32, (HALO, x.shape[1]), 0)
    acc = x * cw_ref[C_CONV - 1:C_CONV, :]
    for s in range(1, C_CONV):
        xs = pltpu.roll(x, s, 0)
        first = jnp.where(row8 < s, pltpu.roll(halo, s, 0), xs[:HALO])
        xs = first if tm == HALO else jnp.concatenate([first, xs[HALO:]], axis=0)
        acc = acc + xs * cw_ref[C_CONV - 1 - s:C_CONV - s, :]
    y = _silu(acc)
    q_ref[...] = _group_norm_scale(y[:, :C_QK], C_DH, 1.0, EPS) * (C_DH ** -0.5)
    k_ref[...] = _group_norm_scale(y[:, C_QK:2 * C_QK], C_DH, 1.0, EPS)
    v_ref[...] = y[:, 2 * C_QK:]
    sm = sm_ref[...]
    beta_ref[...] = _sigmoid(sm[:, :LANES])
    g = -jnp.exp(alog_ref[...]) * _softplus(sm[:, LANES:] + dtb_ref[...])
    gc_ref[...] = _chunk_cumsum(g, chunk)


def _gdn_prep(pc, conv_state, row0, n_seq, t, conv_w, a_log, dt_bias, *, tm, chunk):
    assert t % tm == 0 and row0 % tm == 0 and tm % HALO == 0 and tm % chunk == 0 and chunk & (chunk - 1) == 0
    nb = t // tm
    rb = row0 // tm
    hb = tm // HALO
    tok_in = lambda n, cb: pl.BlockSpec((tm, n), lambda s, i: (rb + s * nb + i, cb))
    tok_out = lambda n: pl.BlockSpec((tm, n), lambda s, i: (s * nb + i, 0))
    row = lambda n: pl.BlockSpec((1, n), lambda s, i: (0, 0))
    sds = lambda n: jax.ShapeDtypeStruct((n_seq * t, n), F32)
    lanes8 = lambda a: jnp.zeros((1, LANES), F32).at[0, :C_V_HEADS].set(a)
    return pl.pallas_call(
        functools.partial(_gdn_prep_kernel, chunk=chunk),
        grid=(n_seq, nb),
        in_specs=[
            tok_in(C_CONV_CH, 0),
            pl.BlockSpec((HALO, C_CONV_CH), lambda s, i: (jnp.maximum((rb + s * nb + i) * hb - 1, 0), 0)),
            pl.BlockSpec((None, HALO, C_CONV_CH), lambda s, i: (s, 0, 0)),
            pl.BlockSpec((C_CONV, C_CONV_CH), lambda s, i: (0, 0)),
            tok_in(2 * LANES, PC_BETA // (2 * LANES)),
            row(LANES), row(LANES),
        ],
        out_specs=[tok_out(C_QK), tok_out(C_QK), tok_out(C_V), tok_out(LANES), tok_out(LANES)],
        out_shape=[sds(C_QK), sds(C_QK), sds(C_V), sds(LANES), sds(LANES)],
        compiler_params=_cparams(("parallel", "arbitrary")),
        name="gdn_prep",
    )(pc, pc, conv_state, conv_w, pc, lanes8(a_log), lanes8(dt_bias))


def _gdn_chunk_kernel(q_ref, k_ref, v_ref, beta_ref, gc_ref, z_ref, on_ref, s0_ref, o_ref, so_ref, s_scr, *, chunk):
    hq = pl.program_id(1)
    i = pl.program_id(2)
    tb = q_ref.shape[0]
    rep = v_ref.shape[1] // C_DH

    @pl.when(i == 0)
    def _():
        s_scr[...] = s0_ref[...]

    lane = lax.broadcasted_iota(jnp.int32, (chunk, LANES), 1)
    ri = lax.broadcasted_iota(jnp.int32, (chunk, chunk), 0)
    ci = lax.broadcasted_iota(jnp.int32, (chunk, chunk), 1)
    nt = (((1,), (1,)), ((), ()))
    ones = jnp.ones((chunk, LANES), BF16)
    n_chunks = tb // chunk
    lmats, rhss, attns, qgs, kds, eglast = [], [], [], [], [], []
    for c in range(n_chunks):
        rs = slice(c * chunk, (c + 1) * chunk)
        q, k = q_ref[rs, :], k_ref[rs, :]
        qb, kbf = q.astype(BF16), k.astype(BF16)
        kkt = lax.dot_general(kbf, kbf, nt, preferred_element_type=F32)
        qkt = lax.dot_general(qb, kbf, nt, preferred_element_type=F32)
        for j in range(rep):
            hsel = lane == hq * rep + j
            beta = jnp.sum(jnp.where(hsel, beta_ref[rs, :], 0.0), axis=-1, keepdims=True)
            gcol = jnp.sum(jnp.where(hsel, gc_ref[rs, :], 0.0), axis=-1, keepdims=True)
            g0 = jnp.where(lane == 0, gcol, 0.0)
            grow = sum(lax.dot_general(ones, part, nt, preferred_element_type=F32) for part in _split3(g0))
            dec = jnp.exp(jnp.where(ri >= ci, gcol - grow, NEG_INF))
            lmats.append(jnp.where(ri > ci, beta * kkt * dec, 0.0))
            rhss.append(jnp.concatenate([v_ref[rs, j * C_DH:(j + 1) * C_DH] * beta, k * (beta * jnp.exp(gcol))],
                                        axis=1))
            attns.append((qkt * dec).astype(BF16))
            qgs.append((q * jnp.exp(gcol)).astype(BF16))
            glast = gcol[chunk - 1:chunk, :]
            kds.append(_transpose_rows(k * jnp.exp(glast - gcol)).astype(BF16))
            eglast.append(jnp.exp(glast))
    xs = _solve_unit_lower_many(lmats, rhss)
    ss = [s_scr[j] for j in range(rep)]
    for c in range(n_chunks):
        rs = slice(c * chunk, (c + 1) * chunk)
        for j in range(rep):
            n = c * rep + j
            sb = ss[j].astype(BF16)
            v_new = xs[n][:, :C_DH] - jnp.dot(xs[n][:, C_DH:].astype(BF16), sb, preferred_element_type=F32)
            vb = v_new.astype(BF16)
            o = jnp.dot(qgs[n], sb, preferred_element_type=F32) + jnp.dot(attns[n], vb, preferred_element_type=F32)
            ss[j] = ss[j] * eglast[n] + jnp.dot(kds[n], _pad_chunk_rows(vb), preferred_element_type=F32)
            o = _rms_rows(o, on_ref[...]) * _silu(z_ref[rs, j * C_DH:(j + 1) * C_DH])
            o_ref[rs, j * C_DH:(j + 1) * C_DH] = o.astype(BF16)
    for j in range(rep):
        s_scr[j] = ss[j]

    @pl.when(i == pl.num_programs(2) - 1)
    def _():
        so_ref[...] = s_scr[...]


def _gdn_chunks(q, k, v, beta, gc, pc, row0, n_seq, t, onorm, s0, *, tb, chunk):
    assert t % tb == 0 and row0 % tb == 0 and tb % chunk == 0
    nb = t // tb
    rb = row0 // tb
    rep = C_V_HEADS // C_QK_HEADS
    tok = lambda n, cb: pl.BlockSpec((tb, n), cb)
    st = pl.BlockSpec((None, rep, C_DH, C_DH), lambda s, h, i: (s, h, 0, 0))
    return pl.pallas_call(
        functools.partial(_gdn_chunk_kernel, chunk=chunk),
        grid=(n_seq, C_QK_HEADS, nb),
        in_specs=[
            tok(C_DH, lambda s, h, i: (s * nb + i, h)),
            tok(C_DH, lambda s, h, i: (s * nb + i, h)),
            tok(rep * C_DH, lambda s, h, i: (s * nb + i, h)),
            tok(LANES, lambda s, h, i: (s * nb + i, 0)),
            tok(LANES, lambda s, h, i: (s * nb + i, 0)),
            tok(rep * C_DH, lambda s, h, i: (rb + s * nb + i, PC_Z // (rep * C_DH) + h)),
            pl.BlockSpec((1, LANES), lambda s, h, i: (0, 0)),
            st,
        ],
        out_specs=[tok(rep * C_DH, lambda s, h, i: (s * nb + i, h)), st],
        out_shape=[jax.ShapeDtypeStruct((n_seq * t, C_V), BF16),
                   jax.ShapeDtypeStruct((n_seq, C_V_HEADS, C_DH, C_DH), F32)],
        scratch_shapes=[pltpu.VMEM((rep, C_DH, C_DH), F32)],
        compiler_params=_cparams(("parallel", "parallel", "arbitrary")),
        name="gdn_chunks",
    )(q, k, v, beta, gc, pc, onorm.reshape(1, LANES), s0)


def _gdn_group(pc, conv_state3, s0, row0, n_seq, t, conv_w, a_log, dt_bias, onorm, *, tm, tb):
    chunk = C_CHUNK if t % C_CHUNK == 0 else t
    conv_state = jnp.pad(conv_state3, ((0, 0), (HALO - (C_CONV - 1), 0), (0, 0)))
    q, k, v, beta, gc = _gdn_prep(pc, conv_state, row0, n_seq, t, conv_w, a_log, dt_bias, tm=tm, chunk=chunk)
    return _gdn_chunks(q, k, v, beta, gc, pc, row0, n_seq, t, onorm, s0, tb=tb, chunk=chunk)


PD_V = 2 * D_K
PD_GATE = PD_V + D_V
PD_GLR = PD_GATE + D_V
PD_COLS = PD_GLR + LANES


def _gla_kernel(q_ref, k_ref, v_ref, gate_ref, glr_ref, w2_ref, bgk_ref, on_ref, s0_ref, o_ref, so_ref, s_scr,
                *, chunk):
    i = pl.program_id(2)
    tb = q_ref.shape[0]

    @pl.when(i == 0)
    def _():
        s_scr[...] = s0_ref[...]

    pre = jnp.dot(glr_ref[...].astype(BF16), w2_ref[...], preferred_element_type=F32) + bgk_ref[...]
    bcum = _chunk_cumsum(-_softplus(-pre) * (1.0 / D_GATE_NORM), chunk)
    ri = lax.broadcasted_iota(jnp.int32, (chunk, 1), 0)
    ci = lax.broadcasted_iota(jnp.int32, (chunk, chunk), 1)
    intra, qgs, decays, upds = [], [], [], []
    for c in range(tb // chunk):
        rs = slice(c * chunk, (c + 1) * chunk)
        q = q_ref[rs, :] * (D_DK ** -0.5)
        k, bc = k_ref[rs, :], bcum[rs, :]
        vb = v_ref[rs, :].astype(BF16)
        attn = jnp.zeros((chunk, chunk), F32)
        for j in range(chunk):
            e = jnp.exp(jnp.where(ri >= j, bc - bc[j:j + 1, :], NEG_INF))
            col = jnp.sum(q * k[j:j + 1, :] * e, axis=-1, keepdims=True)
            attn = jnp.where(ci == j, col, attn)
        intra.append(jnp.dot(attn.astype(BF16), vb, preferred_element_type=F32))
        qgs.append((q * jnp.exp(bc)).astype(BF16))
        blast = bc[chunk - 1:chunk, :]
        ebt = jnp.broadcast_to(jnp.exp(blast), (LANES, LANES)).T
        decays.append(jnp.concatenate([ebt, ebt], axis=1))
        kdt = _transpose_rows(k * jnp.exp(blast - bc)).astype(BF16)
        upds.append(jnp.dot(kdt, _pad_chunk_rows(vb), preferred_element_type=F32))
    s = s_scr[...]
    for c in range(tb // chunk):
        rs = slice(c * chunk, (c + 1) * chunk)
        o = jnp.dot(qgs[c], s.astype(BF16), preferred_element_type=F32) + intra[c]
        s = s * decays[c] + upds[c]
        o = _rms_rows(o, on_ref[...]) * _silu(gate_ref[rs, :])
        o_ref[rs, :] = o.astype(BF16)
    s_scr[...] = s

    @pl.when(i == pl.num_programs(2) - 1)
    def _():
        so_ref[...] = s_scr[...]


def _gla_group(pd, s0, row0, n_seq, t, w_gk2, b_gk, onorm, *, tb):
    chunk = D_CHUNK if t % D_CHUNK == 0 else t
    assert t % tb == 0 and row0 % tb == 0 and tb % chunk == 0 and chunk & (chunk - 1) == 0
    nb = t // tb
    rb = row0 // tb
    w2 = _pad_rows(w_gk2, LANES).astype(BF16)
    tok = lambda n, cb: pl.BlockSpec((tb, n), cb)
    st = pl.BlockSpec((None, None, D_DK, D_DV), lambda s, h, i: (s, h, 0, 0))
    return pl.pallas_call(
        functools.partial(_gla_kernel, chunk=chunk),
        grid=(n_seq, D_HEADS, nb),
        in_specs=[
            tok(D_DK, lambda s, h, i: (rb + s * nb + i, h)),
            tok(D_DK, lambda s, h, i: (rb + s * nb + i, D_K // D_DK + h)),
            tok(D_DV, lambda s, h, i: (rb + s * nb + i, PD_V // D_DV + h)),
            tok(D_DV, lambda s, h, i: (rb + s * nb + i, PD_GATE // D_DV + h)),
            tok(LANES, lambda s, h, i: (rb + s * nb + i, PD_GLR // LANES)),
            pl.BlockSpec((LANES, D_DK), lambda s, h, i: (0, h)),
            pl.BlockSpec((1, D_DK), lambda s, h, i: (0, h)),
            pl.BlockSpec((1, D_DV), lambda s, h, i: (0, 0)),
            st,
        ],
        out_specs=[tok(D_DV, lambda s, h, i: (s * nb + i, h)), st],
        out_shape=[jax.ShapeDtypeStruct((n_seq * t, D_V), BF16),
                   jax.ShapeDtypeStruct((n_seq, D_HEADS, D_DK, D_DV), F32)],
        scratch_shapes=[pltpu.VMEM((D_DK, D_DV), F32)],
        compiler_params=_cparams(("parallel", "parallel", "arbitrary")),
        name="gla_chunks",
    )(pd, pd, pd, pd, pd, w2, b_gk.reshape(1, D_K), onorm.reshape(1, D_DV), s0)


TM = 768
TM_SMALL = 256


def _last_rows(a, row0, n_seq, t, k, n_cols):
    picks = [lax.slice(a, (row0 + t - k + j, 0), (row0 + n_seq * t, n_cols), (t, 1)) for j in range(k)]
    return jnp.stack(picks, axis=1)


def _cross_block(x, layer, n_p, t_p, n_s, t_s, mem_prompt, cache_mem_k, cache_mem_v, norm_mem_w, norm_cross_w,
                 x_w_q, x_w_k, x_w_v, x_w_o, x_q_norm, x_k_norm):
    d = x.shape[1]
    mp = n_p * t_p
    wkv = jnp.concatenate([x_w_k[layer], x_w_v[layer]], axis=1).astype(BF16)
    hw = jnp.concatenate([jnp.tile(x_k_norm[layer], X_HEADS), jnp.ones((X_W,), F32)]).reshape(1, 2 * X_W)
    mem = mem_prompt.reshape(n_p * N_MEM, d)
    kv = _norm_matmul(mem, norm_mem_w[layer], wkv, tm=min(512, n_p * N_MEM), tn=X_W, head_w=hw,
                      n_norm_cols=X_W, gsize=X_DH)
    mk_p = kv[:, :X_W].reshape(n_p, N_MEM, X_W)
    mv_p = kv[:, X_W:].reshape(n_p, N_MEM, X_W)
    qx = _norm_matmul(x, norm_cross_w[layer], x_w_q[layer].astype(BF16), tm=TM, tn=X_W)
    ca_p = _cross_attention(qx, 0, n_p, t_p, mk_p, mv_p, x_q_norm[layer], tq=min(512, t_p))
    ca_s = _cross_attention(qx, mp, n_s, t_s, cache_mem_k[layer].reshape(n_s, N_MEM, X_W),
                            cache_mem_v[layer].reshape(n_s, N_MEM, X_W), x_q_norm[layer], tq=t_s)
    ca = jnp.concatenate([ca_p, ca_s], axis=0)
    x = _matmul_residual([ca], [x_w_o[layer].astype(BF16)], x, tm=TM, tn=1024)
    return x, mk_p.reshape(n_p, N_MEM, X_HEADS, X_DH), mv_p.reshape(n_p, N_MEM, X_HEADS, X_DH)


def kernel(x_prompt, x_sample, cache_diff_k, cache_diff_v, state_rwkv, state_rwkv_shift, cache_mem_k, cache_mem_v, state_gdn, state_gdn_conv, state_gla, page_table, mem_prompt, norm_mix_w, norm_cross_w, norm_mem_w, norm_ffn_w, w_in_even, w_out_even, a_q_norm, a_k_norm, a_lam_q1, a_lam_k1, a_lam_q2, a_lam_k2, a_subln, b_mu, b_w0, b_w2, b_a0, b_a2, b_g2, b_k_k, b_k_a, b_r_k, b_lnx_w, b_lnx_b, ffd_w_gate, ffd_w_up, ffd_w_down, w_in_odd, w_out_odd, c_conv_w, c_a_log, c_dt_bias, c_onorm, d_w_gk2, d_b_gk, d_onorm, moe_w_router, moe_w_gate, moe_w_up, moe_w_down, x_w_q, x_w_k, x_w_v, x_w_o, x_q_norm, x_k_norm):
    n_p, t_p, d = x_prompt.shape
    n_s, t_s, _ = x_sample.shape
    mp, ms = n_p * t_p, n_s * t_s
    depth = norm_mix_w.shape[0]
    x = jnp.concatenate([x_prompt.reshape(mp, d), x_sample.reshape(ms, d)], axis=0)
    assert (mp + ms) % TM == 0 and (mp + ms) % TM_SMALL == 0
    page = cache_diff_k.shape[2]
    slopes = jnp.exp2(-(8.0 / A_HEADS) * jnp.arange(1, A_HEADS + 1, dtype=F32))
    cross_w = (mem_prompt, cache_mem_k, cache_mem_v, norm_mem_w, norm_cross_w, x_w_q, x_w_k, x_w_v, x_w_o,
               x_q_norm, x_k_norm)
    dk_p, dv_p, dk_s, dv_s, rw_p, rw_s, sh_p, sh_s = [], [], [], [], [], [], [], []
    mk_l, mv_l, gd_p, gd_s, cv_p, cv_s, gl_p, gl_s = [], [], [], [], [], [], [], []
    for layer in range(depth):
        i = layer // 2
        if layer % 2 == 0:
            lam_init = 0.8 - 0.6 * math.exp(-0.3 * layer)
            w_in = w_in_even[i]
            hw = jnp.concatenate([jnp.tile(a_q_norm[i], A_Q // A_DH), jnp.tile(a_k_norm[i], A_Q // A_DH),
                                  jnp.ones((A_COLS - 2 * A_Q,), F32)]).reshape(1, A_COLS)
            pa = _norm_matmul(x, norm_mix_w[layer], w_in[:, :A_COLS].astype(BF16), tm=TM, tn=1024, head_w=hw,
                              n_norm_cols=2 * A_Q, gsize=A_DH)
            pb = _norm_matmul(x, norm_mix_w[layer], _rwkv_cols(w_in[:, A_COLS:]).astype(BF16), tm=TM,
                              tn=PB_COLS // 2)
            lam_stack = jnp.stack([a_lam_q1[i], a_lam_k1[i], a_lam_q2[i], a_lam_k2[i]])
            oa_p = _diff_attention_prompt(pa, n_p, t_p, slopes, lam_stack, a_subln[i], lam_init, tq=min(256, t_p))
            oa_s = _diff_attention_sample(pa, mp, n_s, t_s, cache_diff_k, cache_diff_v, i, page_table, lam_stack,
                                          a_subln[i], lam_init)
            ob, st_p, st_s = _rwkv_mixer(pb, n_p, t_p, n_s, t_s, state_rwkv_shift[i], state_rwkv[i], b_mu[i],
                                         b_w0[i], b_w2[i], b_a0[i], b_a2[i], b_g2[i], b_k_k[i], b_k_a[i], b_r_k[i],
                                         b_lnx_w[i], b_lnx_b[i], tm=TM_SMALL, tb=min(256, t_p))
            dk_p.append(pa[:mp, A_Q:2 * A_Q].reshape(n_p, t_p, A_HEADS, 2 * A_DH))
            dv_p.append(pa[:mp, 2 * A_Q:].reshape(n_p, t_p, A_HEADS, 2 * A_DH))
            dk_s.append(pa[mp:, A_Q:2 * A_Q].reshape(n_s, t_s, A_HEADS, 2 * A_DH))
            dv_s.append(pa[mp:, 2 * A_Q:].reshape(n_s, t_s, A_HEADS, 2 * A_DH))
            rw_p.append(st_p)
            rw_s.append(st_s)
            sh_p.append(_rwkv_cols_inv(_last_rows(pb, 0, n_p, t_p, 1, PB_COLS)))
            sh_s.append(_rwkv_cols_inv(_last_rows(pb, mp, n_s, t_s, 1, PB_COLS)))
            oa = jnp.concatenate([oa_p, oa_s], axis=0)
            w_out = w_out_even[i].astype(BF16)
            x = _matmul_residual([oa, ob], [w_out[:A_Q], w_out[A_Q:]], x, tm=TM, tn=1024)
        else:
            w_in = w_in_odd[i]
            c_cols = C_CONV_CH + C_V + 2 * C_V_HEADS
            zc = jnp.zeros((d, LANES - C_V_HEADS), F32)
            wc = jnp.concatenate([w_in[:, :PC_BETA], w_in[:, PC_BETA:PC_BETA + C_V_HEADS], zc,
                                  w_in[:, PC_BETA + C_V_HEADS:c_cols], zc], axis=1).astype(BF16)
            wd_ = jnp.concatenate([w_in[:, c_cols:], jnp.zeros((d, LANES - D_GATE_LORA), F32)], axis=1).astype(BF16)
            pc = _norm_matmul(x, norm_mix_w[layer], wc, tm=TM, tn=PC_COLS // 2)
            pd = _norm_matmul(x, norm_mix_w[layer], wd_, tm=TM, tn=PD_COLS // 5)
            gdn_w = (c_conv_w[i], c_a_log[i], c_dt_bias[i], c_onorm[i])
            oc_p, gs_p = _gdn_group(pc, jnp.zeros((n_p, C_CONV - 1, C_CONV_CH), F32),
                                    jnp.zeros((n_p, C_V_HEADS, C_DH, C_DH), F32), 0, n_p, t_p, *gdn_w,
                                    tm=min(512, t_p), tb=min(512, t_p))
            oc_s, gs_s = _gdn_group(pc, state_gdn_conv[i], state_gdn[i], mp, n_s, t_s, *gdn_w, tm=t_s, tb=t_s)
            gla_w = (d_w_gk2[i], d_b_gk[i], d_onorm[i])
            od_p, ls_p = _gla_group(pd, jnp.zeros((n_p, D_HEADS, D_DK, D_DV), F32), 0, n_p, t_p, *gla_w,
                                    tb=min(256, t_p))
            od_s, ls_s = _gla_group(pd, state_gla[i], mp, n_s, t_s, *gla_w, tb=t_s)
            gd_p.append(gs_p)
            gd_s.append(gs_s)
            cv_p.append(_last_rows(pc, 0, n_p, t_p, C_CONV - 1, C_CONV_CH))
            cv_s.append(_last_rows(pc, mp, n_s, t_s, C_CONV - 1, C_CONV_CH))
            gl_p.append(ls_p)
            gl_s.append(ls_s)
            oc = jnp.concatenate([oc_p, oc_s], axis=0)
            od = jnp.concatenate([od_p, od_s], axis=0)
            w_out = w_out_odd[i].astype(BF16)
            x = _matmul_residual([oc, od], [w_out[:C_V], w_out[C_V:]], x, tm=TM, tn=1024)
        x, mk_p, mv_p = _cross_block(x, layer, n_p, t_p, n_s, t_s, *cross_w)
        mk_l.append(mk_p)
        mv_l.append(mv_p)
        if layer % 2 == 0:
            x = _swiglu_ffn(x, norm_ffn_w[layer], ffd_w_gate[i].astype(BF16), ffd_w_up[i].astype(BF16),
                            ffd_w_down[i].astype(BF16), tm=TM, tf=512)
        else:
            y_p, y_s = _moe_ffn(x, norm_ffn_w[layer], moe_w_router[i], moe_w_gate, moe_w_up, moe_w_down, i,
                                tm=TM, tf=256, tc=TM_SMALL, split=mp)
            if layer + 1 < depth:
                x = jnp.concatenate([y_p, y_s], axis=0)
    if depth % 2 == 1:
        y_p, y_s = x[:mp], x[mp:]
    st = jnp.stack
    return (y_p.reshape(n_p, t_p, d), y_s.reshape(n_s, t_s, d), st(dk_p), st(dv_p), st(dk_s), st(dv_s),
            st(rw_p), st(rw_s), st(sh_p), st(sh_s), st(mk_l), st(mv_l), st(gd_p), st(gd_s), st(cv_p), st(cv_s),
            st(gl_p), st(gl_s))
```

```python
import functools
import math

import jax
import jax.numpy as jnp
from jax import lax
from jax.experimental import pallas as pl
from jax.experimental.pallas import tpu as pltpu

F32 = jnp.float32
BF16 = jnp.bfloat16
NEG_INF = float("-inf")

D_MODEL = 2048
A_HEADS = 8
A_DH = 64
A_Q = 1024
A_COLS = 3072
B_HEADS = 16
B_DH = 64
B_W = 1024
B_DECAY_LORA = 96
B_AAA_LORA = 96
B_GATE_LORA = 256
B_COLS = 3520
B_GN_EPS = 64e-5
C_QK_HEADS = 4
C_V_HEADS = 8
C_DH = 128
C_CONV = 4
C_QK = 512
C_V = 1024
C_CONV_CH = 2048
C_CHUNK = 64
D_HEADS = 4
D_DK = 128
D_DV = 256
D_K = 512
D_V = 1024
D_GATE_LORA = 16
D_GATE_NORM = 16.0
D_CHUNK = 16
N_MEM = 256
X_HEADS = 4
X_DH = 128
X_W = 512
FF_DENSE = 5632
N_EXPERTS = 8
FF_EXPERT = 2816
EPS = 1e-6

LANES = 128
VMEM_LIMIT = 56 * 1024 * 1024


def _cparams(sem):
    return pltpu.CompilerParams(dimension_semantics=sem, vmem_limit_bytes=VMEM_LIMIT)


def _sigmoid(x):
    return 1.0 / (1.0 + jnp.exp(-x))


def _silu(x):
    return x * _sigmoid(x)


def _softplus(x):
    return jnp.maximum(x, 0.0) + jnp.log(1.0 + jnp.exp(-jnp.abs(x)))


def _rms_rows(x, w):
    ms = jnp.mean(x * x, axis=-1, keepdims=True)
    return x * lax.rsqrt(ms + EPS) * w


def _group_sum(x, gsize):
    lane = lax.broadcasted_iota(jnp.int32, (1, LANES), 1)
    outs = []
    for c in range(x.shape[1] // LANES):
        xc = x[:, c * LANES:(c + 1) * LANES]
        if gsize == LANES:
            ss = jnp.broadcast_to(jnp.sum(xc, axis=-1, keepdims=True), xc.shape)
        else:
            lo = jnp.sum(jnp.where(lane < 64, xc, 0.0), axis=-1, keepdims=True)
            hi = jnp.sum(jnp.where(lane >= 64, xc, 0.0), axis=-1, keepdims=True)
            ss = jnp.where(lane < 64, lo, hi)
        outs.append(ss)
    return outs[0] if len(outs) == 1 else jnp.concatenate(outs, axis=1)


def _group_norm_scale(x, gsize, inv_n, eps):
    return x * lax.rsqrt(_group_sum(x * x, gsize) * inv_n + eps)


def _split3(a):
    a1 = a.astype(BF16)
    r1 = a - a1.astype(F32)
    a2 = r1.astype(BF16)
    a3 = (r1 - a2.astype(F32)).astype(BF16)
    return a1, a2, a3


def _dot_f32(a, b):
    a1, a2, a3 = _split3(a)
    b1, b2, b3 = _split3(b)
    d = lambda x, y: jnp.dot(x, y, preferred_element_type=F32)
    return d(a1, b3) + d(a3, b1) + d(a2, b2) + d(a1, b2) + d(a2, b1) + d(a1, b1)


def _norm_mm_kernel(x_ref, nw_ref, w_ref, hw_ref, o_ref, h_scr, *, n_norm_tiles, gsize):
    j = pl.program_id(1)

    @pl.when(j == 0)
    def _():
        h_scr[...] = _rms_rows(x_ref[...], nw_ref[...]).astype(BF16)

    acc = jnp.dot(h_scr[...], w_ref[...], preferred_element_type=F32)
    if n_norm_tiles == 0:
        o_ref[...] = acc
    else:
        @pl.when(j < n_norm_tiles)
        def _():
            o_ref[...] = _group_norm_scale(acc, gsize, 1.0 / gsize, EPS) * hw_ref[...]

        @pl.when(j >= n_norm_tiles)
        def _():
            o_ref[...] = acc


def _norm_matmul(x, nw, w, *, tm, tn, head_w=None, n_norm_cols=0, gsize=LANES):
    m, k = x.shape
    n = w.shape[1]
    assert m % tm == 0 and n % tn == 0 and n_norm_cols % tn == 0
    if head_w is None:
        head_w = jnp.ones((1, n), F32)
    kern = functools.partial(_norm_mm_kernel, n_norm_tiles=n_norm_cols // tn, gsize=gsize)
    return pl.pallas_call(
        kern,
        grid=(m // tm, n // tn),
        in_specs=[
            pl.BlockSpec((tm, k), lambda i, j: (i, 0)),
            pl.BlockSpec((1, k), lambda i, j: (0, 0)),
            pl.BlockSpec((k, tn), lambda i, j: (0, j)),
            pl.BlockSpec((1, tn), lambda i, j: (0, j)),
        ],
        out_specs=pl.BlockSpec((tm, tn), lambda i, j: (i, j)),
        out_shape=jax.ShapeDtypeStruct((m, n), F32),
        scratch_shapes=[pltpu.VMEM((tm, k), BF16)],
        compiler_params=_cparams(("parallel", "arbitrary")),
        name="norm_matmul",
    )(x, nw.reshape(1, k), w, head_w)


def _mm_res_kernel(*refs, n_in):
    x_ref = refs[2 * n_in]
    o_ref = refs[2 * n_in + 1]
    acc = x_ref[...]
    for t in range(n_in):
        acc = acc + jnp.dot(refs[t][...], refs[n_in + t][...], preferred_element_type=F32)
    o_ref[...] = acc


def _matmul_residual(acts, weights, x, *, tm, tn):
    m, n = x.shape
    n_in = len(acts)
    in_specs = [pl.BlockSpec((tm, a.shape[1]), lambda i, j: (i, 0)) for a in acts]
    in_specs += [pl.BlockSpec((w.shape[0], tn), lambda i, j: (0, j)) for w in weights]
    in_specs += [pl.BlockSpec((tm, tn), lambda i, j: (i, j))]
    return pl.pallas_call(
        functools.partial(_mm_res_kernel, n_in=n_in),
        grid=(m // tm, n // tn),
        in_specs=in_specs,
        out_specs=pl.BlockSpec((tm, tn), lambda i, j: (i, j)),
        out_shape=jax.ShapeDtypeStruct((m, n), F32),
        compiler_params=_cparams(("parallel", "arbitrary")),
        name="matmul_residual",
    )(*acts, *weights, x)


def _swiglu_kernel(x_ref, nw_ref, wg_ref, wu_ref, wd_ref, o_ref, h_scr):
    f = pl.program_id(1)

    @pl.when(f == 0)
    def _():
        x = x_ref[...]
        h_scr[...] = _rms_rows(x, nw_ref[...]).astype(BF16)
        o_ref[...] = x

    h = h_scr[...]
    g = jnp.dot(h, wg_ref[...], preferred_element_type=F32)
    u = jnp.dot(h, wu_ref[...], preferred_element_type=F32)
    a = (_silu(g) * u).astype(BF16)
    o_ref[...] += jnp.dot(a, wd_ref[...], preferred_element_type=F32)


def _swiglu_ffn(x, nw, wg, wu, wd, *, tm, tf):
    m, d = x.shape
    ff = wg.shape[1]
    return pl.pallas_call(
        _swiglu_kernel,
        grid=(m // tm, ff // tf),
        in_specs=[
            pl.BlockSpec((tm, d), lambda i, f: (i, 0)),
            pl.BlockSpec((1, d), lambda i, f: (0, 0)),
            pl.BlockSpec((d, tf), lambda i, f: (0, f)),
            pl.BlockSpec((d, tf), lambda i, f: (0, f)),
            pl.BlockSpec((tf, d), lambda i, f: (f, 0)),
        ],
        out_specs=pl.BlockSpec((tm, d), lambda i, f: (i, 0)),
        out_shape=jax.ShapeDtypeStruct((m, d), F32),
        scratch_shapes=[pltpu.VMEM((tm, d), BF16)],
        compiler_params=_cparams(("parallel", "arbitrary")),
        name="swiglu_ffn",
    )(x, nw.reshape(1, d), wg, wu, wd)


MOE_TILE = 768
ROW_SLABS = D_MODEL // LANES


def _rows_to_slabs(ref, x):
    ref[...] = jnp.swapaxes(jnp.stack([x[:, j * LANES:(j + 1) * LANES] for j in range(ROW_SLABS)], axis=0), 0, 1)


def _slabs_to_rows(ref):
    xt = jnp.swapaxes(ref[...], 0, 1)
    return jnp.concatenate([xt[j] for j in range(ROW_SLABS)], axis=1)


def _moe_router_kernel(x_ref, nw_ref, wr_ref, h_ref, info_ref):
    tm = x_ref.shape[0]
    lane = lax.broadcasted_iota(jnp.int32, (tm, LANES), 1)
    hf = _rms_rows(x_ref[...], nw_ref[...])
    _rows_to_slabs(h_ref, hf)
    logits = jnp.where(lane < N_EXPERTS, _dot_f32(hf, wr_ref[...]), NEG_INF)
    m1 = jnp.max(logits, axis=-1, keepdims=True)
    i1 = jnp.min(jnp.where(logits == m1, lane, LANES), axis=-1, keepdims=True)
    rest = jnp.where(lane == i1, NEG_INF, logits)
    m2 = jnp.max(rest, axis=-1, keepdims=True)
    i2 = jnp.min(jnp.where(rest == m2, lane, LANES), axis=-1, keepdims=True)
    e2 = jnp.exp(m2 - m1)
    vals = [i1.astype(F32), i2.astype(F32), 1.0 / (1.0 + e2), e2 / (1.0 + e2)]
    info = jnp.zeros((tm, LANES), F32)
    for k, val in enumerate(vals):
        info = jnp.where(lane == k, val, info)
    info_ref[...] = info


def _moe_gather_kernel(src_ref, nu_ref, h_ref, xs_ref, sem):
    t = pl.program_id(0)
    tg = xs_ref.shape[0]

    def copy(r):
        return pltpu.make_async_copy(h_ref.at[src_ref[t * tg + r]], xs_ref.at[r], sem.at[0])

    def issue(r, carry):
        copy(r).start()
        return carry

    def drain(r, carry):
        copy(r).wait()
        return carry

    @pl.when(t < nu_ref[0])
    def _():
        lax.fori_loop(0, tg, issue, 0, unroll=8)
        lax.fori_loop(0, tg, drain, 0, unroll=8)

    @pl.when(t >= nu_ref[0])
    def _():
        xs_ref[...] = jnp.zeros(xs_ref.shape, F32)


def _moe_expert_kernel(te_ref, nu_ref, xs_ref, wg_ref, wu_ref, wd_ref, y_ref, xb_scr, acc_scr):
    del te_ref
    t = pl.program_id(0)
    f = pl.program_id(1)

    @pl.when(f == 0)
    def _():
        xb_scr[...] = _slabs_to_rows(xs_ref).astype(BF16)
        acc_scr[...] = jnp.zeros(acc_scr.shape, F32)

    @pl.when(t < nu_ref[0])
    def _():
        xb = xb_scr[...]
        g = jnp.dot(xb, wg_ref[...].astype(BF16), preferred_element_type=F32)
        u = jnp.dot(xb, wu_ref[...].astype(BF16), preferred_element_type=F32)
        acc_scr[...] += jnp.dot((_silu(g) * u).astype(BF16), wd_ref[...].astype(BF16),
                                preferred_element_type=F32)

    @pl.when(f == pl.num_programs(1) - 1)
    def _():
        _rows_to_slabs(y_ref, acc_scr[...])


def _moe_combine_kernel(dest_ref, x_ref, info_ref, y_ref, o1_ref, o2_ref, ybuf, sem, *, n_first):
    i = pl.program_id(0)
    tc = x_ref.shape[0]
    n = dest_ref.shape[0] // 2

    def copy(r, slot):
        return pltpu.make_async_copy(y_ref.at[dest_ref[slot * n + i * tc + r]], ybuf.at[slot, r], sem.at[slot])

    def issue(r, carry):
        copy(r, 0).start()
        copy(r, 1).start()
        return carry

    def drain(r, carry):
        copy(r, 0).wait()
        copy(r, 1).wait()
        return carry

    lax.fori_loop(0, tc, issue, 0, unroll=8)
    lax.fori_loop(0, tc, drain, 0, unroll=8)
    info = info_ref[...]
    y1 = _slabs_to_rows(ybuf.at[0])
    y2 = _slabs_to_rows(ybuf.at[1])
    res = x_ref[...] + info[:, 2:3] * y1 + info[:, 3:4] * y2

    @pl.when(i < n_first)
    def _():
        o1_ref[...] = res

    @pl.when(i >= n_first)
    def _():
        o2_ref[...] = res


def _moe_ffn(x, nw, w_router, wg, wu, wd, w_layer, *, tm, tf, tc, split):
    m, d = x.shape
    _, ne, _, ff = wg.shape
    tg = MOE_TILE
    assert d == D_MODEL and m % tm == 0 and m % tc == 0 and ff % tf == 0
    n_tiles = -(-(2 * m + ne * (tg - 1)) // tg)
    p_rows = n_tiles * tg
    wr = jnp.zeros((d, LANES), F32).at[:, :ne].set(w_router)
    h3, info = pl.pallas_call(
        _moe_router_kernel,
        grid=(m // tm,),
        in_specs=[pl.BlockSpec((tm, d), lambda i: (i, 0)), pl.BlockSpec((1, d), lambda i: (0, 0)),
                  pl.BlockSpec((d, LANES), lambda i: (0, 0))],
        out_specs=[pl.BlockSpec((tm, ROW_SLABS, LANES), lambda i: (i, 0, 0)),
                   pl.BlockSpec((tm, LANES), lambda i: (i, 0))],
        out_shape=[jax.ShapeDtypeStruct((m, ROW_SLABS, LANES), F32), jax.ShapeDtypeStruct((m, LANES), F32)],
        compiler_params=_cparams(("parallel",)),
        name="moe_router",
    )(x, nw.reshape(1, d), wr)

    e12 = info[:, :2].astype(jnp.int32)
    onehot = (e12[:, :1] == jnp.arange(ne)) | (e12[:, 1:2] == jnp.arange(ne))
    csum = jnp.cumsum(onehot.astype(jnp.int32), axis=0)
    rank = csum - onehot
    padded = (csum[-1] + tg - 1) // tg * tg
    gend = jnp.cumsum(padded)
    base = (gend - padded)[None, :] + rank
    dest = jnp.concatenate([jnp.take_along_axis(base, e12[:, :1], axis=1)[:, 0],
                            jnp.take_along_axis(base, e12[:, 1:2], axis=1)[:, 0]]).astype(jnp.int32)
    tile_expert = jnp.minimum(jnp.sum(jnp.arange(n_tiles)[:, None] * tg >= gend[None, :], axis=1),
                              ne - 1).astype(jnp.int32)
    n_used = (gend[-1:] // tg).astype(jnp.int32)

    any_spec = pl.BlockSpec(memory_space=pl.ANY)
    tok_ids = jnp.tile(jnp.arange(m, dtype=jnp.int32), 2)
    src = jnp.zeros((p_rows,), jnp.int32).at[dest].set(tok_ids, unique_indices=True, mode="promise_in_bounds")
    xs3 = pl.pallas_call(
        _moe_gather_kernel,
        grid_spec=pltpu.PrefetchScalarGridSpec(
            num_scalar_prefetch=2, grid=(n_tiles,), in_specs=[any_spec],
            out_specs=pl.BlockSpec((tg, ROW_SLABS, LANES), lambda t, sr, nu: (t, 0, 0)),
            scratch_shapes=[pltpu.SemaphoreType.DMA((1,))]),
        out_shape=jax.ShapeDtypeStruct((p_rows, ROW_SLABS, LANES), F32),
        compiler_params=_cparams(("arbitrary",)),
        name="moe_gather",
    )(src, n_used, h3)

    last = lambda t, nu: jnp.minimum(t, nu[0] - 1)
    y3 = pl.pallas_call(
        _moe_expert_kernel,
        grid_spec=pltpu.PrefetchScalarGridSpec(
            num_scalar_prefetch=2,
            grid=(n_tiles, ff // tf),
            in_specs=[
                pl.BlockSpec((tg, ROW_SLABS, LANES), lambda t, f, te, nu: (last(t, nu), 0, 0)),
                pl.BlockSpec((None, None, d, tf), lambda t, f, te, nu: (w_layer, te[last(t, nu)], 0, f)),
                pl.BlockSpec((None, None, d, tf), lambda t, f, te, nu: (w_layer, te[last(t, nu)], 0, f)),
                pl.BlockSpec((None, None, tf, d), lambda t, f, te, nu: (w_layer, te[last(t, nu)], f, 0)),
            ],
            out_specs=pl.BlockSpec((tg, ROW_SLABS, LANES), lambda t, f, te, nu: (t, 0, 0)),
            scratch_shapes=[pltpu.VMEM((tg, d), BF16), pltpu.VMEM((tg, d), F32)]),
        out_shape=jax.ShapeDtypeStruct((p_rows, ROW_SLABS, LANES), F32),
        compiler_params=_cparams(("arbitrary", "arbitrary")),
        name="moe_experts",
    )(tile_expert, n_used, xs3, wg, wu, wd)

    assert split % tc == 0 and 0 < split < m
    n_first = split // tc
    return pl.pallas_call(
        functools.partial(_moe_combine_kernel, n_first=n_first),
        grid_spec=pltpu.PrefetchScalarGridSpec(
            num_scalar_prefetch=1,
            grid=(m // tc,),
            in_specs=[pl.BlockSpec((tc, d), lambda i, de: (i, 0)), pl.BlockSpec((tc, LANES), lambda i, de: (i, 0)),
                      any_spec],
            out_specs=[pl.BlockSpec((tc, d), lambda i, de: (jnp.minimum(i, n_first - 1), 0)),
                       pl.BlockSpec((tc, d), lambda i, de: (jnp.maximum(i - n_first, 0), 0))],
            scratch_shapes=[pltpu.VMEM((2, tc, ROW_SLABS, LANES), F32), pltpu.SemaphoreType.DMA((2,))]),
        out_shape=[jax.ShapeDtypeStruct((split, d), F32), jax.ShapeDtypeStruct((m - split, d), F32)],
        compiler_params=_cparams(("arbitrary",)),
        name="moe_combine",
    )(dest, x, info, y3)


def _diff_lambda_vec(lam_ref, lam_init):
    lv = lam_ref[...]
    d1 = jnp.sum(lv[0:1, :] * lv[1:2, :], axis=-1, keepdims=True)
    d2 = jnp.sum(lv[2:3, :] * lv[3:4, :], axis=-1, keepdims=True)
    return jnp.exp(d1) - jnp.exp(d2) + lam_init


def _softmax_rows(s):
    m = jnp.max(s, axis=-1, keepdims=True)
    e = jnp.exp(s - m)
    return e, jnp.sum(e, axis=-1, keepdims=True)


def _diffattn_prompt_kernel(slope_ref, q_ref, k_ref, v_ref, lam_ref, subln_ref, o_ref, *, tq, lam_init):
    t = q_ref.shape[0]
    h = pl.program_id(1)
    slope = slope_ref[h]
    lane = lax.broadcasted_iota(jnp.int32, (1, LANES), 1)
    lam = _diff_lambda_vec(lam_ref, lam_init)
    kb = k_ref[...].astype(BF16)
    vb = v_ref[...].astype(BF16)
    nt = (((1,), (1,)), ((), ()))
    n_blocks = t // tq
    row = lax.broadcasted_iota(jnp.int32, (tq, t), 0) + (t - tq)
    col = lax.broadcasted_iota(jnp.int32, (tq, t), 1)
    dist = (row - col).astype(F32)
    bias_all = jnp.where(dist >= 0.0, -slope * dist, NEG_INF)
    for qi in range(n_blocks):
        n_keys = (qi + 1) * tq
        q = q_ref[qi * tq:(qi + 1) * tq, :] * (A_DH ** -0.5)
        q1 = jnp.where(lane < A_DH, q, 0.0).astype(BF16)
        q2 = jnp.where(lane >= A_DH, q, 0.0).astype(BF16)
        kk = kb[:n_keys]
        vv = vb[:n_keys]
        off = (n_blocks - 1 - qi) * tq
        bias = bias_all[:, off:off + n_keys]
        e1, l1 = _softmax_rows(lax.dot_general(q1, kk, nt, preferred_element_type=F32) + bias)
        e2, l2 = _softmax_rows(lax.dot_general(q2, kk, nt, preferred_element_type=F32) + bias)
        o = jnp.dot(e1.astype(BF16), vv, preferred_element_type=F32) * (1.0 / l1)
        o = o - jnp.dot(e2.astype(BF16), vv, preferred_element_type=F32) * (lam / l2)
        o = _rms_rows(o, subln_ref[...]) * (1.0 - lam_init)
        o_ref[qi * tq:(qi + 1) * tq, :] = o.astype(BF16)


def _diff_attention_prompt(pa, n_batch, t, slopes, lam_stack, subln, lam_init, *, tq):
    kern = functools.partial(_diffattn_prompt_kernel, tq=tq, lam_init=lam_init)
    return pl.pallas_call(
        kern,
        grid=(n_batch, A_HEADS),
        in_specs=[
            pl.BlockSpec(memory_space=pltpu.SMEM),
            pl.BlockSpec((t, LANES), lambda b, h: (b, h)),
            pl.BlockSpec((t, LANES), lambda b, h: (b, A_HEADS + h)),
            pl.BlockSpec((t, LANES), lambda b, h: (b, 2 * A_HEADS + h)),
            pl.BlockSpec((4, A_DH), lambda b, h: (0, 0)),
            pl.BlockSpec((1, LANES), lambda b, h: (0, 0)),
        ],
        out_specs=pl.BlockSpec((t, LANES), lambda b, h: (b, h)),
        out_shape=jax.ShapeDtypeStruct((n_batch * t, A_HEADS * LANES), BF16),
        compiler_params=_cparams(("parallel", "parallel")),
        name="diff_attention_prompt",
    )(slopes, pa, pa, pa, lam_stack, subln.reshape(1, LANES))


PAGES_PER_STEP = 16


def _diffattn_sample_kernel(pt_ref, q_ref, kn_ref, vn_ref, *rest, past, lam_init):
    del pt_ref
    npg = PAGES_PER_STEP
    k_refs, v_refs = rest[:npg], rest[npg:2 * npg]
    lam_ref, subln_ref, o_ref, qq_scr, m_scr, l_scr, acc_scr, bias_scr = rest[2 * npg:]
    g = pl.program_id(1)
    ts = q_ref.shape[0]
    page = k_refs[0].shape[0]
    lane = lax.broadcasted_iota(jnp.int32, (1, LANES), 1)
    hsub = lax.broadcasted_iota(jnp.int32, (A_HEADS, LANES), 0)
    col = lax.broadcasted_iota(jnp.int32, (A_HEADS, LANES), 1)
    valid = (col // (2 * ts)) == hsub
    pos_q = past + jnp.bitwise_and(col, ts - 1)
    slope = jnp.exp2(-(8.0 / A_HEADS) * (hsub + 1).astype(F32))
    nt = (((1,), (1,)), ((), ()))
    tn = (((0,), (0,)), ((), ()))

    @pl.when(g == 0)
    def _():
        rows = []
        for h in range(A_HEADS):
            q = q_ref[:, h * LANES:(h + 1) * LANES] * (A_DH ** -0.5)
            rows += [jnp.where(lane < A_DH, q, 0.0), jnp.where(lane >= A_DH, q, 0.0)]
        qq_scr[...] = jnp.concatenate(rows, axis=0).astype(BF16)
        m_scr[...] = jnp.where(valid, NEG_INF, 0.0)
        l_scr[...] = jnp.zeros(l_scr.shape, F32)
        acc_scr[...] = jnp.zeros(acc_scr.shape, F32)

    def key_bias(nk, key0, causal):
        kidx = lax.broadcasted_iota(jnp.int32, (nk * A_HEADS, LANES), 0) // A_HEADS
        dist = (pos_q[None] - (key0 + kidx).reshape(nk, A_HEADS, LANES)).astype(F32)
        keep = valid[None] & (dist >= 0.0) if causal else valid[None]
        return jnp.where(keep, -slope[None] * dist, NEG_INF)

    @pl.when(g == 0)
    def _():
        bias_scr[...] = key_bias(page, 0, False)

    def scores(k3):
        nk = k3.shape[0]
        k2 = k3.reshape(nk * A_HEADS, LANES).astype(BF16)
        return lax.dot_general(k2, qq_scr[...], nt, preferred_element_type=F32).reshape(nk, A_HEADS, LANES)

    def absorb(ss, v3s):
        m_old = m_scr[...]
        m_new = m_old
        for s in ss:
            m_new = jnp.maximum(m_new, jnp.max(s, axis=0))
        alpha = jnp.exp(m_old - m_new)
        l_new = alpha * l_scr[...]
        acc = jnp.sum(jnp.where(valid, alpha, 0.0), axis=0, keepdims=True) * acc_scr[...]
        for s, v3 in zip(ss, v3s):
            nk = s.shape[0]
            pe = jnp.exp(s - m_new[None])
            l_new = l_new + jnp.sum(pe, axis=0)
            v2 = v3.reshape(nk * A_HEADS, LANES).astype(BF16)
            acc = acc + lax.dot_general(v2, pe.reshape(nk * A_HEADS, LANES).astype(BF16), tn,
                                        preferred_element_type=F32)
        m_scr[...] = m_new
        l_scr[...] = l_new
        acc_scr[...] = acc

    absorb([scores(k_refs[j][...]) + bias_scr[...] + (slope * ((g * npg + j) * page).astype(F32))[None]
            for j in range(npg)], [v_refs[j][...] for j in range(npg)])

    @pl.when(g == pl.num_programs(1) - 1)
    def _():
        absorb([scores(kn_ref[...]) + key_bias(ts, past, True)], [vn_ref[...]])
        lam = _diff_lambda_vec(lam_ref, lam_init)
        l_row = jnp.sum(jnp.where(valid, l_scr[...], 0.0), axis=0, keepdims=True)
        ot = acc_scr[...] / l_row
        ot = ot - lam * pltpu.roll(ot, LANES - ts, 1)
        o = ot.T
        for h in range(A_HEADS):
            oh = _rms_rows(o[h * 2 * ts:h * 2 * ts + ts, :], subln_ref[...]) * (1.0 - lam_init)
            o_ref[:, h * LANES:(h + 1) * LANES] = oh.astype(BF16)


def _diff_attention_sample(pa, row0, n_batch, ts, cache_k, cache_v, layer_idx, page_table, lam_stack, subln,
                           lam_init):
    n_pages = page_table.shape[1]
    page = cache_k.shape[2]
    npg = PAGES_PER_STEP
    assert row0 % ts == 0 and n_pages % npg == 0 and 2 * ts * A_HEADS == LANES
    rb = row0 // ts
    w = A_HEADS * LANES
    new_rows = lambda c0: pa[row0:, c0:c0 + w].reshape(n_batch, ts, A_HEADS, LANES)
    kern = functools.partial(_diffattn_sample_kernel, past=n_pages * page, lam_init=lam_init)
    page_spec = lambda j: pl.BlockSpec((None, None, page, A_HEADS, LANES),
                                       lambda b, g, pt: (layer_idx, pt[b, g * npg + j], 0, 0, 0))
    new_spec = pl.BlockSpec((None, ts, A_HEADS, LANES), lambda b, g, pt: (b, 0, 0, 0))
    grid_spec = pltpu.PrefetchScalarGridSpec(
        num_scalar_prefetch=1,
        grid=(n_batch, n_pages // npg),
        in_specs=[pl.BlockSpec((ts, w), lambda b, g, pt: (rb + b, 0)), new_spec, new_spec]
        + [page_spec(j) for j in range(npg)] * 2
        + [pl.BlockSpec((4, A_DH), lambda b, g, pt: (0, 0)), pl.BlockSpec((1, LANES), lambda b, g, pt: (0, 0))],
        out_specs=pl.BlockSpec((ts, w), lambda b, g, pt: (b, 0)),
        scratch_shapes=[
            pltpu.VMEM((LANES, LANES), BF16),
            pltpu.VMEM((A_HEADS, LANES), F32),
            pltpu.VMEM((A_HEADS, LANES), F32),
            pltpu.VMEM((LANES, LANES), F32),
            pltpu.VMEM((page, A_HEADS, LANES), F32),
        ],
    )
    return pl.pallas_call(
        kern,
        grid_spec=grid_spec,
        out_shape=jax.ShapeDtypeStruct((n_batch * ts, w), BF16),
        compiler_params=_cparams(("parallel", "arbitrary")),
        name="diff_attention_sample",
    )(page_table, pa, new_rows(w), new_rows(2 * w), *([cache_k] * npg), *([cache_v] * npg), lam_stack,
      subln.reshape(1, LANES))


def _cross_attn_kernel(q_ref, k_ref, v_ref, qn_ref, o_ref):
    nt = (((1,), (1,)), ((), ()))
    if len(k_ref.shape) == 3:
        kt = jnp.swapaxes(k_ref[...], 0, 1).astype(BF16)
        vt = jnp.swapaxes(v_ref[...], 0, 1).astype(BF16)
        heads = [(kt[h], vt[h]) for h in range(X_HEADS)]
    else:
        heads = [(k_ref[:, h * LANES:(h + 1) * LANES].astype(BF16), v_ref[:, h * LANES:(h + 1) * LANES].astype(BF16))
                 for h in range(X_HEADS)]
    for h, (kh, vh) in enumerate(heads):
        sl = slice(h * LANES, (h + 1) * LANES)
        q = _rms_rows(q_ref[:, sl], qn_ref[...]) * (X_DH ** -0.5)
        s = lax.dot_general(q.astype(BF16), kh, nt, preferred_element_type=F32)
        e, l = _softmax_rows(s)
        p = (e * (1.0 / l)).astype(BF16)
        o_ref[:, sl] = jnp.dot(p, vh, preferred_element_type=F32).astype(BF16)


def _cross_attention(q, row0, n_batch, t, mk, mv, q_norm, *, tq, mem_layer=None):
    assert t % tq == 0 and row0 % tq == 0
    nq = t // tq
    rb = row0 // tq
    if mem_layer is None:
        mem_spec = pl.BlockSpec((None, N_MEM, X_W), lambda b, i: (b, 0, 0))
    else:
        mem_spec = pl.BlockSpec((None, None, N_MEM, X_HEADS, X_DH), lambda b, i: (mem_layer, b, 0, 0, 0))
    return pl.pallas_call(
        _cross_attn_kernel,
        grid=(n_batch, nq),
        in_specs=[
            pl.BlockSpec((tq, X_W), lambda b, i: (rb + b * nq + i, 0)),
            mem_spec,
            mem_spec,
            pl.BlockSpec((1, LANES), lambda b, i: (0, 0)),
        ],
        out_specs=pl.BlockSpec((tq, X_W), lambda b, i: (b * nq + i, 0)),
        out_shape=jax.ShapeDtypeStruct((n_batch * t, X_W), BF16),
        compiler_params=_cparams(("parallel", "parallel")),
        name="cross_attention",
    )(q, mk, mv, q_norm.reshape(1, LANES))


PB_LG = 3 * B_W
PB_LW = PB_LG + B_GATE_LORA
PB_LA = PB_LW + LANES
PB_COLS = PB_LA + LANES


def _swap_halves(x):
    tiles = [pltpu.roll(x[:, c * LANES:(c + 1) * LANES], LANES // 2, 1) for c in range(x.shape[1] // LANES)]
    return tiles[0] if len(tiles) == 1 else jnp.concatenate(tiles, axis=1)


def _rwkv_prep_kernel(pb_ref, halo_ref, start_ref, mu_ref, w0_ref, w2_ref, a0_ref, a2_ref, g2_ref, kk_ref, ka_ref,
                      r_ref, kf_ref, v_ref, g_ref, rt_ref, at_ref, bt_ref, kt_ref, bh_ref, kh_ref, pc_ref,
                      *, n_prompt_tiles, chunk_p, chunk_s, t_p, t_s):
    tm = pb_ref.shape[0]
    i = pl.program_id(0)
    pb = pb_ref[...]
    row = lax.broadcasted_iota(jnp.int32, (tm, 1), 0)
    prev = jnp.where(row == 0, halo_ref[HALO - 1:HALO, :], pltpu.roll(pb, 1, 0))
    seq_len = jnp.where(i < n_prompt_tiles, t_p, t_s)
    is_start = jnp.bitwise_and(i * tm + row, seq_len - 1) == 0
    ps = jnp.where(is_start, start_ref[...], prev)
    xs = pb + (ps - pb) * mu_ref[...]
    r = xs[:, :B_W]
    k = xs[:, B_W:2 * B_W]
    lg = xs[:, PB_LG:PB_LW]
    lw_in = xs[:, PB_LW:PB_LA]
    la = xs[:, PB_LA:PB_COLS]
    wl = w0_ref[...] + jnp.dot(jnp.tanh(lw_in).astype(BF16), w2_ref[...], preferred_element_type=F32)
    lw = -jnp.exp(-_softplus(-wl) - 0.5)
    a = _sigmoid(a0_ref[...] + jnp.dot(la.astype(BF16), a2_ref[...], preferred_element_type=F32))
    kk = _group_norm_scale(k * kk_ref[...], B_DH, 1.0, EPS)
    kf = k * (1.0 + (a - 1.0) * ka_ref[...])
    b = kk * a
    r_ref[...] = r
    kf_ref[...] = kf
    v_ref[...] = xs[:, 2 * B_W:3 * B_W]
    g_ref[...] = jnp.dot(_sigmoid(lg).astype(BF16), g2_ref[...], preferred_element_type=F32)

    chunk = jnp.where(pl.program_id(0) < n_prompt_tiles, chunk_p, chunk_s)
    rowc = jnp.bitwise_and(lax.broadcasted_iota(jnp.int32, (tm, B_W), 0), chunk - 1)
    lp = lw
    sfx = lw
    step = 1
    while step < max(chunk_p, chunk_s):
        lp = lp + jnp.where(rowc >= step, pltpu.roll(lp, step, 0), 0.0)
        sfx = sfx + jnp.where(rowc + step < chunk, pltpu.roll(sfx, tm - step, 0), 0.0)
        step *= 2
    sfx = sfx - lw
    e_neg = jnp.exp(-lp)
    e_sfx = jnp.exp(sfx)
    sw = lambda z: _swap_halves(z).astype(BF16)
    rt_ref[...] = sw(r * jnp.exp(lp))
    at_ref[...] = sw(kk * jnp.exp(lp - lw))
    bt_ref[...] = sw(b * e_neg)
    kt_ref[...] = sw(kf * e_neg)
    bh_ref[...] = sw(b * e_sfx)
    kh_ref[...] = sw(kf * e_sfx)
    pc_ref[...] = _swap_halves(jnp.exp(lp + sfx))


def _rwkv_prep(pb, starts, mu, w0, w2, a0, a2, g2, k_k, k_a, *, tm, n_prompt_tiles, chunk_p, chunk_s, t_p, t_s):
    m = pb.shape[0]
    assert tm % chunk_p == 0 and tm % chunk_s == 0 and tm % HALO == 0
    assert t_p & (t_p - 1) == 0 and t_s & (t_s - 1) == 0 and (n_prompt_tiles * tm) % t_s == 0
    npt = n_prompt_tiles
    row = lambda n: pl.BlockSpec((1, n), lambda i: (0, 0))
    full = lambda a: pl.BlockSpec(a.shape, lambda i: (0, 0))
    tok = lambda n: pl.BlockSpec((tm, n), lambda i: (i, 0))
    f32o = jax.ShapeDtypeStruct((m, B_W), F32)
    b16o = jax.ShapeDtypeStruct((m, B_W), BF16)
    kern = functools.partial(_rwkv_prep_kernel, n_prompt_tiles=npt, chunk_p=chunk_p, chunk_s=chunk_s, t_p=t_p,
                             t_s=t_s)
    return pl.pallas_call(
        kern,
        grid=(m // tm,),
        in_specs=[tok(PB_COLS),
                  pl.BlockSpec((HALO, PB_COLS), lambda i: (jnp.maximum(i * (tm // HALO) - 1, 0), 0)),
                  pl.BlockSpec((tm, PB_COLS), lambda i: (jnp.maximum(i - npt + 1, 0), 0)),
                  row(PB_COLS), row(B_W), full(w2), row(B_W), full(a2), full(g2), row(B_W), row(B_W)],
        out_specs=[tok(B_W)] * 11,
        out_shape=[f32o] * 4 + [b16o] * 6 + [f32o],
        compiler_params=_cparams(("parallel",)),
        name="rwkv_prep",
    )(pb, pb, starts, mu, w0.reshape(1, B_W), w2, a0.reshape(1, B_W), a2, g2, k_k.reshape(1, B_W),
      k_a.reshape(1, B_W))


SOLVE_BLOCK = 16


def _split2(a):
    hi = a.astype(BF16)
    return hi, (a - hi.astype(F32)).astype(BF16)


def _solve_unit_lower_many(lmats, rhss):
    ns = len(lmats)
    c = lmats[0].shape[0]
    n = rhss[0].shape[1]
    bs = min(SOLVE_BLOCK, c)
    col = lax.broadcasted_iota(jnp.int32, (bs, c), 1)
    d = lambda x, y: jnp.dot(x, y, preferred_element_type=F32)
    done = [[] for _ in range(ns)]
    for blk in range(c // bs):
        rows = slice(blk * bs, (blk + 1) * bs)
        lrows = [lm[rows, :] for lm in lmats]
        rs = [rh[rows, :] for rh in rhss]
        if blk > 0:
            for s in range(ns):
                xs = jnp.concatenate(done[s] + [jnp.zeros((c - blk * bs, n), F32)], axis=0)
                lh, ll = _split2(jnp.where(col < blk * bs, lrows[s], 0.0))
                xh, xl = _split2(xs)
                rs[s] = rs[s] - (d(lh, xl) + d(ll, xh) + d(lh, xh))
        for j in range(bs - 1):
            for s in range(ns):
                rs[s] = rs[s] - lrows[s][:, blk * bs + j:blk * bs + j + 1] * rs[s][j:j + 1, :]
        for s in range(ns):
            done[s].append(rs[s])
    return [dn[0] if len(dn) == 1 else jnp.concatenate(dn, axis=0) for dn in done]


def _solve_unit_lower(lmat, rhs):
    return _solve_unit_lower_many([lmat], [rhs])[0]


def _rwkv_chunk_kernel(rt_ref, at_ref, bt_ref, kt_ref, bh_ref, kh_ref, v_ref, pc_ref, z0_ref, y_ref, zo_ref, z_scr,
                       *, chunk):
    i = pl.program_id(2)
    tb = rt_ref.shape[0]

    @pl.when(i == 0)
    def _():
        z_scr[...] = z0_ref[...]

    lo = lax.broadcasted_iota(jnp.int32, (1, LANES), 1) < B_DH
    zr = lax.broadcasted_iota(jnp.int32, (LANES, LANES), 0) < B_DH
    zc = lax.broadcasted_iota(jnp.int32, (LANES, LANES), 1) < B_DH
    offdiag = zr != zc
    ri = lax.broadcasted_iota(jnp.int32, (chunk, chunk), 0)
    ci = lax.broadcasted_iota(jnp.int32, (chunk, chunk), 1)
    nt = (((1,), (1,)), ((), ()))
    tn = (((0,), (0,)), ((), ()))
    d = lambda x, y: jnp.dot(x, y, preferred_element_type=F32)
    zero = jnp.zeros((), BF16)
    n_chunks = tb // chunk
    n_pp = rt_ref.shape[1] // LANES
    lmats, rhss, rkvs, rbs = [], [], [], []
    for p in range(n_pp):
        ls = slice(p * LANES, (p + 1) * LANES)
        for c in range(n_chunks):
            rs = slice(c * chunk, (c + 1) * chunk)
            rt, at, bt, kt = rt_ref[rs, ls], at_ref[rs, ls], bt_ref[rs, ls], kt_ref[rs, ls]
            v = v_ref[rs, ls].astype(BF16)
            for head in range(2):
                kmask = lo if head == 1 else jnp.logical_not(lo)
                a_h, r_h = jnp.where(kmask, at, zero), jnp.where(kmask, rt, zero)
                b_h, k_h = jnp.where(kmask, bt, zero), jnp.where(kmask, kt, zero)
                v_h = jnp.where(kmask, zero, v)
                lhs = jnp.concatenate([a_h, r_h], axis=0)
                gb = lax.dot_general(lhs, b_h, nt, preferred_element_type=F32)
                gk = lax.dot_general(lhs, k_h, nt, preferred_element_type=F32)
                lmats.append(jnp.where(ri > ci, gb[:chunk], 0.0))
                rbs.append(jnp.where(ri >= ci, gb[chunk:], 0.0).astype(BF16))
                ak = jnp.where(ri > ci, gk[:chunk], 0.0).astype(BF16)
                rk = jnp.where(ri >= ci, gk[chunk:], 0.0).astype(BF16)
                rhss.append(a_h.astype(F32) + d(ak, v_h))
                rkvs.append(d(rk, v_h))
    xs = _solve_unit_lower_many(lmats, rhss)
    zs = [z_scr[p] for p in range(n_pp)]
    for c in range(n_chunks):
        rs = slice(c * chunk, (c + 1) * chunk)
        for p in range(n_pp):
            ls = slice(p * LANES, (p + 1) * LANES)
            s0 = 2 * (p * n_chunks + c)
            rt = rt_ref[rs, ls]
            stacked = jnp.concatenate([xs[s0].astype(BF16), xs[s0 + 1].astype(BF16),
                                       jnp.where(lo, zero, rt), jnp.where(lo, rt, zero)], axis=0)
            big = lax.dot_general(stacked, zs[p].astype(BF16), nt, preferred_element_type=F32)
            u0 = jnp.where(lo, big[:chunk], 0.0)
            u1 = jnp.where(lo, 0.0, big[chunk:2 * chunk])
            y = jnp.where(lo, big[2 * chunk:3 * chunk], big[3 * chunk:]) + rkvs[s0] + rkvs[s0 + 1]
            y = y - d(rbs[s0], u0.astype(BF16)) - d(rbs[s0 + 1], u1.astype(BF16))
            y_ref[rs, ls] = y
            upd = lax.dot_general(jnp.concatenate([v_ref[rs, ls].astype(BF16), (u0 + u1).astype(BF16)], axis=0),
                                  jnp.concatenate([kh_ref[rs, ls], -bh_ref[rs, ls]], axis=0), tn,
                                  preferred_element_type=F32)
            zs[p] = jnp.where(offdiag, zs[p] * pc_ref[c * chunk:c * chunk + 1, ls] + upd, zs[p])
    for p in range(n_pp):
        z_scr[p] = zs[p]

    @pl.when(i == pl.num_programs(2) - 1)
    def _():
        zo_ref[...] = z_scr[...]


RWKV_PAIRS_PER_STEP = 4


def _rwkv_chunks(rt, at, bt, kt, bh, kh, v, pc, z0, row0, n_seq, t, *, tb, chunk, pps=RWKV_PAIRS_PER_STEP):
    assert t % tb == 0 and row0 % tb == 0 and tb % chunk == 0
    nb = t // tb
    rb = row0 // tb
    n_pairs = B_HEADS // 2
    tok = pl.BlockSpec((tb, pps * LANES), lambda s, p, i: (rb + s * nb + i, p))
    st = pl.BlockSpec((None, pps, LANES, LANES), lambda s, p, i: (s, p, 0, 0))
    return pl.pallas_call(
        functools.partial(_rwkv_chunk_kernel, chunk=chunk),
        grid=(n_seq, n_pairs // pps, nb),
        in_specs=[tok] * 8 + [st],
        out_specs=[pl.BlockSpec((tb, pps * LANES), lambda s, p, i: (s * nb + i, p)), st],
        out_shape=[jax.ShapeDtypeStruct((n_seq * t, B_W), F32),
                   jax.ShapeDtypeStruct((n_seq, n_pairs, LANES, LANES), F32)],
        scratch_shapes=[pltpu.VMEM((pps, LANES, LANES), F32)],
        compiler_params=_cparams(("parallel", "parallel", "arbitrary")),
        name="rwkv_chunks",
    )(rt, at, bt, kt, bh, kh, v, pc, z0)


def _rwkv_post_kernel(y_ref, r_ref, kf_ref, v_ref, g_ref, lw_ref, lb_ref, rk_ref, o_ref):
    y = y_ref[...]
    mean = _group_sum(y, B_DH) * (1.0 / B_DH)
    yc = y - mean
    var = _group_sum(yc * yc, B_DH) * (1.0 / B_DH)
    yn = yc * lax.rsqrt(var + B_GN_EPS) * lw_ref[...] + lb_ref[...]
    bonus = _group_sum(r_ref[...] * kf_ref[...] * rk_ref[...], B_DH)
    o_ref[...] = ((yn + bonus * v_ref[...]) * g_ref[...]).astype(BF16)


def _rwkv_post(y, r, kf, v, g, lnx_w, lnx_b, r_k, *, tm):
    m = y.shape[0]
    tok = pl.BlockSpec((tm, B_W), lambda i: (i, 0))
    row = pl.BlockSpec((1, B_W), lambda i: (0, 0))
    return pl.pallas_call(
        _rwkv_post_kernel,
        grid=(m // tm,),
        in_specs=[tok] * 5 + [row] * 3,
        out_specs=tok,
        out_shape=jax.ShapeDtypeStruct((m, B_W), BF16),
        compiler_params=_cparams(("parallel",)),
        name="rwkv_post",
    )(y, r, kf, v, g, lnx_w.reshape(1, B_W), lnx_b.reshape(1, B_W), r_k.reshape(1, B_W))


def _rwkv_cols(a):
    lw0 = 3 * B_W
    la0 = lw0 + B_DECAY_LORA
    lg0 = la0 + B_AAA_LORA
    z = jnp.zeros(a.shape[:-1] + (LANES - B_DECAY_LORA,), a.dtype)
    return jnp.concatenate([a[..., :lw0], a[..., lg0:], a[..., lw0:la0], z, a[..., la0:lg0], z], axis=-1)


def _rwkv_cols_inv(a):
    return jnp.concatenate([a[..., :PB_LG], a[..., PB_LW:PB_LW + B_DECAY_LORA],
                            a[..., PB_LA:PB_LA + B_AAA_LORA], a[..., PB_LG:PB_LW]], axis=-1)


def _pad_rows(a, n):
    return jnp.pad(a, ((0, n - a.shape[0]), (0, 0)))


def _rwkv_state_to_pairs(s):
    n = s.shape[0]
    s = s.reshape(n, B_HEADS // 2, 2, B_DH, B_DH)
    eye = jnp.broadcast_to(jnp.eye(B_DH, dtype=F32), (n, B_HEADS // 2, B_DH, B_DH))
    top = jnp.concatenate([eye, s[:, :, 0]], axis=-1)
    bot = jnp.concatenate([s[:, :, 1], eye], axis=-1)
    return jnp.concatenate([top, bot], axis=-2)


def _rwkv_state_from_pairs(z):
    n = z.shape[0]
    s = jnp.stack([z[:, :, :B_DH, B_DH:], z[:, :, B_DH:, :B_DH]], axis=2)
    return s.reshape(n, B_HEADS, B_DH, B_DH)


def _rwkv_mixer(pb, n_p, t_p, n_s, t_s, prev_s, state_s, mu, w0, w2, a0, a2, g2, k_k, k_a, r_k, lnx_w, lnx_b,
                *, tm, tb):
    mp = n_p * t_p
    chunk_p = min(64, t_p)
    chunk_s = min(64, t_s)
    assert mp % tm == 0 and (n_s * t_s) % tm == 0 and t_p % chunk_p == 0 and t_s % chunk_s == 0
    starts = jnp.concatenate([
        jnp.zeros((tm, PB_COLS), F32),
        jnp.broadcast_to(_rwkv_cols(prev_s), (n_s, t_s, PB_COLS)).reshape(n_s * t_s, PB_COLS)], axis=0)
    r, kf, v, g, rt, at, bt, kt, bh, kh, pc = _rwkv_prep(
        pb, starts, _rwkv_cols(mu.reshape(1, B_COLS)), w0, _pad_rows(w2, LANES).astype(BF16), a0,
        _pad_rows(a2, LANES).astype(BF16), g2.astype(BF16), k_k, k_a, tm=tm, n_prompt_tiles=mp // tm,
        chunk_p=chunk_p, chunk_s=chunk_s, t_p=t_p, t_s=t_s)
    z0_p = _rwkv_state_to_pairs(jnp.zeros((n_p, B_HEADS, B_DH, B_DH), F32))
    y_p, z_p = _rwkv_chunks(rt, at, bt, kt, bh, kh, v, pc, z0_p, 0, n_p, t_p, tb=tb, chunk=chunk_p)
    y_s, z_s = _rwkv_chunks(rt, at, bt, kt, bh, kh, v, pc, _rwkv_state_to_pairs(state_s), mp, n_s, t_s,
                            tb=t_s, chunk=chunk_s, pps=B_HEADS // 2)
    y = jnp.concatenate([y_p, y_s], axis=0)
    ob = _rwkv_post(y, r, kf, v, g, lnx_w, lnx_b, r_k.reshape(B_W), tm=tm)
    return ob, _rwkv_state_from_pairs(z_p), _rwkv_state_from_pairs(z_s)


PC_Z = C_CONV_CH
PC_BETA = PC_Z + C_V
PC_COLS = PC_BETA + 2 * LANES
HALO = 8


def _chunk_cumsum(g, chunk):
    rowc = jnp.bitwise_and(lax.broadcasted_iota(jnp.int32, g.shape, 0), chunk - 1)
    k = 1
    while k < chunk:
        g = g + jnp.where(rowc >= k, pltpu.roll(g, k, 0), 0.0)
        k *= 2
    return g


def _transpose_rows(a):
    c = a.shape[0]
    if c < LANES:
        a = jnp.concatenate([a, jnp.zeros((LANES - c, LANES), F32)], axis=0)
    return a.T


def _pad_chunk_rows(a):
    c = a.shape[0]
    return a if c == LANES else jnp.concatenate([a, jnp.zeros((LANES - c, a.shape[1]), a.dtype)], axis=0)


def _gdn_prep_kernel(x_ref, halo_ref, st_ref, cw_ref, sm_ref, alog_ref, dtb_ref,
                     q_ref, k_ref, v_ref, beta_ref, gc_ref, *, chunk):
    i = pl.program_id(1)
    x = x_ref[...]
    tm = x.shape[0]
    halo = jnp.where(i == 0, st_ref[...], halo_ref[...])
    row8 = lax.broadcasted_iota(jnp.int32, (HALO, x.shape[1]), 0)
    acc = x * cw_ref[C_CONV - 1:C_CONV, :]
    for s in range(1, C_CONV):
        xs = pltpu.roll(x, s, 0)
        first = jnp.where(row8 < s, pltpu.roll(halo, s, 0), xs[:HALO])
        xs = first if tm == HALO else jnp.concatenate([first, xs[HALO:]], axis=0)
        acc = acc + xs * cw_ref[C_CONV - 1 - s:C_CONV - s, :]
    y = _silu(acc)
    q_ref[...] = _group_norm_scale(y[:, :C_QK], C_DH, 1.0, EPS) * (C_DH ** -0.5)
    k_ref[...] = _group_norm_scale(y[:, C_QK:2 * C_QK], C_DH, 1.0, EPS)
    v_ref[...] = y[:, 2 * C_QK:]
    sm = sm_ref[...]
    beta_ref[...] = _sigmoid(sm[:, :LANES])
    g = -jnp.exp(alog_ref[...]) * _softplus(sm[:, LANES:] + dtb_ref[...])
    gc_ref[...] = _chunk_cumsum(g, chunk)


def _gdn_prep(pc, conv_state, row0, n_seq, t, conv_w, a_log, dt_bias, *, tm, chunk):
    assert t % tm == 0 and row0 % tm == 0 and tm % HALO == 0 and tm % chunk == 0 and chunk & (chunk - 1) == 0
    nb = t // tm
    rb = row0 // tm
    hb = tm // HALO
    tok_in = lambda n, cb: pl.BlockSpec((tm, n), lambda s, i: (rb + s * nb + i, cb))
    tok_out = lambda n: pl.BlockSpec((tm, n), lambda s, i: (s * nb + i, 0))
    row = lambda n: pl.BlockSpec((1, n), lambda s, i: (0, 0))
    sds = lambda n: jax.ShapeDtypeStruct((n_seq * t, n), F32)
    lanes8 = lambda a: jnp.zeros((1, LANES), F32).at[0, :C_V_HEADS].set(a)
    return pl.pallas_call(
        functools.partial(_gdn_prep_kernel, chunk=chunk),
        grid=(n_seq, nb),
        in_specs=[
            tok_in(C_CONV_CH, 0),
            pl.BlockSpec((HALO, C_CONV_CH), lambda s, i: (jnp.maximum((rb + s * nb + i) * hb - 1, 0), 0)),
            pl.BlockSpec((None, HALO, C_CONV_CH), lambda s, i: (s, 0, 0)),
            pl.BlockSpec((C_CONV, C_CONV_CH), lambda s, i: (0, 0)),
            tok_in(2 * LANES, PC_BETA // (2 * LANES)),
            row(LANES), row(LANES),
        ],
        out_specs=[tok_out(C_QK), tok_out(C_QK), tok_out(C_V), tok_out(LANES), tok_out(LANES)],
        out_shape=[sds(C_QK), sds(C_QK), sds(C_V), sds(LANES), sds(LANES)],
        compiler_params=_cparams(("parallel", "arbitrary")),
        name="gdn_prep",
    )(pc, pc, conv_state, conv_w, pc, lanes8(a_log), lanes8(dt_bias))


def _gdn_chunk_kernel(q_ref, k_ref, v_ref, beta_ref, gc_ref, z_ref, on_ref, s0_ref, o_ref, so_ref, s_scr, *, chunk):
    hq = pl.program_id(1)
    i = pl.program_id(2)
    tb = q_ref.shape[0]
    rep = v_ref.shape[1] // C_DH

    @pl.when(i == 0)
    def _():
        s_scr[...] = s0_ref[...]

    lane = lax.broadcasted_iota(jnp.int32, (chunk, LANES), 1)
    ri = lax.broadcasted_iota(jnp.int32, (chunk, chunk), 0)
    ci = lax.broadcasted_iota(jnp.int32, (chunk, chunk), 1)
    nt = (((1,), (1,)), ((), ()))
    ones = jnp.ones((chunk, LANES), BF16)
    n_chunks = tb // chunk
    lmats, rhss, attns, qgs, kds, eglast = [], [], [], [], [], []
    for c in range(n_chunks):
        rs = slice(c * chunk, (c + 1) * chunk)
        q, k = q_ref[rs, :], k_ref[rs, :]
        qb, kbf = q.astype(BF16), k.astype(BF16)
        kkt = lax.dot_general(kbf, kbf, nt, preferred_element_type=F32)
        qkt = lax.dot_general(qb, kbf, nt, preferred_element_type=F32)
        for j in range(rep):
            hsel = lane == hq * rep + j
            beta = jnp.sum(jnp.where(hsel, beta_ref[rs, :], 0.0), axis=-1, keepdims=True)
            gcol = jnp.sum(jnp.where(hsel, gc_ref[rs, :], 0.0), axis=-1, keepdims=True)
            g0 = jnp.where(lane == 0, gcol, 0.0)
            grow = sum(lax.dot_general(ones, part, nt, preferred_element_type=F32) for part in _split3(g0))
            dec = jnp.exp(jnp.where(ri >= ci, gcol - grow, NEG_INF))
            lmats.append(jnp.where(ri > ci, beta * kkt * dec, 0.0))
            rhss.append(jnp.concatenate([v_ref[rs, j * C_DH:(j + 1) * C_DH] * beta, k * (beta * jnp.exp(gcol))],
                                        axis=1))
            attns.append((qkt * dec).astype(BF16))
            qgs.append((q * jnp.exp(gcol)).astype(BF16))
            glast = gcol[chunk - 1:chunk, :]
            kds.append(_transpose_rows(k * jnp.exp(glast - gcol)).astype(BF16))
            eglast.append(jnp.exp(glast))
    xs = _solve_unit_lower_many(lmats, rhss)
    ss = [s_scr[j] for j in range(rep)]
    for c in range(n_chunks):
        rs = slice(c * chunk, (c + 1) * chunk)
        for j in range(rep):
            n = c * rep + j
            sb = ss[j].astype(BF16)
            v_new = xs[n][:, :C_DH] - jnp.dot(xs[n][:, C_DH:].astype(BF16), sb, preferred_element_type=F32)
            vb = v_new.astype(BF16)
            o = jnp.dot(qgs[n], sb, preferred_element_type=F32) + jnp.dot(attns[n], vb, preferred_element_type=F32)
            ss[j] = ss[j] * eglast[n] + jnp.dot(kds[n], _pad_chunk_rows(vb), preferred_element_type=F32)
            o = _rms_rows(o, on_ref[...]) * _silu(z_ref[rs, j * C_DH:(j + 1) * C_DH])
            o_ref[rs, j * C_DH:(j + 1) * C_DH] = o.astype(BF16)
    for j in range(rep):
        s_scr[j] = ss[j]

    @pl.when(i == pl.num_programs(2) - 1)
    def _():
        so_ref[...] = s_scr[...]


def _gdn_chunks(q, k, v, beta, gc, pc, row0, n_seq, t, onorm, s0, *, tb, chunk):
    assert t % tb == 0 and row0 % tb == 0 and tb % chunk == 0
    nb = t // tb
    rb = row0 // tb
    rep = C_V_HEADS // C_QK_HEADS
    tok = lambda n, cb: pl.BlockSpec((tb, n), cb)
    st = pl.BlockSpec((None, rep, C_DH, C_DH), lambda s, h, i: (s, h, 0, 0))
    return pl.pallas_call(
        functools.partial(_gdn_chunk_kernel, chunk=chunk),
        grid=(n_seq, C_QK_HEADS, nb),
        in_specs=[
            tok(C_DH, lambda s, h, i: (s * nb + i, h)),
            tok(C_DH, lambda s, h, i: (s * nb + i, h)),
            tok(rep * C_DH, lambda s, h, i: (s * nb + i, h)),
            tok(LANES, lambda s, h, i: (s * nb + i, 0)),
            tok(LANES, lambda s, h, i: (s * nb + i, 0)),
            tok(rep * C_DH, lambda s, h, i: (rb + s * nb + i, PC_Z // (rep * C_DH) + h)),
            pl.BlockSpec((1, LANES), lambda s, h, i: (0, 0)),
            st,
        ],
        out_specs=[tok(rep * C_DH, lambda s, h, i: (s * nb + i, h)), st],
        out_shape=[jax.ShapeDtypeStruct((n_seq * t, C_V), BF16),
                   jax.ShapeDtypeStruct((n_seq, C_V_HEADS, C_DH, C_DH), F32)],
        scratch_shapes=[pltpu.VMEM((rep, C_DH, C_DH), F32)],
        compiler_params=_cparams(("parallel", "parallel", "arbitrary")),
        name="gdn_chunks",
    )(q, k, v, beta, gc, pc, onorm.reshape(1, LANES), s0)


def _gdn_group(pc, conv_state3, s0, row0, n_seq, t, conv_w, a_log, dt_bias, onorm, *, tm, tb):
    chunk = C_CHUNK if t % C_CHUNK == 0 else t
    conv_state = jnp.pad(conv_state3, ((0, 0), (HALO - (C_CONV - 1), 0), (0, 0)))
    q, k, v, beta, gc = _gdn_prep(pc, conv_state, row0, n_seq, t, conv_w, a_log, dt_bias, tm=tm, chunk=chunk)
    return _gdn_chunks(q, k, v, beta, gc, pc, row0, n_seq, t, onorm, s0, tb=tb, chunk=chunk)


PD_V = 2 * D_K
PD_GATE = PD_V + D_V
PD_GLR = PD_GATE + D_V
PD_COLS = PD_GLR + LANES


def _gla_kernel(q_ref, k_ref, v_ref, gate_ref, glr_ref, w2_ref, bgk_ref, on_ref, s0_ref, o_ref, so_ref, s_scr,
                *, chunk):
    i = pl.program_id(2)
    tb = q_ref.shape[0]

    @pl.when(i == 0)
    def _():
        s_scr[...] = s0_ref[...]

    pre = jnp.dot(glr_ref[...].astype(BF16), w2_ref[...], preferred_element_type=F32) + bgk_ref[...]
    bcum = _chunk_cumsum(-_softplus(-pre) * (1.0 / D_GATE_NORM), chunk)
    ri = lax.broadcasted_iota(jnp.int32, (chunk, 1), 0)
    ci = lax.broadcasted_iota(jnp.int32, (chunk, chunk), 1)
    intra, qgs, decays, upds = [], [], [], []
    for c in range(tb // chunk):
        rs = slice(c * chunk, (c + 1) * chunk)
        q = q_ref[rs, :] * (D_DK ** -0.5)
        k, bc = k_ref[rs, :], bcum[rs, :]
        vb = v_ref[rs, :].astype(BF16)
        attn = jnp.zeros((chunk, chunk), F32)
        for j in range(chunk):
            e = jnp.exp(jnp.where(ri >= j, bc - bc[j:j + 1, :], NEG_INF))
            col = jnp.sum(q * k[j:j + 1, :] * e, axis=-1, keepdims=True)
            attn = jnp.where(ci == j, col, attn)
        intra.append(jnp.dot(attn.astype(BF16), vb, preferred_element_type=F32))
        qgs.append((q * jnp.exp(bc)).astype(BF16))
        blast = bc[chunk - 1:chunk, :]
        ebt = jnp.broadcast_to(jnp.exp(blast), (LANES, LANES)).T
        decays.append(jnp.concatenate([ebt, ebt], axis=1))
        kdt = _transpose_rows(k * jnp.exp(blast - bc)).astype(BF16)
        upds.append(jnp.dot(kdt, _pad_chunk_rows(vb), preferred_element_type=F32))
    s = s_scr[...]
    for c in range(tb // chunk):
        rs = slice(c * chunk, (c + 1) * chunk)
        o = jnp.dot(qgs[c], s.astype(BF16), preferred_element_type=F32) + intra[c]
        s = s * decays[c] + upds[c]
        o = _rms_rows(o, on_ref[...]) * _silu(gate_ref[rs, :])
        o_ref[rs, :] = o.astype(BF16)
    s_scr[...] = s

    @pl.when(i == pl.num_programs(2) - 1)
    def _():
        so_ref[...] = s_scr[...]


def _gla_group(pd, s0, row0, n_seq, t, w_gk2, b_gk, onorm, *, tb):
    chunk = D_CHUNK if t % D_CHUNK == 0 else t
    assert t % tb == 0 and row0 % tb == 0 and tb % chunk == 0 and chunk & (chunk - 1) == 0
    nb = t // tb
    rb = row0 // tb
    w2 = _pad_rows(w_gk2, LANES).astype(BF16)
    tok = lambda n, cb: pl.BlockSpec((tb, n), cb)
    st = pl.BlockSpec((None, None, D_DK, D_DV), lambda s, h, i: (s, h, 0, 0))
    return pl.pallas_call(
        functools.partial(_gla_kernel, chunk=chunk),
        grid=(n_seq, D_HEADS, nb),
        in_specs=[
            tok(D_DK, lambda s, h, i: (rb + s * nb + i, h)),
            tok(D_DK, lambda s, h, i: (rb + s * nb + i, D_K // D_DK + h)),
            tok(D_DV, lambda s, h, i: (rb + s * nb + i, PD_V // D_DV + h)),
            tok(D_DV, lambda s, h, i: (rb + s * nb + i, PD_GATE // D_DV + h)),
            tok(LANES, lambda s, h, i: (rb + s * nb + i, PD_GLR // LANES)),
            pl.BlockSpec((LANES, D_DK), lambda s, h, i: (0, h)),
            pl.BlockSpec((1, D_DK), lambda s, h, i: (0, h)),
            pl.BlockSpec((1, D_DV), lambda s, h, i: (0, 0)),
            st,
        ],
        out_specs=[tok(D_DV, lambda s, h, i: (s * nb + i, h)), st],
        out_shape=[jax.ShapeDtypeStruct((n_seq * t, D_V), BF16),
                   jax.ShapeDtypeStruct((n_seq, D_HEADS, D_DK, D_DV), F32)],
        scratch_shapes=[pltpu.VMEM((D_DK, D_DV), F32)],
        compiler_params=_cparams(("parallel", "parallel", "arbitrary")),
        name="gla_chunks",
    )(pd, pd, pd, pd, pd, w2, b_gk.reshape(1, D_K), onorm.reshape(1, D_DV), s0)


TM = 768
TM_SMALL = 256


def _last_rows(a, row0, n_seq, t, k, n_cols):
    picks = [lax.slice(a, (row0 + t - k + j, 0), (row0 + n_seq * t, n_cols), (t, 1)) for j in range(k)]
    return jnp.stack(picks, axis=1)


def _cross_block(x, layer, n_p, t_p, n_s, t_s, mem_prompt, cache_mem_k, cache_mem_v, norm_mem_w, norm_cross_w,
                 x_w_q, x_w_k, x_w_v, x_w_o, x_q_norm, x_k_norm):
    d = x.shape[1]
    mp = n_p * t_p
    wkv = jnp.concatenate([x_w_k[layer], x_w_v[layer]], axis=1).astype(BF16)
    hw = jnp.concatenate([jnp.tile(x_k_norm[layer], X_HEADS), jnp.ones((X_W,), F32)]).reshape(1, 2 * X_W)
    mem = mem_prompt.reshape(n_p * N_MEM, d)
    kv = _norm_matmul(mem, norm_mem_w[layer], wkv, tm=min(512, n_p * N_MEM), tn=X_W, head_w=hw,
                      n_norm_cols=X_W, gsize=X_DH)
    mk_p = kv[:, :X_W].reshape(n_p, N_MEM, X_W)
    mv_p = kv[:, X_W:].reshape(n_p, N_MEM, X_W)
    qx = _norm_matmul(x, norm_cross_w[layer], x_w_q[layer].astype(BF16), tm=TM, tn=X_W)
    ca_p = _cross_attention(qx, 0, n_p, t_p, mk_p, mv_p, x_q_norm[layer], tq=min(512, t_p))
    ca_s = _cross_attention(qx, mp, n_s, t_s, cache_mem_k, cache_mem_v, x_q_norm[layer], tq=t_s, mem_layer=layer)
    ca = jnp.concatenate([ca_p, ca_s], axis=0)
    x = _matmul_residual([ca], [x_w_o[layer].astype(BF16)], x, tm=TM, tn=1024)
    return x, mk_p.reshape(n_p, N_MEM, X_HEADS, X_DH), mv_p.reshape(n_p, N_MEM, X_HEADS, X_DH)


def kernel(x_prompt, x_sample, cache_diff_k, cache_diff_v, state_rwkv, state_rwkv_shift, cache_mem_k, cache_mem_v, state_gdn, state_gdn_conv, state_gla, page_table, mem_prompt, norm_mix_w, norm_cross_w, norm_mem_w, norm_ffn_w, w_in_even, w_out_even, a_q_norm, a_k_norm, a_lam_q1, a_lam_k1, a_lam_q2, a_lam_k2, a_subln, b_mu, b_w0, b_w2, b_a0, b_a2, b_g2, b_k_k, b_k_a, b_r_k, b_lnx_w, b_lnx_b, ffd_w_gate, ffd_w_up, ffd_w_down, w_in_odd, w_out_odd, c_conv_w, c_a_log, c_dt_bias, c_onorm, d_w_gk2, d_b_gk, d_onorm, moe_w_router, moe_w_gate, moe_w_up, moe_w_down, x_w_q, x_w_k, x_w_v, x_w_o, x_q_norm, x_k_norm):
    n_p, t_p, d = x_prompt.shape
    n_s, t_s, _ = x_sample.shape
    mp, ms = n_p * t_p, n_s * t_s
    depth = norm_mix_w.shape[0]
    x = jnp.concatenate([x_prompt.reshape(mp, d), x_sample.reshape(ms, d)], axis=0)
    assert (mp + ms) % TM == 0 and (mp + ms) % TM_SMALL == 0
    page = cache_diff_k.shape[2]
    slopes = jnp.exp2(-(8.0 / A_HEADS) * jnp.arange(1, A_HEADS + 1, dtype=F32))
    cross_w = (mem_prompt, cache_mem_k, cache_mem_v, norm_mem_w, norm_cross_w, x_w_q, x_w_k, x_w_v, x_w_o,
               x_q_norm, x_k_norm)
    dk_p, dv_p, dk_s, dv_s, rw_p, rw_s, sh_p, sh_s = [], [], [], [], [], [], [], []
    mk_l, mv_l, gd_p, gd_s, cv_p, cv_s, gl_p, gl_s = [], [], [], [], [], [], [], []
    for layer in range(depth):
        i = layer // 2
        if layer % 2 == 0:
            lam_init = 0.8 - 0.6 * math.exp(-0.3 * layer)
            w_in = w_in_even[i]
            hw = jnp.concatenate([jnp.tile(a_q_norm[i], A_Q // A_DH), jnp.tile(a_k_norm[i], A_Q // A_DH),
                                  jnp.ones((A_COLS - 2 * A_Q,), F32)]).reshape(1, A_COLS)
            pa = _norm_matmul(x, norm_mix_w[layer], w_in[:, :A_COLS].astype(BF16), tm=TM, tn=1024, head_w=hw,
                              n_norm_cols=2 * A_Q, gsize=A_DH)
            pb = _norm_matmul(x, norm_mix_w[layer], _rwkv_cols(w_in[:, A_COLS:]).astype(BF16), tm=TM,
                              tn=PB_COLS // 2)
            lam_stack = jnp.stack([a_lam_q1[i], a_lam_k1[i], a_lam_q2[i], a_lam_k2[i]])
            oa_p = _diff_attention_prompt(pa, n_p, t_p, slopes, lam_stack, a_subln[i], lam_init, tq=min(256, t_p))
            oa_s = _diff_attention_sample(pa, mp, n_s, t_s, cache_diff_k, cache_diff_v, i, page_table, lam_stack,
                                          a_subln[i], lam_init)
            ob, st_p, st_s = _rwkv_mixer(pb, n_p, t_p, n_s, t_s, state_rwkv_shift[i], state_rwkv[i], b_mu[i],
                                         b_w0[i], b_w2[i], b_a0[i], b_a2[i], b_g2[i], b_k_k[i], b_k_a[i], b_r_k[i],
                                         b_lnx_w[i], b_lnx_b[i], tm=TM_SMALL, tb=min(256, t_p))
            dk_p.append(pa[:mp, A_Q:2 * A_Q].reshape(n_p, t_p, A_HEADS, 2 * A_DH))
            dv_p.append(pa[:mp, 2 * A_Q:].reshape(n_p, t_p, A_HEADS, 2 * A_DH))
            dk_s.append(pa[mp:, A_Q:2 * A_Q].reshape(n_s, t_s, A_HEADS, 2 * A_DH))
            dv_s.append(pa[mp:, 2 * A_Q:].reshape(n_s, t_s, A_HEADS, 2 * A_DH))
            rw_p.append(st_p)
            rw_s.append(st_s)
            sh_p.append(_rwkv_cols_inv(_last_rows(pb, 0, n_p, t_p, 1, PB_COLS)))
            sh_s.append(_rwkv_cols_inv(_last_rows(pb, mp, n_s, t_s, 1, PB_COLS)))
            oa = jnp.concatenate([oa_p, oa_s], axis=0)
            w_out = w_out_even[i].astype(BF16)
            x = _matmul_residual([oa, ob], [w_out[:A_Q], w_out[A_Q:]], x, tm=TM, tn=1024)
        else:
            w_in = w_in_odd[i]
            c_cols = C_CONV_CH + C_V + 2 * C_V_HEADS
            zc = jnp.zeros((d, LANES - C_V_HEADS), F32)
            wc = jnp.concatenate([w_in[:, :PC_BETA], w_in[:, PC_BETA:PC_BETA + C_V_HEADS], zc,
                                  w_in[:, PC_BETA + C_V_HEADS:c_cols], zc], axis=1).astype(BF16)
            wd_ = jnp.concatenate([w_in[:, c_cols:], jnp.zeros((d, LANES - D_GATE_LORA), F32)], axis=1).astype(BF16)
            pc = _norm_matmul(x, norm_mix_w[layer], wc, tm=TM, tn=PC_COLS // 2)
            pd = _norm_matmul(x, norm_mix_w[layer], wd_, tm=TM, tn=PD_COLS // 5)
            gdn_w = (c_conv_w[i], c_a_log[i], c_dt_bias[i], c_onorm[i])
            oc_p, gs_p = _gdn_group(pc, jnp.zeros((n_p, C_CONV - 1, C_CONV_CH), F32),
                                    jnp.zeros((n_p, C_V_HEADS, C_DH, C_DH), F32), 0, n_p, t_p, *gdn_w,
                                    tm=min(512, t_p), tb=min(512, t_p))
            oc_s, gs_s = _gdn_group(pc, state_gdn_conv[i], state_gdn[i], mp, n_s, t_s, *gdn_w, tm=t_s, tb=t_s)
            gla_w = (d_w_gk2[i], d_b_gk[i], d_onorm[i])
            od_p, ls_p = _gla_group(pd, jnp.zeros((n_p, D_HEADS, D_DK, D_DV), F32), 0, n_p, t_p, *gla_w,
                                    tb=min(256, t_p))
            od_s, ls_s = _gla_group(pd, state_gla[i], mp, n_s, t_s, *gla_w, tb=t_s)
            gd_p.append(gs_p)
            gd_s.append(gs_s)
            cv_p.append(_last_rows(pc, 0, n_p, t_p, C_CONV - 1, C_CONV_CH))
            cv_s.append(_last_rows(pc, mp, n_s, t_s, C_CONV - 1, C_CONV_CH))
            gl_p.append(ls_p)
            gl_s.append(ls_s)
            oc = jnp.concatenate([oc_p, oc_s], axis=0)
            od = jnp.concatenate([od_p, od_s], axis=0)
            w_out = w_out_odd[i].astype(BF16)
            x = _matmul_residual([oc, od], [w_out[:C_V], w_out[C_V:]], x, tm=TM, tn=1024)
        x, mk_p, mv_p = _cross_block(x, layer, n_p, t_p, n_s, t_s, *cross_w)
        mk_l.append(mk_p)
        mv_l.append(mv_p)
        if layer % 2 == 0:
            x = _swiglu_ffn(x, norm_ffn_w[layer], ffd_w_gate[i].astype(BF16), ffd_w_up[i].astype(BF16),
                            ffd_w_down[i].astype(BF16), tm=TM, tf=512)
        else:
            y_p, y_s = _moe_ffn(x, norm_ffn_w[layer], moe_w_router[i], moe_w_gate, moe_w_up, moe_w_down, i,
                                tm=TM, tf=256, tc=TM_SMALL, split=mp)
            if layer + 1 < depth:
                x = jnp.concatenate([y_p, y_s], axis=0)
    if depth % 2 == 1:
        y_p, y_s = x[:mp], x[mp:]
    st = jnp.stack
    return (y_p.reshape(n_p, t_p, d), y_s.reshape(n_s, t_s, d), st(dk_p), st(dv_p), st(dk_s), st(dv_s),
            st(rw_p), st(rw_s), st(sh_p), st(sh_s), st(mk_l), st(mv_l), st(gd_p), st(gd_s), st(cv_p), st(cv_s),
            st(gl_p), st(gl_s))
```

```python
import functools
import math

import jax
import jax.numpy as jnp
from jax import lax
from jax.experimental import pallas as pl
from jax.experimental.pallas import tpu as pltpu

F32 = jnp.float32
BF16 = jnp.bfloat16
NEG_INF = float("-inf")

D_MODEL = 2048
A_HEADS = 8
A_DH = 64
A_Q = 1024
A_COLS = 3072
B_HEADS = 16
B_DH = 64
B_W = 1024
B_DECAY_LORA = 96
B_AAA_LORA = 96
B_GATE_LORA = 256
B_COLS = 3520
B_GN_EPS = 64e-5
C_QK_HEADS = 4
C_V_HEADS = 8
C_DH = 128
C_CONV = 4
C_QK = 512
C_V = 1024
C_CONV_CH = 2048
C_CHUNK = 64
D_HEADS = 4
D_DK = 128
D_DV = 256
D_K = 512
D_V = 1024
D_GATE_LORA = 16
D_GATE_NORM = 16.0
D_CHUNK = 16
N_MEM = 256
X_HEADS = 4
X_DH = 128
X_W = 512
FF_DENSE = 5632
N_EXPERTS = 8
FF_EXPERT = 2816
EPS = 1e-6

LANES = 128
VMEM_LIMIT = 56 * 1024 * 1024


def _cparams(sem):
    return pltpu.CompilerParams(dimension_semantics=sem, vmem_limit_bytes=VMEM_LIMIT)


def _sigmoid(x):
    return 1.0 / (1.0 + jnp.exp(-x))


def _silu(x):
    return x * _sigmoid(x)


def _softplus(x):
    return jnp.maximum(x, 0.0) + jnp.log(1.0 + jnp.exp(-jnp.abs(x)))


def _rms_rows(x, w):
    ms = jnp.mean(x * x, axis=-1, keepdims=True)
    return x * lax.rsqrt(ms + EPS) * w


def _group_sum(x, gsize):
    lane = lax.broadcasted_iota(jnp.int32, (1, LANES), 1)
    outs = []
    for c in range(x.shape[1] // LANES):
        xc = x[:, c * LANES:(c + 1) * LANES]
        if gsize == LANES:
            ss = jnp.broadcast_to(jnp.sum(xc, axis=-1, keepdims=True), xc.shape)
        else:
            lo = jnp.sum(jnp.where(lane < 64, xc, 0.0), axis=-1, keepdims=True)
            hi = jnp.sum(jnp.where(lane >= 64, xc, 0.0), axis=-1, keepdims=True)
            ss = jnp.where(lane < 64, lo, hi)
        outs.append(ss)
    return outs[0] if len(outs) == 1 else jnp.concatenate(outs, axis=1)


def _group_norm_scale(x, gsize, inv_n, eps):
    return x * lax.rsqrt(_group_sum(x * x, gsize) * inv_n + eps)


def _split3(a):
    a1 = a.astype(BF16)
    r1 = a - a1.astype(F32)
    a2 = r1.astype(BF16)
    a3 = (r1 - a2.astype(F32)).astype(BF16)
    return a1, a2, a3


def _dot_f32(a, b):
    a1, a2, a3 = _split3(a)
    b1, b2, b3 = _split3(b)
    d = lambda x, y: jnp.dot(x, y, preferred_element_type=F32)
    return d(a1, b3) + d(a3, b1) + d(a2, b2) + d(a1, b2) + d(a2, b1) + d(a1, b1)


def _norm_mm_kernel(x_ref, nw_ref, w_ref, hw_ref, o_ref, h_scr, *, n_norm_tiles, gsize):
    j = pl.program_id(1)

    @pl.when(j == 0)
    def _():
        h_scr[...] = _rms_rows(x_ref[...], nw_ref[...]).astype(BF16)

    acc = jnp.dot(h_scr[...], w_ref[...], preferred_element_type=F32)
    if n_norm_tiles == 0:
        o_ref[...] = acc
    else:
        @pl.when(j < n_norm_tiles)
        def _():
            o_ref[...] = _group_norm_scale(acc, gsize, 1.0 / gsize, EPS) * hw_ref[...]

        @pl.when(j >= n_norm_tiles)
        def _():
            o_ref[...] = acc


def _norm_matmul(x, nw, w, *, tm, tn, head_w=None, n_norm_cols=0, gsize=LANES):
    m, k = x.shape
    n = w.shape[1]
    assert m % tm == 0 and n % tn == 0 and n_norm_cols % tn == 0
    if head_w is None:
        head_w = jnp.ones((1, n), F32)
    kern = functools.partial(_norm_mm_kernel, n_norm_tiles=n_norm_cols // tn, gsize=gsize)
    return pl.pallas_call(
        kern,
        grid=(m // tm, n // tn),
        in_specs=[
            pl.BlockSpec((tm, k), lambda i, j: (i, 0)),
            pl.BlockSpec((1, k), lambda i, j: (0, 0)),
            pl.BlockSpec((k, tn), lambda i, j: (0, j)),
            pl.BlockSpec((1, tn), lambda i, j: (0, j)),
        ],
        out_specs=pl.BlockSpec((tm, tn), lambda i, j: (i, j)),
        out_shape=jax.ShapeDtypeStruct((m, n), F32),
        scratch_shapes=[pltpu.VMEM((tm, k), BF16)],
        compiler_params=_cparams(("parallel", "arbitrary")),
        name="norm_matmul",
    )(x, nw.reshape(1, k), w, head_w)


def _mm_res_kernel(*refs, n_in):
    x_ref = refs[2 * n_in]
    o_ref = refs[2 * n_in + 1]
    acc = x_ref[...]
    for t in range(n_in):
        acc = acc + jnp.dot(refs[t][...], refs[n_in + t][...], preferred_element_type=F32)
    o_ref[...] = acc


def _matmul_residual(acts, weights, x, *, tm, tn):
    m, n = x.shape
    n_in = len(acts)
    in_specs = [pl.BlockSpec((tm, a.shape[1]), lambda i, j: (i, 0)) for a in acts]
    in_specs += [pl.BlockSpec((w.shape[0], tn), lambda i, j: (0, j)) for w in weights]
    in_specs += [pl.BlockSpec((tm, tn), lambda i, j: (i, j))]
    return pl.pallas_call(
        functools.partial(_mm_res_kernel, n_in=n_in),
        grid=(m // tm, n // tn),
        in_specs=in_specs,
        out_specs=pl.BlockSpec((tm, tn), lambda i, j: (i, j)),
        out_shape=jax.ShapeDtypeStruct((m, n), F32),
        compiler_params=_cparams(("parallel", "arbitrary")),
        name="matmul_residual",
    )(*acts, *weights, x)


def _swiglu_kernel(x_ref, nw_ref, wg_ref, wu_ref, wd_ref, o_ref, h_scr):
    f = pl.program_id(1)

    @pl.when(f == 0)
    def _():
        x = x_ref[...]
        h_scr[...] = _rms_rows(x, nw_ref[...]).astype(BF16)
        o_ref[...] = x

    h = h_scr[...]
    g = jnp.dot(h, wg_ref[...], preferred_element_type=F32)
    u = jnp.dot(h, wu_ref[...], preferred_element_type=F32)
    a = (_silu(g) * u).astype(BF16)
    o_ref[...] += jnp.dot(a, wd_ref[...], preferred_element_type=F32)


def _swiglu_ffn(x, nw, wg, wu, wd, *, tm, tf):
    m, d = x.shape
    ff = wg.shape[1]
    return pl.pallas_call(
        _swiglu_kernel,
        grid=(m // tm, ff // tf),
        in_specs=[
            pl.BlockSpec((tm, d), lambda i, f: (i, 0)),
            pl.BlockSpec((1, d), lambda i, f: (0, 0)),
            pl.BlockSpec((d, tf), lambda i, f: (0, f)),
            pl.BlockSpec((d, tf), lambda i, f: (0, f)),
            pl.BlockSpec((tf, d), lambda i, f: (f, 0)),
        ],
        out_specs=pl.BlockSpec((tm, d), lambda i, f: (i, 0)),
        out_shape=jax.ShapeDtypeStruct((m, d), F32),
        scratch_shapes=[pltpu.VMEM((tm, d), BF16)],
        compiler_params=_cparams(("parallel", "arbitrary")),
        name="swiglu_ffn",
    )(x, nw.reshape(1, d), wg, wu, wd)


MOE_TILE = 768
ROW_SLABS = D_MODEL // LANES


def _rows_to_slabs(ref, x):
    ref[...] = jnp.swapaxes(jnp.stack([x[:, j * LANES:(j + 1) * LANES] for j in range(ROW_SLABS)], axis=0), 0, 1)


def _slabs_to_rows(ref):
    xt = jnp.swapaxes(ref[...], 0, 1)
    return jnp.concatenate([xt[j] for j in range(ROW_SLABS)], axis=1)


def _moe_router_kernel(x_ref, nw_ref, wr_ref, h_ref, info_ref):
    tm = x_ref.shape[0]
    lane = lax.broadcasted_iota(jnp.int32, (tm, LANES), 1)
    hf = _rms_rows(x_ref[...], nw_ref[...])
    _rows_to_slabs(h_ref, hf)
    logits = jnp.where(lane < N_EXPERTS, _dot_f32(hf, wr_ref[...]), NEG_INF)
    m1 = jnp.max(logits, axis=-1, keepdims=True)
    i1 = jnp.min(jnp.where(logits == m1, lane, LANES), axis=-1, keepdims=True)
    rest = jnp.where(lane == i1, NEG_INF, logits)
    m2 = jnp.max(rest, axis=-1, keepdims=True)
    i2 = jnp.min(jnp.where(rest == m2, lane, LANES), axis=-1, keepdims=True)
    e2 = jnp.exp(m2 - m1)
    vals = [i1.astype(F32), i2.astype(F32), 1.0 / (1.0 + e2), e2 / (1.0 + e2)]
    info = jnp.zeros((tm, LANES), F32)
    for k, val in enumerate(vals):
        info = jnp.where(lane == k, val, info)
    info_ref[...] = info


def _moe_gather_kernel(src_ref, nu_ref, h_ref, xs_ref, sem):
    t = pl.program_id(0)
    tg = xs_ref.shape[0]

    def copy(r):
        return pltpu.make_async_copy(h_ref.at[src_ref[t * tg + r]], xs_ref.at[r], sem.at[0])

    def issue(r, carry):
        copy(r).start()
        return carry

    def drain(r, carry):
        copy(r).wait()
        return carry

    @pl.when(t < nu_ref[0])
    def _():
        lax.fori_loop(0, tg, issue, 0, unroll=8)
        lax.fori_loop(0, tg, drain, 0, unroll=8)

    @pl.when(t >= nu_ref[0])
    def _():
        xs_ref[...] = jnp.zeros(xs_ref.shape, F32)


def _moe_expert_kernel(te_ref, nu_ref, xs_ref, wg_ref, wu_ref, wd_ref, y_ref, xb_scr, acc_scr):
    del te_ref
    t = pl.program_id(0)
    f = pl.program_id(1)

    @pl.when(f == 0)
    def _():
        xb_scr[...] = _slabs_to_rows(xs_ref).astype(BF16)
        acc_scr[...] = jnp.zeros(acc_scr.shape, F32)

    @pl.when(t < nu_ref[0])
    def _():
        xb = xb_scr[...]
        g = jnp.dot(xb, wg_ref[...].astype(BF16), preferred_element_type=F32)
        u = jnp.dot(xb, wu_ref[...].astype(BF16), preferred_element_type=F32)
        acc_scr[...] += jnp.dot((_silu(g) * u).astype(BF16), wd_ref[...].astype(BF16),
                                preferred_element_type=F32)

    @pl.when(f == pl.num_programs(1) - 1)
    def _():
        _rows_to_slabs(y_ref, acc_scr[...])


def _moe_combine_kernel(dest_ref, x_ref, info_ref, y_ref, o1_ref, o2_ref, ybuf, sem, *, n_first):
    i = pl.program_id(0)
    tc = x_ref.shape[0]
    n = dest_ref.shape[0] // 2

    def copy(r, slot):
        return pltpu.make_async_copy(y_ref.at[dest_ref[slot * n + i * tc + r]], ybuf.at[slot, r], sem.at[slot])

    def issue(r, carry):
        copy(r, 0).start()
        copy(r, 1).start()
        return carry

    def drain(r, carry):
        copy(r, 0).wait()
        copy(r, 1).wait()
        return carry

    lax.fori_loop(0, tc, issue, 0, unroll=8)
    lax.fori_loop(0, tc, drain, 0, unroll=8)
    info = info_ref[...]
    y1 = _slabs_to_rows(ybuf.at[0])
    y2 = _slabs_to_rows(ybuf.at[1])
    res = x_ref[...] + info[:, 2:3] * y1 + info[:, 3:4] * y2

    @pl.when(i < n_first)
    def _():
        o1_ref[...] = res

    @pl.when(i >= n_first)
    def _():
        o2_ref[...] = res


def _moe_ffn(x, nw, w_router, wg, wu, wd, w_layer, *, tm, tf, tc, split):
    m, d = x.shape
    _, ne, _, ff = wg.shape
    tg = MOE_TILE
    assert d == D_MODEL and m % tm == 0 and m % tc == 0 and ff % tf == 0
    n_tiles = -(-(2 * m + ne * (tg - 1)) // tg)
    p_rows = n_tiles * tg
    wr = jnp.zeros((d, LANES), F32).at[:, :ne].set(w_router)
    h3, info = pl.pallas_call(
        _moe_router_kernel,
        grid=(m // tm,),
        in_specs=[pl.BlockSpec((tm, d), lambda i: (i, 0)), pl.BlockSpec((1, d), lambda i: (0, 0)),
                  pl.BlockSpec((d, LANES), lambda i: (0, 0))],
        out_specs=[pl.BlockSpec((tm, ROW_SLABS, LANES), lambda i: (i, 0, 0)),
                   pl.BlockSpec((tm, LANES), lambda i: (i, 0))],
        out_shape=[jax.ShapeDtypeStruct((m, ROW_SLABS, LANES), F32), jax.ShapeDtypeStruct((m, LANES), F32)],
        compiler_params=_cparams(("parallel",)),
        name="moe_router",
    )(x, nw.reshape(1, d), wr)

    e12 = info[:, :2].astype(jnp.int32)
    onehot = (e12[:, :1] == jnp.arange(ne)) | (e12[:, 1:2] == jnp.arange(ne))
    csum = jnp.cumsum(onehot.astype(jnp.int32), axis=0)
    rank = csum - onehot
    padded = (csum[-1] + tg - 1) // tg * tg
    gend = jnp.cumsum(padded)
    base = (gend - padded)[None, :] + rank
    dest = jnp.concatenate([jnp.take_along_axis(base, e12[:, :1], axis=1)[:, 0],
                            jnp.take_along_axis(base, e12[:, 1:2], axis=1)[:, 0]]).astype(jnp.int32)
    tile_expert = jnp.minimum(jnp.sum(jnp.arange(n_tiles)[:, None] * tg >= gend[None, :], axis=1),
                              ne - 1).astype(jnp.int32)
    n_used = (gend[-1:] // tg).astype(jnp.int32)

    any_spec = pl.BlockSpec(memory_space=pl.ANY)
    tok_ids = jnp.tile(jnp.arange(m, dtype=jnp.int32), 2)
    src = jnp.zeros((p_rows,), jnp.int32).at[dest].set(tok_ids, unique_indices=True, mode="promise_in_bounds")
    xs3 = pl.pallas_call(
        _moe_gather_kernel,
        grid_spec=pltpu.PrefetchScalarGridSpec(
            num_scalar_prefetch=2, grid=(n_tiles,), in_specs=[any_spec],
            out_specs=pl.BlockSpec((tg, ROW_SLABS, LANES), lambda t, sr, nu: (t, 0, 0)),
            scratch_shapes=[pltpu.SemaphoreType.DMA((1,))]),
        out_shape=jax.ShapeDtypeStruct((p_rows, ROW_SLABS, LANES), F32),
        compiler_params=_cparams(("arbitrary",)),
        name="moe_gather",
    )(src, n_used, h3)

    last = lambda t, nu: jnp.minimum(t, nu[0] - 1)
    y3 = pl.pallas_call(
        _moe_expert_kernel,
        grid_spec=pltpu.PrefetchScalarGridSpec(
            num_scalar_prefetch=2,
            grid=(n_tiles, ff // tf),
            in_specs=[
                pl.BlockSpec((tg, ROW_SLABS, LANES), lambda t, f, te, nu: (last(t, nu), 0, 0)),
                pl.BlockSpec((None, None, d, tf), lambda t, f, te, nu: (w_layer, te[last(t, nu)], 0, f)),
                pl.BlockSpec((None, None, d, tf), lambda t, f, te, nu: (w_layer, te[last(t, nu)], 0, f)),
                pl.BlockSpec((None, None, tf, d), lambda t, f, te, nu: (w_layer, te[last(t, nu)], f, 0)),
            ],
            out_specs=pl.BlockSpec((tg, ROW_SLABS, LANES), lambda t, f, te, nu: (t, 0, 0)),
            scratch_shapes=[pltpu.VMEM((tg, d), BF16), pltpu.VMEM((tg, d), F32)]),
        out_shape=jax.ShapeDtypeStruct((p_rows, ROW_SLABS, LANES), F32),
        compiler_params=_cparams(("arbitrary", "arbitrary")),
        name="moe_experts",
    )(tile_expert, n_used, xs3, wg, wu, wd)

    assert split % tc == 0 and 0 < split < m
    n_first = split // tc
    return pl.pallas_call(
        functools.partial(_moe_combine_kernel, n_first=n_first),
        grid_spec=pltpu.PrefetchScalarGridSpec(
            num_scalar_prefetch=1,
            grid=(m // tc,),
            in_specs=[pl.BlockSpec((tc, d), lambda i, de: (i, 0)), pl.BlockSpec((tc, LANES), lambda i, de: (i, 0)),
                      any_spec],
            out_specs=[pl.BlockSpec((tc, d), lambda i, de: (jnp.minimum(i, n_first - 1), 0)),
                       pl.BlockSpec((tc, d), lambda i, de: (jnp.maximum(i - n_first, 0), 0))],
            scratch_shapes=[pltpu.VMEM((2, tc, ROW_SLABS, LANES), F32), pltpu.SemaphoreType.DMA((2,))]),
        out_shape=[jax.ShapeDtypeStruct((split, d), F32), jax.ShapeDtypeStruct((m - split, d), F32)],
        compiler_params=_cparams(("arbitrary",)),
        name="moe_combine",
    )(dest, x, info, y3)


def _diff_lambda_vec(lam_ref, lam_init):
    lv = lam_ref[...]
    d1 = jnp.sum(lv[0:1, :] * lv[1:2, :], axis=-1, keepdims=True)
    d2 = jnp.sum(lv[2:3, :] * lv[3:4, :], axis=-1, keepdims=True)
    return jnp.exp(d1) - jnp.exp(d2) + lam_init


def _softmax_rows(s):
    m = jnp.max(s, axis=-1, keepdims=True)
    e = jnp.exp(s - m)
    return e, jnp.sum(e, axis=-1, keepdims=True)


def _diffattn_prompt_kernel(slope_ref, q_ref, k_ref, v_ref, lam_ref, subln_ref, o_ref, *, tq, lam_init):
    t = q_ref.shape[0]
    h = pl.program_id(1)
    slope = slope_ref[h]
    lane = lax.broadcasted_iota(jnp.int32, (1, LANES), 1)
    lam = _diff_lambda_vec(lam_ref, lam_init)
    kb = k_ref[...].astype(BF16)
    vb = v_ref[...].astype(BF16)
    nt = (((1,), (1,)), ((), ()))
    n_blocks = t // tq
    row = lax.broadcasted_iota(jnp.int32, (tq, t), 0) + (t - tq)
    col = lax.broadcasted_iota(jnp.int32, (tq, t), 1)
    dist = (row - col).astype(F32)
    bias_all = jnp.where(dist >= 0.0, -slope * dist, NEG_INF)
    for qi in range(n_blocks):
        n_keys = (qi + 1) * tq
        q = q_ref[qi * tq:(qi + 1) * tq, :] * (A_DH ** -0.5)
        q1 = jnp.where(lane < A_DH, q, 0.0).astype(BF16)
        q2 = jnp.where(lane >= A_DH, q, 0.0).astype(BF16)
        kk = kb[:n_keys]
        vv = vb[:n_keys]
        off = (n_blocks - 1 - qi) * tq
        bias = bias_all[:, off:off + n_keys]
        e1, l1 = _softmax_rows(lax.dot_general(q1, kk, nt, preferred_element_type=F32) + bias)
        e2, l2 = _softmax_rows(lax.dot_general(q2, kk, nt, preferred_element_type=F32) + bias)
        o = jnp.dot(e1.astype(BF16), vv, preferred_element_type=F32) * (1.0 / l1)
        o = o - jnp.dot(e2.astype(BF16), vv, preferred_element_type=F32) * (lam / l2)
        o = _rms_rows(o, subln_ref[...]) * (1.0 - lam_init)
        o_ref[qi * tq:(qi + 1) * tq, :] = o.astype(BF16)


def _diff_attention_prompt(pa, n_batch, t, slopes, lam_stack, subln, lam_init, *, tq):
    kern = functools.partial(_diffattn_prompt_kernel, tq=tq, lam_init=lam_init)
    return pl.pallas_call(
        kern,
        grid=(n_batch, A_HEADS),
        in_specs=[
            pl.BlockSpec(memory_space=pltpu.SMEM),
            pl.BlockSpec((t, LANES), lambda b, h: (b, h)),
            pl.BlockSpec((t, LANES), lambda b, h: (b, A_HEADS + h)),
            pl.BlockSpec((t, LANES), lambda b, h: (b, 2 * A_HEADS + h)),
            pl.BlockSpec((4, A_DH), lambda b, h: (0, 0)),
            pl.BlockSpec((1, LANES), lambda b, h: (0, 0)),
        ],
        out_specs=pl.BlockSpec((t, LANES), lambda b, h: (b, h)),
        out_shape=jax.ShapeDtypeStruct((n_batch * t, A_HEADS * LANES), BF16),
        compiler_params=_cparams(("parallel", "parallel")),
        name="diff_attention_prompt",
    )(slopes, pa, pa, pa, lam_stack, subln.reshape(1, LANES))


PAGES_PER_STEP = 16


def _diffattn_sample_kernel(pt_ref, q_ref, kn_ref, vn_ref, *rest, past, lam_init):
    del pt_ref
    npg = PAGES_PER_STEP
    k_refs, v_refs = rest[:npg], rest[npg:2 * npg]
    lam_ref, subln_ref, o_ref, qq_scr, m_scr, l_scr, acc_scr, bias_scr = rest[2 * npg:]
    g = pl.program_id(1)
    ts = q_ref.shape[0]
    page = k_refs[0].shape[0]
    lane = lax.broadcasted_iota(jnp.int32, (1, LANES), 1)
    hsub = lax.broadcasted_iota(jnp.int32, (A_HEADS, LANES), 0)
    col = lax.broadcasted_iota(jnp.int32, (A_HEADS, LANES), 1)
    valid = (col // (2 * ts)) == hsub
    pos_q = past + jnp.bitwise_and(col, ts - 1)
    slope = jnp.exp2(-(8.0 / A_HEADS) * (hsub + 1).astype(F32))
    nt = (((1,), (1,)), ((), ()))
    tn = (((0,), (0,)), ((), ()))

    @pl.when(g == 0)
    def _():
        rows = []
        for h in range(A_HEADS):
            q = q_ref[:, h * LANES:(h + 1) * LANES] * (A_DH ** -0.5)
            rows += [jnp.where(lane < A_DH, q, 0.0), jnp.where(lane >= A_DH, q, 0.0)]
        qq_scr[...] = jnp.concatenate(rows, axis=0).astype(BF16)
        m_scr[...] = jnp.where(valid, NEG_INF, 0.0)
        l_scr[...] = jnp.zeros(l_scr.shape, F32)
        acc_scr[...] = jnp.zeros(acc_scr.shape, F32)

    def key_bias(nk, key0, causal):
        kidx = lax.broadcasted_iota(jnp.int32, (nk * A_HEADS, LANES), 0) // A_HEADS
        dist = (pos_q[None] - (key0 + kidx).reshape(nk, A_HEADS, LANES)).astype(F32)
        keep = valid[None] & (dist >= 0.0) if causal else valid[None]
        return jnp.where(keep, -slope[None] * dist, NEG_INF)

    @pl.when(g == 0)
    def _():
        bias_scr[...] = key_bias(page, 0, False)

    def scores(k3):
        nk = k3.shape[0]
        k2 = k3.reshape(nk * A_HEADS, LANES).astype(BF16)
        return lax.dot_general(k2, qq_scr[...], nt, preferred_element_type=F32).reshape(nk, A_HEADS, LANES)

    def absorb(ss, v3s):
        m_old = m_scr[...]
        m_new = m_old
        for s in ss:
            m_new = jnp.maximum(m_new, jnp.max(s, axis=0))
        alpha = jnp.exp(m_old - m_new)
        l_new = alpha * l_scr[...]
        acc = jnp.sum(jnp.where(valid, alpha, 0.0), axis=0, keepdims=True) * acc_scr[...]
        for s, v3 in zip(ss, v3s):
            nk = s.shape[0]
            pe = jnp.exp(s - m_new[None])
            l_new = l_new + jnp.sum(pe, axis=0)
            v2 = v3.reshape(nk * A_HEADS, LANES).astype(BF16)
            acc = acc + lax.dot_general(v2, pe.reshape(nk * A_HEADS, LANES).astype(BF16), tn,
                                        preferred_element_type=F32)
        m_scr[...] = m_new
        l_scr[...] = l_new
        acc_scr[...] = acc

    absorb([scores(k_refs[j][...]) + bias_scr[...] + (slope * ((g * npg + j) * page).astype(F32))[None]
            for j in range(npg)], [v_refs[j][...] for j in range(npg)])

    @pl.when(g == pl.num_programs(1) - 1)
    def _():
        absorb([scores(kn_ref[...]) + key_bias(ts, past, True)], [vn_ref[...]])
        lam = _diff_lambda_vec(lam_ref, lam_init)
        l_row = jnp.sum(jnp.where(valid, l_scr[...], 0.0), axis=0, keepdims=True)
        ot = acc_scr[...] / l_row
        ot = ot - lam * pltpu.roll(ot, LANES - ts, 1)
        o = ot.T
        for h in range(A_HEADS):
            oh = _rms_rows(o[h * 2 * ts:h * 2 * ts + ts, :], subln_ref[...]) * (1.0 - lam_init)
            o_ref[:, h * LANES:(h + 1) * LANES] = oh.astype(BF16)


def _diff_attention_sample(pa, row0, n_batch, ts, cache_k, cache_v, layer_idx, page_table, lam_stack, subln,
                           lam_init):
    n_pages = page_table.shape[1]
    page = cache_k.shape[2]
    npg = PAGES_PER_STEP
    assert row0 % ts == 0 and n_pages % npg == 0 and 2 * ts * A_HEADS == LANES
    rb = row0 // ts
    w = A_HEADS * LANES
    new_rows = lambda c0: pa[row0:, c0:c0 + w].reshape(n_batch, ts, A_HEADS, LANES)
    kern = functools.partial(_diffattn_sample_kernel, past=n_pages * page, lam_init=lam_init)
    page_spec = lambda j: pl.BlockSpec((None, None, page, A_HEADS, LANES),
                                       lambda b, g, pt: (layer_idx, pt[b, g * npg + j], 0, 0, 0))
    new_spec = pl.BlockSpec((None, ts, A_HEADS, LANES), lambda b, g, pt: (b, 0, 0, 0))
    grid_spec = pltpu.PrefetchScalarGridSpec(
        num_scalar_prefetch=1,
        grid=(n_batch, n_pages // npg),
        in_specs=[pl.BlockSpec((ts, w), lambda b, g, pt: (rb + b, 0)), new_spec, new_spec]
        + [page_spec(j) for j in range(npg)] * 2
        + [pl.BlockSpec((4, A_DH), lambda b, g, pt: (0, 0)), pl.BlockSpec((1, LANES), lambda b, g, pt: (0, 0))],
        out_specs=pl.BlockSpec((ts, w), lambda b, g, pt: (b, 0)),
        scratch_shapes=[
            pltpu.VMEM((LANES, LANES), BF16),
            pltpu.VMEM((A_HEADS, LANES), F32),
            pltpu.VMEM((A_HEADS, LANES), F32),
            pltpu.VMEM((LANES, LANES), F32),
            pltpu.VMEM((page, A_HEADS, LANES), F32),
        ],
    )
    return pl.pallas_call(
        kern,
        grid_spec=grid_spec,
        out_shape=jax.ShapeDtypeStruct((n_batch * ts, w), BF16),
        compiler_params=_cparams(("parallel", "arbitrary")),
        name="diff_attention_sample",
    )(page_table, pa, new_rows(w), new_rows(2 * w), *([cache_k] * npg), *([cache_v] * npg), lam_stack,
      subln.reshape(1, LANES))


def _cross_attn_kernel(q_ref, k_ref, v_ref, qn_ref, o_ref):
    nt = (((1,), (1,)), ((), ()))
    if len(k_ref.shape) == 3:
        kt = jnp.swapaxes(k_ref[...], 0, 1).astype(BF16)
        vt = jnp.swapaxes(v_ref[...], 0, 1).astype(BF16)
        heads = [(kt[h], vt[h]) for h in range(X_HEADS)]
    else:
        heads = [(k_ref[:, h * LANES:(h + 1) * LANES].astype(BF16), v_ref[:, h * LANES:(h + 1) * LANES].astype(BF16))
                 for h in range(X_HEADS)]
    for h, (kh, vh) in enumerate(heads):
        sl = slice(h * LANES, (h + 1) * LANES)
        q = _rms_rows(q_ref[:, sl], qn_ref[...]) * (X_DH ** -0.5)
        s = lax.dot_general(q.astype(BF16), kh, nt, preferred_element_type=F32)
        e, l = _softmax_rows(s)
        p = (e * (1.0 / l)).astype(BF16)
        o_ref[:, sl] = jnp.dot(p, vh, preferred_element_type=F32).astype(BF16)


def _cross_attention(q, row0, n_batch, t, mk, mv, q_norm, *, tq, mem_layer=None):
    assert t % tq == 0 and row0 % tq == 0
    nq = t // tq
    rb = row0 // tq
    if mem_layer is None:
        mem_spec = pl.BlockSpec((None, N_MEM, X_W), lambda b, i: (b, 0, 0))
    else:
        mem_spec = pl.BlockSpec((None, None, N_MEM, X_HEADS, X_DH), lambda b, i: (mem_layer, b, 0, 0, 0))
    return pl.pallas_call(
        _cross_attn_kernel,
        grid=(n_batch, nq),
        in_specs=[
            pl.BlockSpec((tq, X_W), lambda b, i: (rb + b * nq + i, 0)),
            mem_spec,
            mem_spec,
            pl.BlockSpec((1, LANES), lambda b, i: (0, 0)),
        ],
        out_specs=pl.BlockSpec((tq, X_W), lambda b, i: (b * nq + i, 0)),
        out_shape=jax.ShapeDtypeStruct((n_batch * t, X_W), BF16),
        compiler_params=_cparams(("parallel", "parallel")),
        name="cross_attention",
    )(q, mk, mv, q_norm.reshape(1, LANES))


PB_LG = 3 * B_W
PB_LW = PB_LG + B_GATE_LORA
PB_LA = PB_LW + LANES
PB_COLS = PB_LA + LANES


def _swap_halves(x):
    tiles = [pltpu.roll(x[:, c * LANES:(c + 1) * LANES], LANES // 2, 1) for c in range(x.shape[1] // LANES)]
    return tiles[0] if len(tiles) == 1 else jnp.concatenate(tiles, axis=1)


def _rwkv_prep_kernel(pb_ref, halo_ref, start_ref, mu_ref, w0_ref, w2_ref, a0_ref, a2_ref, g2_ref, kk_ref, ka_ref,
                      r_ref, kf_ref, v_ref, g_ref, rt_ref, at_ref, bt_ref, kt_ref, bh_ref, kh_ref, pc_ref,
                      *, n_prompt_tiles, chunk_p, chunk_s, t_p, t_s):
    tm = pb_ref.shape[0]
    i = pl.program_id(0)
    pb = pb_ref[...]
    row = lax.broadcasted_iota(jnp.int32, (tm, 1), 0)
    prev = jnp.where(row == 0, halo_ref[HALO - 1:HALO, :], pltpu.roll(pb, 1, 0))
    seq_len = jnp.where(i < n_prompt_tiles, t_p, t_s)
    is_start = jnp.bitwise_and(i * tm + row, seq_len - 1) == 0
    ps = jnp.where(is_start, start_ref[...], prev)
    xs = pb + (ps - pb) * mu_ref[...]
    r = xs[:, :B_W]
    k = xs[:, B_W:2 * B_W]
    lg = xs[:, PB_LG:PB_LW]
    lw_in = xs[:, PB_LW:PB_LA]
    la = xs[:, PB_LA:PB_COLS]
    wl = w0_ref[...] + jnp.dot(jnp.tanh(lw_in).astype(BF16), w2_ref[...], preferred_element_type=F32)
    lw = -jnp.exp(-_softplus(-wl) - 0.5)
    a = _sigmoid(a0_ref[...] + jnp.dot(la.astype(BF16), a2_ref[...], preferred_element_type=F32))
    kk = _group_norm_scale(k * kk_ref[...], B_DH, 1.0, EPS)
    kf = k * (1.0 + (a - 1.0) * ka_ref[...])
    b = kk * a
    r_ref[...] = r
    kf_ref[...] = kf
    v_ref[...] = xs[:, 2 * B_W:3 * B_W]
    g_ref[...] = jnp.dot(_sigmoid(lg).astype(BF16), g2_ref[...], preferred_element_type=F32)

    chunk = jnp.where(pl.program_id(0) < n_prompt_tiles, chunk_p, chunk_s)
    rowc = jnp.bitwise_and(lax.broadcasted_iota(jnp.int32, (tm, B_W), 0), chunk - 1)
    lp = lw
    sfx = lw
    step = 1
    while step < max(chunk_p, chunk_s):
        lp = lp + jnp.where(rowc >= step, pltpu.roll(lp, step, 0), 0.0)
        sfx = sfx + jnp.where(rowc + step < chunk, pltpu.roll(sfx, tm - step, 0), 0.0)
        step *= 2
    sfx = sfx - lw
    e_neg = jnp.exp(-lp)
    e_sfx = jnp.exp(sfx)
    sw = lambda z: _swap_halves(z).astype(BF16)
    rt_ref[...] = sw(r * jnp.exp(lp))
    at_ref[...] = sw(kk * jnp.exp(lp - lw))
    bt_ref[...] = sw(b * e_neg)
    kt_ref[...] = sw(kf * e_neg)
    bh_ref[...] = sw(b * e_sfx)
    kh_ref[...] = sw(kf * e_sfx)
    pc_ref[...] = _swap_halves(jnp.exp(lp + sfx))


def _rwkv_prep(pb, starts, mu, w0, w2, a0, a2, g2, k_k, k_a, *, tm, n_prompt_tiles, chunk_p, chunk_s, t_p, t_s):
    m = pb.shape[0]
    assert tm % chunk_p == 0 and tm % chunk_s == 0 and tm % HALO == 0
    assert t_p & (t_p - 1) == 0 and t_s & (t_s - 1) == 0 and (n_prompt_tiles * tm) % t_s == 0
    npt = n_prompt_tiles
    row = lambda n: pl.BlockSpec((1, n), lambda i: (0, 0))
    full = lambda a: pl.BlockSpec(a.shape, lambda i: (0, 0))
    tok = lambda n: pl.BlockSpec((tm, n), lambda i: (i, 0))
    f32o = jax.ShapeDtypeStruct((m, B_W), F32)
    b16o = jax.ShapeDtypeStruct((m, B_W), BF16)
    kern = functools.partial(_rwkv_prep_kernel, n_prompt_tiles=npt, chunk_p=chunk_p, chunk_s=chunk_s, t_p=t_p,
                             t_s=t_s)
    return pl.pallas_call(
        kern,
        grid=(m // tm,),
        in_specs=[tok(PB_COLS),
                  pl.BlockSpec((HALO, PB_COLS), lambda i: (jnp.maximum(i * (tm // HALO) - 1, 0), 0)),
                  pl.BlockSpec((tm, PB_COLS), lambda i: (jnp.maximum(i - npt + 1, 0), 0)),
                  row(PB_COLS), row(B_W), full(w2), row(B_W), full(a2), full(g2), row(B_W), row(B_W)],
        out_specs=[tok(B_W)] * 11,
        out_shape=[f32o] * 4 + [b16o] * 6 + [f32o],
        compiler_params=_cparams(("parallel",)),
        name="rwkv_prep",
    )(pb, pb, starts, mu, w0.reshape(1, B_W), w2, a0.reshape(1, B_W), a2, g2, k_k.reshape(1, B_W),
      k_a.reshape(1, B_W))


SOLVE_BLOCK = 16


def _split2(a):
    hi = a.astype(BF16)
    return hi, (a - hi.astype(F32)).astype(BF16)


def _solve_unit_lower_many(lmats, rhss):
    ns = len(lmats)
    c = lmats[0].shape[0]
    n = rhss[0].shape[1]
    bs = min(SOLVE_BLOCK, c)
    col = lax.broadcasted_iota(jnp.int32, (bs, c), 1)
    d = lambda x, y: jnp.dot(x, y, preferred_element_type=F32)
    done = [[] for _ in range(ns)]
    for blk in range(c // bs):
        rows = slice(blk * bs, (blk + 1) * bs)
        lrows = [lm[rows, :] for lm in lmats]
        rs = [rh[rows, :] for rh in rhss]
        if blk > 0:
            for s in range(ns):
                xs = jnp.concatenate(done[s] + [jnp.zeros((c - blk * bs, n), F32)], axis=0)
                lh, ll = _split2(jnp.where(col < blk * bs, lrows[s], 0.0))
                xh, xl = _split2(xs)
                rs[s] = rs[s] - (d(lh, xl) + d(ll, xh) + d(lh, xh))
        for j in range(bs - 1):
            for s in range(ns):
                rs[s] = rs[s] - lrows[s][:, blk * bs + j:blk * bs + j + 1] * rs[s][j:j + 1, :]
        for s in range(ns):
            done[s].append(rs[s])
    return [dn[0] if len(dn) == 1 else jnp.concatenate(dn, axis=0) for dn in done]


def _solve_unit_lower(lmat, rhs):
    return _solve_unit_lower_many([lmat], [rhs])[0]


def _rwkv_chunk_kernel(rt_ref, at_ref, bt_ref, kt_ref, bh_ref, kh_ref, v_ref, pc_ref, z0_ref, y_ref, zo_ref, z_scr,
                       *, chunk):
    i = pl.program_id(2)
    tb = rt_ref.shape[0]

    @pl.when(i == 0)
    def _():
        z_scr[...] = z0_ref[...]

    lo = lax.broadcasted_iota(jnp.int32, (1, LANES), 1) < B_DH
    zr = lax.broadcasted_iota(jnp.int32, (LANES, LANES), 0) < B_DH
    zc = lax.broadcasted_iota(jnp.int32, (LANES, LANES), 1) < B_DH
    offdiag = zr != zc
    ri = lax.broadcasted_iota(jnp.int32, (chunk, chunk), 0)
    ci = lax.broadcasted_iota(jnp.int32, (chunk, chunk), 1)
    nt = (((1,), (1,)), ((), ()))
    tn = (((0,), (0,)), ((), ()))
    d = lambda x, y: jnp.dot(x, y, preferred_element_type=F32)
    zero = jnp.zeros((), BF16)
    n_chunks = tb // chunk
    n_pp = rt_ref.shape[1] // LANES
    lmats, rhss, rkvs, rbs = [], [], [], []
    for p in range(n_pp):
        ls = slice(p * LANES, (p + 1) * LANES)
        for c in range(n_chunks):
            rs = slice(c * chunk, (c + 1) * chunk)
            rt, at, bt, kt = rt_ref[rs, ls], at_ref[rs, ls], bt_ref[rs, ls], kt_ref[rs, ls]
            v = v_ref[rs, ls].astype(BF16)
            for head in range(2):
                kmask = lo if head == 1 else jnp.logical_not(lo)
                a_h, r_h = jnp.where(kmask, at, zero), jnp.where(kmask, rt, zero)
                b_h, k_h = jnp.where(kmask, bt, zero), jnp.where(kmask, kt, zero)
                v_h = jnp.where(kmask, zero, v)
                lhs = jnp.concatenate([a_h, r_h], axis=0)
                gb = lax.dot_general(lhs, b_h, nt, preferred_element_type=F32)
                gk = lax.dot_general(lhs, k_h, nt, preferred_element_type=F32)
                lmats.append(jnp.where(ri > ci, gb[:chunk], 0.0))
                rbs.append(jnp.where(ri >= ci, gb[chunk:], 0.0).astype(BF16))
                ak = jnp.where(ri > ci, gk[:chunk], 0.0).astype(BF16)
                rk = jnp.where(ri >= ci, gk[chunk:], 0.0).astype(BF16)
                rhss.append(a_h.astype(F32) + d(ak, v_h))
                rkvs.append(d(rk, v_h))
    xs = _solve_unit_lower_many(lmats, rhss)
    n_sq = z_scr.shape[0]
    cps = n_chunks // n_sq
    zs = {(sq, p): z_scr[sq, p] for sq in range(n_sq) for p in range(n_pp)}
    for c in range(n_chunks):
        rs = slice(c * chunk, (c + 1) * chunk)
        for p in range(n_pp):
            ls = slice(p * LANES, (p + 1) * LANES)
            s0 = 2 * (p * n_chunks + c)
            key = (c // cps, p)
            rt = rt_ref[rs, ls]
            stacked = jnp.concatenate([xs[s0].astype(BF16), xs[s0 + 1].astype(BF16),
                                       jnp.where(lo, zero, rt), jnp.where(lo, rt, zero)], axis=0)
            big = lax.dot_general(stacked, zs[key].astype(BF16), nt, preferred_element_type=F32)
            u0 = jnp.where(lo, big[:chunk], 0.0)
            u1 = jnp.where(lo, 0.0, big[chunk:2 * chunk])
            y = jnp.where(lo, big[2 * chunk:3 * chunk], big[3 * chunk:]) + rkvs[s0] + rkvs[s0 + 1]
            y = y - d(rbs[s0], u0.astype(BF16)) - d(rbs[s0 + 1], u1.astype(BF16))
            y_ref[rs, ls] = y
            upd = lax.dot_general(jnp.concatenate([v_ref[rs, ls].astype(BF16), (u0 + u1).astype(BF16)], axis=0),
                                  jnp.concatenate([kh_ref[rs, ls], -bh_ref[rs, ls]], axis=0), tn,
                                  preferred_element_type=F32)
            zs[key] = jnp.where(offdiag, zs[key] * pc_ref[c * chunk:c * chunk + 1, ls] + upd, zs[key])
    for (sq, p), val in zs.items():
        z_scr[sq, p] = val

    @pl.when(i == pl.num_programs(2) - 1)
    def _():
        zo_ref[...] = z_scr[...]


RWKV_PAIRS_PER_STEP = 4


def _rwkv_chunks(rt, at, bt, kt, bh, kh, v, pc, z0, row0, n_seq, t, *, tb, chunk, pps=RWKV_PAIRS_PER_STEP):
    spb, nb, n_groups, rb = _seq_blocking(row0, n_seq, t, tb, chunk)
    n_pairs = B_HEADS // 2
    tok = pl.BlockSpec((tb, pps * LANES), lambda s, p, i: (rb + s * nb + i, p))
    st = pl.BlockSpec((spb, pps, LANES, LANES), lambda s, p, i: (s, p, 0, 0))
    return pl.pallas_call(
        functools.partial(_rwkv_chunk_kernel, chunk=chunk),
        grid=(n_groups, n_pairs // pps, nb),
        in_specs=[tok] * 8 + [st],
        out_specs=[pl.BlockSpec((tb, pps * LANES), lambda s, p, i: (s * nb + i, p)), st],
        out_shape=[jax.ShapeDtypeStruct((n_seq * t, B_W), F32),
                   jax.ShapeDtypeStruct((n_seq, n_pairs, LANES, LANES), F32)],
        scratch_shapes=[pltpu.VMEM((spb, pps, LANES, LANES), F32)],
        compiler_params=_cparams(("parallel", "parallel", "arbitrary")),
        name="rwkv_chunks",
    )(rt, at, bt, kt, bh, kh, v, pc, z0)


def _rwkv_post_kernel(y_ref, r_ref, kf_ref, v_ref, g_ref, lw_ref, lb_ref, rk_ref, o_ref):
    y = y_ref[...]
    mean = _group_sum(y, B_DH) * (1.0 / B_DH)
    yc = y - mean
    var = _group_sum(yc * yc, B_DH) * (1.0 / B_DH)
    yn = yc * lax.rsqrt(var + B_GN_EPS) * lw_ref[...] + lb_ref[...]
    bonus = _group_sum(r_ref[...] * kf_ref[...] * rk_ref[...], B_DH)
    o_ref[...] = ((yn + bonus * v_ref[...]) * g_ref[...]).astype(BF16)


def _rwkv_post(y, r, kf, v, g, lnx_w, lnx_b, r_k, *, tm):
    m = y.shape[0]
    tok = pl.BlockSpec((tm, B_W), lambda i: (i, 0))
    row = pl.BlockSpec((1, B_W), lambda i: (0, 0))
    return pl.pallas_call(
        _rwkv_post_kernel,
        grid=(m // tm,),
        in_specs=[tok] * 5 + [row] * 3,
        out_specs=tok,
        out_shape=jax.ShapeDtypeStruct((m, B_W), BF16),
        compiler_params=_cparams(("parallel",)),
        name="rwkv_post",
    )(y, r, kf, v, g, lnx_w.reshape(1, B_W), lnx_b.reshape(1, B_W), r_k.reshape(1, B_W))


def _rwkv_cols(a):
    lw0 = 3 * B_W
    la0 = lw0 + B_DECAY_LORA
    lg0 = la0 + B_AAA_LORA
    z = jnp.zeros(a.shape[:-1] + (LANES - B_DECAY_LORA,), a.dtype)
    return jnp.concatenate([a[..., :lw0], a[..., lg0:], a[..., lw0:la0], z, a[..., la0:lg0], z], axis=-1)


def _rwkv_cols_inv(a):
    return jnp.concatenate([a[..., :PB_LG], a[..., PB_LW:PB_LW + B_DECAY_LORA],
                            a[..., PB_LA:PB_LA + B_AAA_LORA], a[..., PB_LG:PB_LW]], axis=-1)


def _pad_rows(a, n):
    return jnp.pad(a, ((0, n - a.shape[0]), (0, 0)))


def _rwkv_state_to_pairs(s):
    n = s.shape[0]
    s = s.reshape(n, B_HEADS // 2, 2, B_DH, B_DH)
    eye = jnp.broadcast_to(jnp.eye(B_DH, dtype=F32), (n, B_HEADS // 2, B_DH, B_DH))
    top = jnp.concatenate([eye, s[:, :, 0]], axis=-1)
    bot = jnp.concatenate([s[:, :, 1], eye], axis=-1)
    return jnp.concatenate([top, bot], axis=-2)


def _rwkv_state_from_pairs(z):
    n = z.shape[0]
    s = jnp.stack([z[:, :, :B_DH, B_DH:], z[:, :, B_DH:, :B_DH]], axis=2)
    return s.reshape(n, B_HEADS, B_DH, B_DH)


def _rwkv_mixer(pb, n_p, t_p, n_s, t_s, prev_s, state_s, mu, w0, w2, a0, a2, g2, k_k, k_a, r_k, lnx_w, lnx_b,
                *, tm, tb):
    mp = n_p * t_p
    chunk_p = min(64, t_p)
    chunk_s = min(64, t_s)
    assert mp % tm == 0 and (n_s * t_s) % tm == 0 and t_p % chunk_p == 0 and t_s % chunk_s == 0
    starts = jnp.concatenate([
        jnp.zeros((tm, PB_COLS), F32),
        jnp.broadcast_to(_rwkv_cols(prev_s), (n_s, t_s, PB_COLS)).reshape(n_s * t_s, PB_COLS)], axis=0)
    r, kf, v, g, rt, at, bt, kt, bh, kh, pc = _rwkv_prep(
        pb, starts, _rwkv_cols(mu.reshape(1, B_COLS)), w0, _pad_rows(w2, LANES).astype(BF16), a0,
        _pad_rows(a2, LANES).astype(BF16), g2.astype(BF16), k_k, k_a, tm=tm, n_prompt_tiles=mp // tm,
        chunk_p=chunk_p, chunk_s=chunk_s, t_p=t_p, t_s=t_s)
    z0_p = _rwkv_state_to_pairs(jnp.zeros((n_p, B_HEADS, B_DH, B_DH), F32))
    y_p, z_p = _rwkv_chunks(rt, at, bt, kt, bh, kh, v, pc, z0_p, 0, n_p, t_p, tb=tb, chunk=chunk_p)
    y_s, z_s = _rwkv_chunks(rt, at, bt, kt, bh, kh, v, pc, _rwkv_state_to_pairs(state_s), mp, n_s, t_s,
                            tb=t_s, chunk=chunk_s, pps=B_HEADS // 2)
    y = jnp.concatenate([y_p, y_s], axis=0)
    ob = _rwkv_post(y, r, kf, v, g, lnx_w, lnx_b, r_k.reshape(B_W), tm=tm)
    return ob, _rwkv_state_from_pairs(z_p), _rwkv_state_from_pairs(z_s)


PC_Z = C_CONV_CH
PC_BETA = PC_Z + C_V
PC_COLS = PC_BETA + 2 * LANES
HALO = 8


def _chunk_cumsum(g, chunk):
    rowc = jnp.bitwise_and(lax.broadcasted_iota(jnp.int32, g.shape, 0), chunk - 1)
    k = 1
    while k < chunk:
        g = g + jnp.where(rowc >= k, pltpu.roll(g, k, 0), 0.0)
        k *= 2
    return g


def _transpose_rows(a):
    c = a.shape[0]
    if c < LANES:
        a = jnp.concatenate([a, jnp.zeros((LANES - c, LANES), F32)], axis=0)
    return a.T


def _pad_chunk_rows(a):
    c = a.shape[0]
    return a if c == LANES else jnp.concatenate([a, jnp.zeros((LANES - c, a.shape[1]), a.dtype)], axis=0)


def _gdn_prep_kernel(x_ref, halo_ref, st_ref, cw_ref, sm_ref, alog_ref, dtb_ref,
                     q_ref, k_ref, v_ref, beta_ref, gc_ref, *, chunk):
    i = pl.program_id(1)
    x = x_ref[...]
    tm = x.shape[0]
    halo = jnp.where(i == 0, st_ref[...], halo_ref[...])
    row8 = lax.broadcasted_iota(jnp.int32, (HALO, x.shape[1]), 0)
    acc = x * cw_ref[C_CONV - 1:C_CONV, :]
    for s in range(1, C_CONV):
        xs = pltpu.roll(x, s, 0)
        first = jnp.where(row8 < s, pltpu.roll(halo, s, 0), xs[:HALO])
        xs = first if tm == HALO else jnp.concatenate([first, xs[HALO:]], axis=0)
        acc = acc + xs * cw_ref[C_CONV - 1 - s:C_CONV - s, :]
    y = _silu(acc)
    q_ref[...] = _group_norm_scale(y[:, :C_QK], C_DH, 1.0, EPS) * (C_DH ** -0.5)
    k_ref[...] = _group_norm_scale(y[:, C_QK:2 * C_QK], C_DH, 1.0, EPS)
    v_ref[...] = y[:, 2 * C_QK:]
    sm = sm_ref[...]
    beta_ref[...] = _sigmoid(sm[:, :LANES])
    g = -jnp.exp(alog_ref[...]) * _softplus(sm[:, LANES:] + dtb_ref[...])
    gc_ref[...] = _chunk_cumsum(g, chunk)


def _gdn_prep(pc, conv_state, row0, n_seq, t, conv_w, a_log, dt_bias, *, tm, chunk):
    assert t % tm == 0 and row0 % tm == 0 and tm % HALO == 0 and tm % chunk == 0 and chunk & (chunk - 1) == 0
    nb = t // tm
    rb = row0 // tm
    hb = tm // HALO
    tok_in = lambda n, cb: pl.BlockSpec((tm, n), lambda s, i: (rb + s * nb + i, cb))
    tok_out = lambda n: pl.BlockSpec((tm, n), lambda s, i: (s * nb + i, 0))
    row = lambda n: pl.BlockSpec((1, n), lambda s, i: (0, 0))
    sds = lambda n: jax.ShapeDtypeStruct((n_seq * t, n), F32)
    lanes8 = lambda a: jnp.zeros((1, LANES), F32).at[0, :C_V_HEADS].set(a)
    return pl.pallas_call(
        functools.partial(_gdn_prep_kernel, chunk=chunk),
        grid=(n_seq, nb),
        in_specs=[
            tok_in(C_CONV_CH, 0),
            pl.BlockSpec((HALO, C_CONV_CH), lambda s, i: (jnp.maximum((rb + s * nb + i) * hb - 1, 0), 0)),
            pl.BlockSpec((None, HALO, C_CONV_CH), lambda s, i: (s, 0, 0)),
            pl.BlockSpec((C_CONV, C_CONV_CH), lambda s, i: (0, 0)),
            tok_in(2 * LANES, PC_BETA // (2 * LANES)),
            row(LANES), row(LANES),
        ],
        out_specs=[tok_out(C_QK), tok_out(C_QK), tok_out(C_V), tok_out(LANES), tok_out(LANES)],
        out_shape=[sds(C_QK), sds(C_QK), sds(C_V), sds(LANES), sds(LANES)],
        compiler_params=_cparams(("parallel", "arbitrary")),
        name="gdn_prep",
    )(pc, pc, conv_state, conv_w, pc, lanes8(a_log), lanes8(dt_bias))


def _gdn_chunk_kernel(q_ref, k_ref, v_ref, beta_ref, gc_ref, z_ref, on_ref, s0_ref, o_ref, so_ref, s_scr, *, chunk):
    hq = pl.program_id(1)
    i = pl.program_id(2)
    tb = q_ref.shape[0]
    rep = v_ref.shape[1] // C_DH

    @pl.when(i == 0)
    def _():
        s_scr[...] = s0_ref[...]

    lane = lax.broadcasted_iota(jnp.int32, (chunk, LANES), 1)
    ri = lax.broadcasted_iota(jnp.int32, (chunk, chunk), 0)
    ci = lax.broadcasted_iota(jnp.int32, (chunk, chunk), 1)
    nt = (((1,), (1,)), ((), ()))
    ones = jnp.ones((chunk, LANES), BF16)
    n_chunks = tb // chunk
    lmats, rhss, attns, qgs, kds, eglast = [], [], [], [], [], []
    for c in range(n_chunks):
        rs = slice(c * chunk, (c + 1) * chunk)
        q, k = q_ref[rs, :], k_ref[rs, :]
        qb, kbf = q.astype(BF16), k.astype(BF16)
        kkt = lax.dot_general(kbf, kbf, nt, preferred_element_type=F32)
        qkt = lax.dot_general(qb, kbf, nt, preferred_element_type=F32)
        for j in range(rep):
            hsel = lane == hq * rep + j
            beta = jnp.sum(jnp.where(hsel, beta_ref[rs, :], 0.0), axis=-1, keepdims=True)
            gcol = jnp.sum(jnp.where(hsel, gc_ref[rs, :], 0.0), axis=-1, keepdims=True)
            g0 = jnp.where(lane == 0, gcol, 0.0)
            grow = sum(lax.dot_general(ones, part, nt, preferred_element_type=F32) for part in _split3(g0))
            dec = jnp.exp(jnp.where(ri >= ci, gcol - grow, NEG_INF))
            lmats.append(jnp.where(ri > ci, beta * kkt * dec, 0.0))
            rhss.append(jnp.concatenate([v_ref[rs, j * C_DH:(j + 1) * C_DH] * beta, k * (beta * jnp.exp(gcol))],
                                        axis=1))
            attns.append((qkt * dec).astype(BF16))
            qgs.append((q * jnp.exp(gcol)).astype(BF16))
            glast = gcol[chunk - 1:chunk, :]
            kds.append(_transpose_rows(k * jnp.exp(glast - gcol)).astype(BF16))
            eglast.append(jnp.exp(glast))
    xs = _solve_unit_lower_many(lmats, rhss)
    n_sq = s_scr.shape[0]
    cps = n_chunks // n_sq
    ss = {(sq, j): s_scr[sq, j] for sq in range(n_sq) for j in range(rep)}
    for c in range(n_chunks):
        rs = slice(c * chunk, (c + 1) * chunk)
        for j in range(rep):
            n = c * rep + j
            key = (c // cps, j)
            sb = ss[key].astype(BF16)
            v_new = xs[n][:, :C_DH] - jnp.dot(xs[n][:, C_DH:].astype(BF16), sb, preferred_element_type=F32)
            vb = v_new.astype(BF16)
            o = jnp.dot(qgs[n], sb, preferred_element_type=F32) + jnp.dot(attns[n], vb, preferred_element_type=F32)
            ss[key] = ss[key] * eglast[n] + jnp.dot(kds[n], _pad_chunk_rows(vb), preferred_element_type=F32)
            o = _rms_rows(o, on_ref[...]) * _silu(z_ref[rs, j * C_DH:(j + 1) * C_DH])
            o_ref[rs, j * C_DH:(j + 1) * C_DH] = o.astype(BF16)
    for (sq, j), val in ss.items():
        s_scr[sq, j] = val

    @pl.when(i == pl.num_programs(2) - 1)
    def _():
        so_ref[...] = s_scr[...]


def _seq_blocking(row0, n_seq, t, tb, chunk):
    assert row0 % tb == 0 and tb % chunk == 0 and (t % tb == 0 or (tb % t == 0 and n_seq % (tb // t) == 0))
    spb = max(tb // t, 1)
    return spb, max(t // tb, 1), n_seq // spb, row0 // tb


def _gdn_chunks(q, k, v, beta, gc, pc, row0, n_seq, t, onorm, s0, *, tb, chunk):
    spb, nb, n_groups, rb = _seq_blocking(row0, n_seq, t, tb, chunk)
    rep = C_V_HEADS // C_QK_HEADS
    tok = lambda n, cb: pl.BlockSpec((tb, n), cb)
    st = pl.BlockSpec((spb, rep, C_DH, C_DH), lambda s, h, i: (s, h, 0, 0))
    return pl.pallas_call(
        functools.partial(_gdn_chunk_kernel, chunk=chunk),
        grid=(n_groups, C_QK_HEADS, nb),
        in_specs=[
            tok(C_DH, lambda s, h, i: (s * nb + i, h)),
            tok(C_DH, lambda s, h, i: (s * nb + i, h)),
            tok(rep * C_DH, lambda s, h, i: (s * nb + i, h)),
            tok(LANES, lambda s, h, i: (s * nb + i, 0)),
            tok(LANES, lambda s, h, i: (s * nb + i, 0)),
            tok(rep * C_DH, lambda s, h, i: (rb + s * nb + i, PC_Z // (rep * C_DH) + h)),
            pl.BlockSpec((1, LANES), lambda s, h, i: (0, 0)),
            st,
        ],
        out_specs=[tok(rep * C_DH, lambda s, h, i: (s * nb + i, h)), st],
        out_shape=[jax.ShapeDtypeStruct((n_seq * t, C_V), BF16),
                   jax.ShapeDtypeStruct((n_seq, C_V_HEADS, C_DH, C_DH), F32)],
        scratch_shapes=[pltpu.VMEM((spb, rep, C_DH, C_DH), F32)],
        compiler_params=_cparams(("parallel", "parallel", "arbitrary")),
        name="gdn_chunks",
    )(q, k, v, beta, gc, pc, onorm.reshape(1, LANES), s0)


def _gdn_group(pc, conv_state3, s0, row0, n_seq, t, conv_w, a_log, dt_bias, onorm, *, tm, tb):
    chunk = C_CHUNK if t % C_CHUNK == 0 else t
    conv_state = jnp.pad(conv_state3, ((0, 0), (HALO - (C_CONV - 1), 0), (0, 0)))
    q, k, v, beta, gc = _gdn_prep(pc, conv_state, row0, n_seq, t, conv_w, a_log, dt_bias, tm=tm, chunk=chunk)
    return _gdn_chunks(q, k, v, beta, gc, pc, row0, n_seq, t, onorm, s0, tb=tb, chunk=chunk)


PD_V = 2 * D_K
PD_GATE = PD_V + D_V
PD_GLR = PD_GATE + D_V
PD_COLS = PD_GLR + LANES


def _gla_kernel(q_ref, k_ref, v_ref, gate_ref, glr_ref, w2_ref, bgk_ref, on_ref, s0_ref, o_ref, so_ref, s_scr,
                *, chunk):
    i = pl.program_id(2)
    tb = q_ref.shape[0]

    @pl.when(i == 0)
    def _():
        s_scr[...] = s0_ref[...]

    pre = jnp.dot(glr_ref[...].astype(BF16), w2_ref[...], preferred_element_type=F32) + bgk_ref[...]
    bcum = _chunk_cumsum(-_softplus(-pre) * (1.0 / D_GATE_NORM), chunk)
    ri = lax.broadcasted_iota(jnp.int32, (chunk, 1), 0)
    ci = lax.broadcasted_iota(jnp.int32, (chunk, chunk), 1)
    intra, qgs, decays, upds = [], [], [], []
    for c in range(tb // chunk):
        rs = slice(c * chunk, (c + 1) * chunk)
        q = q_ref[rs, :] * (D_DK ** -0.5)
        k, bc = k_ref[rs, :], bcum[rs, :]
        vb = v_ref[rs, :].astype(BF16)
        attn = jnp.zeros((chunk, chunk), F32)
        for j in range(chunk):
            e = jnp.exp(jnp.where(ri >= j, bc - bc[j:j + 1, :], NEG_INF))
            col = jnp.sum(q * k[j:j + 1, :] * e, axis=-1, keepdims=True)
            attn = jnp.where(ci == j, col, attn)
        intra.append(jnp.dot(attn.astype(BF16), vb, preferred_element_type=F32))
        qgs.append((q * jnp.exp(bc)).astype(BF16))
        blast = bc[chunk - 1:chunk, :]
        ebt = jnp.broadcast_to(jnp.exp(blast), (LANES, LANES)).T
        decays.append(jnp.concatenate([ebt, ebt], axis=1))
        kdt = _transpose_rows(k * jnp.exp(blast - bc)).astype(BF16)
        upds.append(jnp.dot(kdt, _pad_chunk_rows(vb), preferred_element_type=F32))
    n_sq = s_scr.shape[0]
    cps = (tb // chunk) // n_sq
    ss = [s_scr[sq] for sq in range(n_sq)]
    for c in range(tb // chunk):
        rs = slice(c * chunk, (c + 1) * chunk)
        sq = c // cps
        o = jnp.dot(qgs[c], ss[sq].astype(BF16), preferred_element_type=F32) + intra[c]
        ss[sq] = ss[sq] * decays[c] + upds[c]
        o = _rms_rows(o, on_ref[...]) * _silu(gate_ref[rs, :])
        o_ref[rs, :] = o.astype(BF16)
    for sq in range(n_sq):
        s_scr[sq] = ss[sq]

    @pl.when(i == pl.num_programs(2) - 1)
    def _():
        so_ref[...] = s_scr[...]


def _gla_group(pd, s0, row0, n_seq, t, w_gk2, b_gk, onorm, *, tb):
    chunk = D_CHUNK if t % D_CHUNK == 0 else t
    assert chunk & (chunk - 1) == 0
    spb, nb, n_groups, rb = _seq_blocking(row0, n_seq, t, tb, chunk)
    w2 = _pad_rows(w_gk2, LANES).astype(BF16)
    tok = lambda n, cb: pl.BlockSpec((tb, n), cb)
    st = pl.BlockSpec((spb, None, D_DK, D_DV), lambda s, h, i: (s, h, 0, 0))
    return pl.pallas_call(
        functools.partial(_gla_kernel, chunk=chunk),
        grid=(n_groups, D_HEADS, nb),
        in_specs=[
            tok(D_DK, lambda s, h, i: (rb + s * nb + i, h)),
            tok(D_DK, lambda s, h, i: (rb + s * nb + i, D_K // D_DK + h)),
            tok(D_DV, lambda s, h, i: (rb + s * nb + i, PD_V // D_DV + h)),
            tok(D_DV, lambda s, h, i: (rb + s * nb + i, PD_GATE // D_DV + h)),
            tok(LANES, lambda s, h, i: (rb + s * nb + i, PD_GLR // LANES)),
            pl.BlockSpec((LANES, D_DK), lambda s, h, i: (0, h)),
            pl.BlockSpec((1, D_DK), lambda s, h, i: (0, h)),
            pl.BlockSpec((1, D_DV), lambda s, h, i: (0, 0)),
            st,
        ],
        out_specs=[tok(D_DV, lambda s, h, i: (s * nb + i, h)), st],
        out_shape=[jax.ShapeDtypeStruct((n_seq * t, D_V), BF16),
                   jax.ShapeDtypeStruct((n_seq, D_HEADS, D_DK, D_DV), F32)],
        scratch_shapes=[pltpu.VMEM((spb, D_DK, D_DV), F32)],
        compiler_params=_cparams(("parallel", "parallel", "arbitrary")),
        name="gla_chunks",
    )(pd, pd, pd, pd, pd, w2, b_gk.reshape(1, D_K), onorm.reshape(1, D_DV), s0)


TM = 768
TM_SMALL = 256


def _seqs_per_block(n_seq, want):
    return max(s for s in range(1, want + 1) if n_seq % s == 0)


def _last_rows(a, row0, n_seq, t, k, n_cols):
    picks = [lax.slice(a, (row0 + t - k + j, 0), (row0 + n_seq * t, n_cols), (t, 1)) for j in range(k)]
    return jnp.stack(picks, axis=1)


def _cross_block(x, layer, n_p, t_p, n_s, t_s, mem_prompt, cache_mem_k, cache_mem_v, norm_mem_w, norm_cross_w,
                 x_w_q, x_w_k, x_w_v, x_w_o, x_q_norm, x_k_norm):
    d = x.shape[1]
    mp = n_p * t_p
    wkv = jnp.concatenate([x_w_k[layer], x_w_v[layer]], axis=1).astype(BF16)
    hw = jnp.concatenate([jnp.tile(x_k_norm[layer], X_HEADS), jnp.ones((X_W,), F32)]).reshape(1, 2 * X_W)
    mem = mem_prompt.reshape(n_p * N_MEM, d)
    kv = _norm_matmul(mem, norm_mem_w[layer], wkv, tm=min(512, n_p * N_MEM), tn=X_W, head_w=hw,
                      n_norm_cols=X_W, gsize=X_DH)
    mk_p = kv[:, :X_W].reshape(n_p, N_MEM, X_W)
    mv_p = kv[:, X_W:].reshape(n_p, N_MEM, X_W)
    qx = _norm_matmul(x, norm_cross_w[layer], x_w_q[layer].astype(BF16), tm=TM, tn=X_W)
    ca_p = _cross_attention(qx, 0, n_p, t_p, mk_p, mv_p, x_q_norm[layer], tq=min(512, t_p))
    ca_s = _cross_attention(qx, mp, n_s, t_s, cache_mem_k, cache_mem_v, x_q_norm[layer], tq=t_s, mem_layer=layer)
    ca = jnp.concatenate([ca_p, ca_s], axis=0)
    x = _matmul_residual([ca], [x_w_o[layer].astype(BF16)], x, tm=TM, tn=1024)
    return x, mk_p.reshape(n_p, N_MEM, X_HEADS, X_DH), mv_p.reshape(n_p, N_MEM, X_HEADS, X_DH)


def kernel(x_prompt, x_sample, cache_diff_k, cache_diff_v, state_rwkv, state_rwkv_shift, cache_mem_k, cache_mem_v, state_gdn, state_gdn_conv, state_gla, page_table, mem_prompt, norm_mix_w, norm_cross_w, norm_mem_w, norm_ffn_w, w_in_even, w_out_even, a_q_norm, a_k_norm, a_lam_q1, a_lam_k1, a_lam_q2, a_lam_k2, a_subln, b_mu, b_w0, b_w2, b_a0, b_a2, b_g2, b_k_k, b_k_a, b_r_k, b_lnx_w, b_lnx_b, ffd_w_gate, ffd_w_up, ffd_w_down, w_in_odd, w_out_odd, c_conv_w, c_a_log, c_dt_bias, c_onorm, d_w_gk2, d_b_gk, d_onorm, moe_w_router, moe_w_gate, moe_w_up, moe_w_down, x_w_q, x_w_k, x_w_v, x_w_o, x_q_norm, x_k_norm):
    n_p, t_p, d = x_prompt.shape
    n_s, t_s, _ = x_sample.shape
    mp, ms = n_p * t_p, n_s * t_s
    depth = norm_mix_w.shape[0]
    x = jnp.concatenate([x_prompt.reshape(mp, d), x_sample.reshape(ms, d)], axis=0)
    assert (mp + ms) % TM == 0 and (mp + ms) % TM_SMALL == 0
    page = cache_diff_k.shape[2]
    slopes = jnp.exp2(-(8.0 / A_HEADS) * jnp.arange(1, A_HEADS + 1, dtype=F32))
    cross_w = (mem_prompt, cache_mem_k, cache_mem_v, norm_mem_w, norm_cross_w, x_w_q, x_w_k, x_w_v, x_w_o,
               x_q_norm, x_k_norm)
    dk_p, dv_p, dk_s, dv_s, rw_p, rw_s, sh_p, sh_s = [], [], [], [], [], [], [], []
    mk_l, mv_l, gd_p, gd_s, cv_p, cv_s, gl_p, gl_s = [], [], [], [], [], [], [], []
    for layer in range(depth):
        i = layer // 2
        if layer % 2 == 0:
            lam_init = 0.8 - 0.6 * math.exp(-0.3 * layer)
            w_in = w_in_even[i]
            hw = jnp.concatenate([jnp.tile(a_q_norm[i], A_Q // A_DH), jnp.tile(a_k_norm[i], A_Q // A_DH),
                                  jnp.ones((A_COLS - 2 * A_Q,), F32)]).reshape(1, A_COLS)
            pa = _norm_matmul(x, norm_mix_w[layer], w_in[:, :A_COLS].astype(BF16), tm=TM, tn=1024, head_w=hw,
                              n_norm_cols=2 * A_Q, gsize=A_DH)
            pb = _norm_matmul(x, norm_mix_w[layer], _rwkv_cols(w_in[:, A_COLS:]).astype(BF16), tm=TM,
                              tn=PB_COLS // 2)
            lam_stack = jnp.stack([a_lam_q1[i], a_lam_k1[i], a_lam_q2[i], a_lam_k2[i]])
            oa_p = _diff_attention_prompt(pa, n_p, t_p, slopes, lam_stack, a_subln[i], lam_init, tq=min(256, t_p))
            oa_s = _diff_attention_sample(pa, mp, n_s, t_s, cache_diff_k, cache_diff_v, i, page_table, lam_stack,
                                          a_subln[i], lam_init)
            ob, st_p, st_s = _rwkv_mixer(pb, n_p, t_p, n_s, t_s, state_rwkv_shift[i], state_rwkv[i], b_mu[i],
                                         b_w0[i], b_w2[i], b_a0[i], b_a2[i], b_g2[i], b_k_k[i], b_k_a[i], b_r_k[i],
                                         b_lnx_w[i], b_lnx_b[i], tm=TM_SMALL, tb=min(256, t_p))
            dk_p.append(pa[:mp, A_Q:2 * A_Q].reshape(n_p, t_p, A_HEADS, 2 * A_DH))
            dv_p.append(pa[:mp, 2 * A_Q:].reshape(n_p, t_p, A_HEADS, 2 * A_DH))
            dk_s.append(pa[mp:, A_Q:2 * A_Q].reshape(n_s, t_s, A_HEADS, 2 * A_DH))
            dv_s.append(pa[mp:, 2 * A_Q:].reshape(n_s, t_s, A_HEADS, 2 * A_DH))
            rw_p.append(st_p)
            rw_s.append(st_s)
            sh_p.append(_rwkv_cols_inv(_last_rows(pb, 0, n_p, t_p, 1, PB_COLS)))
            sh_s.append(_rwkv_cols_inv(_last_rows(pb, mp, n_s, t_s, 1, PB_COLS)))
            oa = jnp.concatenate([oa_p, oa_s], axis=0)
            w_out = w_out_even[i].astype(BF16)
            x = _matmul_residual([oa, ob], [w_out[:A_Q], w_out[A_Q:]], x, tm=TM, tn=1024)
        else:
            w_in = w_in_odd[i]
            c_cols = C_CONV_CH + C_V + 2 * C_V_HEADS
            zc = jnp.zeros((d, LANES - C_V_HEADS), F32)
            wc = jnp.concatenate([w_in[:, :PC_BETA], w_in[:, PC_BETA:PC_BETA + C_V_HEADS], zc,
                                  w_in[:, PC_BETA + C_V_HEADS:c_cols], zc], axis=1).astype(BF16)
            wd_ = jnp.concatenate([w_in[:, c_cols:], jnp.zeros((d, LANES - D_GATE_LORA), F32)], axis=1).astype(BF16)
            pc = _norm_matmul(x, norm_mix_w[layer], wc, tm=TM, tn=PC_COLS // 2)
            pd = _norm_matmul(x, norm_mix_w[layer], wd_, tm=TM, tn=PD_COLS // 5)
            gdn_w = (c_conv_w[i], c_a_log[i], c_dt_bias[i], c_onorm[i])
            oc_p, gs_p = _gdn_group(pc, jnp.zeros((n_p, C_CONV - 1, C_CONV_CH), F32),
                                    jnp.zeros((n_p, C_V_HEADS, C_DH, C_DH), F32), 0, n_p, t_p, *gdn_w,
                                    tm=min(512, t_p), tb=min(512, t_p))
            oc_s, gs_s = _gdn_group(pc, state_gdn_conv[i], state_gdn[i], mp, n_s, t_s, *gdn_w, tm=t_s,
                                    tb=_seqs_per_block(n_s, 8) * t_s)
            gla_w = (d_w_gk2[i], d_b_gk[i], d_onorm[i])
            od_p, ls_p = _gla_group(pd, jnp.zeros((n_p, D_HEADS, D_DK, D_DV), F32), 0, n_p, t_p, *gla_w,
                                    tb=min(256, t_p))
            od_s, ls_s = _gla_group(pd, state_gla[i], mp, n_s, t_s, *gla_w, tb=_seqs_per_block(n_s, 8) * t_s)
            gd_p.append(gs_p)
            gd_s.append(gs_s)
            cv_p.append(_last_rows(pc, 0, n_p, t_p, C_CONV - 1, C_CONV_CH))
            cv_s.append(_last_rows(pc, mp, n_s, t_s, C_CONV - 1, C_CONV_CH))
            gl_p.append(ls_p)
            gl_s.append(ls_s)
            oc = jnp.concatenate([oc_p, oc_s], axis=0)
            od = jnp.concatenate([od_p, od_s], axis=0)
            w_out = w_out_odd[i].astype(BF16)
            x = _matmul_residual([oc, od], [w_out[:C_V], w_out[C_V:]], x, tm=TM, tn=1024)
        x, mk_p, mv_p = _cross_block(x, layer, n_p, t_p, n_s, t_s, *cross_w)
        mk_l.append(mk_p)
        mv_l.append(mv_p)
        if layer % 2 == 0:
            x = _swiglu_ffn(x, norm_ffn_w[layer], ffd_w_gate[i].astype(BF16), ffd_w_up[i].astype(BF16),
                            ffd_w_down[i].astype(BF16), tm=TM, tf=512)
        else:
            y_p, y_s = _moe_ffn(x, norm_ffn_w[layer], moe_w_router[i], moe_w_gate, moe_w_up, moe_w_down, i,
                                tm=TM, tf=256, tc=TM_SMALL, split=mp)
            if layer + 1 < depth:
                x = jnp.concatenate([y_p, y_s], axis=0)
    if depth % 2 == 1:
        y_p, y_s = x[:mp], x[mp:]
    st = jnp.stack
    return (y_p.reshape(n_p, t_p, d), y_s.reshape(n_s, t_s, d), st(dk_p), st(dv_p), st(dk_s), st(dv_s),
            st(rw_p), st(rw_s), st(sh_p), st(sh_s), st(mk_l), st(mv_l), st(gd_p), st(gd_s), st(cv_p), st(cv_s),
            st(gl_p), st(gl_s))
```

```python
import functools
import math

import jax
import jax.numpy as jnp
from jax import lax
from jax.experimental import pallas as pl
from jax.experimental.pallas import tpu as pltpu

F32 = jnp.float32
BF16 = jnp.bfloat16
NEG_INF = float("-inf")

D_MODEL = 2048
A_HEADS = 8
A_DH = 64
A_Q = 1024
A_COLS = 3072
B_HEADS = 16
B_DH = 64
B_W = 1024
B_DECAY_LORA = 96
B_AAA_LORA = 96
B_GATE_LORA = 256
B_COLS = 3520
B_GN_EPS = 64e-5
C_QK_HEADS = 4
C_V_HEADS = 8
C_DH = 128
C_CONV = 4
C_QK = 512
C_V = 1024
C_CONV_CH = 2048
C_CHUNK = 64
D_HEADS = 4
D_DK = 128
D_DV = 256
D_K = 512
D_V = 1024
D_GATE_LORA = 16
D_GATE_NORM = 16.0
D_CHUNK = 16
N_MEM = 256
X_HEADS = 4
X_DH = 128
X_W = 512
FF_DENSE = 5632
N_EXPERTS = 8
FF_EXPERT = 2816
EPS = 1e-6

LANES = 128
VMEM_LIMIT = 56 * 1024 * 1024


def _cparams(sem):
    return pltpu.CompilerParams(dimension_semantics=sem, vmem_limit_bytes=VMEM_LIMIT)


def _sigmoid(x):
    return 1.0 / (1.0 + jnp.exp(-x))


def _silu(x):
    return x * _sigmoid(x)


def _softplus(x):
    return jnp.maximum(x, 0.0) + jnp.log(1.0 + jnp.exp(-jnp.abs(x)))


def _rms_rows(x, w):
    ms = jnp.mean(x * x, axis=-1, keepdims=True)
    return x * lax.rsqrt(ms + EPS) * w


def _group_sum(x, gsize):
    lane = lax.broadcasted_iota(jnp.int32, (1, LANES), 1)
    outs = []
    for c in range(x.shape[1] // LANES):
        xc = x[:, c * LANES:(c + 1) * LANES]
        if gsize == LANES:
            ss = jnp.broadcast_to(jnp.sum(xc, axis=-1, keepdims=True), xc.shape)
        else:
            lo = jnp.sum(jnp.where(lane < 64, xc, 0.0), axis=-1, keepdims=True)
            hi = jnp.sum(jnp.where(lane >= 64, xc, 0.0), axis=-1, keepdims=True)
            ss = jnp.where(lane < 64, lo, hi)
        outs.append(ss)
    return outs[0] if len(outs) == 1 else jnp.concatenate(outs, axis=1)


def _group_norm_scale(x, gsize, inv_n, eps):
    return x * lax.rsqrt(_group_sum(x * x, gsize) * inv_n + eps)


def _split3(a):
    a1 = a.astype(BF16)
    r1 = a - a1.astype(F32)
    a2 = r1.astype(BF16)
    a3 = (r1 - a2.astype(F32)).astype(BF16)
    return a1, a2, a3


def _dot_f32(a, b):
    a1 = a.astype(BF16)
    a2 = (a - a1.astype(F32)).astype(BF16)
    b1 = b.astype(BF16)
    b2 = (b - b1.astype(F32)).astype(BF16)
    d = lambda x, y: jnp.dot(x, y, preferred_element_type=F32)
    return d(a1, b2) + d(a2, b1) + d(a1, b1)


def _norm_mm_kernel(x_ref, nw_ref, w_ref, hw_ref, o_ref, h_scr, *, n_norm_tiles, gsize):
    j = pl.program_id(1)

    @pl.when(j == 0)
    def _():
        h_scr[...] = _rms_rows(x_ref[...], nw_ref[...]).astype(BF16)

    acc = jnp.dot(h_scr[...], w_ref[...], preferred_element_type=F32)
    if n_norm_tiles == 0:
        o_ref[...] = acc
    else:
        @pl.when(j < n_norm_tiles)
        def _():
            o_ref[...] = _group_norm_scale(acc, gsize, 1.0 / gsize, EPS) * hw_ref[...]

        @pl.when(j >= n_norm_tiles)
        def _():
            o_ref[...] = acc


def _norm_matmul(x, nw, w, *, tm, tn, head_w=None, n_norm_cols=0, gsize=LANES):
    m, k = x.shape
    n = w.shape[1]
    assert m % tm == 0 and n % tn == 0 and n_norm_cols % tn == 0
    if head_w is None:
        head_w = jnp.ones((1, n), F32)
    kern = functools.partial(_norm_mm_kernel, n_norm_tiles=n_norm_cols // tn, gsize=gsize)
    return pl.pallas_call(
        kern,
        grid=(m // tm, n // tn),
        in_specs=[
            pl.BlockSpec((tm, k), lambda i, j: (i, 0)),
            pl.BlockSpec((1, k), lambda i, j: (0, 0)),
            pl.BlockSpec((k, tn), lambda i, j: (0, j)),
            pl.BlockSpec((1, tn), lambda i, j: (0, j)),
        ],
        out_specs=pl.BlockSpec((tm, tn), lambda i, j: (i, j)),
        out_shape=jax.ShapeDtypeStruct((m, n), F32),
        scratch_shapes=[pltpu.VMEM((tm, k), BF16)],
        compiler_params=_cparams(("parallel", "arbitrary")),
        name="norm_matmul",
    )(x, nw.reshape(1, k), w, head_w)


def _mm_res_kernel(*refs, n_in):
    x_ref = refs[2 * n_in]
    o_ref = refs[2 * n_in + 1]
    acc = x_ref[...]
    for t in range(n_in):
        acc = acc + jnp.dot(refs[t][...], refs[n_in + t][...], preferred_element_type=F32)
    o_ref[...] = acc


def _matmul_residual(acts, weights, x, *, tm, tn):
    m, n = x.shape
    n_in = len(acts)
    in_specs = [pl.BlockSpec((tm, a.shape[1]), lambda i, j: (i, 0)) for a in acts]
    in_specs += [pl.BlockSpec((w.shape[0], tn), lambda i, j: (0, j)) for w in weights]
    in_specs += [pl.BlockSpec((tm, tn), lambda i, j: (i, j))]
    return pl.pallas_call(
        functools.partial(_mm_res_kernel, n_in=n_in),
        grid=(m // tm, n // tn),
        in_specs=in_specs,
        out_specs=pl.BlockSpec((tm, tn), lambda i, j: (i, j)),
        out_shape=jax.ShapeDtypeStruct((m, n), F32),
        compiler_params=_cparams(("parallel", "arbitrary")),
        name="matmul_residual",
    )(*acts, *weights, x)


def _swiglu_kernel(x_ref, nw_ref, wg_ref, wu_ref, wd_ref, o_ref, h_scr):
    f = pl.program_id(1)

    @pl.when(f == 0)
    def _():
        x = x_ref[...]
        h_scr[...] = _rms_rows(x, nw_ref[...]).astype(BF16)
        o_ref[...] = x

    h = h_scr[...]
    g = jnp.dot(h, wg_ref[...], preferred_element_type=F32)
    u = jnp.dot(h, wu_ref[...], preferred_element_type=F32)
    a = (_silu(g) * u).astype(BF16)
    o_ref[...] += jnp.dot(a, wd_ref[...], preferred_element_type=F32)


def _swiglu_ffn(x, nw, wg, wu, wd, *, tm, tf):
    m, d = x.shape
    ff = wg.shape[1]
    return pl.pallas_call(
        _swiglu_kernel,
        grid=(m // tm, ff // tf),
        in_specs=[
            pl.BlockSpec((tm, d), lambda i, f: (i, 0)),
            pl.BlockSpec((1, d), lambda i, f: (0, 0)),
            pl.BlockSpec((d, tf), lambda i, f: (0, f)),
            pl.BlockSpec((d, tf), lambda i, f: (0, f)),
            pl.BlockSpec((tf, d), lambda i, f: (f, 0)),
        ],
        out_specs=pl.BlockSpec((tm, d), lambda i, f: (i, 0)),
        out_shape=jax.ShapeDtypeStruct((m, d), F32),
        scratch_shapes=[pltpu.VMEM((tm, d), BF16)],
        compiler_params=_cparams(("parallel", "arbitrary")),
        name="swiglu_ffn",
    )(x, nw.reshape(1, d), wg, wu, wd)


MOE_TILE = 768
ROW_SLABS = D_MODEL // LANES


def _rows_to_slabs(ref, x):
    ref[...] = jnp.swapaxes(jnp.stack([x[:, j * LANES:(j + 1) * LANES] for j in range(ROW_SLABS)], axis=0), 0, 1)


def _slabs_to_rows(ref):
    xt = jnp.swapaxes(ref[...], 0, 1)
    return jnp.concatenate([xt[j] for j in range(ROW_SLABS)], axis=1)


def _moe_router_kernel(x_ref, nw_ref, wr_ref, h_ref, info_ref):
    tm = x_ref.shape[0]
    lane = lax.broadcasted_iota(jnp.int32, (tm, LANES), 1)
    hf = _rms_rows(x_ref[...], nw_ref[...])
    _rows_to_slabs(h_ref, hf)
    logits = jnp.where(lane < N_EXPERTS, _dot_f32(hf, wr_ref[...]), NEG_INF)
    m1 = jnp.max(logits, axis=-1, keepdims=True)
    i1 = jnp.min(jnp.where(logits == m1, lane, LANES), axis=-1, keepdims=True)
    rest = jnp.where(lane == i1, NEG_INF, logits)
    m2 = jnp.max(rest, axis=-1, keepdims=True)
    i2 = jnp.min(jnp.where(rest == m2, lane, LANES), axis=-1, keepdims=True)
    e2 = jnp.exp(m2 - m1)
    vals = [i1.astype(F32), i2.astype(F32), 1.0 / (1.0 + e2), e2 / (1.0 + e2)]
    info = jnp.zeros((tm, LANES), F32)
    for k, val in enumerate(vals):
        info = jnp.where(lane == k, val, info)
    info_ref[...] = info


def _moe_gather_kernel(src_ref, nu_ref, h_ref, xs_ref, sem):
    t = pl.program_id(0)
    tg = xs_ref.shape[0]

    def copy(r):
        return pltpu.make_async_copy(h_ref.at[src_ref[t * tg + r]], xs_ref.at[r], sem.at[0])

    def issue(r, carry):
        copy(r).start()
        return carry

    def drain(r, carry):
        copy(r).wait()
        return carry

    @pl.when(t < nu_ref[0])
    def _():
        lax.fori_loop(0, tg, issue, 0, unroll=8)
        lax.fori_loop(0, tg, drain, 0, unroll=8)

    @pl.when(t >= nu_ref[0])
    def _():
        xs_ref[...] = jnp.zeros(xs_ref.shape, F32)


def _moe_expert_kernel(te_ref, nu_ref, xs_ref, wg_ref, wu_ref, wd_ref, y_ref, xb_scr, acc_scr):
    del te_ref
    t = pl.program_id(0)
    f = pl.program_id(1)

    @pl.when(f == 0)
    def _():
        xb_scr[...] = _slabs_to_rows(xs_ref).astype(BF16)
        acc_scr[...] = jnp.zeros(acc_scr.shape, F32)

    @pl.when(t < nu_ref[0])
    def _():
        xb = xb_scr[...]
        g = jnp.dot(xb, wg_ref[...].astype(BF16), preferred_element_type=F32)
        u = jnp.dot(xb, wu_ref[...].astype(BF16), preferred_element_type=F32)
        acc_scr[...] += jnp.dot((_silu(g) * u).astype(BF16), wd_ref[...].astype(BF16),
                                preferred_element_type=F32)

    @pl.when(f == pl.num_programs(1) - 1)
    def _():
        _rows_to_slabs(y_ref, acc_scr[...])


def _moe_combine_kernel(dest_ref, x_ref, info_ref, y_ref, o1_ref, o2_ref, ybuf, sem, *, n_first):
    i = pl.program_id(0)
    tc = x_ref.shape[0]
    n = dest_ref.shape[0] // 2

    def copy(r, slot):
        return pltpu.make_async_copy(y_ref.at[dest_ref[slot * n + i * tc + r]], ybuf.at[slot, r], sem.at[slot])

    def issue(r, carry):
        copy(r, 0).start()
        copy(r, 1).start()
        return carry

    def drain(r, carry):
        copy(r, 0).wait()
        copy(r, 1).wait()
        return carry

    lax.fori_loop(0, tc, issue, 0, unroll=8)
    lax.fori_loop(0, tc, drain, 0, unroll=8)
    info = info_ref[...]
    y1 = _slabs_to_rows(ybuf.at[0])
    y2 = _slabs_to_rows(ybuf.at[1])
    res = x_ref[...] + info[:, 2:3] * y1 + info[:, 3:4] * y2

    @pl.when(i < n_first)
    def _():
        o1_ref[...] = res

    @pl.when(i >= n_first)
    def _():
        o2_ref[...] = res


def _moe_ffn(x, nw, w_router, wg, wu, wd, w_layer, *, tm, tf, tc, split):
    m, d = x.shape
    _, ne, _, ff = wg.shape
    tg = MOE_TILE
    assert d == D_MODEL and m % tm == 0 and m % tc == 0 and ff % tf == 0
    n_tiles = -(-(2 * m + ne * (tg - 1)) // tg)
    p_rows = n_tiles * tg
    wr = jnp.zeros((d, LANES), F32).at[:, :ne].set(w_router)
    h3, info = pl.pallas_call(
        _moe_router_kernel,
        grid=(m // tm,),
        in_specs=[pl.BlockSpec((tm, d), lambda i: (i, 0)), pl.BlockSpec((1, d), lambda i: (0, 0)),
                  pl.BlockSpec((d, LANES), lambda i: (0, 0))],
        out_specs=[pl.BlockSpec((tm, ROW_SLABS, LANES), lambda i: (i, 0, 0)),
                   pl.BlockSpec((tm, LANES), lambda i: (i, 0))],
        out_shape=[jax.ShapeDtypeStruct((m, ROW_SLABS, LANES), F32), jax.ShapeDtypeStruct((m, LANES), F32)],
        compiler_params=_cparams(("parallel",)),
        name="moe_router",
    )(x, nw.reshape(1, d), wr)

    e12 = info[:, :2].astype(jnp.int32)
    onehot = (e12[:, :1] == jnp.arange(ne)) | (e12[:, 1:2] == jnp.arange(ne))
    csum = jnp.cumsum(onehot.astype(jnp.int32), axis=0)
    rank = csum - onehot
    padded = (csum[-1] + tg - 1) // tg * tg
    gend = jnp.cumsum(padded)
    base = (gend - padded)[None, :] + rank
    dest = jnp.concatenate([jnp.take_along_axis(base, e12[:, :1], axis=1)[:, 0],
                            jnp.take_along_axis(base, e12[:, 1:2], axis=1)[:, 0]]).astype(jnp.int32)
    tile_expert = jnp.minimum(jnp.sum(jnp.arange(n_tiles)[:, None] * tg >= gend[None, :], axis=1),
                              ne - 1).astype(jnp.int32)
    n_used = (gend[-1:] // tg).astype(jnp.int32)

    any_spec = pl.BlockSpec(memory_space=pl.ANY)
    tok_ids = jnp.tile(jnp.arange(m, dtype=jnp.int32), 2)
    src = jnp.zeros((p_rows,), jnp.int32).at[dest].set(tok_ids, unique_indices=True, mode="promise_in_bounds")
    xs3 = pl.pallas_call(
        _moe_gather_kernel,
        grid_spec=pltpu.PrefetchScalarGridSpec(
            num_scalar_prefetch=2, grid=(n_tiles,), in_specs=[any_spec],
            out_specs=pl.BlockSpec((tg, ROW_SLABS, LANES), lambda t, sr, nu: (t, 0, 0)),
            scratch_shapes=[pltpu.SemaphoreType.DMA((1,))]),
        out_shape=jax.ShapeDtypeStruct((p_rows, ROW_SLABS, LANES), F32),
        compiler_params=_cparams(("arbitrary",)),
        name="moe_gather",
    )(src, n_used, h3)

    last = lambda t, nu: jnp.minimum(t, nu[0] - 1)
    y3 = pl.pallas_call(
        _moe_expert_kernel,
        grid_spec=pltpu.PrefetchScalarGridSpec(
            num_scalar_prefetch=2,
            grid=(n_tiles, ff // tf),
            in_specs=[
                pl.BlockSpec((tg, ROW_SLABS, LANES), lambda t, f, te, nu: (last(t, nu), 0, 0)),
                pl.BlockSpec((None, None, d, tf), lambda t, f, te, nu: (w_layer, te[last(t, nu)], 0, f)),
                pl.BlockSpec((None, None, d, tf), lambda t, f, te, nu: (w_layer, te[last(t, nu)], 0, f)),
                pl.BlockSpec((None, None, tf, d), lambda t, f, te, nu: (w_layer, te[last(t, nu)], f, 0)),
            ],
            out_specs=pl.BlockSpec((tg, ROW_SLABS, LANES), lambda t, f, te, nu: (t, 0, 0)),
            scratch_shapes=[pltpu.VMEM((tg, d), BF16), pltpu.VMEM((tg, d), F32)]),
        out_shape=jax.ShapeDtypeStruct((p_rows, ROW_SLABS, LANES), F32),
        compiler_params=_cparams(("arbitrary", "arbitrary")),
        name="moe_experts",
    )(tile_expert, n_used, xs3, wg, wu, wd)

    assert split % tc == 0 and 0 < split < m
    n_first = split // tc
    return pl.pallas_call(
        functools.partial(_moe_combine_kernel, n_first=n_first),
        grid_spec=pltpu.PrefetchScalarGridSpec(
            num_scalar_prefetch=1,
            grid=(m // tc,),
            in_specs=[pl.BlockSpec((tc, d), lambda i, de: (i, 0)), pl.BlockSpec((tc, LANES), lambda i, de: (i, 0)),
                      any_spec],
            out_specs=[pl.BlockSpec((tc, d), lambda i, de: (jnp.minimum(i, n_first - 1), 0)),
                       pl.BlockSpec((tc, d), lambda i, de: (jnp.maximum(i - n_first, 0), 0))],
            scratch_shapes=[pltpu.VMEM((2, tc, ROW_SLABS, LANES), F32), pltpu.SemaphoreType.DMA((2,))]),
        out_shape=[jax.ShapeDtypeStruct((split, d), F32), jax.ShapeDtypeStruct((m - split, d), F32)],
        compiler_params=_cparams(("arbitrary",)),
        name="moe_combine",
    )(dest, x, info, y3)


def _diff_lambda_vec(lam_ref, lam_init):
    lv = lam_ref[...]
    d1 = jnp.sum(lv[0:1, :] * lv[1:2, :], axis=-1, keepdims=True)
    d2 = jnp.sum(lv[2:3, :] * lv[3:4, :], axis=-1, keepdims=True)
    return jnp.exp(d1) - jnp.exp(d2) + lam_init


def _softmax_rows(s):
    m = jnp.max(s, axis=-1, keepdims=True)
    e = jnp.exp(s - m)
    return e, jnp.sum(e, axis=-1, keepdims=True)


def _diffattn_prompt_kernel(slope_ref, q_ref, k_ref, v_ref, lam_ref, subln_ref, o_ref, *, tq, lam_init):
    t = q_ref.shape[0]
    h = pl.program_id(1)
    slope = slope_ref[h]
    lane = lax.broadcasted_iota(jnp.int32, (1, LANES), 1)
    lam = _diff_lambda_vec(lam_ref, lam_init)
    kb = k_ref[...].astype(BF16)
    vb = v_ref[...].astype(BF16)
    nt = (((1,), (1,)), ((), ()))
    n_blocks = t // tq
    row = lax.broadcasted_iota(jnp.int32, (tq, t), 0) + (t - tq)
    col = lax.broadcasted_iota(jnp.int32, (tq, t), 1)
    dist = (row - col).astype(F32)
    bias_all = jnp.where(dist >= 0.0, -slope * dist, NEG_INF)
    for qi in range(n_blocks):
        n_keys = (qi + 1) * tq
        q = q_ref[qi * tq:(qi + 1) * tq, :] * (A_DH ** -0.5)
        q1 = jnp.where(lane < A_DH, q, 0.0).astype(BF16)
        q2 = jnp.where(lane >= A_DH, q, 0.0).astype(BF16)
        kk = kb[:n_keys]
        vv = vb[:n_keys]
        off = (n_blocks - 1 - qi) * tq
        bias = bias_all[:, off:off + n_keys]
        e1, l1 = _softmax_rows(lax.dot_general(q1, kk, nt, preferred_element_type=F32) + bias)
        e2, l2 = _softmax_rows(lax.dot_general(q2, kk, nt, preferred_element_type=F32) + bias)
        o = jnp.dot(e1.astype(BF16), vv, preferred_element_type=F32) * (1.0 / l1)
        o = o - jnp.dot(e2.astype(BF16), vv, preferred_element_type=F32) * (lam / l2)
        o = _rms_rows(o, subln_ref[...]) * (1.0 - lam_init)
        o_ref[qi * tq:(qi + 1) * tq, :] = o.astype(BF16)


def _diff_attention_prompt(pa, n_batch, t, slopes, lam_stack, subln, lam_init, *, tq):
    kern = functools.partial(_diffattn_prompt_kernel, tq=tq, lam_init=lam_init)
    return pl.pallas_call(
        kern,
        grid=(n_batch, A_HEADS),
        in_specs=[
            pl.BlockSpec(memory_space=pltpu.SMEM),
            pl.BlockSpec((t, LANES), lambda b, h: (b, h)),
            pl.BlockSpec((t, LANES), lambda b, h: (b, A_HEADS + h)),
            pl.BlockSpec((t, LANES), lambda b, h: (b, 2 * A_HEADS + h)),
            pl.BlockSpec((4, A_DH), lambda b, h: (0, 0)),
            pl.BlockSpec((1, LANES), lambda b, h: (0, 0)),
        ],
        out_specs=pl.BlockSpec((t, LANES), lambda b, h: (b, h)),
        out_shape=jax.ShapeDtypeStruct((n_batch * t, A_HEADS * LANES), BF16),
        compiler_params=_cparams(("parallel", "parallel")),
        name="diff_attention_prompt",
    )(slopes, pa, pa, pa, lam_stack, subln.reshape(1, LANES))


PAGES_PER_STEP = 16


def _diffattn_sample_kernel(pt_ref, q_ref, kn_ref, vn_ref, *rest, past, lam_init):
    del pt_ref
    npg = PAGES_PER_STEP
    k_refs, v_refs = rest[:npg], rest[npg:2 * npg]
    lam_ref, subln_ref, o_ref, qq_scr, m_scr, l_scr, acc_scr, bias_scr = rest[2 * npg:]
    g = pl.program_id(1)
    ts = q_ref.shape[0]
    page = k_refs[0].shape[0]
    lane = lax.broadcasted_iota(jnp.int32, (1, LANES), 1)
    hsub = lax.broadcasted_iota(jnp.int32, (A_HEADS, LANES), 0)
    col = lax.broadcasted_iota(jnp.int32, (A_HEADS, LANES), 1)
    valid = (col // (2 * ts)) == hsub
    pos_q = past + jnp.bitwise_and(col, ts - 1)
    slope = jnp.exp2(-(8.0 / A_HEADS) * (hsub + 1).astype(F32))
    nt = (((1,), (1,)), ((), ()))
    tn = (((0,), (0,)), ((), ()))

    @pl.when(g == 0)
    def _():
        rows = []
        for h in range(A_HEADS):
            q = q_ref[:, h * LANES:(h + 1) * LANES] * (A_DH ** -0.5)
            rows += [jnp.where(lane < A_DH, q, 0.0), jnp.where(lane >= A_DH, q, 0.0)]
        qq_scr[...] = jnp.concatenate(rows, axis=0).astype(BF16)
        m_scr[...] = jnp.where(valid, NEG_INF, 0.0)
        l_scr[...] = jnp.zeros(l_scr.shape, F32)
        acc_scr[...] = jnp.zeros(acc_scr.shape, F32)

    def key_bias(nk, key0, causal):
        kidx = lax.broadcasted_iota(jnp.int32, (nk * A_HEADS, LANES), 0) // A_HEADS
        dist = (pos_q[None] - (key0 + kidx).reshape(nk, A_HEADS, LANES)).astype(F32)
        keep = valid[None] & (dist >= 0.0) if causal else valid[None]
        return jnp.where(keep, -slope[None] * dist, NEG_INF)

    @pl.when(g == 0)
    def _():
        bias_scr[...] = key_bias(page, 0, False)

    def scores(k3):
        nk = k3.shape[0]
        k2 = k3.reshape(nk * A_HEADS, LANES).astype(BF16)
        return lax.dot_general(k2, qq_scr[...], nt, preferred_element_type=F32).reshape(nk, A_HEADS, LANES)

    def absorb(ss, v3s):
        m_old = m_scr[...]
        m_new = m_old
        for s in ss:
            m_new = jnp.maximum(m_new, jnp.max(s, axis=0))
        alpha = jnp.exp(m_old - m_new)
        l_new = alpha * l_scr[...]
        acc = jnp.sum(jnp.where(valid, alpha, 0.0), axis=0, keepdims=True) * acc_scr[...]
        for s, v3 in zip(ss, v3s):
            nk = s.shape[0]
            pe = jnp.exp(s - m_new[None])
            l_new = l_new + jnp.sum(pe, axis=0)
            v2 = v3.reshape(nk * A_HEADS, LANES).astype(BF16)
            acc = acc + lax.dot_general(v2, pe.reshape(nk * A_HEADS, LANES).astype(BF16), tn,
                                        preferred_element_type=F32)
        m_scr[...] = m_new
        l_scr[...] = l_new
        acc_scr[...] = acc

    absorb([scores(k_refs[j][...]) + bias_scr[...] + (slope * ((g * npg + j) * page).astype(F32))[None]
            for j in range(npg)], [v_refs[j][...] for j in range(npg)])

    @pl.when(g == pl.num_programs(1) - 1)
    def _():
        absorb([scores(kn_ref[...]) + key_bias(ts, past, True)], [vn_ref[...]])
        lam = _diff_lambda_vec(lam_ref, lam_init)
        l_row = jnp.sum(jnp.where(valid, l_scr[...], 0.0), axis=0, keepdims=True)
        ot = acc_scr[...] / l_row
        ot = ot - lam * pltpu.roll(ot, LANES - ts, 1)
        o = ot.T
        for h in range(A_HEADS):
            oh = _rms_rows(o[h * 2 * ts:h * 2 * ts + ts, :], subln_ref[...]) * (1.0 - lam_init)
            o_ref[:, h * LANES:(h + 1) * LANES] = oh.astype(BF16)


def _diff_attention_sample(pa, row0, n_batch, ts, cache_k, cache_v, layer_idx, page_table, lam_stack, subln,
                           lam_init):
    n_pages = page_table.shape[1]
    page = cache_k.shape[2]
    npg = PAGES_PER_STEP
    assert row0 % ts == 0 and n_pages % npg == 0 and 2 * ts * A_HEADS == LANES
    rb = row0 // ts
    w = A_HEADS * LANES
    new_rows = lambda c0: pa[row0:, c0:c0 + w].reshape(n_batch, ts, A_HEADS, LANES)
    kern = functools.partial(_diffattn_sample_kernel, past=n_pages * page, lam_init=lam_init)
    page_spec = lambda j: pl.BlockSpec((None, None, page, A_HEADS, LANES),
                                       lambda b, g, pt: (layer_idx, pt[b, g * npg + j], 0, 0, 0))
    new_spec = pl.BlockSpec((None, ts, A_HEADS, LANES), lambda b, g, pt: (b, 0, 0, 0))
    grid_spec = pltpu.PrefetchScalarGridSpec(
        num_scalar_prefetch=1,
        grid=(n_batch, n_pages // npg),
        in_specs=[pl.BlockSpec((ts, w), lambda b, g, pt: (rb + b, 0)), new_spec, new_spec]
        + [page_spec(j) for j in range(npg)] * 2
        + [pl.BlockSpec((4, A_DH), lambda b, g, pt: (0, 0)), pl.BlockSpec((1, LANES), lambda b, g, pt: (0, 0))],
        out_specs=pl.BlockSpec((ts, w), lambda b, g, pt: (b, 0)),
        scratch_shapes=[
            pltpu.VMEM((LANES, LANES), BF16),
            pltpu.VMEM((A_HEADS, LANES), F32),
            pltpu.VMEM((A_HEADS, LANES), F32),
            pltpu.VMEM((LANES, LANES), F32),
            pltpu.VMEM((page, A_HEADS, LANES), F32),
        ],
    )
    return pl.pallas_call(
        kern,
        grid_spec=grid_spec,
        out_shape=jax.ShapeDtypeStruct((n_batch * ts, w), BF16),
        compiler_params=_cparams(("parallel", "arbitrary")),
        name="diff_attention_sample",
    )(page_table, pa, new_rows(w), new_rows(2 * w), *([cache_k] * npg), *([cache_v] * npg), lam_stack,
      subln.reshape(1, LANES))


def _cross_attn_kernel(q_ref, k_ref, v_ref, qn_ref, o_ref):
    nt = (((1,), (1,)), ((), ()))
    if len(k_ref.shape) == 3:
        kt = jnp.swapaxes(k_ref[...], 0, 1).astype(BF16)
        vt = jnp.swapaxes(v_ref[...], 0, 1).astype(BF16)
        heads = [(kt[h], vt[h]) for h in range(X_HEADS)]
    else:
        heads = [(k_ref[:, h * LANES:(h + 1) * LANES].astype(BF16), v_ref[:, h * LANES:(h + 1) * LANES].astype(BF16))
                 for h in range(X_HEADS)]
    for h, (kh, vh) in enumerate(heads):
        sl = slice(h * LANES, (h + 1) * LANES)
        q = _rms_rows(q_ref[:, sl], qn_ref[...]) * (X_DH ** -0.5)
        s = lax.dot_general(q.astype(BF16), kh, nt, preferred_element_type=F32)
        e, l = _softmax_rows(s)
        p = (e * (1.0 / l)).astype(BF16)
        o_ref[:, sl] = jnp.dot(p, vh, preferred_element_type=F32).astype(BF16)


def _cross_attention(q, row0, n_batch, t, mk, mv, q_norm, *, tq, mem_layer=None):
    assert t % tq == 0 and row0 % tq == 0
    nq = t // tq
    rb = row0 // tq
    if mem_layer is None:
        mem_spec = pl.BlockSpec((None, N_MEM, X_W), lambda b, i: (b, 0, 0))
    else:
        mem_spec = pl.BlockSpec((None, None, N_MEM, X_HEADS, X_DH), lambda b, i: (mem_layer, b, 0, 0, 0))
    return pl.pallas_call(
        _cross_attn_kernel,
        grid=(n_batch, nq),
        in_specs=[
            pl.BlockSpec((tq, X_W), lambda b, i: (rb + b * nq + i, 0)),
            mem_spec,
            mem_spec,
            pl.BlockSpec((1, LANES), lambda b, i: (0, 0)),
        ],
        out_specs=pl.BlockSpec((tq, X_W), lambda b, i: (b * nq + i, 0)),
        out_shape=jax.ShapeDtypeStruct((n_batch * t, X_W), BF16),
        compiler_params=_cparams(("parallel", "parallel")),
        name="cross_attention",
    )(q, mk, mv, q_norm.reshape(1, LANES))


PB_LG = 3 * B_W
PB_LW = PB_LG + B_GATE_LORA
PB_LA = PB_LW + LANES
PB_COLS = PB_LA + LANES


def _swap_halves(x):
    tiles = [pltpu.roll(x[:, c * LANES:(c + 1) * LANES], LANES // 2, 1) for c in range(x.shape[1] // LANES)]
    return tiles[0] if len(tiles) == 1 else jnp.concatenate(tiles, axis=1)


def _rwkv_prep_kernel(pb_ref, halo_ref, start_ref, mu_ref, w0_ref, w2_ref, a0_ref, a2_ref, g2_ref, kk_ref, ka_ref,
                      r_ref, kf_ref, v_ref, g_ref, rt_ref, at_ref, bt_ref, kt_ref, bh_ref, kh_ref, pc_ref,
                      *, n_prompt_tiles, chunk_p, chunk_s, t_p, t_s):
    tm = pb_ref.shape[0]
    i = pl.program_id(0)
    pb = pb_ref[...]
    row = lax.broadcasted_iota(jnp.int32, (tm, 1), 0)
    prev = jnp.where(row == 0, halo_ref[HALO - 1:HALO, :], pltpu.roll(pb, 1, 0))
    seq_len = jnp.where(i < n_prompt_tiles, t_p, t_s)
    is_start = jnp.bitwise_and(i * tm + row, seq_len - 1) == 0
    ps = jnp.where(is_start, start_ref[...], prev)
    xs = pb + (ps - pb) * mu_ref[...]
    r = xs[:, :B_W]
    k = xs[:, B_W:2 * B_W]
    lg = xs[:, PB_LG:PB_LW]
    lw_in = xs[:, PB_LW:PB_LA]
    la = xs[:, PB_LA:PB_COLS]
    wl = w0_ref[...] + jnp.dot(jnp.tanh(lw_in).astype(BF16), w2_ref[...], preferred_element_type=F32)
    lw = -jnp.exp(-_softplus(-wl) - 0.5)
    a = _sigmoid(a0_ref[...] + jnp.dot(la.astype(BF16), a2_ref[...], preferred_element_type=F32))
    kk = _group_norm_scale(k * kk_ref[...], B_DH, 1.0, EPS)
    kf = k * (1.0 + (a - 1.0) * ka_ref[...])
    b = kk * a
    r_ref[...] = r
    kf_ref[...] = kf
    v_ref[...] = xs[:, 2 * B_W:3 * B_W]
    g_ref[...] = jnp.dot(_sigmoid(lg).astype(BF16), g2_ref[...], preferred_element_type=F32)

    chunk = jnp.where(pl.program_id(0) < n_prompt_tiles, chunk_p, chunk_s)
    rowc = jnp.bitwise_and(lax.broadcasted_iota(jnp.int32, (tm, B_W), 0), chunk - 1)
    lp = lw
    sfx = lw
    step = 1
    while step < max(chunk_p, chunk_s):
        lp = lp + jnp.where(rowc >= step, pltpu.roll(lp, step, 0), 0.0)
        sfx = sfx + jnp.where(rowc + step < chunk, pltpu.roll(sfx, tm - step, 0), 0.0)
        step *= 2
    sfx = sfx - lw
    e_neg = jnp.exp(-lp)
    e_sfx = jnp.exp(sfx)
    sw = lambda z: _swap_halves(z).astype(BF16)
    rt_ref[...] = sw(r * jnp.exp(lp))
    at_ref[...] = sw(kk * jnp.exp(lp - lw))
    bt_ref[...] = sw(b * e_neg)
    kt_ref[...] = sw(kf * e_neg)
    bh_ref[...] = sw(b * e_sfx)
    kh_ref[...] = sw(kf * e_sfx)
    pc_ref[...] = _swap_halves(jnp.exp(lp + sfx))


def _rwkv_prep(pb, starts, mu, w0, w2, a0, a2, g2, k_k, k_a, *, tm, n_prompt_tiles, chunk_p, chunk_s, t_p, t_s):
    m = pb.shape[0]
    assert tm % chunk_p == 0 and tm % chunk_s == 0 and tm % HALO == 0
    assert t_p & (t_p - 1) == 0 and t_s & (t_s - 1) == 0 and (n_prompt_tiles * tm) % t_s == 0
    npt = n_prompt_tiles
    row = lambda n: pl.BlockSpec((1, n), lambda i: (0, 0))
    full = lambda a: pl.BlockSpec(a.shape, lambda i: (0, 0))
    tok = lambda n: pl.BlockSpec((tm, n), lambda i: (i, 0))
    f32o = jax.ShapeDtypeStruct((m, B_W), F32)
    b16o = jax.ShapeDtypeStruct((m, B_W), BF16)
    kern = functools.partial(_rwkv_prep_kernel, n_prompt_tiles=npt, chunk_p=chunk_p, chunk_s=chunk_s, t_p=t_p,
                             t_s=t_s)
    return pl.pallas_call(
        kern,
        grid=(m // tm,),
        in_specs=[tok(PB_COLS),
                  pl.BlockSpec((HALO, PB_COLS), lambda i: (jnp.maximum(i * (tm // HALO) - 1, 0), 0)),
                  pl.BlockSpec((tm, PB_COLS), lambda i: (jnp.maximum(i - npt + 1, 0), 0)),
                  row(PB_COLS), row(B_W), full(w2), row(B_W), full(a2), full(g2), row(B_W), row(B_W)],
        out_specs=[tok(B_W)] * 11,
        out_shape=[f32o] * 4 + [b16o] * 6 + [f32o],
        compiler_params=_cparams(("parallel",)),
        name="rwkv_prep",
    )(pb, pb, starts, mu, w0.reshape(1, B_W), w2, a0.reshape(1, B_W), a2, g2, k_k.reshape(1, B_W),
      k_a.reshape(1, B_W))


SOLVE_BLOCK = 16


def _split2(a):
    hi = a.astype(BF16)
    return hi, (a - hi.astype(F32)).astype(BF16)


def _solve_unit_lower_many(lmats, rhss):
    ns = len(lmats)
    c = lmats[0].shape[0]
    n = rhss[0].shape[1]
    bs = min(SOLVE_BLOCK, c)
    col = lax.broadcasted_iota(jnp.int32, (bs, c), 1)
    d = lambda x, y: jnp.dot(x, y, preferred_element_type=F32)
    done = [[] for _ in range(ns)]
    for blk in range(c // bs):
        rows = slice(blk * bs, (blk + 1) * bs)
        lrows = [lm[rows, :] for lm in lmats]
        rs = [rh[rows, :] for rh in rhss]
        if blk > 0:
            for s in range(ns):
                xs = jnp.concatenate(done[s] + [jnp.zeros((c - blk * bs, n), F32)], axis=0)
                lh, ll = _split2(jnp.where(col < blk * bs, lrows[s], 0.0))
                xh, xl = _split2(xs)
                rs[s] = rs[s] - (d(lh, xl) + d(ll, xh) + d(lh, xh))
        for j in range(bs - 1):
            for s in range(ns):
                rs[s] = rs[s] - lrows[s][:, blk * bs + j:blk * bs + j + 1] * rs[s][j:j + 1, :]
        for s in range(ns):
            done[s].append(rs[s])
    return [dn[0] if len(dn) == 1 else jnp.concatenate(dn, axis=0) for dn in done]


def _solve_unit_lower(lmat, rhs):
    return _solve_unit_lower_many([lmat], [rhs])[0]


def _rwkv_chunk_kernel(rt_ref, at_ref, bt_ref, kt_ref, bh_ref, kh_ref, v_ref, pc_ref, z0_ref, y_ref, zo_ref, z_scr,
                       *, chunk):
    i = pl.program_id(2)
    tb = rt_ref.shape[0]

    @pl.when(i == 0)
    def _():
        z_scr[...] = z0_ref[...]

    lo = lax.broadcasted_iota(jnp.int32, (1, LANES), 1) < B_DH
    zr = lax.broadcasted_iota(jnp.int32, (LANES, LANES), 0) < B_DH
    zc = lax.broadcasted_iota(jnp.int32, (LANES, LANES), 1) < B_DH
    offdiag = zr != zc
    ri = lax.broadcasted_iota(jnp.int32, (chunk, chunk), 0)
    ci = lax.broadcasted_iota(jnp.int32, (chunk, chunk), 1)
    nt = (((1,), (1,)), ((), ()))
    tn = (((0,), (0,)), ((), ()))
    d = lambda x, y: jnp.dot(x, y, preferred_element_type=F32)
    zero = jnp.zeros((), BF16)
    n_chunks = tb // chunk
    n_pp = rt_ref.shape[1] // LANES
    lmats, rhss, rkvs, rbs = [], [], [], []
    for p in range(n_pp):
        ls = slice(p * LANES, (p + 1) * LANES)
        for c in range(n_chunks):
            rs = slice(c * chunk, (c + 1) * chunk)
            rt, at, bt, kt = rt_ref[rs, ls], at_ref[rs, ls], bt_ref[rs, ls], kt_ref[rs, ls]
            v = v_ref[rs, ls].astype(BF16)
            for head in range(2):
                kmask = lo if head == 1 else jnp.logical_not(lo)
                a_h, r_h = jnp.where(kmask, at, zero), jnp.where(kmask, rt, zero)
                b_h, k_h = jnp.where(kmask, bt, zero), jnp.where(kmask, kt, zero)
                v_h = jnp.where(kmask, zero, v)
                lhs = jnp.concatenate([a_h, r_h], axis=0)
                gb = lax.dot_general(lhs, b_h, nt, preferred_element_type=F32)
                gk = lax.dot_general(lhs, k_h, nt, preferred_element_type=F32)
                lmats.append(jnp.where(ri > ci, gb[:chunk], 0.0))
                rbs.append(jnp.where(ri >= ci, gb[chunk:], 0.0).astype(BF16))
                ak = jnp.where(ri > ci, gk[:chunk], 0.0).astype(BF16)
                rk = jnp.where(ri >= ci, gk[chunk:], 0.0).astype(BF16)
                rhss.append(a_h.astype(F32) + d(ak, v_h))
                rkvs.append(d(rk, v_h))
    xs = _solve_unit_lower_many(lmats, rhss)
    n_sq = z_scr.shape[0]
    cps = n_chunks // n_sq
    zs = {(sq, p): z_scr[sq, p] for sq in range(n_sq) for p in range(n_pp)}
    for c in range(n_chunks):
        rs = slice(c * chunk, (c + 1) * chunk)
        for p in range(n_pp):
            ls = slice(p * LANES, (p + 1) * LANES)
            s0 = 2 * (p * n_chunks + c)
            key = (c // cps, p)
            rt = rt_ref[rs, ls]
            stacked = jnp.concatenate([xs[s0].astype(BF16), xs[s0 + 1].astype(BF16),
                                       jnp.where(lo, zero, rt), jnp.where(lo, rt, zero)], axis=0)
            big = lax.dot_general(stacked, zs[key].astype(BF16), nt, preferred_element_type=F32)
            u0 = jnp.where(lo, big[:chunk], 0.0)
            u1 = jnp.where(lo, 0.0, big[chunk:2 * chunk])
            y = jnp.where(lo, big[2 * chunk:3 * chunk], big[3 * chunk:]) + rkvs[s0] + rkvs[s0 + 1]
            y = y - d(rbs[s0], u0.astype(BF16)) - d(rbs[s0 + 1], u1.astype(BF16))
            y_ref[rs, ls] = y
            upd = lax.dot_general(jnp.concatenate([v_ref[rs, ls].astype(BF16), (u0 + u1).astype(BF16)], axis=0),
                                  jnp.concatenate([kh_ref[rs, ls], -bh_ref[rs, ls]], axis=0), tn,
                                  preferred_element_type=F32)
            zs[key] = jnp.where(offdiag, zs[key] * pc_ref[c * chunk:c * chunk + 1, ls] + upd, zs[key])
    for (sq, p), val in zs.items():
        z_scr[sq, p] = val

    @pl.when(i == pl.num_programs(2) - 1)
    def _():
        zo_ref[...] = z_scr[...]


RWKV_PAIRS_PER_STEP = 4


def _rwkv_chunks(rt, at, bt, kt, bh, kh, v, pc, z0, row0, n_seq, t, *, tb, chunk, pps=RWKV_PAIRS_PER_STEP):
    spb, nb, n_groups, rb = _seq_blocking(row0, n_seq, t, tb, chunk)
    n_pairs = B_HEADS // 2
    tok = pl.BlockSpec((tb, pps * LANES), lambda s, p, i: (rb + s * nb + i, p))
    st = pl.BlockSpec((spb, pps, LANES, LANES), lambda s, p, i: (s, p, 0, 0))
    return pl.pallas_call(
        functools.partial(_rwkv_chunk_kernel, chunk=chunk),
        grid=(n_groups, n_pairs // pps, nb),
        in_specs=[tok] * 8 + [st],
        out_specs=[pl.BlockSpec((tb, pps * LANES), lambda s, p, i: (s * nb + i, p)), st],
        out_shape=[jax.ShapeDtypeStruct((n_seq * t, B_W), F32),
                   jax.ShapeDtypeStruct((n_seq, n_pairs, LANES, LANES), F32)],
        scratch_shapes=[pltpu.VMEM((spb, pps, LANES, LANES), F32)],
        compiler_params=_cparams(("parallel", "parallel", "arbitrary")),
        name="rwkv_chunks",
    )(rt, at, bt, kt, bh, kh, v, pc, z0)


def _rwkv_post_kernel(yp_ref, ys_ref, r_ref, kf_ref, v_ref, g_ref, lw_ref, lb_ref, rk_ref, o_ref, *, n_prompt_tiles):
    y = jnp.where(pl.program_id(0) < n_prompt_tiles, yp_ref[...], ys_ref[...])
    mean = _group_sum(y, B_DH) * (1.0 / B_DH)
    yc = y - mean
    var = _group_sum(yc * yc, B_DH) * (1.0 / B_DH)
    yn = yc * lax.rsqrt(var + B_GN_EPS) * lw_ref[...] + lb_ref[...]
    bonus = _group_sum(r_ref[...] * kf_ref[...] * rk_ref[...], B_DH)
    o_ref[...] = ((yn + bonus * v_ref[...]) * g_ref[...]).astype(BF16)


def _rwkv_post(y_p, y_s, r, kf, v, g, lnx_w, lnx_b, r_k, *, tm):
    m = r.shape[0]
    assert y_p.shape[0] % tm == 0 and y_s.shape[0] % tm == 0
    npt = y_p.shape[0] // tm
    tok = pl.BlockSpec((tm, B_W), lambda i: (i, 0))
    row = pl.BlockSpec((1, B_W), lambda i: (0, 0))
    return pl.pallas_call(
        functools.partial(_rwkv_post_kernel, n_prompt_tiles=npt),
        grid=(m // tm,),
        in_specs=[pl.BlockSpec((tm, B_W), lambda i: (jnp.minimum(i, npt - 1), 0)),
                  pl.BlockSpec((tm, B_W), lambda i: (jnp.maximum(i - npt, 0), 0))] + [tok] * 4 + [row] * 3,
        out_specs=tok,
        out_shape=jax.ShapeDtypeStruct((m, B_W), BF16),
        compiler_params=_cparams(("parallel",)),
        name="rwkv_post",
    )(y_p, y_s, r, kf, v, g, lnx_w.reshape(1, B_W), lnx_b.reshape(1, B_W), r_k.reshape(1, B_W))


def _rwkv_cols(a):
    lw0 = 3 * B_W
    la0 = lw0 + B_DECAY_LORA
    lg0 = la0 + B_AAA_LORA
    z = jnp.zeros(a.shape[:-1] + (LANES - B_DECAY_LORA,), a.dtype)
    return jnp.concatenate([a[..., :lw0], a[..., lg0:], a[..., lw0:la0], z, a[..., la0:lg0], z], axis=-1)


def _rwkv_cols_inv(a):
    return jnp.concatenate([a[..., :PB_LG], a[..., PB_LW:PB_LW + B_DECAY_LORA],
                            a[..., PB_LA:PB_LA + B_AAA_LORA], a[..., PB_LG:PB_LW]], axis=-1)


def _pad_rows(a, n):
    return jnp.pad(a, ((0, n - a.shape[0]), (0, 0)))


def _rwkv_state_to_pairs(s):
    n = s.shape[0]
    s = s.reshape(n, B_HEADS // 2, 2, B_DH, B_DH)
    eye = jnp.broadcast_to(jnp.eye(B_DH, dtype=F32), (n, B_HEADS // 2, B_DH, B_DH))
    top = jnp.concatenate([eye, s[:, :, 0]], axis=-1)
    bot = jnp.concatenate([s[:, :, 1], eye], axis=-1)
    return jnp.concatenate([top, bot], axis=-2)


def _rwkv_state_from_pairs(z):
    n = z.shape[0]
    s = jnp.stack([z[:, :, :B_DH, B_DH:], z[:, :, B_DH:, :B_DH]], axis=2)
    return s.reshape(n, B_HEADS, B_DH, B_DH)


def _rwkv_mixer(pb, n_p, t_p, n_s, t_s, prev_s, state_s, mu, w0, w2, a0, a2, g2, k_k, k_a, r_k, lnx_w, lnx_b,
                *, tm, tb):
    mp = n_p * t_p
    chunk_p = min(64, t_p)
    chunk_s = min(64, t_s)
    assert mp % tm == 0 and (n_s * t_s) % tm == 0 and t_p % chunk_p == 0 and t_s % chunk_s == 0
    starts = jnp.concatenate([
        jnp.zeros((tm, PB_COLS), F32),
        jnp.broadcast_to(_rwkv_cols(prev_s), (n_s, t_s, PB_COLS)).reshape(n_s * t_s, PB_COLS)], axis=0)
    r, kf, v, g, rt, at, bt, kt, bh, kh, pc = _rwkv_prep(
        pb, starts, _rwkv_cols(mu.reshape(1, B_COLS)), w0, _pad_rows(w2, LANES).astype(BF16), a0,
        _pad_rows(a2, LANES).astype(BF16), g2.astype(BF16), k_k, k_a, tm=tm, n_prompt_tiles=mp // tm,
        chunk_p=chunk_p, chunk_s=chunk_s, t_p=t_p, t_s=t_s)
    z0_p = _rwkv_state_to_pairs(jnp.zeros((n_p, B_HEADS, B_DH, B_DH), F32))
    y_p, z_p = _rwkv_chunks(rt, at, bt, kt, bh, kh, v, pc, z0_p, 0, n_p, t_p, tb=tb, chunk=chunk_p)
    y_s, z_s = _rwkv_chunks(rt, at, bt, kt, bh, kh, v, pc, _rwkv_state_to_pairs(state_s), mp, n_s, t_s,
                            tb=t_s, chunk=chunk_s, pps=B_HEADS // 2)
    ob = _rwkv_post(y_p, y_s, r, kf, v, g, lnx_w, lnx_b, r_k.reshape(B_W), tm=tm)
    return ob, _rwkv_state_from_pairs(z_p), _rwkv_state_from_pairs(z_s)


PC_Z = C_CONV_CH
PC_BETA = PC_Z + C_V
PC_COLS = PC_BETA + 2 * LANES
HALO = 8


def _chunk_cumsum(g, chunk):
    rowc = jnp.bitwise_and(lax.broadcasted_iota(jnp.int32, g.shape, 0), chunk - 1)
    k = 1
    while k < chunk:
        g = g + jnp.where(rowc >= k, pltpu.roll(g, k, 0), 0.0)
        k *= 2
    return g


def _transpose_rows(a):
    c = a.shape[0]
    if c < LANES:
        a = jnp.concatenate([a, jnp.zeros((LANES - c, LANES), F32)], axis=0)
    return a.T


def _pad_chunk_rows(a):
    c = a.shape[0]
    return a if c == LANES else jnp.concatenate([a, jnp.zeros((LANES - c, a.shape[1]), a.dtype)], axis=0)


def _gdn_prep_kernel(x_ref, halo_ref, st_ref, cw_ref, sm_ref, alog_ref, dtb_ref,
                     q_ref, k_ref, v_ref, beta_ref, gc_ref, *, chunk):
    i = pl.program_id(1)
    x = x_ref[...]
    tm = x.shape[0]
    halo = jnp.where(i == 0, st_ref[...], halo_ref[...])
    row8 = lax.broadcasted_iota(jnp.int32, (HALO, x.shape[1]), 0)
    acc = x * cw_ref[C_CONV - 1:C_CONV, :]
    for s in range(1, C_CONV):
        xs = pltpu.roll(x, s, 0)
        first = jnp.where(row8 < s, pltpu.roll(halo, s, 0), xs[:HALO])
        xs = first if tm == HALO else jnp.concatenate([first, xs[HALO:]], axis=0)
        acc = acc + xs * cw_ref[C_CONV - 1 - s:C_CONV - s, :]
    y = _silu(acc)
    q_ref[...] = _group_norm_scale(y[:, :C_QK], C_DH, 1.0, EPS) * (C_DH ** -0.5)
    k_ref[...] = _group_norm_scale(y[:, C_QK:2 * C_QK], C_DH, 1.0, EPS)
    v_ref[...] = y[:, 2 * C_QK:]
    sm = sm_ref[...]
    beta_ref[...] = _sigmoid(sm[:, :LANES])
    g = -jnp.exp(alog_ref[...]) * _softplus(sm[:, LANES:] + dtb_ref[...])
    gc_ref[...] = _chunk_cumsum(g, chunk)


def _gdn_prep(pc, conv_state, row0, n_seq, t, conv_w, a_log, dt_bias, *, tm, chunk):
    assert t % tm == 0 and row0 % tm == 0 and tm % HALO == 0 and tm % chunk == 0 and chunk & (chunk - 1) == 0
    nb = t // tm
    rb = row0 // tm
    hb = tm // HALO
    tok_in = lambda n, cb: pl.BlockSpec((tm, n), lambda s, i: (rb + s * nb + i, cb))
    tok_out = lambda n: pl.BlockSpec((tm, n), lambda s, i: (s * nb + i, 0))
    row = lambda n: pl.BlockSpec((1, n), lambda s, i: (0, 0))
    sds = lambda n: jax.ShapeDtypeStruct((n_seq * t, n), F32)
    lanes8 = lambda a: jnp.zeros((1, LANES), F32).at[0, :C_V_HEADS].set(a)
    return pl.pallas_call(
        functools.partial(_gdn_prep_kernel, chunk=chunk),
        grid=(n_seq, nb),
        in_specs=[
            tok_in(C_CONV_CH, 0),
            pl.BlockSpec((HALO, C_CONV_CH), lambda s, i: (jnp.maximum((rb + s * nb + i) * hb - 1, 0), 0)),
            pl.BlockSpec((None, HALO, C_CONV_CH), lambda s, i: (s, 0, 0)),
            pl.BlockSpec((C_CONV, C_CONV_CH), lambda s, i: (0, 0)),
            tok_in(2 * LANES, PC_BETA // (2 * LANES)),
            row(LANES), row(LANES),
        ],
        out_specs=[tok_out(C_QK), tok_out(C_QK), tok_out(C_V), tok_out(LANES), tok_out(LANES)],
        out_shape=[sds(C_QK), sds(C_QK), sds(C_V), sds(LANES), sds(LANES)],
        compiler_params=_cparams(("parallel", "arbitrary")),
        name="gdn_prep",
    )(pc, pc, conv_state, conv_w, pc, lanes8(a_log), lanes8(dt_bias))


def _gdn_chunk_kernel(q_ref, k_ref, v_ref, beta_ref, gc_ref, z_ref, on_ref, s0_ref, o_ref, so_ref, s_scr, *, chunk):
    hq = pl.program_id(1)
    i = pl.program_id(2)
    tb = q_ref.shape[0]
    rep = v_ref.shape[1] // C_DH

    @pl.when(i == 0)
    def _():
        s_scr[...] = s0_ref[...]

    lane = lax.broadcasted_iota(jnp.int32, (chunk, LANES), 1)
    ri = lax.broadcasted_iota(jnp.int32, (chunk, chunk), 0)
    ci = lax.broadcasted_iota(jnp.int32, (chunk, chunk), 1)
    nt = (((1,), (1,)), ((), ()))
    ones = jnp.ones((chunk, LANES), BF16)
    n_chunks = tb // chunk
    lmats, rhss, attns, qgs, kds, eglast = [], [], [], [], [], []
    for c in range(n_chunks):
        rs = slice(c * chunk, (c + 1) * chunk)
        q, k = q_ref[rs, :], k_ref[rs, :]
        qb, kbf = q.astype(BF16), k.astype(BF16)
        kkt = lax.dot_general(kbf, kbf, nt, preferred_element_type=F32)
        qkt = lax.dot_general(qb, kbf, nt, preferred_element_type=F32)
        for j in range(rep):
            hsel = lane == hq * rep + j
            beta = jnp.sum(jnp.where(hsel, beta_ref[rs, :], 0.0), axis=-1, keepdims=True)
            gcol = jnp.sum(jnp.where(hsel, gc_ref[rs, :], 0.0), axis=-1, keepdims=True)
            g0 = jnp.where(lane == 0, gcol, 0.0)
            grow = sum(lax.dot_general(ones, part, nt, preferred_element_type=F32) for part in _split3(g0))
            dec = jnp.exp(jnp.where(ri >= ci, gcol - grow, NEG_INF))
            lmats.append(jnp.where(ri > ci, beta * kkt * dec, 0.0))
            rhss.append(jnp.concatenate([v_ref[rs, j * C_DH:(j + 1) * C_DH] * beta, k * (beta * jnp.exp(gcol))],
                                        axis=1))
            attns.append((qkt * dec).astype(BF16))
            qgs.append((q * jnp.exp(gcol)).astype(BF16))
            glast = gcol[chunk - 1:chunk, :]
            kds.append(_transpose_rows(k * jnp.exp(glast - gcol)).astype(BF16))
            eglast.append(jnp.exp(glast))
    xs = _solve_unit_lower_many(lmats, rhss)
    n_sq = s_scr.shape[0]
    cps = n_chunks // n_sq
    ss = {(sq, j): s_scr[sq, j] for sq in range(n_sq) for j in range(rep)}
    for c in range(n_chunks):
        rs = slice(c * chunk, (c + 1) * chunk)
        for j in range(rep):
            n = c * rep + j
            key = (c // cps, j)
            sb = ss[key].astype(BF16)
            v_new = xs[n][:, :C_DH] - jnp.dot(xs[n][:, C_DH:].astype(BF16), sb, preferred_element_type=F32)
            vb = v_new.astype(BF16)
            o = jnp.dot(qgs[n], sb, preferred_element_type=F32) + jnp.dot(attns[n], vb, preferred_element_type=F32)
            ss[key] = ss[key] * eglast[n] + jnp.dot(kds[n], _pad_chunk_rows(vb), preferred_element_type=F32)
            o = _rms_rows(o, on_ref[...]) * _silu(z_ref[rs, j * C_DH:(j + 1) * C_DH])
            o_ref[rs, j * C_DH:(j + 1) * C_DH] = o.astype(BF16)
    for (sq, j), val in ss.items():
        s_scr[sq, j] = val

    @pl.when(i == pl.num_programs(2) - 1)
    def _():
        so_ref[...] = s_scr[...]


def _seq_blocking(row0, n_seq, t, tb, chunk):
    assert row0 % tb == 0 and tb % chunk == 0 and (t % tb == 0 or (tb % t == 0 and n_seq % (tb // t) == 0))
    spb = max(tb // t, 1)
    return spb, max(t // tb, 1), n_seq // spb, row0 // tb


def _gdn_chunks(q, k, v, beta, gc, pc, row0, n_seq, t, onorm, s0, *, tb, chunk):
    spb, nb, n_groups, rb = _seq_blocking(row0, n_seq, t, tb, chunk)
    rep = C_V_HEADS // C_QK_HEADS
    tok = lambda n, cb: pl.BlockSpec((tb, n), cb)
    st = pl.BlockSpec((spb, rep, C_DH, C_DH), lambda s, h, i: (s, h, 0, 0))
    return pl.pallas_call(
        functools.partial(_gdn_chunk_kernel, chunk=chunk),
        grid=(n_groups, C_QK_HEADS, nb),
        in_specs=[
            tok(C_DH, lambda s, h, i: (s * nb + i, h)),
            tok(C_DH, lambda s, h, i: (s * nb + i, h)),
            tok(rep * C_DH, lambda s, h, i: (s * nb + i, h)),
            tok(LANES, lambda s, h, i: (s * nb + i, 0)),
            tok(LANES, lambda s, h, i: (s * nb + i, 0)),
            tok(rep * C_DH, lambda s, h, i: (rb + s * nb + i, PC_Z // (rep * C_DH) + h)),
            pl.BlockSpec((1, LANES), lambda s, h, i: (0, 0)),
            st,
        ],
        out_specs=[tok(rep * C_DH, lambda s, h, i: (s * nb + i, h)), st],
        out_shape=[jax.ShapeDtypeStruct((n_seq * t, C_V), BF16),
                   jax.ShapeDtypeStruct((n_seq, C_V_HEADS, C_DH, C_DH), F32)],
        scratch_shapes=[pltpu.VMEM((spb, rep, C_DH, C_DH), F32)],
        compiler_params=_cparams(("parallel", "parallel", "arbitrary")),
        name="gdn_chunks",
    )(q, k, v, beta, gc, pc, onorm.reshape(1, LANES), s0)


def _gdn_group(pc, conv_state3, s0, row0, n_seq, t, conv_w, a_log, dt_bias, onorm, *, tm, tb):
    chunk = C_CHUNK if t % C_CHUNK == 0 else t
    conv_state = jnp.pad(conv_state3, ((0, 0), (HALO - (C_CONV - 1), 0), (0, 0)))
    q, k, v, beta, gc = _gdn_prep(pc, conv_state, row0, n_seq, t, conv_w, a_log, dt_bias, tm=tm, chunk=chunk)
    return _gdn_chunks(q, k, v, beta, gc, pc, row0, n_seq, t, onorm, s0, tb=tb, chunk=chunk)


PD_V = 2 * D_K
PD_GATE = PD_V + D_V
PD_GLR = PD_GATE + D_V
PD_COLS = PD_GLR + LANES


def _gla_kernel(q_ref, k_ref, v_ref, gate_ref, glr_ref, w2_ref, bgk_ref, on_ref, s0_ref, o_ref, so_ref, s_scr,
                *, chunk):
    i = pl.program_id(2)
    tb = q_ref.shape[0]

    @pl.when(i == 0)
    def _():
        s_scr[...] = s0_ref[...]

    pre = jnp.dot(glr_ref[...].astype(BF16), w2_ref[...], preferred_element_type=F32) + bgk_ref[...]
    bcum = _chunk_cumsum(-_softplus(-pre) * (1.0 / D_GATE_NORM), chunk)
    ri = lax.broadcasted_iota(jnp.int32, (chunk, 1), 0)
    ci = lax.broadcasted_iota(jnp.int32, (chunk, chunk), 1)
    intra, qgs, decays, upds = [], [], [], []
    for c in range(tb // chunk):
        rs = slice(c * chunk, (c + 1) * chunk)
        q = q_ref[rs, :] * (D_DK ** -0.5)
        k, bc = k_ref[rs, :], bcum[rs, :]
        vb = v_ref[rs, :].astype(BF16)
        attn = jnp.zeros((chunk, chunk), F32)
        for j in range(chunk):
            e = jnp.exp(jnp.where(ri >= j, bc - bc[j:j + 1, :], NEG_INF))
            col = jnp.sum(q * k[j:j + 1, :] * e, axis=-1, keepdims=True)
            attn = jnp.where(ci == j, col, attn)
        intra.append(jnp.dot(attn.astype(BF16), vb, preferred_element_type=F32))
        qgs.append((q * jnp.exp(bc)).astype(BF16))
        blast = bc[chunk - 1:chunk, :]
        ebt = jnp.broadcast_to(jnp.exp(blast), (LANES, LANES)).T
        decays.append(jnp.concatenate([ebt, ebt], axis=1))
        kdt = _transpose_rows(k * jnp.exp(blast - bc)).astype(BF16)
        upds.append(jnp.dot(kdt, _pad_chunk_rows(vb), preferred_element_type=F32))
    n_sq = s_scr.shape[0]
    cps = (tb // chunk) // n_sq
    ss = [s_scr[sq] for sq in range(n_sq)]
    for c in range(tb // chunk):
        rs = slice(c * chunk, (c + 1) * chunk)
        sq = c // cps
        o = jnp.dot(qgs[c], ss[sq].astype(BF16), preferred_element_type=F32) + intra[c]
        ss[sq] = ss[sq] * decays[c] + upds[c]
        o = _rms_rows(o, on_ref[...]) * _silu(gate_ref[rs, :])
        o_ref[rs, :] = o.astype(BF16)
    for sq in range(n_sq):
        s_scr[sq] = ss[sq]

    @pl.when(i == pl.num_programs(2) - 1)
    def _():
        so_ref[...] = s_scr[...]


def _gla_group(pd, s0, row0, n_seq, t, w_gk2, b_gk, onorm, *, tb):
    chunk = D_CHUNK if t % D_CHUNK == 0 else t
    assert chunk & (chunk - 1) == 0
    spb, nb, n_groups, rb = _seq_blocking(row0, n_seq, t, tb, chunk)
    w2 = _pad_rows(w_gk2, LANES).astype(BF16)
    tok = lambda n, cb: pl.BlockSpec((tb, n), cb)
    st = pl.BlockSpec((spb, None, D_DK, D_DV), lambda s, h, i: (s, h, 0, 0))
    return pl.pallas_call(
        functools.partial(_gla_kernel, chunk=chunk),
        grid=(n_groups, D_HEADS, nb),
        in_specs=[
            tok(D_DK, lambda s, h, i: (rb + s * nb + i, h)),
            tok(D_DK, lambda s, h, i: (rb + s * nb + i, D_K // D_DK + h)),
            tok(D_DV, lambda s, h, i: (rb + s * nb + i, PD_V // D_DV + h)),
            tok(D_DV, lambda s, h, i: (rb + s * nb + i, PD_GATE // D_DV + h)),
            tok(LANES, lambda s, h, i: (rb + s * nb + i, PD_GLR // LANES)),
            pl.BlockSpec((LANES, D_DK), lambda s, h, i: (0, h)),
            pl.BlockSpec((1, D_DK), lambda s, h, i: (0, h)),
            pl.BlockSpec((1, D_DV), lambda s, h, i: (0, 0)),
            st,
        ],
        out_specs=[tok(D_DV, lambda s, h, i: (s * nb + i, h)), st],
        out_shape=[jax.ShapeDtypeStruct((n_seq * t, D_V), BF16),
                   jax.ShapeDtypeStruct((n_seq, D_HEADS, D_DK, D_DV), F32)],
        scratch_shapes=[pltpu.VMEM((spb, D_DK, D_DV), F32)],
        compiler_params=_cparams(("parallel", "parallel", "arbitrary")),
        name="gla_chunks",
    )(pd, pd, pd, pd, pd, w2, b_gk.reshape(1, D_K), onorm.reshape(1, D_DV), s0)


TM = 768
TM_SMALL = 256


def _seqs_per_block(n_seq, want):
    return max(s for s in range(1, want + 1) if n_seq % s == 0)


def _last_rows(a, row0, n_seq, t, k, n_cols):
    picks = [lax.slice(a, (row0 + t - k + j, 0), (row0 + n_seq * t, n_cols), (t, 1)) for j in range(k)]
    return jnp.stack(picks, axis=1)


def _cross_block(x, layer, n_p, t_p, n_s, t_s, mem_prompt, cache_mem_k, cache_mem_v, norm_mem_w, norm_cross_w,
                 x_w_q, x_w_k, x_w_v, x_w_o, x_q_norm, x_k_norm):
    d = x.shape[1]
    mp = n_p * t_p
    wkv = jnp.concatenate([x_w_k[layer], x_w_v[layer]], axis=1).astype(BF16)
    hw = jnp.concatenate([jnp.tile(x_k_norm[layer], X_HEADS), jnp.ones((X_W,), F32)]).reshape(1, 2 * X_W)
    mem = mem_prompt.reshape(n_p * N_MEM, d)
    kv = _norm_matmul(mem, norm_mem_w[layer], wkv, tm=min(512, n_p * N_MEM), tn=X_W, head_w=hw,
                      n_norm_cols=X_W, gsize=X_DH)
    mk_p = kv[:, :X_W].reshape(n_p, N_MEM, X_W)
    mv_p = kv[:, X_W:].reshape(n_p, N_MEM, X_W)
    qx = _norm_matmul(x, norm_cross_w[layer], x_w_q[layer].astype(BF16), tm=TM, tn=X_W)
    ca_p = _cross_attention(qx, 0, n_p, t_p, mk_p, mv_p, x_q_norm[layer], tq=min(512, t_p))
    ca_s = _cross_attention(qx, mp, n_s, t_s, cache_mem_k, cache_mem_v, x_q_norm[layer], tq=t_s, mem_layer=layer)
    ca = jnp.concatenate([ca_p, ca_s], axis=0)
    x = _matmul_residual([ca], [x_w_o[layer].astype(BF16)], x, tm=TM, tn=1024)
    return x, mk_p.reshape(n_p, N_MEM, X_HEADS, X_DH), mv_p.reshape(n_p, N_MEM, X_HEADS, X_DH)


def kernel(x_prompt, x_sample, cache_diff_k, cache_diff_v, state_rwkv, state_rwkv_shift, cache_mem_k, cache_mem_v, state_gdn, state_gdn_conv, state_gla, page_table, mem_prompt, norm_mix_w, norm_cross_w, norm_mem_w, norm_ffn_w, w_in_even, w_out_even, a_q_norm, a_k_norm, a_lam_q1, a_lam_k1, a_lam_q2, a_lam_k2, a_subln, b_mu, b_w0, b_w2, b_a0, b_a2, b_g2, b_k_k, b_k_a, b_r_k, b_lnx_w, b_lnx_b, ffd_w_gate, ffd_w_up, ffd_w_down, w_in_odd, w_out_odd, c_conv_w, c_a_log, c_dt_bias, c_onorm, d_w_gk2, d_b_gk, d_onorm, moe_w_router, moe_w_gate, moe_w_up, moe_w_down, x_w_q, x_w_k, x_w_v, x_w_o, x_q_norm, x_k_norm):
    n_p, t_p, d = x_prompt.shape
    n_s, t_s, _ = x_sample.shape
    mp, ms = n_p * t_p, n_s * t_s
    depth = norm_mix_w.shape[0]
    x = jnp.concatenate([x_prompt.reshape(mp, d), x_sample.reshape(ms, d)], axis=0)
    assert (mp + ms) % TM == 0 and (mp + ms) % TM_SMALL == 0
    page = cache_diff_k.shape[2]
    slopes = jnp.exp2(-(8.0 / A_HEADS) * jnp.arange(1, A_HEADS + 1, dtype=F32))
    cross_w = (mem_prompt, cache_mem_k, cache_mem_v, norm_mem_w, norm_cross_w, x_w_q, x_w_k, x_w_v, x_w_o,
               x_q_norm, x_k_norm)
    dk_p, dv_p, dk_s, dv_s, rw_p, rw_s, sh_p, sh_s = [], [], [], [], [], [], [], []
    mk_l, mv_l, gd_p, gd_s, cv_p, cv_s, gl_p, gl_s = [], [], [], [], [], [], [], []
    for layer in range(depth):
        i = layer // 2
        if layer % 2 == 0:
            lam_init = 0.8 - 0.6 * math.exp(-0.3 * layer)
            w_in = w_in_even[i]
            hw = jnp.concatenate([jnp.tile(a_q_norm[i], A_Q // A_DH), jnp.tile(a_k_norm[i], A_Q // A_DH),
                                  jnp.ones((A_COLS - 2 * A_Q,), F32)]).reshape(1, A_COLS)
            pa = _norm_matmul(x, norm_mix_w[layer], w_in[:, :A_COLS].astype(BF16), tm=TM, tn=1024, head_w=hw,
                              n_norm_cols=2 * A_Q, gsize=A_DH)
            pb = _norm_matmul(x, norm_mix_w[layer], _rwkv_cols(w_in[:, A_COLS:]).astype(BF16), tm=TM,
                              tn=PB_COLS // 2)
            lam_stack = jnp.stack([a_lam_q1[i], a_lam_k1[i], a_lam_q2[i], a_lam_k2[i]])
            oa_p = _diff_attention_prompt(pa, n_p, t_p, slopes, lam_stack, a_subln[i], lam_init, tq=min(256, t_p))
            oa_s = _diff_attention_sample(pa, mp, n_s, t_s, cache_diff_k, cache_diff_v, i, page_table, lam_stack,
                                          a_subln[i], lam_init)
            ob, st_p, st_s = _rwkv_mixer(pb, n_p, t_p, n_s, t_s, state_rwkv_shift[i], state_rwkv[i], b_mu[i],
                                         b_w0[i], b_w2[i], b_a0[i], b_a2[i], b_g2[i], b_k_k[i], b_k_a[i], b_r_k[i],
                                         b_lnx_w[i], b_lnx_b[i], tm=TM_SMALL, tb=min(256, t_p))
            dk_p.append(pa[:mp, A_Q:2 * A_Q].reshape(n_p, t_p, A_HEADS, 2 * A_DH))
            dv_p.append(pa[:mp, 2 * A_Q:].reshape(n_p, t_p, A_HEADS, 2 * A_DH))
            dk_s.append(pa[mp:, A_Q:2 * A_Q].reshape(n_s, t_s, A_HEADS, 2 * A_DH))
            dv_s.append(pa[mp:, 2 * A_Q:].reshape(n_s, t_s, A_HEADS, 2 * A_DH))
            rw_p.append(st_p)
            rw_s.append(st_s)
            sh_p.append(_rwkv_cols_inv(_last_rows(pb, 0, n_p, t_p, 1, PB_COLS)))
            sh_s.append(_rwkv_cols_inv(_last_rows(pb, mp, n_s, t_s, 1, PB_COLS)))
            oa = jnp.concatenate([oa_p, oa_s], axis=0)
            w_out = w_out_even[i].astype(BF16)
            x = _matmul_residual([oa, ob], [w_out[:A_Q], w_out[A_Q:]], x, tm=TM, tn=1024)
        else:
            w_in = w_in_odd[i]
            c_cols = C_CONV_CH + C_V + 2 * C_V_HEADS
            zc = jnp.zeros((d, LANES - C_V_HEADS), F32)
            wc = jnp.concatenate([w_in[:, :PC_BETA], w_in[:, PC_BETA:PC_BETA + C_V_HEADS], zc,
                                  w_in[:, PC_BETA + C_V_HEADS:c_cols], zc], axis=1).astype(BF16)
            wd_ = jnp.concatenate([w_in[:, c_cols:], jnp.zeros((d, LANES - D_GATE_LORA), F32)], axis=1).astype(BF16)
            pc = _norm_matmul(x, norm_mix_w[layer], wc, tm=TM, tn=PC_COLS // 2)
            pd = _norm_matmul(x, norm_mix_w[layer], wd_, tm=TM, tn=PD_COLS // 5)
            gdn_w = (c_conv_w[i], c_a_log[i], c_dt_bias[i], c_onorm[i])
            oc_p, gs_p = _gdn_group(pc, jnp.zeros((n_p, C_CONV - 1, C_CONV_CH), F32),
                                    jnp.zeros((n_p, C_V_HEADS, C_DH, C_DH), F32), 0, n_p, t_p, *gdn_w,
                                    tm=min(512, t_p), tb=min(512, t_p))
            oc_s, gs_s = _gdn_group(pc, state_gdn_conv[i], state_gdn[i], mp, n_s, t_s, *gdn_w, tm=t_s,
                                    tb=_seqs_per_block(n_s, 8) * t_s)
            gla_w = (d_w_gk2[i], d_b_gk[i], d_onorm[i])
            od_p, ls_p = _gla_group(pd, jnp.zeros((n_p, D_HEADS, D_DK, D_DV), F32), 0, n_p, t_p, *gla_w,
                                    tb=min(512, t_p))
            od_s, ls_s = _gla_group(pd, state_gla[i], mp, n_s, t_s, *gla_w, tb=_seqs_per_block(n_s, 8) * t_s)
            gd_p.append(gs_p)
            gd_s.append(gs_s)
            cv_p.append(_last_rows(pc, 0, n_p, t_p, C_CONV - 1, C_CONV_CH))
            cv_s.append(_last_rows(pc, mp, n_s, t_s, C_CONV - 1, C_CONV_CH))
            gl_p.append(ls_p)
            gl_s.append(ls_s)
            oc = jnp.concatenate([oc_p, oc_s], axis=0)
            od = jnp.concatenate([od_p, od_s], axis=0)
            w_out = w_out_odd[i].astype(BF16)
            x = _matmul_residual([oc, od], [w_out[:C_V], w_out[C_V:]], x, tm=TM, tn=1024)
        x, mk_p, mv_p = _cross_block(x, layer, n_p, t_p, n_s, t_s, *cross_w)
        mk_l.append(mk_p)
        mv_l.append(mv_p)
        if layer % 2 == 0:
            x = _swiglu_ffn(x, norm_ffn_w[layer], ffd_w_gate[i].astype(BF16), ffd_w_up[i].astype(BF16),
                            ffd_w_down[i].astype(BF16), tm=TM, tf=512)
        else:
            y_p, y_s = _moe_ffn(x, norm_ffn_w[layer], moe_w_router[i], moe_w_gate, moe_w_up, moe_w_down, i,
                                tm=TM, tf=256, tc=TM_SMALL, split=mp)
            if layer + 1 < depth:
                x = jnp.concatenate([y_p, y_s], axis=0)
    if depth % 2 == 1:
        y_p, y_s = x[:mp], x[mp:]
    st = jnp.stack
    return (y_p.reshape(n_p, t_p, d), y_s.reshape(n_s, t_s, d), st(dk_p), st(dv_p), st(dk_s), st(dv_s),
            st(rw_p), st(rw_s), st(sh_p), st(sh_s), st(mk_l), st(mv_l), st(gd_p), st(gd_s), st(cv_p), st(cv_s),
            st(gl_p), st(gl_s))
```

```python
import functools
import math

import jax
import jax.numpy as jnp
from jax import lax
from jax.experimental import pallas as pl
from jax.experimental.pallas import tpu as pltpu

F32 = jnp.float32
BF16 = jnp.bfloat16
NEG_INF = float("-inf")

D_MODEL = 2048
A_HEADS = 8
A_DH = 64
A_Q = 1024
A_COLS = 3072
B_HEADS = 16
B_DH = 64
B_W = 1024
B_DECAY_LORA = 96
B_AAA_LORA = 96
B_GATE_LORA = 256
B_COLS = 3520
B_GN_EPS = 64e-5
C_QK_HEADS = 4
C_V_HEADS = 8
C_DH = 128
C_CONV = 4
C_QK = 512
C_V = 1024
C_CONV_CH = 2048
C_CHUNK = 64
D_HEADS = 4
D_DK = 128
D_DV = 256
D_K = 512
D_V = 1024
D_GATE_LORA = 16
D_GATE_NORM = 16.0
D_CHUNK = 16
N_MEM = 256
X_HEADS = 4
X_DH = 128
X_W = 512
FF_DENSE = 5632
N_EXPERTS = 8
FF_EXPERT = 2816
EPS = 1e-6

LANES = 128
VMEM_LIMIT = 56 * 1024 * 1024


def _cparams(sem):
    return pltpu.CompilerParams(dimension_semantics=sem, vmem_limit_bytes=VMEM_LIMIT)


def _sigmoid(x):
    return 1.0 / (1.0 + jnp.exp(-x))


def _silu(x):
    return x * _sigmoid(x)


def _softplus(x):
    return jnp.maximum(x, 0.0) + jnp.log(1.0 + jnp.exp(-jnp.abs(x)))


def _rms_rows(x, w):
    ms = jnp.mean(x * x, axis=-1, keepdims=True)
    return x * lax.rsqrt(ms + EPS) * w


def _group_sum(x, gsize):
    lane = lax.broadcasted_iota(jnp.int32, (1, LANES), 1)
    outs = []
    for c in range(x.shape[1] // LANES):
        xc = x[:, c * LANES:(c + 1) * LANES]
        if gsize == LANES:
            ss = jnp.broadcast_to(jnp.sum(xc, axis=-1, keepdims=True), xc.shape)
        else:
            lo = jnp.sum(jnp.where(lane < 64, xc, 0.0), axis=-1, keepdims=True)
            hi = jnp.sum(jnp.where(lane >= 64, xc, 0.0), axis=-1, keepdims=True)
            ss = jnp.where(lane < 64, lo, hi)
        outs.append(ss)
    return outs[0] if len(outs) == 1 else jnp.concatenate(outs, axis=1)


def _group_norm_scale(x, gsize, inv_n, eps):
    return x * lax.rsqrt(_group_sum(x * x, gsize) * inv_n + eps)


def _split3(a):
    a1 = a.astype(BF16)
    r1 = a - a1.astype(F32)
    a2 = r1.astype(BF16)
    a3 = (r1 - a2.astype(F32)).astype(BF16)
    return a1, a2, a3


def _dot_f32(a, b):
    a1 = a.astype(BF16)
    a2 = (a - a1.astype(F32)).astype(BF16)
    b1 = b.astype(BF16)
    b2 = (b - b1.astype(F32)).astype(BF16)
    d = lambda x, y: jnp.dot(x, y, preferred_element_type=F32)
    return d(a1, b2) + d(a2, b1) + d(a1, b1)


def _norm_mm_kernel(x_ref, nw_ref, w_ref, hw_ref, o_ref, h_scr, *, n_norm_tiles, gsize):
    j = pl.program_id(1)

    @pl.when(j == 0)
    def _():
        h_scr[...] = _rms_rows(x_ref[...], nw_ref[...]).astype(BF16)

    acc = jnp.dot(h_scr[...], w_ref[...], preferred_element_type=F32)
    if n_norm_tiles == 0:
        o_ref[...] = acc
    else:
        @pl.when(j < n_norm_tiles)
        def _():
            o_ref[...] = _group_norm_scale(acc, gsize, 1.0 / gsize, EPS) * hw_ref[...]

        @pl.when(j >= n_norm_tiles)
        def _():
            o_ref[...] = acc


def _norm_matmul(x, nw, w, *, tm, tn, head_w=None, n_norm_cols=0, gsize=LANES):
    m, k = x.shape
    n = w.shape[1]
    assert m % tm == 0 and n % tn == 0 and n_norm_cols % tn == 0
    if head_w is None:
        head_w = jnp.ones((1, n), F32)
    kern = functools.partial(_norm_mm_kernel, n_norm_tiles=n_norm_cols // tn, gsize=gsize)
    return pl.pallas_call(
        kern,
        grid=(m // tm, n // tn),
        in_specs=[
            pl.BlockSpec((tm, k), lambda i, j: (i, 0)),
            pl.BlockSpec((1, k), lambda i, j: (0, 0)),
            pl.BlockSpec((k, tn), lambda i, j: (0, j)),
            pl.BlockSpec((1, tn), lambda i, j: (0, j)),
        ],
        out_specs=pl.BlockSpec((tm, tn), lambda i, j: (i, j)),
        out_shape=jax.ShapeDtypeStruct((m, n), F32),
        scratch_shapes=[pltpu.VMEM((tm, k), BF16)],
        compiler_params=_cparams(("parallel", "arbitrary")),
        name="norm_matmul",
    )(x, nw.reshape(1, k), w, head_w)


def _mm_res_kernel(*refs, n_in):
    x_ref = refs[2 * n_in]
    o_ref = refs[2 * n_in + 1]
    acc = x_ref[...]
    for t in range(n_in):
        acc = acc + jnp.dot(refs[t][...], refs[n_in + t][...], preferred_element_type=F32)
    o_ref[...] = acc


def _matmul_residual(acts, weights, x, *, tm, tn):
    m, n = x.shape
    n_in = len(acts)
    in_specs = [pl.BlockSpec((tm, a.shape[1]), lambda i, j: (i, 0)) for a in acts]
    in_specs += [pl.BlockSpec((w.shape[0], tn), lambda i, j: (0, j)) for w in weights]
    in_specs += [pl.BlockSpec((tm, tn), lambda i, j: (i, j))]
    return pl.pallas_call(
        functools.partial(_mm_res_kernel, n_in=n_in),
        grid=(m // tm, n // tn),
        in_specs=in_specs,
        out_specs=pl.BlockSpec((tm, tn), lambda i, j: (i, j)),
        out_shape=jax.ShapeDtypeStruct((m, n), F32),
        compiler_params=_cparams(("parallel", "arbitrary")),
        name="matmul_residual",
    )(*acts, *weights, x)


def _swiglu_kernel(x_ref, nw_ref, wg_ref, wu_ref, wd_ref, o_ref, h_scr):
    f = pl.program_id(1)

    @pl.when(f == 0)
    def _():
        x = x_ref[...]
        h_scr[...] = _rms_rows(x, nw_ref[...]).astype(BF16)
        o_ref[...] = x

    h = h_scr[...]
    g = jnp.dot(h, wg_ref[...], preferred_element_type=F32)
    u = jnp.dot(h, wu_ref[...], preferred_element_type=F32)
    a = (_silu(g) * u).astype(BF16)
    o_ref[...] += jnp.dot(a, wd_ref[...], preferred_element_type=F32)


def _swiglu_ffn(x, nw, wg, wu, wd, *, tm, tf):
    m, d = x.shape
    ff = wg.shape[1]
    return pl.pallas_call(
        _swiglu_kernel,
        grid=(m // tm, ff // tf),
        in_specs=[
            pl.BlockSpec((tm, d), lambda i, f: (i, 0)),
            pl.BlockSpec((1, d), lambda i, f: (0, 0)),
            pl.BlockSpec((d, tf), lambda i, f: (0, f)),
            pl.BlockSpec((d, tf), lambda i, f: (0, f)),
            pl.BlockSpec((tf, d), lambda i, f: (f, 0)),
        ],
        out_specs=pl.BlockSpec((tm, d), lambda i, f: (i, 0)),
        out_shape=jax.ShapeDtypeStruct((m, d), F32),
        scratch_shapes=[pltpu.VMEM((tm, d), BF16)],
        compiler_params=_cparams(("parallel", "arbitrary")),
        name="swiglu_ffn",
    )(x, nw.reshape(1, d), wg, wu, wd)


MOE_TILE = 768
ROW_SLABS = D_MODEL // LANES


def _rows_to_slabs(ref, x):
    ref[...] = jnp.swapaxes(jnp.stack([x[:, j * LANES:(j + 1) * LANES] for j in range(ROW_SLABS)], axis=0), 0, 1)


def _slabs_to_rows(ref):
    xt = jnp.swapaxes(ref[...], 0, 1)
    return jnp.concatenate([xt[j] for j in range(ROW_SLABS)], axis=1)


def _moe_router_kernel(x_ref, nw_ref, wr_ref, h_ref, info_ref):
    tm = x_ref.shape[0]
    lane = lax.broadcasted_iota(jnp.int32, (tm, LANES), 1)
    hf = _rms_rows(x_ref[...], nw_ref[...])
    _rows_to_slabs(h_ref, hf)
    logits = jnp.where(lane < N_EXPERTS, _dot_f32(hf, wr_ref[...]), NEG_INF)
    m1 = jnp.max(logits, axis=-1, keepdims=True)
    i1 = jnp.min(jnp.where(logits == m1, lane, LANES), axis=-1, keepdims=True)
    rest = jnp.where(lane == i1, NEG_INF, logits)
    m2 = jnp.max(rest, axis=-1, keepdims=True)
    i2 = jnp.min(jnp.where(rest == m2, lane, LANES), axis=-1, keepdims=True)
    e2 = jnp.exp(m2 - m1)
    vals = [i1.astype(F32), i2.astype(F32), 1.0 / (1.0 + e2), e2 / (1.0 + e2)]
    info = jnp.zeros((tm, LANES), F32)
    for k, val in enumerate(vals):
        info = jnp.where(lane == k, val, info)
    info_ref[...] = info


def _moe_gather_kernel(src_ref, nu_ref, h_ref, xs_ref, sem):
    t = pl.program_id(0)
    tg = xs_ref.shape[0]

    def copy(r):
        return pltpu.make_async_copy(h_ref.at[src_ref[t * tg + r]], xs_ref.at[r], sem.at[0])

    def issue(k, carry):
        copy(2 * k).start(priority=0)
        copy(2 * k + 1).start(priority=1)
        return carry

    def drain(r, carry):
        copy(r).wait()
        return carry

    @pl.when(t < nu_ref[0])
    def _():
        lax.fori_loop(0, tg // 2, issue, 0, unroll=4)
        lax.fori_loop(0, tg, drain, 0, unroll=8)

    @pl.when(t >= nu_ref[0])
    def _():
        xs_ref[...] = jnp.zeros(xs_ref.shape, F32)


def _moe_expert_kernel(te_ref, nu_ref, xs_ref, wg_ref, wu_ref, wd_ref, y_ref, xb_scr, acc_scr):
    del te_ref
    t = pl.program_id(0)
    f = pl.program_id(1)

    @pl.when(f == 0)
    def _():
        xb_scr[...] = _slabs_to_rows(xs_ref).astype(BF16)
        acc_scr[...] = jnp.zeros(acc_scr.shape, F32)

    @pl.when(t < nu_ref[0])
    def _():
        xb = xb_scr[...]
        g = jnp.dot(xb, wg_ref[...].astype(BF16), preferred_element_type=F32)
        u = jnp.dot(xb, wu_ref[...].astype(BF16), preferred_element_type=F32)
        acc_scr[...] += jnp.dot((_silu(g) * u).astype(BF16), wd_ref[...].astype(BF16),
                                preferred_element_type=F32)

    @pl.when(f == pl.num_programs(1) - 1)
    def _():
        _rows_to_slabs(y_ref, acc_scr[...])


def _moe_combine_kernel(dest_ref, x_ref, info_ref, y_ref, o1_ref, o2_ref, ybuf, sem, *, n_first):
    i = pl.program_id(0)
    tc = x_ref.shape[0]
    n = dest_ref.shape[0] // 2

    def copy(r, slot):
        return pltpu.make_async_copy(y_ref.at[dest_ref[slot * n + i * tc + r]], ybuf.at[slot, r], sem.at[slot])

    def issue(r, carry):
        copy(r, 0).start(priority=0)
        copy(r, 1).start(priority=1)
        return carry

    def drain(r, carry):
        copy(r, 0).wait()
        copy(r, 1).wait()
        return carry

    lax.fori_loop(0, tc, issue, 0, unroll=8)
    lax.fori_loop(0, tc, drain, 0, unroll=8)
    info = info_ref[...]
    y1 = _slabs_to_rows(ybuf.at[0])
    y2 = _slabs_to_rows(ybuf.at[1])
    res = x_ref[...] + info[:, 2:3] * y1 + info[:, 3:4] * y2

    @pl.when(i < n_first)
    def _():
        o1_ref[...] = res

    @pl.when(i >= n_first)
    def _():
        o2_ref[...] = res


def _moe_ffn(x, nw, w_router, wg, wu, wd, w_layer, *, tm, tf, tc, split):
    m, d = x.shape
    _, ne, _, ff = wg.shape
    tg = MOE_TILE
    assert d == D_MODEL and m % tm == 0 and m % tc == 0 and ff % tf == 0
    n_tiles = -(-(2 * m + ne * (tg - 1)) // tg)
    p_rows = n_tiles * tg
    wr = jnp.zeros((d, LANES), F32).at[:, :ne].set(w_router)
    h3, info = pl.pallas_call(
        _moe_router_kernel,
        grid=(m // tm,),
        in_specs=[pl.BlockSpec((tm, d), lambda i: (i, 0)), pl.BlockSpec((1, d), lambda i: (0, 0)),
                  pl.BlockSpec((d, LANES), lambda i: (0, 0))],
        out_specs=[pl.BlockSpec((tm, ROW_SLABS, LANES), lambda i: (i, 0, 0)),
                   pl.BlockSpec((tm, LANES), lambda i: (i, 0))],
        out_shape=[jax.ShapeDtypeStruct((m, ROW_SLABS, LANES), F32), jax.ShapeDtypeStruct((m, LANES), F32)],
        compiler_params=_cparams(("parallel",)),
        name="moe_router",
    )(x, nw.reshape(1, d), wr)

    e12 = info[:, :2].astype(jnp.int32)
    onehot = (e12[:, :1] == jnp.arange(ne)) | (e12[:, 1:2] == jnp.arange(ne))
    csum = jnp.cumsum(onehot.astype(jnp.int32), axis=0)
    rank = csum - onehot
    padded = (csum[-1] + tg - 1) // tg * tg
    gend = jnp.cumsum(padded)
    base = (gend - padded)[None, :] + rank
    dest = jnp.concatenate([jnp.take_along_axis(base, e12[:, :1], axis=1)[:, 0],
                            jnp.take_along_axis(base, e12[:, 1:2], axis=1)[:, 0]]).astype(jnp.int32)
    tile_expert = jnp.minimum(jnp.sum(jnp.arange(n_tiles)[:, None] * tg >= gend[None, :], axis=1),
                              ne - 1).astype(jnp.int32)
    n_used = (gend[-1:] // tg).astype(jnp.int32)

    any_spec = pl.BlockSpec(memory_space=pl.ANY)
    tok_ids = jnp.tile(jnp.arange(m, dtype=jnp.int32), 2)
    src = jnp.zeros((p_rows,), jnp.int32).at[dest].set(tok_ids, unique_indices=True, mode="promise_in_bounds")
    xs3 = pl.pallas_call(
        _moe_gather_kernel,
        grid_spec=pltpu.PrefetchScalarGridSpec(
            num_scalar_prefetch=2, grid=(n_tiles,), in_specs=[any_spec],
            out_specs=pl.BlockSpec((tg, ROW_SLABS, LANES), lambda t, sr, nu: (t, 0, 0)),
            scratch_shapes=[pltpu.SemaphoreType.DMA((1,))]),
        out_shape=jax.ShapeDtypeStruct((p_rows, ROW_SLABS, LANES), F32),
        compiler_params=_cparams(("arbitrary",)),
        name="moe_gather",
    )(src, n_used, h3)

    last = lambda t, nu: jnp.minimum(t, nu[0] - 1)
    y3 = pl.pallas_call(
        _moe_expert_kernel,
        grid_spec=pltpu.PrefetchScalarGridSpec(
            num_scalar_prefetch=2,
            grid=(n_tiles, ff // tf),
            in_specs=[
                pl.BlockSpec((tg, ROW_SLABS, LANES), lambda t, f, te, nu: (last(t, nu), 0, 0)),
                pl.BlockSpec((None, None, d, tf), lambda t, f, te, nu: (w_layer, te[last(t, nu)], 0, f)),
                pl.BlockSpec((None, None, d, tf), lambda t, f, te, nu: (w_layer, te[last(t, nu)], 0, f)),
                pl.BlockSpec((None, None, tf, d), lambda t, f, te, nu: (w_layer, te[last(t, nu)], f, 0)),
            ],
            out_specs=pl.BlockSpec((tg, ROW_SLABS, LANES), lambda t, f, te, nu: (t, 0, 0)),
            scratch_shapes=[pltpu.VMEM((tg, d), BF16), pltpu.VMEM((tg, d), F32)]),
        out_shape=jax.ShapeDtypeStruct((p_rows, ROW_SLABS, LANES), F32),
        compiler_params=_cparams(("arbitrary", "arbitrary")),
        name="moe_experts",
    )(tile_expert, n_used, xs3, wg, wu, wd)

    assert split % tc == 0 and 0 < split < m
    n_first = split // tc
    return pl.pallas_call(
        functools.partial(_moe_combine_kernel, n_first=n_first),
        grid_spec=pltpu.PrefetchScalarGridSpec(
            num_scalar_prefetch=1,
            grid=(m // tc,),
            in_specs=[pl.BlockSpec((tc, d), lambda i, de: (i, 0)), pl.BlockSpec((tc, LANES), lambda i, de: (i, 0)),
                      any_spec],
            out_specs=[pl.BlockSpec((tc, d), lambda i, de: (jnp.minimum(i, n_first - 1), 0)),
                       pl.BlockSpec((tc, d), lambda i, de: (jnp.maximum(i - n_first, 0), 0))],
            scratch_shapes=[pltpu.VMEM((2, tc, ROW_SLABS, LANES), F32), pltpu.SemaphoreType.DMA((2,))]),
        out_shape=[jax.ShapeDtypeStruct((split, d), F32), jax.ShapeDtypeStruct((m - split, d), F32)],
        compiler_params=_cparams(("arbitrary",)),
        name="moe_combine",
    )(dest, x, info, y3)


def _diff_lambda_vec(lam_ref, lam_init):
    lv = lam_ref[...]
    d1 = jnp.sum(lv[0:1, :] * lv[1:2, :], axis=-1, keepdims=True)
    d2 = jnp.sum(lv[2:3, :] * lv[3:4, :], axis=-1, keepdims=True)
    return jnp.exp(d1) - jnp.exp(d2) + lam_init


def _softmax_rows(s):
    m = jnp.max(s, axis=-1, keepdims=True)
    e = jnp.exp(s - m)
    return e, jnp.sum(e, axis=-1, keepdims=True)


def _diffattn_prompt_kernel(slope_ref, q_ref, k_ref, v_ref, lam_ref, subln_ref, o_ref, *, tq, lam_init):
    t = q_ref.shape[0]
    h = pl.program_id(1)
    slope = slope_ref[h]
    lane = lax.broadcasted_iota(jnp.int32, (1, LANES), 1)
    lam = _diff_lambda_vec(lam_ref, lam_init)
    kb = k_ref[...].astype(BF16)
    vb = v_ref[...].astype(BF16)
    nt = (((1,), (1,)), ((), ()))
    n_blocks = t // tq
    row = lax.broadcasted_iota(jnp.int32, (tq, t), 0) + (t - tq)
    col = lax.broadcasted_iota(jnp.int32, (tq, t), 1)
    dist = (row - col).astype(F32)
    bias_all = jnp.where(dist >= 0.0, -slope * dist, NEG_INF)
    for qi in range(n_blocks):
        n_keys = (qi + 1) * tq
        q = q_ref[qi * tq:(qi + 1) * tq, :] * (A_DH ** -0.5)
        q1 = jnp.where(lane < A_DH, q, 0.0).astype(BF16)
        q2 = jnp.where(lane >= A_DH, q, 0.0).astype(BF16)
        kk = kb[:n_keys]
        vv = vb[:n_keys]
        off = (n_blocks - 1 - qi) * tq
        bias = bias_all[:, off:off + n_keys]
        e1, l1 = _softmax_rows(lax.dot_general(q1, kk, nt, preferred_element_type=F32) + bias)
        e2, l2 = _softmax_rows(lax.dot_general(q2, kk, nt, preferred_element_type=F32) + bias)
        o = jnp.dot(e1.astype(BF16), vv, preferred_element_type=F32) * (1.0 / l1)
        o = o - jnp.dot(e2.astype(BF16), vv, preferred_element_type=F32) * (lam / l2)
        o = _rms_rows(o, subln_ref[...]) * (1.0 - lam_init)
        o_ref[qi * tq:(qi + 1) * tq, :] = o.astype(BF16)


def _diff_attention_prompt(pa, n_batch, t, slopes, lam_stack, subln, lam_init, *, tq):
    kern = functools.partial(_diffattn_prompt_kernel, tq=tq, lam_init=lam_init)
    return pl.pallas_call(
        kern,
        grid=(n_batch, A_HEADS),
        in_specs=[
            pl.BlockSpec(memory_space=pltpu.SMEM),
            pl.BlockSpec((t, LANES), lambda b, h: (b, h)),
            pl.BlockSpec((t, LANES), lambda b, h: (b, A_HEADS + h)),
            pl.BlockSpec((t, LANES), lambda b, h: (b, 2 * A_HEADS + h)),
            pl.BlockSpec((4, A_DH), lambda b, h: (0, 0)),
            pl.BlockSpec((1, LANES), lambda b, h: (0, 0)),
        ],
        out_specs=pl.BlockSpec((t, LANES), lambda b, h: (b, h)),
        out_shape=jax.ShapeDtypeStruct((n_batch * t, A_HEADS * LANES), BF16),
        compiler_params=_cparams(("parallel", "parallel")),
        name="diff_attention_prompt",
    )(slopes, pa, pa, pa, lam_stack, subln.reshape(1, LANES))


PAGES_PER_STEP = 16


def _diffattn_sample_kernel(pt_ref, q_ref, kn_ref, vn_ref, *rest, past, lam_init):
    del pt_ref
    npg = PAGES_PER_STEP
    k_refs, v_refs = rest[:npg], rest[npg:2 * npg]
    lam_ref, subln_ref, o_ref, qq_scr, m_scr, l_scr, acc_scr, bias_scr = rest[2 * npg:]
    g = pl.program_id(1)
    ts = q_ref.shape[0]
    page = k_refs[0].shape[0]
    lane = lax.broadcasted_iota(jnp.int32, (1, LANES), 1)
    hsub = lax.broadcasted_iota(jnp.int32, (A_HEADS, LANES), 0)
    col = lax.broadcasted_iota(jnp.int32, (A_HEADS, LANES), 1)
    valid = (col // (2 * ts)) == hsub
    pos_q = past + jnp.bitwise_and(col, ts - 1)
    slope = jnp.exp2(-(8.0 / A_HEADS) * (hsub + 1).astype(F32))
    nt = (((1,), (1,)), ((), ()))
    tn = (((0,), (0,)), ((), ()))

    @pl.when(g == 0)
    def _():
        rows = []
        for h in range(A_HEADS):
            q = q_ref[:, h * LANES:(h + 1) * LANES] * (A_DH ** -0.5)
            rows += [jnp.where(lane < A_DH, q, 0.0), jnp.where(lane >= A_DH, q, 0.0)]
        qq_scr[...] = jnp.concatenate(rows, axis=0).astype(BF16)
        m_scr[...] = jnp.where(valid, NEG_INF, 0.0)
        l_scr[...] = jnp.zeros(l_scr.shape, F32)
        acc_scr[...] = jnp.zeros(acc_scr.shape, F32)

    def key_bias(nk, key0, causal):
        kidx = lax.broadcasted_iota(jnp.int32, (nk * A_HEADS, LANES), 0) // A_HEADS
        dist = (pos_q[None] - (key0 + kidx).reshape(nk, A_HEADS, LANES)).astype(F32)
        keep = valid[None] & (dist >= 0.0) if causal else valid[None]
        return jnp.where(keep, -slope[None] * dist, NEG_INF)

    @pl.when(g == 0)
    def _():
        bias_scr[...] = key_bias(page, 0, False)

    def scores(k3):
        nk = k3.shape[0]
        k2 = k3.reshape(nk * A_HEADS, LANES).astype(BF16)
        return lax.dot_general(k2, qq_scr[...], nt, preferred_element_type=F32).reshape(nk, A_HEADS, LANES)

    def absorb(ss, v3s):
        m_old = m_scr[...]
        m_new = m_old
        for s in ss:
            m_new = jnp.maximum(m_new, jnp.max(s, axis=0))
        alpha = jnp.exp(m_old - m_new)
        l_new = alpha * l_scr[...]
        acc = jnp.sum(jnp.where(valid, alpha, 0.0), axis=0, keepdims=True) * acc_scr[...]
        for s, v3 in zip(ss, v3s):
            nk = s.shape[0]
            pe = jnp.exp(s - m_new[None])
            l_new = l_new + jnp.sum(pe, axis=0)
            v2 = v3.reshape(nk * A_HEADS, LANES).astype(BF16)
            acc = acc + lax.dot_general(v2, pe.reshape(nk * A_HEADS, LANES).astype(BF16), tn,
                                        preferred_element_type=F32)
        m_scr[...] = m_new
        l_scr[...] = l_new
        acc_scr[...] = acc

    absorb([scores(k_refs[j][...]) + bias_scr[...] + (slope * ((g * npg + j) * page).astype(F32))[None]
            for j in range(npg)], [v_refs[j][...] for j in range(npg)])

    @pl.when(g == pl.num_programs(1) - 1)
    def _():
        absorb([scores(kn_ref[...]) + key_bias(ts, past, True)], [vn_ref[...]])
        lam = _diff_lambda_vec(lam_ref, lam_init)
        l_row = jnp.sum(jnp.where(valid, l_scr[...], 0.0), axis=0, keepdims=True)
        ot = acc_scr[...] / l_row
        ot = ot - lam * pltpu.roll(ot, LANES - ts, 1)
        o = ot.T
        for h in range(A_HEADS):
            oh = _rms_rows(o[h * 2 * ts:h * 2 * ts + ts, :], subln_ref[...]) * (1.0 - lam_init)
            o_ref[:, h * LANES:(h + 1) * LANES] = oh.astype(BF16)


def _diff_attention_sample(pa, row0, n_batch, ts, cache_k, cache_v, layer_idx, page_table, lam_stack, subln,
                           lam_init):
    n_pages = page_table.shape[1]
    page = cache_k.shape[2]
    npg = PAGES_PER_STEP
    assert row0 % ts == 0 and n_pages % npg == 0 and 2 * ts * A_HEADS == LANES
    rb = row0 // ts
    w = A_HEADS * LANES
    new_rows = lambda c0: pa[row0:, c0:c0 + w].reshape(n_batch, ts, A_HEADS, LANES)
    kern = functools.partial(_diffattn_sample_kernel, past=n_pages * page, lam_init=lam_init)
    page_spec = lambda j: pl.BlockSpec((None, None, page, A_HEADS, LANES),
                                       lambda b, g, pt: (layer_idx, pt[b, g * npg + j], 0, 0, 0))
    new_spec = pl.BlockSpec((None, ts, A_HEADS, LANES), lambda b, g, pt: (b, 0, 0, 0))
    grid_spec = pltpu.PrefetchScalarGridSpec(
        num_scalar_prefetch=1,
        grid=(n_batch, n_pages // npg),
        in_specs=[pl.BlockSpec((ts, w), lambda b, g, pt: (rb + b, 0)), new_spec, new_spec]
        + [page_spec(j) for j in range(npg)] * 2
        + [pl.BlockSpec((4, A_DH), lambda b, g, pt: (0, 0)), pl.BlockSpec((1, LANES), lambda b, g, pt: (0, 0))],
        out_specs=pl.BlockSpec((ts, w), lambda b, g, pt: (b, 0)),
        scratch_shapes=[
            pltpu.VMEM((LANES, LANES), BF16),
            pltpu.VMEM((A_HEADS, LANES), F32),
            pltpu.VMEM((A_HEADS, LANES), F32),
            pltpu.VMEM((LANES, LANES), F32),
            pltpu.VMEM((page, A_HEADS, LANES), F32),
        ],
    )
    return pl.pallas_call(
        kern,
        grid_spec=grid_spec,
        out_shape=jax.ShapeDtypeStruct((n_batch * ts, w), BF16),
        compiler_params=_cparams(("parallel", "arbitrary")),
        name="diff_attention_sample",
    )(page_table, pa, new_rows(w), new_rows(2 * w), *([cache_k] * npg), *([cache_v] * npg), lam_stack,
      subln.reshape(1, LANES))


def _cross_attn_kernel(q_ref, k_ref, v_ref, qn_ref, o_ref):
    nt = (((1,), (1,)), ((), ()))
    if len(k_ref.shape) == 3:
        kt = jnp.swapaxes(k_ref[...], 0, 1).astype(BF16)
        vt = jnp.swapaxes(v_ref[...], 0, 1).astype(BF16)
        heads = [(kt[h], vt[h]) for h in range(X_HEADS)]
    else:
        heads = [(k_ref[:, h * LANES:(h + 1) * LANES].astype(BF16), v_ref[:, h * LANES:(h + 1) * LANES].astype(BF16))
                 for h in range(X_HEADS)]
    for h, (kh, vh) in enumerate(heads):
        sl = slice(h * LANES, (h + 1) * LANES)
        q = _rms_rows(q_ref[:, sl], qn_ref[...]) * (X_DH ** -0.5)
        s = lax.dot_general(q.astype(BF16), kh, nt, preferred_element_type=F32)
        e, l = _softmax_rows(s)
        p = (e * (1.0 / l)).astype(BF16)
        o_ref[:, sl] = jnp.dot(p, vh, preferred_element_type=F32).astype(BF16)


def _cross_attention(q, row0, n_batch, t, mk, mv, q_norm, *, tq, mem_layer=None):
    assert t % tq == 0 and row0 % tq == 0
    nq = t // tq
    rb = row0 // tq
    if mem_layer is None:
        mem_spec = pl.BlockSpec((None, N_MEM, X_W), lambda b, i: (b, 0, 0))
    else:
        mem_spec = pl.BlockSpec((None, None, N_MEM, X_HEADS, X_DH), lambda b, i: (mem_layer, b, 0, 0, 0))
    return pl.pallas_call(
        _cross_attn_kernel,
        grid=(n_batch, nq),
        in_specs=[
            pl.BlockSpec((tq, X_W), lambda b, i: (rb + b * nq + i, 0)),
            mem_spec,
            mem_spec,
            pl.BlockSpec((1, LANES), lambda b, i: (0, 0)),
        ],
        out_specs=pl.BlockSpec((tq, X_W), lambda b, i: (b * nq + i, 0)),
        out_shape=jax.ShapeDtypeStruct((n_batch * t, X_W), BF16),
        compiler_params=_cparams(("parallel", "parallel")),
        name="cross_attention",
    )(q, mk, mv, q_norm.reshape(1, LANES))


PB_LG = 3 * B_W
PB_LW = PB_LG + B_GATE_LORA
PB_LA = PB_LW + LANES
PB_COLS = PB_LA + LANES


def _swap_halves(x):
    tiles = [pltpu.roll(x[:, c * LANES:(c + 1) * LANES], LANES // 2, 1) for c in range(x.shape[1] // LANES)]
    return tiles[0] if len(tiles) == 1 else jnp.concatenate(tiles, axis=1)


def _rwkv_prep_kernel(pb_ref, halo_ref, start_ref, mu_ref, w0_ref, w2_ref, a0_ref, a2_ref, g2_ref, kk_ref, ka_ref,
                      r_ref, kf_ref, v_ref, g_ref, rt_ref, at_ref, bt_ref, kt_ref, bh_ref, kh_ref, pc_ref,
                      *, n_prompt_tiles, chunk_p, chunk_s, t_p, t_s):
    tm = pb_ref.shape[0]
    i = pl.program_id(0)
    pb = pb_ref[...]
    row = lax.broadcasted_iota(jnp.int32, (tm, 1), 0)
    prev = jnp.where(row == 0, halo_ref[HALO - 1:HALO, :], pltpu.roll(pb, 1, 0))
    seq_len = jnp.where(i < n_prompt_tiles, t_p, t_s)
    is_start = jnp.bitwise_and(i * tm + row, seq_len - 1) == 0
    ps = jnp.where(is_start, start_ref[...], prev)
    xs = pb + (ps - pb) * mu_ref[...]
    r = xs[:, :B_W]
    k = xs[:, B_W:2 * B_W]
    lg = xs[:, PB_LG:PB_LW]
    lw_in = xs[:, PB_LW:PB_LA]
    la = xs[:, PB_LA:PB_COLS]
    wl = w0_ref[...] + jnp.dot(jnp.tanh(lw_in).astype(BF16), w2_ref[...], preferred_element_type=F32)
    lw = -jnp.exp(-_softplus(-wl) - 0.5)
    a = _sigmoid(a0_ref[...] + jnp.dot(la.astype(BF16), a2_ref[...], preferred_element_type=F32))
    kk = _group_norm_scale(k * kk_ref[...], B_DH, 1.0, EPS)
    kf = k * (1.0 + (a - 1.0) * ka_ref[...])
    b = kk * a
    r_ref[...] = r
    kf_ref[...] = kf
    v_ref[...] = xs[:, 2 * B_W:3 * B_W]
    g_ref[...] = jnp.dot(_sigmoid(lg).astype(BF16), g2_ref[...], preferred_element_type=F32)

    chunk = jnp.where(pl.program_id(0) < n_prompt_tiles, chunk_p, chunk_s)
    rowc = jnp.bitwise_and(lax.broadcasted_iota(jnp.int32, (tm, B_W), 0), chunk - 1)
    lp = lw
    sfx = lw
    step = 1
    while step < max(chunk_p, chunk_s):
        lp = lp + jnp.where(rowc >= step, pltpu.roll(lp, step, 0), 0.0)
        sfx = sfx + jnp.where(rowc + step < chunk, pltpu.roll(sfx, tm - step, 0), 0.0)
        step *= 2
    sfx = sfx - lw
    e_neg = jnp.exp(-lp)
    e_sfx = jnp.exp(sfx)
    sw = lambda z: _swap_halves(z).astype(BF16)
    rt_ref[...] = sw(r * jnp.exp(lp))
    at_ref[...] = sw(kk * jnp.exp(lp - lw))
    bt_ref[...] = sw(b * e_neg)
    kt_ref[...] = sw(kf * e_neg)
    bh_ref[...] = sw(b * e_sfx)
    kh_ref[...] = sw(kf * e_sfx)
    pc_ref[...] = _swap_halves(jnp.exp(lp + sfx))


def _rwkv_prep(pb, starts, mu, w0, w2, a0, a2, g2, k_k, k_a, *, tm, n_prompt_tiles, chunk_p, chunk_s, t_p, t_s):
    m = pb.shape[0]
    assert tm % chunk_p == 0 and tm % chunk_s == 0 and tm % HALO == 0
    assert t_p & (t_p - 1) == 0 and t_s & (t_s - 1) == 0 and (n_prompt_tiles * tm) % t_s == 0
    npt = n_prompt_tiles
    row = lambda n: pl.BlockSpec((1, n), lambda i: (0, 0))
    full = lambda a: pl.BlockSpec(a.shape, lambda i: (0, 0))
    tok = lambda n: pl.BlockSpec((tm, n), lambda i: (i, 0))
    f32o = jax.ShapeDtypeStruct((m, B_W), F32)
    b16o = jax.ShapeDtypeStruct((m, B_W), BF16)
    kern = functools.partial(_rwkv_prep_kernel, n_prompt_tiles=npt, chunk_p=chunk_p, chunk_s=chunk_s, t_p=t_p,
                             t_s=t_s)
    return pl.pallas_call(
        kern,
        grid=(m // tm,),
        in_specs=[tok(PB_COLS),
                  pl.BlockSpec((HALO, PB_COLS), lambda i: (jnp.maximum(i * (tm // HALO) - 1, 0), 0)),
                  pl.BlockSpec((tm, PB_COLS), lambda i: (jnp.maximum(i - npt + 1, 0), 0)),
                  row(PB_COLS), row(B_W), full(w2), row(B_W), full(a2), full(g2), row(B_W), row(B_W)],
        out_specs=[tok(B_W)] * 11,
        out_shape=[f32o] * 4 + [b16o] * 6 + [f32o],
        compiler_params=_cparams(("parallel",)),
        name="rwkv_prep",
    )(pb, pb, starts, mu, w0.reshape(1, B_W), w2, a0.reshape(1, B_W), a2, g2, k_k.reshape(1, B_W),
      k_a.reshape(1, B_W))


SOLVE_BLOCK = 16


def _split2(a):
    hi = a.astype(BF16)
    return hi, (a - hi.astype(F32)).astype(BF16)


def _solve_unit_lower_many(lmats, rhss):
    ns = len(lmats)
    c = lmats[0].shape[0]
    n = rhss[0].shape[1]
    bs = min(SOLVE_BLOCK, c)
    col = lax.broadcasted_iota(jnp.int32, (bs, c), 1)
    d = lambda x, y: jnp.dot(x, y, preferred_element_type=F32)
    done = [[] for _ in range(ns)]
    for blk in range(c // bs):
        rows = slice(blk * bs, (blk + 1) * bs)
        lrows = [lm[rows, :] for lm in lmats]
        rs = [rh[rows, :] for rh in rhss]
        if blk > 0:
            for s in range(ns):
                xs = jnp.concatenate(done[s] + [jnp.zeros((c - blk * bs, n), F32)], axis=0)
                lh, ll = _split2(jnp.where(col < blk * bs, lrows[s], 0.0))
                xh, xl = _split2(xs)
                rs[s] = rs[s] - (d(lh, xl) + d(ll, xh) + d(lh, xh))
        for j in range(bs - 1):
            for s in range(ns):
                rs[s] = rs[s] - lrows[s][:, blk * bs + j:blk * bs + j + 1] * rs[s][j:j + 1, :]
        for s in range(ns):
            done[s].append(rs[s])
    return [dn[0] if len(dn) == 1 else jnp.concatenate(dn, axis=0) for dn in done]


def _solve_unit_lower(lmat, rhs):
    return _solve_unit_lower_many([lmat], [rhs])[0]


def _rwkv_chunk_kernel(rt_ref, at_ref, bt_ref, kt_ref, bh_ref, kh_ref, v_ref, pc_ref, z0_ref, y_ref, zo_ref, z_scr,
                       *, chunk):
    i = pl.program_id(2)
    tb = rt_ref.shape[0]

    @pl.when(i == 0)
    def _():
        z_scr[...] = z0_ref[...]

    lo = lax.broadcasted_iota(jnp.int32, (1, LANES), 1) < B_DH
    zr = lax.broadcasted_iota(jnp.int32, (LANES, LANES), 0) < B_DH
    zc = lax.broadcasted_iota(jnp.int32, (LANES, LANES), 1) < B_DH
    offdiag = zr != zc
    ri = lax.broadcasted_iota(jnp.int32, (chunk, chunk), 0)
    ci = lax.broadcasted_iota(jnp.int32, (chunk, chunk), 1)
    nt = (((1,), (1,)), ((), ()))
    tn = (((0,), (0,)), ((), ()))
    d = lambda x, y: jnp.dot(x, y, preferred_element_type=F32)
    zero = jnp.zeros((), BF16)
    n_chunks = tb // chunk
    n_pp = rt_ref.shape[1] // LANES
    lmats, rhss, rkvs, rbs = [], [], [], []
    for p in range(n_pp):
        ls = slice(p * LANES, (p + 1) * LANES)
        for c in range(n_chunks):
            rs = slice(c * chunk, (c + 1) * chunk)
            rt, at, bt, kt = rt_ref[rs, ls], at_ref[rs, ls], bt_ref[rs, ls], kt_ref[rs, ls]
            v = v_ref[rs, ls].astype(BF16)
            for head in range(2):
                kmask = lo if head == 1 else jnp.logical_not(lo)
                a_h, r_h = jnp.where(kmask, at, zero), jnp.where(kmask, rt, zero)
                b_h, k_h = jnp.where(kmask, bt, zero), jnp.where(kmask, kt, zero)
                v_h = jnp.where(kmask, zero, v)
                lhs = jnp.concatenate([a_h, r_h], axis=0)
                gb = lax.dot_general(lhs, b_h, nt, preferred_element_type=F32)
                gk = lax.dot_general(lhs, k_h, nt, preferred_element_type=F32)
                lmats.append(jnp.where(ri > ci, gb[:chunk], 0.0))
                rbs.append(jnp.where(ri >= ci, gb[chunk:], 0.0).astype(BF16))
                ak = jnp.where(ri > ci, gk[:chunk], 0.0).astype(BF16)
                rk = jnp.where(ri >= ci, gk[chunk:], 0.0).astype(BF16)
                rhss.append(a_h.astype(F32) + d(ak, v_h))
                rkvs.append(d(rk, v_h))
    xs = _solve_unit_lower_many(lmats, rhss)
    n_sq = z_scr.shape[0]
    cps = n_chunks // n_sq
    zs = {(sq, p): z_scr[sq, p] for sq in range(n_sq) for p in range(n_pp)}
    for c in range(n_chunks):
        rs = slice(c * chunk, (c + 1) * chunk)
        for p in range(n_pp):
            ls = slice(p * LANES, (p + 1) * LANES)
            s0 = 2 * (p * n_chunks + c)
            key = (c // cps, p)
            rt = rt_ref[rs, ls]
            stacked = jnp.concatenate([xs[s0].astype(BF16), xs[s0 + 1].astype(BF16),
                                       jnp.where(lo, zero, rt), jnp.where(lo, rt, zero)], axis=0)
            big = lax.dot_general(stacked, zs[key].astype(BF16), nt, preferred_element_type=F32)
            u0 = jnp.where(lo, big[:chunk], 0.0)
            u1 = jnp.where(lo, 0.0, big[chunk:2 * chunk])
            y = jnp.where(lo, big[2 * chunk:3 * chunk], big[3 * chunk:]) + rkvs[s0] + rkvs[s0 + 1]
            y = y - d(rbs[s0], u0.astype(BF16)) - d(rbs[s0 + 1], u1.astype(BF16))
            y_ref[rs, ls] = y
            upd = lax.dot_general(jnp.concatenate([v_ref[rs, ls].astype(BF16), (u0 + u1).astype(BF16)], axis=0),
                                  jnp.concatenate([kh_ref[rs, ls], -bh_ref[rs, ls]], axis=0), tn,
                                  preferred_element_type=F32)
            zs[key] = jnp.where(offdiag, zs[key] * pc_ref[c * chunk:c * chunk + 1, ls] + upd, zs[key])
    for (sq, p), val in zs.items():
        z_scr[sq, p] = val

    @pl.when(i == pl.num_programs(2) - 1)
    def _():
        zo_ref[...] = z_scr[...]


RWKV_PAIRS_PER_STEP = 4


def _rwkv_chunks(rt, at, bt, kt, bh, kh, v, pc, z0, row0, n_seq, t, *, tb, chunk, pps=RWKV_PAIRS_PER_STEP):
    spb, nb, n_groups, rb = _seq_blocking(row0, n_seq, t, tb, chunk)
    n_pairs = B_HEADS // 2
    tok = pl.BlockSpec((tb, pps * LANES), lambda s, p, i: (rb + s * nb + i, p))
    st = pl.BlockSpec((spb, pps, LANES, LANES), lambda s, p, i: (s, p, 0, 0))
    return pl.pallas_call(
        functools.partial(_rwkv_chunk_kernel, chunk=chunk),
        grid=(n_groups, n_pairs // pps, nb),
        in_specs=[tok] * 8 + [st],
        out_specs=[pl.BlockSpec((tb, pps * LANES), lambda s, p, i: (s * nb + i, p)), st],
        out_shape=[jax.ShapeDtypeStruct((n_seq * t, B_W), F32),
                   jax.ShapeDtypeStruct((n_seq, n_pairs, LANES, LANES), F32)],
        scratch_shapes=[pltpu.VMEM((spb, pps, LANES, LANES), F32)],
        compiler_params=_cparams(("parallel", "parallel", "arbitrary")),
        name="rwkv_chunks",
    )(rt, at, bt, kt, bh, kh, v, pc, z0)


def _rwkv_post_kernel(yp_ref, ys_ref, r_ref, kf_ref, v_ref, g_ref, lw_ref, lb_ref, rk_ref, o_ref, *, n_prompt_tiles):
    y = jnp.where(pl.program_id(0) < n_prompt_tiles, yp_ref[...], ys_ref[...])
    mean = _group_sum(y, B_DH) * (1.0 / B_DH)
    yc = y - mean
    var = _group_sum(yc * yc, B_DH) * (1.0 / B_DH)
    yn = yc * lax.rsqrt(var + B_GN_EPS) * lw_ref[...] + lb_ref[...]
    bonus = _group_sum(r_ref[...] * kf_ref[...] * rk_ref[...], B_DH)
    o_ref[...] = ((yn + bonus * v_ref[...]) * g_ref[...]).astype(BF16)


def _rwkv_post(y_p, y_s, r, kf, v, g, lnx_w, lnx_b, r_k, *, tm):
    m = r.shape[0]
    assert y_p.shape[0] % tm == 0 and y_s.shape[0] % tm == 0
    npt = y_p.shape[0] // tm
    tok = pl.BlockSpec((tm, B_W), lambda i: (i, 0))
    row = pl.BlockSpec((1, B_W), lambda i: (0, 0))
    return pl.pallas_call(
        functools.partial(_rwkv_post_kernel, n_prompt_tiles=npt),
        grid=(m // tm,),
        in_specs=[pl.BlockSpec((tm, B_W), lambda i: (jnp.minimum(i, npt - 1), 0)),
                  pl.BlockSpec((tm, B_W), lambda i: (jnp.maximum(i - npt, 0), 0))] + [tok] * 4 + [row] * 3,
        out_specs=tok,
        out_shape=jax.ShapeDtypeStruct((m, B_W), BF16),
        compiler_params=_cparams(("parallel",)),
        name="rwkv_post",
    )(y_p, y_s, r, kf, v, g, lnx_w.reshape(1, B_W), lnx_b.reshape(1, B_W), r_k.reshape(1, B_W))


def _rwkv_cols(a):
    lw0 = 3 * B_W
    la0 = lw0 + B_DECAY_LORA
    lg0 = la0 + B_AAA_LORA
    z = jnp.zeros(a.shape[:-1] + (LANES - B_DECAY_LORA,), a.dtype)
    return jnp.concatenate([a[..., :lw0], a[..., lg0:], a[..., lw0:la0], z, a[..., la0:lg0], z], axis=-1)


def _rwkv_cols_inv(a):
    return jnp.concatenate([a[..., :PB_LG], a[..., PB_LW:PB_LW + B_DECAY_LORA],
                            a[..., PB_LA:PB_LA + B_AAA_LORA], a[..., PB_LG:PB_LW]], axis=-1)


def _pad_rows(a, n):
    return jnp.pad(a, ((0, n - a.shape[0]), (0, 0)))


def _rwkv_state_to_pairs(s):
    n = s.shape[0]
    s = s.reshape(n, B_HEADS // 2, 2, B_DH, B_DH)
    eye = jnp.broadcast_to(jnp.eye(B_DH, dtype=F32), (n, B_HEADS // 2, B_DH, B_DH))
    top = jnp.concatenate([eye, s[:, :, 0]], axis=-1)
    bot = jnp.concatenate([s[:, :, 1], eye], axis=-1)
    return jnp.concatenate([top, bot], axis=-2)


def _rwkv_state_from_pairs(z):
    n = z.shape[0]
    s = jnp.stack([z[:, :, :B_DH, B_DH:], z[:, :, B_DH:, :B_DH]], axis=2)
    return s.reshape(n, B_HEADS, B_DH, B_DH)


def _rwkv_mixer(pb, n_p, t_p, n_s, t_s, prev_s, state_s, mu, w0, w2, a0, a2, g2, k_k, k_a, r_k, lnx_w, lnx_b,
                *, tm, tb):
    mp = n_p * t_p
    chunk_p = min(64, t_p)
    chunk_s = min(64, t_s)
    assert mp % tm == 0 and (n_s * t_s) % tm == 0 and t_p % chunk_p == 0 and t_s % chunk_s == 0
    starts = jnp.concatenate([
        jnp.zeros((tm, PB_COLS), F32),
        jnp.broadcast_to(_rwkv_cols(prev_s), (n_s, t_s, PB_COLS)).reshape(n_s * t_s, PB_COLS)], axis=0)
    r, kf, v, g, rt, at, bt, kt, bh, kh, pc = _rwkv_prep(
        pb, starts, _rwkv_cols(mu.reshape(1, B_COLS)), w0, _pad_rows(w2, LANES).astype(BF16), a0,
        _pad_rows(a2, LANES).astype(BF16), g2.astype(BF16), k_k, k_a, tm=tm, n_prompt_tiles=mp // tm,
        chunk_p=chunk_p, chunk_s=chunk_s, t_p=t_p, t_s=t_s)
    z0_p = _rwkv_state_to_pairs(jnp.zeros((n_p, B_HEADS, B_DH, B_DH), F32))
    y_p, z_p = _rwkv_chunks(rt, at, bt, kt, bh, kh, v, pc, z0_p, 0, n_p, t_p, tb=tb, chunk=chunk_p)
    y_s, z_s = _rwkv_chunks(rt, at, bt, kt, bh, kh, v, pc, _rwkv_state_to_pairs(state_s), mp, n_s, t_s,
                            tb=t_s, chunk=chunk_s, pps=B_HEADS // 2)
    ob = _rwkv_post(y_p, y_s, r, kf, v, g, lnx_w, lnx_b, r_k.reshape(B_W), tm=tm)
    return ob, _rwkv_state_from_pairs(z_p), _rwkv_state_from_pairs(z_s)


PC_Z = C_CONV_CH
PC_BETA = PC_Z + C_V
PC_COLS = PC_BETA + 2 * LANES
HALO = 8


def _chunk_cumsum(g, chunk):
    rowc = jnp.bitwise_and(lax.broadcasted_iota(jnp.int32, g.shape, 0), chunk - 1)
    k = 1
    while k < chunk:
        g = g + jnp.where(rowc >= k, pltpu.roll(g, k, 0), 0.0)
        k *= 2
    return g


def _transpose_rows(a):
    c = a.shape[0]
    if c < LANES:
        a = jnp.concatenate([a, jnp.zeros((LANES - c, LANES), F32)], axis=0)
    return a.T


def _pad_chunk_rows(a):
    c = a.shape[0]
    return a if c == LANES else jnp.concatenate([a, jnp.zeros((LANES - c, a.shape[1]), a.dtype)], axis=0)


def _gdn_prep_kernel(x_ref, halo_ref, st_ref, cw_ref, sm_ref, alog_ref, dtb_ref,
                     q_ref, k_ref, v_ref, beta_ref, gc_ref, *, chunk):
    i = pl.program_id(1)
    x = x_ref[...]
    tm = x.shape[0]
    halo = jnp.where(i == 0, st_ref[...], halo_ref[...])
    row8 = lax.broadcasted_iota(jnp.int32, (HALO, x.shape[1]), 0)
    acc = x * cw_ref[C_CONV - 1:C_CONV, :]
    for s in range(1, C_CONV):
        xs = pltpu.roll(x, s, 0)
        first = jnp.where(row8 < s, pltpu.roll(halo, s, 0), xs[:HALO])
        xs = first if tm == HALO else jnp.concatenate([first, xs[HALO:]], axis=0)
        acc = acc + xs * cw_ref[C_CONV - 1 - s:C_CONV - s, :]
    y = _silu(acc)
    q_ref[...] = _group_norm_scale(y[:, :C_QK], C_DH, 1.0, EPS) * (C_DH ** -0.5)
    k_ref[...] = _group_norm_scale(y[:, C_QK:2 * C_QK], C_DH, 1.0, EPS)
    v_ref[...] = y[:, 2 * C_QK:]
    sm = sm_ref[...]
    beta_ref[...] = _sigmoid(sm[:, :LANES])
    g = -jnp.exp(alog_ref[...]) * _softplus(sm[:, LANES:] + dtb_ref[...])
    gc_ref[...] = _chunk_cumsum(g, chunk)


def _gdn_prep(pc, conv_state, row0, n_seq, t, conv_w, a_log, dt_bias, *, tm, chunk):
    assert t % tm == 0 and row0 % tm == 0 and tm % HALO == 0 and tm % chunk == 0 and chunk & (chunk - 1) == 0
    nb = t // tm
    rb = row0 // tm
    hb = tm // HALO
    tok_in = lambda n, cb: pl.BlockSpec((tm, n), lambda s, i: (rb + s * nb + i, cb))
    tok_out = lambda n: pl.BlockSpec((tm, n), lambda s, i: (s * nb + i, 0))
    row = lambda n: pl.BlockSpec((1, n), lambda s, i: (0, 0))
    sds = lambda n: jax.ShapeDtypeStruct((n_seq * t, n), F32)
    lanes8 = lambda a: jnp.zeros((1, LANES), F32).at[0, :C_V_HEADS].set(a)
    return pl.pallas_call(
        functools.partial(_gdn_prep_kernel, chunk=chunk),
        grid=(n_seq, nb),
        in_specs=[
            tok_in(C_CONV_CH, 0),
            pl.BlockSpec((HALO, C_CONV_CH), lambda s, i: (jnp.maximum((rb + s * nb + i) * hb - 1, 0), 0)),
            pl.BlockSpec((None, HALO, C_CONV_CH), lambda s, i: (s, 0, 0)),
            pl.BlockSpec((C_CONV, C_CONV_CH), lambda s, i: (0, 0)),
            tok_in(2 * LANES, PC_BETA // (2 * LANES)),
            row(LANES), row(LANES),
        ],
        out_specs=[tok_out(C_QK), tok_out(C_QK), tok_out(C_V), tok_out(LANES), tok_out(LANES)],
        out_shape=[sds(C_QK), sds(C_QK), sds(C_V), sds(LANES), sds(LANES)],
        compiler_params=_cparams(("parallel", "arbitrary")),
        name="gdn_prep",
    )(pc, pc, conv_state, conv_w, pc, lanes8(a_log), lanes8(dt_bias))


def _gdn_chunk_kernel(q_ref, k_ref, v_ref, beta_ref, gc_ref, z_ref, on_ref, s0_ref, o_ref, so_ref, s_scr, *, chunk):
    hq = pl.program_id(1)
    i = pl.program_id(2)
    tb = q_ref.shape[0]
    rep = v_ref.shape[1] // C_DH

    @pl.when(i == 0)
    def _():
        s_scr[...] = s0_ref[...]

    lane = lax.broadcasted_iota(jnp.int32, (chunk, LANES), 1)
    ri = lax.broadcasted_iota(jnp.int32, (chunk, chunk), 0)
    ci = lax.broadcasted_iota(jnp.int32, (chunk, chunk), 1)
    nt = (((1,), (1,)), ((), ()))
    ones = jnp.ones((chunk, LANES), BF16)
    n_chunks = tb // chunk
    lmats, rhss, attns, qgs, kds, eglast = [], [], [], [], [], []
    for c in range(n_chunks):
        rs = slice(c * chunk, (c + 1) * chunk)
        q, k = q_ref[rs, :], k_ref[rs, :]
        qb, kbf = q.astype(BF16), k.astype(BF16)
        kkt = lax.dot_general(kbf, kbf, nt, preferred_element_type=F32)
        qkt = lax.dot_general(qb, kbf, nt, preferred_element_type=F32)
        for j in range(rep):
            hsel = lane == hq * rep + j
            beta = jnp.sum(jnp.where(hsel, beta_ref[rs, :], 0.0), axis=-1, keepdims=True)
            gcol = jnp.sum(jnp.where(hsel, gc_ref[rs, :], 0.0), axis=-1, keepdims=True)
            g0 = jnp.where(lane == 0, gcol, 0.0)
            grow = sum(lax.dot_general(ones, part, nt, preferred_element_type=F32) for part in _split3(g0))
            dec = jnp.exp(jnp.where(ri >= ci, gcol - grow, NEG_INF))
            lmats.append(jnp.where(ri > ci, beta * kkt * dec, 0.0))
            rhss.append(jnp.concatenate([v_ref[rs, j * C_DH:(j + 1) * C_DH] * beta, k * (beta * jnp.exp(gcol))],
                                        axis=1))
            attns.append((qkt * dec).astype(BF16))
            qgs.append((q * jnp.exp(gcol)).astype(BF16))
            glast = gcol[chunk - 1:chunk, :]
            kds.append(_transpose_rows(k * jnp.exp(glast - gcol)).astype(BF16))
            eglast.append(jnp.exp(glast))
    xs = _solve_unit_lower_many(lmats, rhss)
    n_sq = s_scr.shape[0]
    cps = n_chunks // n_sq
    ss = {(sq, j): s_scr[sq, j] for sq in range(n_sq) for j in range(rep)}
    for c in range(n_chunks):
        rs = slice(c * chunk, (c + 1) * chunk)
        for j in range(rep):
            n = c * rep + j
            key = (c // cps, j)
            sb = ss[key].astype(BF16)
            v_new = xs[n][:, :C_DH] - jnp.dot(xs[n][:, C_DH:].astype(BF16), sb, preferred_element_type=F32)
            vb = v_new.astype(BF16)
            o = jnp.dot(qgs[n], sb, preferred_element_type=F32) + jnp.dot(attns[n], vb, preferred_element_type=F32)
            ss[key] = ss[key] * eglast[n] + jnp.dot(kds[n], _pad_chunk_rows(vb), preferred_element_type=F32)
            o = _rms_rows(o, on_ref[...]) * _silu(z_ref[rs, j * C_DH:(j + 1) * C_DH])
            o_ref[rs, j * C_DH:(j + 1) * C_DH] = o.astype(BF16)
    for (sq, j), val in ss.items():
        s_scr[sq, j] = val

    @pl.when(i == pl.num_programs(2) - 1)
    def _():
        so_ref[...] = s_scr[...]


def _seq_blocking(row0, n_seq, t, tb, chunk):
    assert row0 % tb == 0 and tb % chunk == 0 and (t % tb == 0 or (tb % t == 0 and n_seq % (tb // t) == 0))
    spb = max(tb // t, 1)
    return spb, max(t // tb, 1), n_seq // spb, row0 // tb


def _gdn_chunks(q, k, v, beta, gc, pc, row0, n_seq, t, onorm, s0, *, tb, chunk):
    spb, nb, n_groups, rb = _seq_blocking(row0, n_seq, t, tb, chunk)
    rep = C_V_HEADS // C_QK_HEADS
    tok = lambda n, cb: pl.BlockSpec((tb, n), cb)
    st = pl.BlockSpec((spb, rep, C_DH, C_DH), lambda s, h, i: (s, h, 0, 0))
    return pl.pallas_call(
        functools.partial(_gdn_chunk_kernel, chunk=chunk),
        grid=(n_groups, C_QK_HEADS, nb),
        in_specs=[
            tok(C_DH, lambda s, h, i: (s * nb + i, h)),
            tok(C_DH, lambda s, h, i: (s * nb + i, h)),
            tok(rep * C_DH, lambda s, h, i: (s * nb + i, h)),
            tok(LANES, lambda s, h, i: (s * nb + i, 0)),
            tok(LANES, lambda s, h, i: (s * nb + i, 0)),
            tok(rep * C_DH, lambda s, h, i: (rb + s * nb + i, PC_Z // (rep * C_DH) + h)),
            pl.BlockSpec((1, LANES), lambda s, h, i: (0, 0)),
            st,
        ],
        out_specs=[tok(rep * C_DH, lambda s, h, i: (s * nb + i, h)), st],
        out_shape=[jax.ShapeDtypeStruct((n_seq * t, C_V), BF16),
                   jax.ShapeDtypeStruct((n_seq, C_V_HEADS, C_DH, C_DH), F32)],
        scratch_shapes=[pltpu.VMEM((spb, rep, C_DH, C_DH), F32)],
        compiler_params=_cparams(("parallel", "parallel", "arbitrary")),
        name="gdn_chunks",
    )(q, k, v, beta, gc, pc, onorm.reshape(1, LANES), s0)


def _gdn_group(pc, conv_state3, s0, row0, n_seq, t, conv_w, a_log, dt_bias, onorm, *, tm, tb):
    chunk = C_CHUNK if t % C_CHUNK == 0 else t
    conv_state = jnp.pad(conv_state3, ((0, 0), (HALO - (C_CONV - 1), 0), (0, 0)))
    q, k, v, beta, gc = _gdn_prep(pc, conv_state, row0, n_seq, t, conv_w, a_log, dt_bias, tm=tm, chunk=chunk)
    return _gdn_chunks(q, k, v, beta, gc, pc, row0, n_seq, t, onorm, s0, tb=tb, chunk=chunk)


PD_V = 2 * D_K
PD_GATE = PD_V + D_V
PD_GLR = PD_GATE + D_V
PD_COLS = PD_GLR + LANES


def _gla_kernel(q_ref, k_ref, v_ref, gate_ref, glr_ref, w2_ref, bgk_ref, on_ref, s0_ref, o_ref, so_ref, s_scr,
                *, chunk):
    i = pl.program_id(2)
    tb = q_ref.shape[0]

    @pl.when(i == 0)
    def _():
        s_scr[...] = s0_ref[...]

    pre = jnp.dot(glr_ref[...].astype(BF16), w2_ref[...], preferred_element_type=F32) + bgk_ref[...]
    bcum = _chunk_cumsum(-_softplus(-pre) * (1.0 / D_GATE_NORM), chunk)
    ri = lax.broadcasted_iota(jnp.int32, (chunk, 1), 0)
    ci = lax.broadcasted_iota(jnp.int32, (chunk, chunk), 1)
    intra, qgs, decays, upds = [], [], [], []
    for c in range(tb // chunk):
        rs = slice(c * chunk, (c + 1) * chunk)
        q = q_ref[rs, :] * (D_DK ** -0.5)
        k, bc = k_ref[rs, :], bcum[rs, :]
        vb = v_ref[rs, :].astype(BF16)
        attn = jnp.zeros((chunk, chunk), F32)
        for j in range(chunk):
            e = jnp.exp(jnp.where(ri >= j, bc - bc[j:j + 1, :], NEG_INF))
            col = jnp.sum(q * k[j:j + 1, :] * e, axis=-1, keepdims=True)
            attn = jnp.where(ci == j, col, attn)
        intra.append(jnp.dot(attn.astype(BF16), vb, preferred_element_type=F32))
        qgs.append((q * jnp.exp(bc)).astype(BF16))
        blast = bc[chunk - 1:chunk, :]
        ebt = jnp.broadcast_to(jnp.exp(blast), (LANES, LANES)).T
        decays.append(jnp.concatenate([ebt, ebt], axis=1))
        kdt = _transpose_rows(k * jnp.exp(blast - bc)).astype(BF16)
        upds.append(jnp.dot(kdt, _pad_chunk_rows(vb), preferred_element_type=F32))
    n_sq = s_scr.shape[0]
    cps = (tb // chunk) // n_sq
    ss = [s_scr[sq] for sq in range(n_sq)]
    for c in range(tb // chunk):
        rs = slice(c * chunk, (c + 1) * chunk)
        sq = c // cps
        o = jnp.dot(qgs[c], ss[sq].astype(BF16), preferred_element_type=F32) + intra[c]
        ss[sq] = ss[sq] * decays[c] + upds[c]
        o = _rms_rows(o, on_ref[...]) * _silu(gate_ref[rs, :])
        o_ref[rs, :] = o.astype(BF16)
    for sq in range(n_sq):
        s_scr[sq] = ss[sq]

    @pl.when(i == pl.num_programs(2) - 1)
    def _():
        so_ref[...] = s_scr[...]


def _gla_group(pd, s0, row0, n_seq, t, w_gk2, b_gk, onorm, *, tb):
    chunk = D_CHUNK if t % D_CHUNK == 0 else t
    assert chunk & (chunk - 1) == 0
    spb, nb, n_groups, rb = _seq_blocking(row0, n_seq, t, tb, chunk)
    w2 = _pad_rows(w_gk2, LANES).astype(BF16)
    tok = lambda n, cb: pl.BlockSpec((tb, n), cb)
    st = pl.BlockSpec((spb, None, D_DK, D_DV), lambda s, h, i: (s, h, 0, 0))
    return pl.pallas_call(
        functools.partial(_gla_kernel, chunk=chunk),
        grid=(n_groups, D_HEADS, nb),
        in_specs=[
            tok(D_DK, lambda s, h, i: (rb + s * nb + i, h)),
            tok(D_DK, lambda s, h, i: (rb + s * nb + i, D_K // D_DK + h)),
            tok(D_DV, lambda s, h, i: (rb + s * nb + i, PD_V // D_DV + h)),
            tok(D_DV, lambda s, h, i: (rb + s * nb + i, PD_GATE // D_DV + h)),
            tok(LANES, lambda s, h, i: (rb + s * nb + i, PD_GLR // LANES)),
            pl.BlockSpec((LANES, D_DK), lambda s, h, i: (0, h)),
            pl.BlockSpec((1, D_DK), lambda s, h, i: (0, h)),
            pl.BlockSpec((1, D_DV), lambda s, h, i: (0, 0)),
            st,
        ],
        out_specs=[tok(D_DV, lambda s, h, i: (s * nb + i, h)), st],
        out_shape=[jax.ShapeDtypeStruct((n_seq * t, D_V), BF16),
                   jax.ShapeDtypeStruct((n_seq, D_HEADS, D_DK, D_DV), F32)],
        scratch_shapes=[pltpu.VMEM((spb, D_DK, D_DV), F32)],
        compiler_params=_cparams(("parallel", "parallel", "arbitrary")),
        name="gla_chunks",
    )(pd, pd, pd, pd, pd, w2, b_gk.reshape(1, D_K), onorm.reshape(1, D_DV), s0)


TM = 768
TM_SMALL = 256


def _seqs_per_block(n_seq, want):
    return max(s for s in range(1, want + 1) if n_seq % s == 0)


def _last_rows(a, row0, n_seq, t, k, n_cols):
    picks = [lax.slice(a, (row0 + t - k + j, 0), (row0 + n_seq * t, n_cols), (t, 1)) for j in range(k)]
    return jnp.stack(picks, axis=1)


def _cross_block(x, layer, n_p, t_p, n_s, t_s, mem_prompt, cache_mem_k, cache_mem_v, norm_mem_w, norm_cross_w,
                 x_w_q, x_w_k, x_w_v, x_w_o, x_q_norm, x_k_norm):
    d = x.shape[1]
    mp = n_p * t_p
    wkv = jnp.concatenate([x_w_k[layer], x_w_v[layer]], axis=1).astype(BF16)
    hw = jnp.concatenate([jnp.tile(x_k_norm[layer], X_HEADS), jnp.ones((X_W,), F32)]).reshape(1, 2 * X_W)
    mem = mem_prompt.reshape(n_p * N_MEM, d)
    kv = _norm_matmul(mem, norm_mem_w[layer], wkv, tm=min(512, n_p * N_MEM), tn=X_W, head_w=hw,
                      n_norm_cols=X_W, gsize=X_DH)
    mk_p = kv[:, :X_W].reshape(n_p, N_MEM, X_W)
    mv_p = kv[:, X_W:].reshape(n_p, N_MEM, X_W)
    qx = _norm_matmul(x, norm_cross_w[layer], x_w_q[layer].astype(BF16), tm=TM, tn=X_W)
    ca_p = _cross_attention(qx, 0, n_p, t_p, mk_p, mv_p, x_q_norm[layer], tq=min(512, t_p))
    ca_s = _cross_attention(qx, mp, n_s, t_s, cache_mem_k, cache_mem_v, x_q_norm[layer], tq=t_s, mem_layer=layer)
    ca = jnp.concatenate([ca_p, ca_s], axis=0)
    x = _matmul_residual([ca], [x_w_o[layer].astype(BF16)], x, tm=TM, tn=1024)
    return x, mk_p.reshape(n_p, N_MEM, X_HEADS, X_DH), mv_p.reshape(n_p, N_MEM, X_HEADS, X_DH)


def kernel(x_prompt, x_sample, cache_diff_k, cache_diff_v, state_rwkv, state_rwkv_shift, cache_mem_k, cache_mem_v, state_gdn, state_gdn_conv, state_gla, page_table, mem_prompt, norm_mix_w, norm_cross_w, norm_mem_w, norm_ffn_w, w_in_even, w_out_even, a_q_norm, a_k_norm, a_lam_q1, a_lam_k1, a_lam_q2, a_lam_k2, a_subln, b_mu, b_w0, b_w2, b_a0, b_a2, b_g2, b_k_k, b_k_a, b_r_k, b_lnx_w, b_lnx_b, ffd_w_gate, ffd_w_up, ffd_w_down, w_in_odd, w_out_odd, c_conv_w, c_a_log, c_dt_bias, c_onorm, d_w_gk2, d_b_gk, d_onorm, moe_w_router, moe_w_gate, moe_w_up, moe_w_down, x_w_q, x_w_k, x_w_v, x_w_o, x_q_norm, x_k_norm):
    n_p, t_p, d = x_prompt.shape
    n_s, t_s, _ = x_sample.shape
    mp, ms = n_p * t_p, n_s * t_s
    depth = norm_mix_w.shape[0]
    x = jnp.concatenate([x_prompt.reshape(mp, d), x_sample.reshape(ms, d)], axis=0)
    assert (mp + ms) % TM == 0 and (mp + ms) % TM_SMALL == 0
    page = cache_diff_k.shape[2]
    slopes = jnp.exp2(-(8.0 / A_HEADS) * jnp.arange(1, A_HEADS + 1, dtype=F32))
    cross_w = (mem_prompt, cache_mem_k, cache_mem_v, norm_mem_w, norm_cross_w, x_w_q, x_w_k, x_w_v, x_w_o,
               x_q_norm, x_k_norm)
    dk_p, dv_p, dk_s, dv_s, rw_p, rw_s, sh_p, sh_s = [], [], [], [], [], [], [], []
    mk_l, mv_l, gd_p, gd_s, cv_p, cv_s, gl_p, gl_s = [], [], [], [], [], [], [], []
    for layer in range(depth):
        i = layer // 2
        if layer % 2 == 0:
            lam_init = 0.8 - 0.6 * math.exp(-0.3 * layer)
            w_in = w_in_even[i]
            hw = jnp.concatenate([jnp.tile(a_q_norm[i], A_Q // A_DH), jnp.tile(a_k_norm[i], A_Q // A_DH),
                                  jnp.ones((A_COLS - 2 * A_Q,), F32)]).reshape(1, A_COLS)
            pa = _norm_matmul(x, norm_mix_w[layer], w_in[:, :A_COLS].astype(BF16), tm=TM, tn=1024, head_w=hw,
                              n_norm_cols=2 * A_Q, gsize=A_DH)
            pb = _norm_matmul(x, norm_mix_w[layer], _rwkv_cols(w_in[:, A_COLS:]).astype(BF16), tm=TM,
                              tn=PB_COLS // 2)
            lam_stack = jnp.stack([a_lam_q1[i], a_lam_k1[i], a_lam_q2[i], a_lam_k2[i]])
            oa_p = _diff_attention_prompt(pa, n_p, t_p, slopes, lam_stack, a_subln[i], lam_init, tq=min(256, t_p))
            oa_s = _diff_attention_sample(pa, mp, n_s, t_s, cache_diff_k, cache_diff_v, i, page_table, lam_stack,
                                          a_subln[i], lam_init)
            ob, st_p, st_s = _rwkv_mixer(pb, n_p, t_p, n_s, t_s, state_rwkv_shift[i], state_rwkv[i], b_mu[i],
                                         b_w0[i], b_w2[i], b_a0[i], b_a2[i], b_g2[i], b_k_k[i], b_k_a[i], b_r_k[i],
                                         b_lnx_w[i], b_lnx_b[i], tm=TM_SMALL, tb=min(256, t_p))
            dk_p.append(pa[:mp, A_Q:2 * A_Q].reshape(n_p, t_p, A_HEADS, 2 * A_DH))
            dv_p.append(pa[:mp, 2 * A_Q:].reshape(n_p, t_p, A_HEADS, 2 * A_DH))
            dk_s.append(pa[mp:, A_Q:2 * A_Q].reshape(n_s, t_s, A_HEADS, 2 * A_DH))
            dv_s.append(pa[mp:, 2 * A_Q:].reshape(n_s, t_s, A_HEADS, 2 * A_DH))
            rw_p.append(st_p)
            rw_s.append(st_s)
            sh_p.append(_rwkv_cols_inv(_last_rows(pb, 0, n_p, t_p, 1, PB_COLS)))
            sh_s.append(_rwkv_cols_inv(_last_rows(pb, mp, n_s, t_s, 1, PB_COLS)))
            oa = jnp.concatenate([oa_p, oa_s], axis=0)
            w_out = w_out_even[i].astype(BF16)
            x = _matmul_residual([oa, ob], [w_out[:A_Q], w_out[A_Q:]], x, tm=TM, tn=1024)
        else:
            w_in = w_in_odd[i]
            c_cols = C_CONV_CH + C_V + 2 * C_V_HEADS
            zc = jnp.zeros((d, LANES - C_V_HEADS), F32)
            wc = jnp.concatenate([w_in[:, :PC_BETA], w_in[:, PC_BETA:PC_BETA + C_V_HEADS], zc,
                                  w_in[:, PC_BETA + C_V_HEADS:c_cols], zc], axis=1).astype(BF16)
            wd_ = jnp.concatenate([w_in[:, c_cols:], jnp.zeros((d, LANES - D_GATE_LORA), F32)], axis=1).astype(BF16)
            pc = _norm_matmul(x, norm_mix_w[layer], wc, tm=TM, tn=PC_COLS // 2)
            pd = _norm_matmul(x, norm_mix_w[layer], wd_, tm=TM, tn=PD_COLS // 5)
            gdn_w = (c_conv_w[i], c_a_log[i], c_dt_bias[i], c_onorm[i])
            oc_p, gs_p = _gdn_group(pc, jnp.zeros((n_p, C_CONV - 1, C_CONV_CH), F32),
                                    jnp.zeros((n_p, C_V_HEADS, C_DH, C_DH), F32), 0, n_p, t_p, *gdn_w,
                                    tm=min(512, t_p), tb=min(512, t_p))
            oc_s, gs_s = _gdn_group(pc, state_gdn_conv[i], state_gdn[i], mp, n_s, t_s, *gdn_w, tm=t_s,
                                    tb=_seqs_per_block(n_s, 8) * t_s)
            gla_w = (d_w_gk2[i], d_b_gk[i], d_onorm[i])
            od_p, ls_p = _gla_group(pd, jnp.zeros((n_p, D_HEADS, D_DK, D_DV), F32), 0, n_p, t_p, *gla_w,
                                    tb=min(512, t_p))
            od_s, ls_s = _gla_group(pd, state_gla[i], mp, n_s, t_s, *gla_w, tb=_seqs_per_block(n_s, 8) * t_s)
            gd_p.append(gs_p)
            gd_s.append(gs_s)
            cv_p.append(_last_rows(pc, 0, n_p, t_p, C_CONV - 1, C_CONV_CH))
            cv_s.append(_last_rows(pc, mp, n_s, t_s, C_CONV - 1, C_CONV_CH))
            gl_p.append(ls_p)
            gl_s.append(ls_s)
            oc = jnp.concatenate([oc_p, oc_s], axis=0)
            od = jnp.concatenate([od_p, od_s], axis=0)
            w_out = w_out_odd[i].astype(BF16)
            x = _matmul_residual([oc, od], [w_out[:C_V], w_out[C_V:]], x, tm=TM, tn=1024)
        x, mk_p, mv_p = _cross_block(x, layer, n_p, t_p, n_s, t_s, *cross_w)
        mk_l.append(mk_p)
        mv_l.append(mv_p)
        if layer % 2 == 0:
            x = _swiglu_ffn(x, norm_ffn_w[layer], ffd_w_gate[i].astype(BF16), ffd_w_up[i].astype(BF16),
                            ffd_w_down[i].astype(BF16), tm=TM, tf=512)
        else:
            y_p, y_s = _moe_ffn(x, norm_ffn_w[layer], moe_w_router[i], moe_w_gate, moe_w_up, moe_w_down, i,
                                tm=TM, tf=256, tc=TM_SMALL, split=mp)
            if layer + 1 < depth:
                x = jnp.concatenate([y_p, y_s], axis=0)
    if depth % 2 == 1:
        y_p, y_s = x[:mp], x[mp:]
    st = jnp.stack
    return (y_p.reshape(n_p, t_p, d), y_s.reshape(n_s, t_s, d), st(dk_p), st(dv_p), st(dk_s), st(dv_s),
            st(rw_p), st(rw_s), st(sh_p), st(sh_s), st(mk_l), st(mv_l), st(gd_p), st(gd_s), st(cv_p), st(cv_s),
            st(gl_p), st(gl_s))
```
